```python
import math
import jax, jax.numpy as jnp
from jax import lax
import numpy as np

D_MODEL = 1024
BATCH = 8
SEQ = 8192
DEPTH = 4

SSM_GROUP = 16
N_GROUPS = D_MODEL // SSM_GROUP
SSM_STATE = 64
N_HEADS = 16
HEAD_DIM = D_MODEL // N_HEADS
ATTN_DIM = N_HEADS * HEAD_DIM
D_FF = 4 * D_MODEL
Q_BLOCK = 128
N_A_LAYERS = DEPTH // 2
N_B_LAYERS = DEPTH - N_A_LAYERS
RMS_EPS = 1e-6
DT_MIN = 1e-3
DT_MAX = 1e-1

kernel_name = "s5_fox_yoco_hybrid_trunk"


def rmsnorm(x, g):
    xf = x.astype(jnp.float32)
    y = xf * lax.rsqrt(jnp.mean(xf * xf, axis=-1, keepdims=True) + RMS_EPS)
    return (y * g.astype(jnp.float32)).astype(x.dtype)


def sqrelu_mlp(h, w1, w2):
    a = jnp.square(jax.nn.relu(h @ w1))
    return a @ w2


def _ssm_binop(e1, e2):
    a1, b1 = e1
    a2, b2 = e2
    return a2 * a1, a2 * b1 + b2


def s5_mixer(u, log_dt, a_re, a_im, b_re, b_im, c_re, c_im, d_skip, w_glu):
    f32 = jnp.float32
    bsz, length, _ = u.shape
    uf = u.astype(f32).reshape(bsz, length, N_GROUPS, SSM_GROUP)
    lam = lax.complex(a_re.astype(f32), a_im.astype(f32))
    dt = jnp.exp(log_dt.astype(f32))[:, None]
    lam_bar = jnp.exp(lam * dt)
    b = lax.complex(b_re.astype(f32), b_im.astype(f32))
    b_bar = ((lam_bar - 1.0) / lam)[..., None] * b
    bu = jnp.einsum('blgc,gpc->blgp', uf.astype(jnp.complex64), b_bar)
    a_elems = jnp.broadcast_to(lam_bar, bu.shape)
    _, states = lax.associative_scan(_ssm_binop, (a_elems, bu), axis=1)
    c = lax.complex(c_re.astype(f32), c_im.astype(f32))
    y = jnp.real(jnp.einsum('blgp,gcp->blgc', states, c))
    y = y + d_skip.astype(f32).reshape(N_GROUPS, SSM_GROUP) * uf
    z = jax.nn.gelu(y.reshape(bsz, length, D_MODEL)).astype(u.dtype)
    zw = z @ w_glu
    val, gate = zw[..., :D_MODEL], zw[..., D_MODEL:]
    return val * jax.nn.sigmoid(gate)


def shared_kv(h, kv_norm, w_kvf, b_f):
    bsz, length, _ = h.shape
    kvf = rmsnorm(h, kv_norm) @ w_kvf
    k = kvf[..., :ATTN_DIM].reshape(bsz, length, N_HEADS, HEAD_DIM).transpose(0, 2, 1, 3)
    v = kvf[..., ATTN_DIM:2 * ATTN_DIM].reshape(bsz, length, N_HEADS, HEAD_DIM).transpose(0, 2, 1, 3)
    f_logit = kvf[..., 2 * ATTN_DIM:].astype(jnp.float32) + b_f.astype(jnp.float32)
    log_f = jax.nn.log_sigmoid(f_logit)
    cum_log_f = jnp.cumsum(log_f, axis=1).transpose(0, 2, 1)
    return k, v, cum_log_f


def fox_attention(hn, wq, wo, k, v, cum_log_f):
    bsz, length, _ = hn.shape
    nb = length // Q_BLOCK
    q = (hn @ wq) * (HEAD_DIM ** -0.5)
    q_blocks = q.reshape(bsz, nb, Q_BLOCK, N_HEADS, HEAD_DIM).transpose(1, 0, 3, 2, 4)
    f_blocks = cum_log_f.reshape(bsz, N_HEADS, nb, Q_BLOCK).transpose(2, 0, 1, 3)
    pos_q = jnp.arange(length, dtype=jnp.int32).reshape(nb, Q_BLOCK)
    pos_k = jnp.arange(length, dtype=jnp.int32)

    def one_block(args):
        qb, fq, pq = args
        s = jnp.einsum('bhqd,bhkd->bhqk', qb, k).astype(jnp.float32)
        s = s + fq[..., None] - cum_log_f[:, :, None, :]
        mask = pq[:, None] >= pos_k[None, :]
        s = jnp.where(mask[None, None], s, -jnp.inf)
        p = jax.nn.softmax(s, axis=-1)
        return jnp.einsum('bhqk,bhkd->bhqd', p.astype(v.dtype), v)

    o = lax.map(one_block, (q_blocks, f_blocks, pos_q))
    o = o.transpose(1, 0, 3, 2, 4).reshape(bsz, length, ATTN_DIM)
    return o @ wo


def _fwd_setup_inputs(seed: int = 0) -> dict:
    key = jax.random.key(seed)
    ks = jax.random.split(key, 20)
    f32 = jnp.float32

    def nrm(k, shape, scale):
        return scale * jax.random.normal(k, shape, f32)

    x = nrm(ks[0], (BATCH, SEQ, D_MODEL), 1.0)
    mix_norm = 1.0 + nrm(ks[1], (DEPTH, D_MODEL), 0.05)
    mlp_norm = 1.0 + nrm(ks[2], (DEPTH, D_MODEL), 0.05)
    mlp_w1 = nrm(ks[3], (DEPTH, D_MODEL, D_FF), D_MODEL ** -0.5)
    mlp_w2 = nrm(ks[4], (DEPTH, D_FF, D_MODEL), 0.5 * D_FF ** -0.5)
    ssm_log_dt = jax.random.uniform(ks[5], (N_A_LAYERS, N_GROUPS), f32,
                                    math.log(DT_MIN), math.log(DT_MAX))
    ssm_a_re = -0.5 + nrm(ks[6], (N_A_LAYERS, N_GROUPS, SSM_STATE), 0.01)
    ssm_a_im = math.pi * jnp.arange(SSM_STATE, dtype=f32) + nrm(ks[7], (N_A_LAYERS, N_GROUPS, SSM_STATE), 0.01)
    ssm_b_re = nrm(ks[8], (N_A_LAYERS, N_GROUPS, SSM_STATE, SSM_GROUP), (2 * SSM_GROUP) ** -0.5)
    ssm_b_im = nrm(ks[9], (N_A_LAYERS, N_GROUPS, SSM_STATE, SSM_GROUP), (2 * SSM_GROUP) ** -0.5)
    ssm_c_re = nrm(ks[10], (N_A_LAYERS, N_GROUPS, SSM_GROUP, SSM_STATE), (2 * SSM_STATE) ** -0.5)
    ssm_c_im = nrm(ks[11], (N_A_LAYERS, N_GROUPS, SSM_GROUP, SSM_STATE), (2 * SSM_STATE) ** -0.5)
    ssm_d = nrm(ks[12], (N_A_LAYERS, D_MODEL), 1.0)
    ssm_w_glu = nrm(ks[13], (N_A_LAYERS, D_MODEL, 2 * D_MODEL), D_MODEL ** -0.5)
    kv_norm = 1.0 + nrm(ks[14], (D_MODEL,), 0.05)
    w_kvf = nrm(ks[15], (D_MODEL, 2 * ATTN_DIM + N_HEADS), D_MODEL ** -0.5)
    b_f = jax.random.uniform(ks[16], (N_HEADS,), f32, 0.5, 3.0)
    attn_wq = nrm(ks[17], (N_B_LAYERS, D_MODEL, ATTN_DIM), D_MODEL ** -0.5)
    attn_wo = nrm(ks[18], (N_B_LAYERS, ATTN_DIM, D_MODEL), ATTN_DIM ** -0.5)
    final_norm = 1.0 + nrm(ks[19], (D_MODEL,), 0.05)
    return {"x": x, "mix_norm": mix_norm, "mlp_norm": mlp_norm, "mlp_w1": mlp_w1, "mlp_w2": mlp_w2,
            "ssm_log_dt": ssm_log_dt, "ssm_a_re": ssm_a_re, "ssm_a_im": ssm_a_im,
            "ssm_b_re": ssm_b_re, "ssm_b_im": ssm_b_im, "ssm_c_re": ssm_c_re, "ssm_c_im": ssm_c_im,
            "ssm_d": ssm_d, "ssm_w_glu": ssm_w_glu, "kv_norm": kv_norm, "w_kvf": w_kvf, "b_f": b_f,
            "attn_wq": attn_wq, "attn_wo": attn_wo, "final_norm": final_norm}


def _fwd_reference(x, mix_norm, mlp_norm, mlp_w1, mlp_w2, ssm_log_dt, ssm_a_re, ssm_a_im,
              ssm_b_re, ssm_b_im, ssm_c_re, ssm_c_im, ssm_d, ssm_w_glu, kv_norm, w_kvf, b_f,
              attn_wq, attn_wo, final_norm):
    h = x
    k = v = cum_log_f = None
    for i in range(DEPTH):
        hn = rmsnorm(h, mix_norm[i])
        if i < N_A_LAYERS:
            h = h + s5_mixer(hn, ssm_log_dt[i], ssm_a_re[i], ssm_a_im[i], ssm_b_re[i], ssm_b_im[i],
                             ssm_c_re[i], ssm_c_im[i], ssm_d[i], ssm_w_glu[i])
        else:
            j = i - N_A_LAYERS
            h = h + fox_attention(hn, attn_wq[j], attn_wo[j], k, v, cum_log_f)
        h = h + sqrelu_mlp(rmsnorm(h, mlp_norm[i]), mlp_w1[i], mlp_w2[i])
        if i == N_A_LAYERS - 1:
            k, v, cum_log_f = shared_kv(h, kv_norm, w_kvf, b_f)
    return rmsnorm(h, final_norm)


import jax as _jax
import jax.numpy as _jnp

TWIN_FORMAT = 'train_step'
FWD_PARAMS = ['x', 'mix_norm', 'mlp_norm', 'mlp_w1', 'mlp_w2', 'ssm_log_dt', 'ssm_a_re', 'ssm_a_im', 'ssm_b_re', 'ssm_b_im', 'ssm_c_re', 'ssm_c_im', 'ssm_d', 'ssm_w_glu', 'kv_norm', 'w_kvf', 'b_f', 'attn_wq', 'attn_wo', 'final_norm']
TWIN_WEIGHTS = ['mix_norm', 'mlp_norm', 'mlp_w1', 'mlp_w2', 'ssm_log_dt', 'ssm_a_re', 'ssm_a_im', 'ssm_b_re', 'ssm_b_im', 'ssm_c_re', 'ssm_c_im', 'ssm_d', 'ssm_w_glu', 'kv_norm', 'w_kvf', 'b_f', 'attn_wq', 'attn_wo', 'final_norm']
TWIN_DIFF_INPUT = 'x'
TWIN_INPUTS = ['x', 'mix_norm', 'mlp_norm', 'mlp_w1', 'mlp_w2', 'ssm_log_dt', 'ssm_a_re', 'ssm_a_im', 'ssm_b_re', 'ssm_b_im', 'ssm_c_re', 'ssm_c_im', 'ssm_d', 'ssm_w_glu', 'kv_norm', 'w_kvf', 'b_f', 'attn_wq', 'attn_wo', 'final_norm', 'loss_target', 'm_mix_norm', 'm_mlp_norm', 'm_mlp_w1', 'm_mlp_w2', 'm_ssm_log_dt', 'm_ssm_a_re', 'm_ssm_a_im', 'm_ssm_b_re', 'm_ssm_b_im', 'm_ssm_c_re', 'm_ssm_c_im', 'm_ssm_d', 'm_ssm_w_glu', 'm_kv_norm', 'm_w_kvf', 'm_b_f', 'm_attn_wq', 'm_attn_wo', 'm_final_norm', 'v_mix_norm', 'v_mlp_norm', 'v_mlp_w1', 'v_mlp_w2', 'v_ssm_log_dt', 'v_ssm_a_re', 'v_ssm_a_im', 'v_ssm_b_re', 'v_ssm_b_im', 'v_ssm_c_re', 'v_ssm_c_im', 'v_ssm_d', 'v_ssm_w_glu', 'v_kv_norm', 'v_w_kvf', 'v_b_f', 'v_attn_wq', 'v_attn_wo', 'v_final_norm']
TWIN_OUTPUTS = ['loss', 'grad_x', 'grad_mix_norm', 'grad_mlp_norm', 'grad_mlp_w1', 'grad_mlp_w2', 'grad_ssm_log_dt', 'grad_ssm_a_re', 'grad_ssm_a_im', 'grad_ssm_b_re', 'grad_ssm_b_im', 'grad_ssm_c_re', 'grad_ssm_c_im', 'grad_ssm_d', 'grad_ssm_w_glu', 'grad_kv_norm', 'grad_w_kvf', 'grad_b_f', 'grad_attn_wq', 'grad_attn_wo', 'grad_final_norm', 'delta_mix_norm', 'delta_mlp_norm', 'delta_mlp_w1', 'delta_mlp_w2', 'delta_ssm_log_dt', 'delta_ssm_a_re', 'delta_ssm_a_im', 'delta_ssm_b_re', 'delta_ssm_b_im', 'delta_ssm_c_re', 'delta_ssm_c_im', 'delta_ssm_d', 'delta_ssm_w_glu', 'delta_kv_norm', 'delta_w_kvf', 'delta_b_f', 'delta_attn_wq', 'delta_attn_wo', 'delta_final_norm', 'new_m_mix_norm', 'new_m_mlp_norm', 'new_m_mlp_w1', 'new_m_mlp_w2', 'new_m_ssm_log_dt', 'new_m_ssm_a_re', 'new_m_ssm_a_im', 'new_m_ssm_b_re', 'new_m_ssm_b_im', 'new_m_ssm_c_re', 'new_m_ssm_c_im', 'new_m_ssm_d', 'new_m_ssm_w_glu', 'new_m_kv_norm', 'new_m_w_kvf', 'new_m_b_f', 'new_m_attn_wq', 'new_m_attn_wo', 'new_m_final_norm', 'new_v_mix_norm', 'new_v_mlp_norm', 'new_v_mlp_w1', 'new_v_mlp_w2', 'new_v_ssm_log_dt', 'new_v_ssm_a_re', 'new_v_ssm_a_im', 'new_v_ssm_b_re', 'new_v_ssm_b_im', 'new_v_ssm_c_re', 'new_v_ssm_c_im', 'new_v_ssm_d', 'new_v_ssm_w_glu', 'new_v_kv_norm', 'new_v_w_kvf', 'new_v_b_f', 'new_v_attn_wq', 'new_v_attn_wo', 'new_v_final_norm']
TWIN_LEAF_KINDS = {'loss': 'loss', 'grad_x': 'grad_x', 'grad_mix_norm': 'grad_w', 'grad_mlp_norm': 'grad_w', 'grad_mlp_w1': 'grad_w', 'grad_mlp_w2': 'grad_w', 'grad_ssm_log_dt': 'grad_w', 'grad_ssm_a_re': 'grad_w', 'grad_ssm_a_im': 'grad_w', 'grad_ssm_b_re': 'grad_w', 'grad_ssm_b_im': 'grad_w', 'grad_ssm_c_re': 'grad_w', 'grad_ssm_c_im': 'grad_w', 'grad_ssm_d': 'grad_w', 'grad_ssm_w_glu': 'grad_w', 'grad_kv_norm': 'grad_w', 'grad_w_kvf': 'grad_w', 'grad_b_f': 'grad_w', 'grad_attn_wq': 'grad_w', 'grad_attn_wo': 'grad_w', 'grad_final_norm': 'grad_w', 'delta_mix_norm': 'delta_w', 'delta_mlp_norm': 'delta_w', 'delta_mlp_w1': 'delta_w', 'delta_mlp_w2': 'delta_w', 'delta_ssm_log_dt': 'delta_w', 'delta_ssm_a_re': 'delta_w', 'delta_ssm_a_im': 'delta_w', 'delta_ssm_b_re': 'delta_w', 'delta_ssm_b_im': 'delta_w', 'delta_ssm_c_re': 'delta_w', 'delta_ssm_c_im': 'delta_w', 'delta_ssm_d': 'delta_w', 'delta_ssm_w_glu': 'delta_w', 'delta_kv_norm': 'delta_w', 'delta_w_kvf': 'delta_w', 'delta_b_f': 'delta_w', 'delta_attn_wq': 'delta_w', 'delta_attn_wo': 'delta_w', 'delta_final_norm': 'delta_w', 'new_m_mix_norm': 'new_m', 'new_m_mlp_norm': 'new_m', 'new_m_mlp_w1': 'new_m', 'new_m_mlp_w2': 'new_m', 'new_m_ssm_log_dt': 'new_m', 'new_m_ssm_a_re': 'new_m', 'new_m_ssm_a_im': 'new_m', 'new_m_ssm_b_re': 'new_m', 'new_m_ssm_b_im': 'new_m', 'new_m_ssm_c_re': 'new_m', 'new_m_ssm_c_im': 'new_m', 'new_m_ssm_d': 'new_m', 'new_m_ssm_w_glu': 'new_m', 'new_m_kv_norm': 'new_m', 'new_m_w_kvf': 'new_m', 'new_m_b_f': 'new_m', 'new_m_attn_wq': 'new_m', 'new_m_attn_wo': 'new_m', 'new_m_final_norm': 'new_m', 'new_v_mix_norm': 'new_v', 'new_v_mlp_norm': 'new_v', 'new_v_mlp_w1': 'new_v', 'new_v_mlp_w2': 'new_v', 'new_v_ssm_log_dt': 'new_v', 'new_v_ssm_a_re': 'new_v', 'new_v_ssm_a_im': 'new_v', 'new_v_ssm_b_re': 'new_v', 'new_v_ssm_b_im': 'new_v', 'new_v_ssm_c_re': 'new_v', 'new_v_ssm_c_im': 'new_v', 'new_v_ssm_d': 'new_v', 'new_v_ssm_w_glu': 'new_v', 'new_v_kv_norm': 'new_v', 'new_v_w_kvf': 'new_v', 'new_v_b_f': 'new_v', 'new_v_attn_wq': 'new_v', 'new_v_attn_wo': 'new_v', 'new_v_final_norm': 'new_v'}


def _forward(args):
    return _fwd_reference(*[args[k] for k in FWD_PARAMS])


def _output_shape():
    def fwd():
        inp = _fwd_setup_inputs(0)
        return _fwd_reference(*[inp[k] for k in FWD_PARAMS])
    out = _jax.eval_shape(fwd)
    return out.shape, out.dtype

N_MICROBATCH = 1
ADAM_LR = 0.001
ADAM_B1 = 0.9
ADAM_B2 = 0.999
ADAM_EPS = 1e-08
ADAM_WD = 0.01
ADAM_STEP = 10
PER_EXAMPLE_BATCH_AXIS = {'x': 0, 'loss_target': 0}
SHARED_INPUTS = []
_WEIGHT_DTYPES = {'mix_norm': _jnp.float32, 'mlp_norm': _jnp.float32, 'mlp_w1': _jnp.float32, 'mlp_w2': _jnp.float32, 'ssm_log_dt': _jnp.float32, 'ssm_a_re': _jnp.float32, 'ssm_a_im': _jnp.float32, 'ssm_b_re': _jnp.float32, 'ssm_b_im': _jnp.float32, 'ssm_c_re': _jnp.float32, 'ssm_c_im': _jnp.float32, 'ssm_d': _jnp.float32, 'ssm_w_glu': _jnp.float32, 'kv_norm': _jnp.float32, 'w_kvf': _jnp.float32, 'b_f': _jnp.float32, 'attn_wq': _jnp.float32, 'attn_wo': _jnp.float32, 'final_norm': _jnp.float32}
MOMENT_SCALE = {'mix_norm': 2.626385e-01, 'mlp_norm': 1.825658e-01, 'mlp_w1': 9.428319e-02, 'mlp_w2': 8.254386e-01, 'ssm_log_dt': 3.200501e+00, 'ssm_a_re': 1.939843e-02, 'ssm_a_im': 1.112128e-02, 'ssm_b_re': 6.799248e-03, 'ssm_b_im': 7.183962e-03, 'ssm_c_re': 1.302519e-02, 'ssm_c_im': 1.325191e-02, 'ssm_d': 3.262036e-01, 'ssm_w_glu': 2.218674e-01, 'kv_norm': 5.357189e-01, 'w_kvf': 3.893044e-01, 'b_f': 8.186022e-01, 'attn_wq': 4.639434e-02, 'attn_wo': 3.895813e-01, 'final_norm': 6.583227e+01}


def _to_microbatches(a, axis):
    t = _jnp.moveaxis(a, axis, 0)
    t = t.reshape((N_MICROBATCH, t.shape[0] // N_MICROBATCH) + t.shape[1:])
    return _jnp.moveaxis(t, 1, axis + 1)


def setup_inputs(seed: int = 0) -> dict:
    inp = _fwd_setup_inputs(seed)
    key = _jax.random.fold_in(_jax.random.key(seed), 7919)
    shape, _ = _output_shape()
    out = dict(inp)
    out["loss_target"] = _jax.random.normal(_jax.random.fold_in(key, 0), shape, _jnp.float32)
    for i, name in enumerate(TWIN_WEIGHTS):
        w = inp[name].astype(_jnp.float32)
        if MOMENT_SCALE is None:
            s = _jnp.sqrt(_jnp.mean(_jnp.square(w)) + 1e-30)
        else:
            s = MOMENT_SCALE[name]
        km, kv = _jax.random.split(_jax.random.fold_in(key, i + 1))
        out[name] = w
        out["m_" + name] = s * _jax.random.normal(km, w.shape, _jnp.float32)
        out["v_" + name] = (s * s) * _jax.random.uniform(kv, w.shape, _jnp.float32, 0.5, 1.5)
    if N_MICROBATCH > 1:
        for name, axis in PER_EXAMPLE_BATCH_AXIS.items():
            out[name] = _to_microbatches(out[name], axis)
    return {'x': out['x'], 'mix_norm': out['mix_norm'], 'mlp_norm': out['mlp_norm'], 'mlp_w1': out['mlp_w1'], 'mlp_w2': out['mlp_w2'], 'ssm_log_dt': out['ssm_log_dt'], 'ssm_a_re': out['ssm_a_re'], 'ssm_a_im': out['ssm_a_im'], 'ssm_b_re': out['ssm_b_re'], 'ssm_b_im': out['ssm_b_im'], 'ssm_c_re': out['ssm_c_re'], 'ssm_c_im': out['ssm_c_im'], 'ssm_d': out['ssm_d'], 'ssm_w_glu': out['ssm_w_glu'], 'kv_norm': out['kv_norm'], 'w_kvf': out['w_kvf'], 'b_f': out['b_f'], 'attn_wq': out['attn_wq'], 'attn_wo': out['attn_wo'], 'final_norm': out['final_norm'], 'loss_target': out['loss_target'], 'm_mix_norm': out['m_mix_norm'], 'm_mlp_norm': out['m_mlp_norm'], 'm_mlp_w1': out['m_mlp_w1'], 'm_mlp_w2': out['m_mlp_w2'], 'm_ssm_log_dt': out['m_ssm_log_dt'], 'm_ssm_a_re': out['m_ssm_a_re'], 'm_ssm_a_im': out['m_ssm_a_im'], 'm_ssm_b_re': out['m_ssm_b_re'], 'm_ssm_b_im': out['m_ssm_b_im'], 'm_ssm_c_re': out['m_ssm_c_re'], 'm_ssm_c_im': out['m_ssm_c_im'], 'm_ssm_d': out['m_ssm_d'], 'm_ssm_w_glu': out['m_ssm_w_glu'], 'm_kv_norm': out['m_kv_norm'], 'm_w_kvf': out['m_w_kvf'], 'm_b_f': out['m_b_f'], 'm_attn_wq': out['m_attn_wq'], 'm_attn_wo': out['m_attn_wo'], 'm_final_norm': out['m_final_norm'], 'v_mix_norm': out['v_mix_norm'], 'v_mlp_norm': out['v_mlp_norm'], 'v_mlp_w1': out['v_mlp_w1'], 'v_mlp_w2': out['v_mlp_w2'], 'v_ssm_log_dt': out['v_ssm_log_dt'], 'v_ssm_a_re': out['v_ssm_a_re'], 'v_ssm_a_im': out['v_ssm_a_im'], 'v_ssm_b_re': out['v_ssm_b_re'], 'v_ssm_b_im': out['v_ssm_b_im'], 'v_ssm_c_re': out['v_ssm_c_re'], 'v_ssm_c_im': out['v_ssm_c_im'], 'v_ssm_d': out['v_ssm_d'], 'v_ssm_w_glu': out['v_ssm_w_glu'], 'v_kv_norm': out['v_kv_norm'], 'v_w_kvf': out['v_w_kvf'], 'v_b_f': out['v_b_f'], 'v_attn_wq': out['v_attn_wq'], 'v_attn_wo': out['v_attn_wo'], 'v_final_norm': out['v_final_norm']}


def _loss(weights, diff, rest, loss_target):
    with _jax.named_scope("forward"):
        args = {**rest, TWIN_DIFF_INPUT: diff, **{k: w.astype(_WEIGHT_DTYPES[k]) for k, w in weights.items()}}
        y = _forward(args)
    with _jax.named_scope("loss_head"):
        err = _jnp.square(y.astype(_jnp.float32) - loss_target)
        return 0.5 * _jnp.sum(_jnp.mean(err, axis=-1)) if err.ndim else 0.5 * err


def _adamw(w, g, m, v):
    m = ADAM_B1 * m + (1.0 - ADAM_B1) * g
    v = ADAM_B2 * v + (1.0 - ADAM_B2) * _jnp.square(g)
    m_hat = m / (1.0 - ADAM_B1 ** ADAM_STEP)
    v_hat = v / (1.0 - ADAM_B2 ** ADAM_STEP)
    delta = -ADAM_LR * (m_hat / (_jnp.sqrt(v_hat) + ADAM_EPS) + ADAM_WD * w)
    return delta, m, v


def reference(x, mix_norm, mlp_norm, mlp_w1, mlp_w2, ssm_log_dt, ssm_a_re, ssm_a_im, ssm_b_re, ssm_b_im, ssm_c_re, ssm_c_im, ssm_d, ssm_w_glu, kv_norm, w_kvf, b_f, attn_wq, attn_wo, final_norm, loss_target, m_mix_norm, m_mlp_norm, m_mlp_w1, m_mlp_w2, m_ssm_log_dt, m_ssm_a_re, m_ssm_a_im, m_ssm_b_re, m_ssm_b_im, m_ssm_c_re, m_ssm_c_im, m_ssm_d, m_ssm_w_glu, m_kv_norm, m_w_kvf, m_b_f, m_attn_wq, m_attn_wo, m_final_norm, v_mix_norm, v_mlp_norm, v_mlp_w1, v_mlp_w2, v_ssm_log_dt, v_ssm_a_re, v_ssm_a_im, v_ssm_b_re, v_ssm_b_im, v_ssm_c_re, v_ssm_c_im, v_ssm_d, v_ssm_w_glu, v_kv_norm, v_w_kvf, v_b_f, v_attn_wq, v_attn_wo, v_final_norm):
    given = dict(x=x, mix_norm=mix_norm, mlp_norm=mlp_norm, mlp_w1=mlp_w1, mlp_w2=mlp_w2, ssm_log_dt=ssm_log_dt, ssm_a_re=ssm_a_re, ssm_a_im=ssm_a_im, ssm_b_re=ssm_b_re, ssm_b_im=ssm_b_im, ssm_c_re=ssm_c_re, ssm_c_im=ssm_c_im, ssm_d=ssm_d, ssm_w_glu=ssm_w_glu, kv_norm=kv_norm, w_kvf=w_kvf, b_f=b_f, attn_wq=attn_wq, attn_wo=attn_wo, final_norm=final_norm, loss_target=loss_target, m_mix_norm=m_mix_norm, m_mlp_norm=m_mlp_norm, m_mlp_w1=m_mlp_w1, m_mlp_w2=m_mlp_w2, m_ssm_log_dt=m_ssm_log_dt, m_ssm_a_re=m_ssm_a_re, m_ssm_a_im=m_ssm_a_im, m_ssm_b_re=m_ssm_b_re, m_ssm_b_im=m_ssm_b_im, m_ssm_c_re=m_ssm_c_re, m_ssm_c_im=m_ssm_c_im, m_ssm_d=m_ssm_d, m_ssm_w_glu=m_ssm_w_glu, m_kv_norm=m_kv_norm, m_w_kvf=m_w_kvf, m_b_f=m_b_f, m_attn_wq=m_attn_wq, m_attn_wo=m_attn_wo, m_final_norm=m_final_norm, v_mix_norm=v_mix_norm, v_mlp_norm=v_mlp_norm, v_mlp_w1=v_mlp_w1, v_mlp_w2=v_mlp_w2, v_ssm_log_dt=v_ssm_log_dt, v_ssm_a_re=v_ssm_a_re, v_ssm_a_im=v_ssm_a_im, v_ssm_b_re=v_ssm_b_re, v_ssm_b_im=v_ssm_b_im, v_ssm_c_re=v_ssm_c_re, v_ssm_c_im=v_ssm_c_im, v_ssm_d=v_ssm_d, v_ssm_w_glu=v_ssm_w_glu, v_kv_norm=v_kv_norm, v_w_kvf=v_w_kvf, v_b_f=v_b_f, v_attn_wq=v_attn_wq, v_attn_wo=v_attn_wo, v_final_norm=v_final_norm)
    weights = {n: given[n] for n in TWIN_WEIGHTS}
    shared = {n: given[n] for n in SHARED_INPUTS}
    per_example = {n: given[n] for n in ['x']}
    grad_fn = _jax.value_and_grad(_loss, argnums=(0, 1))

    def one_microbatch(ex, loss_target):
        ex = dict(ex)
        diff = ex.pop(TWIN_DIFF_INPUT)
        return grad_fn(weights, diff, {**shared, **ex}, loss_target)

    if N_MICROBATCH == 1:
        loss, (grad_w, grad_x) = one_microbatch(per_example, given["loss_target"])
    else:
        def body(carry, xs):
            loss_sum, grad_sum = carry
            l_k, (gw_k, gx_k) = one_microbatch(xs[0], xs[1])
            with _jax.named_scope("update"):
                return (loss_sum + l_k, _jax.tree.map(_jnp.add, grad_sum, gw_k)), gx_k

        init = (_jnp.zeros((), _jnp.float32), _jax.tree.map(_jnp.zeros_like, weights))
        (loss, grad_w), grad_x = _jax.lax.scan(body, init, (per_example, given["loss_target"]))
    with _jax.named_scope("update"):
        delta_w, new_m, new_v = {}, {}, {}
        for n in TWIN_WEIGHTS:
            delta_w[n], new_m[n], new_v[n] = _adamw(weights[n], grad_w[n], given["m_" + n], given["v_" + n])
    return (loss, grad_x, *[grad_w[n] for n in TWIN_WEIGHTS], *[delta_w[n] for n in TWIN_WEIGHTS],
            *[new_m[n] for n in TWIN_WEIGHTS], *[new_v[n] for n in TWIN_WEIGHTS])
```

```python
import functools

import jax
import jax.numpy as jnp
from jax import lax
from jax.experimental import pallas as pl
from jax.experimental.pallas import tpu as pltpu

F32 = jnp.float32
BF16 = jnp.bfloat16
HIGHEST = lax.Precision.HIGHEST

V7X_VMEM_BYTES = 64 << 20
VMEM_LIMIT_BYTES = (V7X_VMEM_BYTES * 3) // 4
LANES = 128
SUBLANES = 8

N_DEV = 8
RMS_EPS = 1e-6
SSM_GROUP = 16
SSM_STATE = 64
HEAD_DIM = 64
GROUPS_PER_BLOCK = LANES // SSM_GROUP
BLOCK_STATE = GROUPS_PER_BLOCK * SSM_STATE
BLOCK_COLS = 2 * BLOCK_STATE
NEG_BIG = -1e30

ADAM_LR = 0.001
ADAM_B1 = 0.9
ADAM_B2 = 0.999
ADAM_EPS = 1e-08
ADAM_WD = 0.01
ADAM_STEP = 10

TILES = {"row": 512, "mm": (1024, 512, 1024), "blk": 512, "scan": 256, "cum": 256, "attn": 512, "adam": 256}


def _pick(dim, pref, align=LANES):
    if dim <= pref:
        return dim
    for a in (align, SUBLANES):
        d = (pref // a) * a
        while d >= a:
            if dim % d == 0:
                return d
            d -= a
    return dim


def _params(*sem):
    return pltpu.CompilerParams(dimension_semantics=sem, vmem_limit_bytes=VMEM_LIMIT_BYTES)


def rowcall(name, fn, rows, consts, out_rows, out_accs=(), tm=None):
    n_rows = rows[0].shape[0]
    tm = _pick(n_rows, tm or TILES["row"], SUBLANES)
    nr, nc, no, na = len(rows), len(consts), len(out_rows), len(out_accs)

    def body(*refs):
        ins = [r[...] for r in refs[: nr + nc]]
        outs = fn(*ins)
        if not isinstance(outs, (tuple, list)):
            outs = (outs,)
        for r, o in zip(refs[nr + nc: nr + nc + no], outs[:no]):
            r[...] = o.astype(r.dtype)
        if na:
            i = pl.program_id(0)
            for r, o in zip(refs[nr + nc + no:], outs[no:]):
                @pl.when(i == 0)
                def _(r=r, o=o):
                    r[...] = o

                @pl.when(i > 0)
                def _(r=r, o=o):
                    r[...] += o

    in_specs = [pl.BlockSpec((tm, a.shape[1]), lambda i: (i, 0)) for a in rows]
    in_specs += [pl.BlockSpec(c.shape, lambda i, n=c.ndim: (0,) * n) for c in consts]
    out_shape = [jax.ShapeDtypeStruct((n_rows, c), dt) for c, dt in out_rows]
    out_specs = [pl.BlockSpec((tm, c), lambda i: (i, 0)) for c, _ in out_rows]
    out_shape += [jax.ShapeDtypeStruct(s, F32) for s in out_accs]
    out_specs += [pl.BlockSpec(s, lambda i, n=len(s): (0,) * n) for s in out_accs]
    res = pl.pallas_call(
        body, name=name, grid=(n_rows // tm,), in_specs=in_specs, out_specs=out_specs, out_shape=out_shape,
        compiler_params=_params("arbitrary" if na else "parallel"),
    )(*rows, *consts)
    return res


_DIMS = {"nn": (((1,), (0,)), ((), ())), "nt": (((1,), (1,)), ((), ())), "tn": (((0,), (0,)), ((), ()))}


def matmul(name, a, b, mode="nn", *, a_fn=None, scale=None, resid=None, post=None, post_arg=None,
           bm=None, bn=None, bk=None):
    if mode == "nn":
        (m, k), (k2, n) = a.shape, b.shape
    elif mode == "nt":
        (m, k), (n, k2) = a.shape, b.shape
    else:
        (k, m), (k2, n) = a.shape, b.shape
    assert k == k2, (name, a.shape, b.shape, mode)
    bm, bn, bk = _pick(m, bm or TILES["mm"][0]), _pick(n, bn or TILES["mm"][1]), _pick(k, bk or TILES["mm"][2])
    nk = k // bk
    a_spec = pl.BlockSpec((bk, bm), lambda i, j, kk: (kk, i)) if mode == "tn" else pl.BlockSpec((bm, bk), lambda i, j, kk: (i, kk))
    b_spec = pl.BlockSpec((bn, bk), lambda i, j, kk: (j, kk)) if mode == "nt" else pl.BlockSpec((bk, bn), lambda i, j, kk: (kk, j))
    mn_spec = pl.BlockSpec((bm, bn), lambda i, j, kk: (i, j))
    extra = [x for x in (resid, post_arg) if x is not None]
    has_resid, has_post = resid is not None, post is not None

    def body(*refs):
        a_ref, b_ref = refs[0], refs[1]
        ex = refs[2: 2 + len(extra)]
        o_ref = refs[2 + len(extra)]
        av = a_ref[...]
        if a_fn is not None:
            av = a_fn(av.astype(F32))
        p = lax.dot_general(av.astype(BF16), b_ref[...].astype(BF16), _DIMS[mode], preferred_element_type=F32)

        def finish(acc):
            if scale is not None:
                acc = acc * scale
            idx = 0
            if has_resid:
                acc = acc + ex[idx][...]
                idx += 1
            if has_post:
                acc = post(acc, ex[idx][...])
            o_ref[...] = acc.astype(o_ref.dtype)

        if nk == 1:
            finish(p)
        else:
            acc_ref = refs[-1]
            kk = pl.program_id(2)

            @pl.when(kk == 0)
            def _():
                acc_ref[...] = p

            @pl.when(kk > 0)
            def _():
                acc_ref[...] += p

            @pl.when(kk == nk - 1)
            def _():
                finish(acc_ref[...])

    return pl.pallas_call(
        body, name=name, grid=(m // bm, n // bn, nk),
        in_specs=[a_spec, b_spec] + [mn_spec] * len(extra), out_specs=mn_spec,
        out_shape=jax.ShapeDtypeStruct((m, n), F32),
        scratch_shapes=[pltpu.VMEM((bm, bn), F32)] if nk > 1 else [],
        compiler_params=_params("parallel", "parallel", "arbitrary"),
    )(a, b, *extra)


def bmm(name, a, w, mode, tm=None):
    n_rows = a.shape[0]
    nb, ka, kb = w.shape
    ca, co = (ka, kb) if mode == "nn" else (kb, ka)
    assert a.shape[1] == nb * ca
    tm = _pick(n_rows, tm or TILES["blk"], SUBLANES)

    def body(a_ref, w_ref, o_ref):
        o_ref[...] = lax.dot_general(a_ref[...], w_ref[...], _DIMS[mode], precision=HIGHEST, preferred_element_type=F32)

    return pl.pallas_call(
        body, name=name, grid=(n_rows // tm, nb),
        in_specs=[pl.BlockSpec((tm, ca), lambda i, b: (i, b)), pl.BlockSpec((None, ka, kb), lambda i, b: (b, 0, 0))],
        out_specs=pl.BlockSpec((tm, co), lambda i, b: (i, b)),
        out_shape=jax.ShapeDtypeStruct((n_rows, nb * co), F32),
        compiler_params=_params("parallel", "parallel"),
    )(a, w)


def bmm_tn(name, a, b, nb, tk=None):
    n_rows = a.shape[0]
    ka, kb = a.shape[1] // nb, b.shape[1] // nb
    tk = _pick(n_rows, tk or TILES["blk"], SUBLANES)

    def body(a_ref, b_ref, o_ref):
        p = lax.dot_general(a_ref[...], b_ref[...], _DIMS["tn"], precision=HIGHEST, preferred_element_type=F32)
        kk = pl.program_id(1)

        @pl.when(kk == 0)
        def _():
            o_ref[...] = p

        @pl.when(kk > 0)
        def _():
            o_ref[...] += p

    return pl.pallas_call(
        body, name=name, grid=(nb, n_rows // tk),
        in_specs=[pl.BlockSpec((tk, ka), lambda bb, kk: (kk, bb)), pl.BlockSpec((tk, kb), lambda bb, kk: (kk, bb))],
        out_specs=pl.BlockSpec((None, ka, kb), lambda bb, kk: (bb, 0, 0)),
        out_shape=jax.ShapeDtypeStruct((nb, ka, kb), F32),
        compiler_params=_params("parallel", "arbitrary"),
    )(a, b)


def _cmul(ar, ai, br, bi):
    return ar * br - ai * bi, ar * bi + ai * br


def _scan_tables(lam_ref, reverse):
    shape = (SUBLANES, BLOCK_STATE)
    lr = jnp.broadcast_to(lam_ref[0:1, :], shape)
    li = jnp.broadcast_to(lam_ref[1:2, :], shape)
    if reverse:
        li = -li
    row = lax.broadcasted_iota(jnp.int32, shape, 0)
    tt = (SUBLANES - 1 - row) if reverse else row
    l1 = (lr, li)
    l2 = _cmul(*l1, *l1)
    l4 = _cmul(*l2, *l2)
    pr, pi = l1
    for bit, lp in enumerate((l1, l2, l4)):
        qr, qi = _cmul(pr, pi, *lp)
        on = ((tt >> bit) & 1) == 1
        pr, pi = jnp.where(on, qr, pr), jnp.where(on, qi, pi)
    steps = []
    for d, lp in ((1, l1), (2, l2), (4, l4)):
        ok = tt >= d
        steps.append((d, jnp.where(ok, lp[0], 0.0), jnp.where(ok, lp[1], 0.0)))
    return steps, (pr, pi)


def s5_scan(name, x, lam, reverse, tm=None):
    n_rows = x.shape[0]
    nb = lam.shape[0]
    tm = _pick(n_rows, tm or TILES["scan"], SUBLANES)
    nt = n_rows // tm
    ng = tm // SUBLANES
    last = 0 if reverse else SUBLANES - 1

    def body(x_ref, lam_ref, o_ref, carry_ref):
        it = pl.program_id(1)
        steps, (pr, pi) = _scan_tables(lam_ref, reverse)

        @pl.when(it == 0)
        def _():
            carry_ref[...] = jnp.zeros_like(carry_ref)

        def group(r, carry):
            cr, ci = carry
            rr = (ng - 1 - r) if reverse else r
            off = pl.multiple_of(rr * SUBLANES, SUBLANES)
            xr = x_ref[pl.ds(off, SUBLANES), 0:BLOCK_STATE]
            xi = x_ref[pl.ds(off, SUBLANES), BLOCK_STATE:BLOCK_COLS]
            for d, mr, mi in steps:
                sh = (SUBLANES - d) if reverse else d
                yr, yi = pltpu.roll(xr, sh, 0), pltpu.roll(xi, sh, 0)
                xr, xi = xr + mr * yr - mi * yi, xi + mr * yi + mi * yr
            xr, xi = xr + pr * cr - pi * ci, xi + pr * ci + pi * cr
            o_ref[pl.ds(off, SUBLANES), 0:BLOCK_STATE] = xr
            o_ref[pl.ds(off, SUBLANES), BLOCK_STATE:BLOCK_COLS] = xi
            shape = (SUBLANES, BLOCK_STATE)
            return jnp.broadcast_to(xr[last:last + 1, :], shape), jnp.broadcast_to(xi[last:last + 1, :], shape)

        cr, ci = lax.fori_loop(0, ng, group, (carry_ref[0], carry_ref[1]))
        carry_ref[0] = cr
        carry_ref[1] = ci

    tile = (lambda b, it: (nt - 1 - it, b)) if reverse else (lambda b, it: (it, b))
    return pl.pallas_call(
        body, name=name, grid=(nb, nt),
        in_specs=[pl.BlockSpec((tm, BLOCK_COLS), tile), pl.BlockSpec((None, 2, BLOCK_STATE), lambda b, it: (b, 0, 0))],
        out_specs=pl.BlockSpec((tm, BLOCK_COLS), tile),
        out_shape=jax.ShapeDtypeStruct(x.shape, F32),
        scratch_shapes=[pltpu.VMEM((2, SUBLANES, BLOCK_STATE), F32)],
        compiler_params=_params("parallel", "arbitrary"),
    )(x, lam)


def s5_dlam(name, g, s, nb, tm=None):
    n_rows = g.shape[0]
    tm = _pick(n_rows, tm or TILES["scan"], SUBLANES)
    nt = n_rows // tm
    ng = tm // SUBLANES
    shape = (SUBLANES, BLOCK_STATE)

    def body(g_ref, s_ref, o_ref, carry_ref, acc_ref):
        it = pl.program_id(1)

        @pl.when(it == 0)
        def _():
            carry_ref[...] = jnp.zeros_like(carry_ref)
            acc_ref[...] = jnp.zeros_like(acc_ref)

        first = lax.broadcasted_iota(jnp.int32, shape, 0) == 0

        def group(r, carry):
            cr, ci, ar, ai = carry
            off = pl.multiple_of(r * SUBLANES, SUBLANES)
            sr = s_ref[pl.ds(off, SUBLANES), 0:BLOCK_STATE]
            si = s_ref[pl.ds(off, SUBLANES), BLOCK_STATE:BLOCK_COLS]
            gr = g_ref[pl.ds(off, SUBLANES), 0:BLOCK_STATE]
            gi = g_ref[pl.ds(off, SUBLANES), BLOCK_STATE:BLOCK_COLS]
            pr = jnp.where(first, cr, pltpu.roll(sr, 1, 0))
            pi = jnp.where(first, ci, pltpu.roll(si, 1, 0))
            ar = ar + gr * pr + gi * pi
            ai = ai + gi * pr - gr * pi
            return (jnp.broadcast_to(sr[SUBLANES - 1:SUBLANES, :], shape),
                    jnp.broadcast_to(si[SUBLANES - 1:SUBLANES, :], shape), ar, ai)

        cr, ci, ar, ai = lax.fori_loop(0, ng, group, (carry_ref[0], carry_ref[1], acc_ref[0], acc_ref[1]))
        carry_ref[0] = cr
        carry_ref[1] = ci
        acc_ref[0] = ar
        acc_ref[1] = ai

        @pl.when(it == nt - 1)
        def _():
            o_ref[0:1, :] = jnp.sum(ar, axis=0, keepdims=True)
            o_ref[1:2, :] = jnp.sum(ai, axis=0, keepdims=True)

    tile = lambda b, it: (it, b)
    return pl.pallas_call(
        body, name=name, grid=(nb, nt),
        in_specs=[pl.BlockSpec((tm, BLOCK_COLS), tile), pl.BlockSpec((tm, BLOCK_COLS), tile)],
        out_specs=pl.BlockSpec((None, 2, BLOCK_STATE), lambda b, it: (b, 0, 0)),
        out_shape=jax.ShapeDtypeStruct((nb, 2, BLOCK_STATE), F32),
        scratch_shapes=[pltpu.VMEM((2,) + shape, F32), pltpu.VMEM((2,) + shape, F32)],
        compiler_params=_params("parallel", "arbitrary"),
    )(g, s)


def _rms(x, g):
    return x * lax.rsqrt(jnp.mean(x * x, axis=-1, keepdims=True) + RMS_EPS) * g


def _sigmoid(x):
    return 1.0 / (1.0 + jnp.exp(-x))


def _gelu(x):
    return 0.5 * x * (1.0 + jnp.tanh(0.7978845608028654 * (x + 0.044715 * (x * x * x))))


def _log_sigmoid(x):
    return jnp.minimum(x, 0.0) - jnp.log(1.0 + jnp.exp(-jnp.abs(x)))


def _sqrelu(x):
    r = jnp.maximum(x, 0.0)
    return r * r


def _s5_discretise(ldt, ar, ai, br, bi):
    dt = jnp.exp(ldt)
    er = jnp.exp(ar * dt)
    lr, li = er * jnp.cos(ai * dt), er * jnp.sin(ai * dt)
    nr, ni = lr - 1.0, li
    den = ar * ar + ai * ai
    cr, ci = (nr * ar + ni * ai) / den, (ni * ar - nr * ai) / den
    return lr, li, cr * br - ci * bi, cr * bi + ci * br


def rms_fwd(name, x, g):
    return rowcall(name, _rms, [x], [g], [(x.shape[1], F32)])[0]


def rms_bwd(name, x, g, dy, add=None):
    def fn(x, dy, *rest):
        g = rest[-1]
        _, vjp = jax.vjp(_rms, x, g)
        dx, dg = vjp(dy)
        if add is not None:
            dx = dx + rest[0]
        return dx, dg

    rows = [x, dy] + ([add] if add is not None else [])
    return rowcall(name, fn, rows, [g], [(x.shape[1], F32)], [g.shape])


def cum_logf(name, fl, bf, tm=None):
    n_rows, w = fl.shape
    tm = _pick(n_rows, tm or TILES["cum"], SUBLANES)

    def body(fl_ref, bf_ref, o_ref, carry_ref):
        it = pl.program_id(0)

        @pl.when(it == 0)
        def _():
            carry_ref[...] = jnp.zeros_like(carry_ref)

        ls = _log_sigmoid(fl_ref[...] + bf_ref[...])
        tri = (lax.broadcasted_iota(jnp.int32, (tm, tm), 0) >= lax.broadcasted_iota(jnp.int32, (tm, tm), 1)).astype(F32)
        c = jnp.dot(tri, ls, precision=HIGHEST, preferred_element_type=F32) + carry_ref[0:1, :]
        o_ref[...] = c
        carry_ref[...] = jnp.broadcast_to(c[tm - 1:tm, :], carry_ref.shape)

    return pl.pallas_call(
        body, name=name, grid=(n_rows // tm,),
        in_specs=[pl.BlockSpec((tm, w), lambda i: (i, 0)), pl.BlockSpec((1, w), lambda i: (0, 0))],
        out_specs=pl.BlockSpec((tm, w), lambda i: (i, 0)),
        out_shape=jax.ShapeDtypeStruct((n_rows, w), F32),
        scratch_shapes=[pltpu.VMEM((SUBLANES, w), F32)],
        compiler_params=_params("arbitrary"),
    )(fl, bf)


def cum_logf_bwd(name, fl, bf, plus, minus, tm=None):
    n_rows, w = fl.shape
    tm = _pick(n_rows, tm or TILES["cum"], SUBLANES)
    nt = n_rows // tm
    n_p, n_m = len(plus), len(minus)

    def body(*refs):
        fl_ref, bf_ref = refs[0], refs[1]
        d_refs = refs[2: 2 + n_p + n_m]
        o_ref, db_ref, carry_ref = refs[2 + n_p + n_m:]
        it = pl.program_id(0)

        @pl.when(it == 0)
        def _():
            carry_ref[...] = jnp.zeros_like(carry_ref)

        d = d_refs[0][...]
        for r in d_refs[1:n_p]:
            d = d + r[...]
        for r in d_refs[n_p:]:
            d = d - r[...]
        tri = (lax.broadcasted_iota(jnp.int32, (tm, tm), 0) <= lax.broadcasted_iota(jnp.int32, (tm, tm), 1)).astype(F32)
        c = jnp.dot(tri, d, precision=HIGHEST, preferred_element_type=F32) + carry_ref[0:1, :]
        carry_ref[...] = jnp.broadcast_to(c[0:1, :], carry_ref.shape)
        dfl = c * _sigmoid(-(fl_ref[...] + bf_ref[...]))
        o_ref[...] = dfl
        part = jnp.sum(dfl, axis=0, keepdims=True)

        @pl.when(it == 0)
        def _():
            db_ref[...] = part

        @pl.when(it > 0)
        def _():
            db_ref[...] += part

    rev = lambda i: (nt - 1 - i, 0)
    return pl.pallas_call(
        body, name=name, grid=(nt,),
        in_specs=[pl.BlockSpec((tm, w), rev), pl.BlockSpec((1, w), lambda i: (0, 0))] + [pl.BlockSpec((tm, w), rev)] * (n_p + n_m),
        out_specs=[pl.BlockSpec((tm, w), rev), pl.BlockSpec((1, w), lambda i: (0, 0))],
        out_shape=[jax.ShapeDtypeStruct((n_rows, w), F32), jax.ShapeDtypeStruct((1, w), F32)],
        scratch_shapes=[pltpu.VMEM((SUBLANES, w), F32)],
        compiler_params=_params("arbitrary"),
    )(fl, bf, *plus, *minus)


def _fox_scores(q_ref, k_ref, fq_ref, fk_ref, i, j, tq, tk):
    s = lax.dot_general(q_ref[...].astype(BF16), k_ref[...].astype(BF16), _DIMS["nt"], preferred_element_type=F32)
    s = s + fq_ref[...] - fk_ref[...]
    row = i * tq + lax.broadcasted_iota(jnp.int32, (tq, tk), 0)
    col = j * tk + lax.broadcasted_iota(jnp.int32, (tq, tk), 1)
    return jnp.where(row >= col, s, NEG_BIG)


def fox_fwd(name, q, k, v, fq, fk, t=None):
    nh, n_rows, dh = q.shape
    t = _pick(n_rows, t or TILES["attn"])
    nt = n_rows // t

    def body(q_ref, k_ref, v_ref, fq_ref, fk_ref, o_ref, lse_ref, m_ref, l_ref, acc_ref):
        i, j = pl.program_id(1), pl.program_id(2)

        @pl.when(j == 0)
        def _():
            m_ref[...] = jnp.full_like(m_ref, NEG_BIG)
            l_ref[...] = jnp.zeros_like(l_ref)
            acc_ref[...] = jnp.zeros_like(acc_ref)

        @pl.when(j <= i)
        def _():
            s = _fox_scores(q_ref, k_ref, fq_ref, fk_ref, i, j, t, t)
            m_new = jnp.maximum(m_ref[...], jnp.max(s, axis=-1, keepdims=True))
            alpha = jnp.exp(m_ref[...] - m_new)
            p = jnp.exp(s - m_new)
            l_ref[...] = alpha * l_ref[...] + jnp.sum(p, axis=-1, keepdims=True)
            acc_ref[...] = alpha * acc_ref[...] + jnp.dot(p.astype(BF16), v_ref[...].astype(BF16), preferred_element_type=F32)
            m_ref[...] = m_new

        @pl.when(j == i)
        def _():
            o_ref[...] = acc_ref[...] / l_ref[...]
            lse_ref[...] = m_ref[...] + jnp.log(l_ref[...])

    qmap = lambda h, i, j: (h, i, 0)
    kmap = lambda h, i, j: (h, jnp.minimum(j, i), 0)
    return pl.pallas_call(
        body, name=name, grid=(nh, nt, nt),
        in_specs=[pl.BlockSpec((None, t, dh), qmap), pl.BlockSpec((None, t, dh), kmap), pl.BlockSpec((None, t, dh), kmap),
                  pl.BlockSpec((None, t, 1), qmap), pl.BlockSpec((None, 1, t), lambda h, i, j: (h, 0, jnp.minimum(j, i)))],
        out_specs=[pl.BlockSpec((None, t, dh), qmap), pl.BlockSpec((None, t, 1), qmap)],
        out_shape=[jax.ShapeDtypeStruct((nh, n_rows, dh), F32), jax.ShapeDtypeStruct((nh, n_rows, 1), F32)],
        scratch_shapes=[pltpu.VMEM((t, 1), F32), pltpu.VMEM((t, 1), F32), pltpu.VMEM((t, dh), F32)],
        compiler_params=_params("parallel", "parallel", "arbitrary"),
    )(q, k, v, fq, fk)


def fox_bwd_q(name, q, k, v, fq, fk, do, lse, delta, scale, t=None):
    nh, n_rows, dh = q.shape
    t = _pick(n_rows, t or TILES["attn"])
    nt = n_rows // t

    def body(q_ref, k_ref, v_ref, fq_ref, fk_ref, do_ref, lse_ref, dl_ref, dq_ref, dfq_ref, acc_ref, f_ref):
        i, j = pl.program_id(1), pl.program_id(2)

        @pl.when(j == 0)
        def _():
            acc_ref[...] = jnp.zeros_like(acc_ref)
            f_ref[...] = jnp.zeros_like(f_ref)

        @pl.when(j <= i)
        def _():
            s = _fox_scores(q_ref, k_ref, fq_ref, fk_ref, i, j, t, t)
            p = jnp.exp(s - lse_ref[...])
            dp = lax.dot_general(do_ref[...].astype(BF16), v_ref[...].astype(BF16), _DIMS["nt"], preferred_element_type=F32)
            ds = p * (dp - dl_ref[...])
            acc_ref[...] += jnp.dot(ds.astype(BF16), k_ref[...].astype(BF16), preferred_element_type=F32)
            f_ref[...] += jnp.sum(ds, axis=-1, keepdims=True)

        @pl.when(j == i)
        def _():
            dq_ref[...] = acc_ref[...] * scale
            dfq_ref[...] = f_ref[...]

    qmap = lambda h, i, j: (h, i, 0)
    kmap = lambda h, i, j: (h, jnp.minimum(j, i), 0)
    return pl.pallas_call(
        body, name=name, grid=(nh, nt, nt),
        in_specs=[pl.BlockSpec((None, t, dh), qmap), pl.BlockSpec((None, t, dh), kmap), pl.BlockSpec((None, t, dh), kmap),
                  pl.BlockSpec((None, t, 1), qmap), pl.BlockSpec((None, 1, t), lambda h, i, j: (h, 0, jnp.minimum(j, i))),
                  pl.BlockSpec((None, t, dh), qmap), pl.BlockSpec((None, t, 1), qmap), pl.BlockSpec((None, t, 1), qmap)],
        out_specs=[pl.BlockSpec((None, t, dh), qmap), pl.BlockSpec((None, t, 1), qmap)],
        out_shape=[jax.ShapeDtypeStruct((nh, n_rows, dh), F32), jax.ShapeDtypeStruct((nh, n_rows, 1), F32)],
        scratch_shapes=[pltpu.VMEM((t, dh), F32), pltpu.VMEM((t, 1), F32)],
        compiler_params=_params("parallel", "parallel", "arbitrary"),
    )(q, k, v, fq, fk, do, lse, delta)


def fox_bwd_kv(name, q, k, v, fq, fk, do, lse, delta, t=None):
    nh, n_rows, dh = q.shape
    t = _pick(n_rows, t or TILES["attn"])
    nt = n_rows // t

    def body(q_ref, k_ref, v_ref, fq_ref, fk_ref, do_ref, lse_ref, dl_ref, dk_ref, dv_ref, dfk_ref, ak_ref, av_ref, f_ref):
        j, i = pl.program_id(1), pl.program_id(2)

        @pl.when(i == 0)
        def _():
            ak_ref[...] = jnp.zeros_like(ak_ref)
            av_ref[...] = jnp.zeros_like(av_ref)
            f_ref[...] = jnp.zeros_like(f_ref)

        @pl.when(i >= j)
        def _():
            s = _fox_scores(q_ref, k_ref, fq_ref, fk_ref, i, j, t, t)
            p = jnp.exp(s - lse_ref[...])
            dob = do_ref[...].astype(BF16)
            dp = lax.dot_general(dob, v_ref[...].astype(BF16), _DIMS["nt"], preferred_element_type=F32)
            ds = p * (dp - dl_ref[...])
            av_ref[...] += lax.dot_general(p.astype(BF16), dob, _DIMS["tn"], preferred_element_type=F32)
            ak_ref[...] += lax.dot_general(ds.astype(BF16), q_ref[...].astype(BF16), _DIMS["tn"], preferred_element_type=F32)
            f_ref[...] += jnp.sum(ds, axis=0, keepdims=True)

        @pl.when(i == nt - 1)
        def _():
            dk_ref[...] = ak_ref[...]
            dv_ref[...] = av_ref[...]
            dfk_ref[...] = f_ref[...]

    qmap = lambda h, j, i: (h, jnp.maximum(i, j), 0)
    kmap = lambda h, j, i: (h, j, 0)
    return pl.pallas_call(
        body, name=name, grid=(nh, nt, nt),
        in_specs=[pl.BlockSpec((None, t, dh), qmap), pl.BlockSpec((None, t, dh), kmap), pl.BlockSpec((None, t, dh), kmap),
                  pl.BlockSpec((None, t, 1), qmap), pl.BlockSpec((None, 1, t), lambda h, j, i: (h, 0, j)),
                  pl.BlockSpec((None, t, dh), qmap), pl.BlockSpec((None, t, 1), qmap), pl.BlockSpec((None, t, 1), qmap)],
        out_specs=[pl.BlockSpec((None, t, dh), kmap), pl.BlockSpec((None, t, dh), kmap), pl.BlockSpec((None, 1, t), lambda h, j, i: (h, 0, j))],
        out_shape=[jax.ShapeDtypeStruct((nh, n_rows, dh), F32), jax.ShapeDtypeStruct((nh, n_rows, dh), F32),
                   jax.ShapeDtypeStruct((nh, 1, n_rows), F32)],
        scratch_shapes=[pltpu.VMEM((t, dh), F32), pltpu.VMEM((t, dh), F32), pltpu.VMEM((1, t), F32)],
        compiler_params=_params("parallel", "parallel", "arbitrary"),
    )(q, k, v, fq, fk, do, lse, delta)


_PEER_FLIPS = [(bx, by, bc) for bx in (0, 1) for by in (0, 1) for bc in (0, 1)][1:]


def _exchange(name, tensors, scatter):
    n = len(tensors)
    n_peer = len(_PEER_FLIPS)

    def body(*refs):
        ins, outs = refs[:n], refs[n: 2 * n]
        send_sems, recv_sems, local_sems = refs[2 * n:]
        x, y, c = lax.axis_index("x"), lax.axis_index("y"), lax.axis_index("c")
        me = 4 * x + 2 * y + c
        copies = []
        for t in range(n):
            src_me = ins[t].at[me] if scatter else ins[t]
            local = pltpu.make_async_copy(src_me, outs[t].at[me], local_sems.at[t])
            local.start()
            copies.append(local)
            for kk, (bx, by, bc) in enumerate(_PEER_FLIPS):
                px, py, pc = (1 - x if bx else x), (1 - y if by else y), (1 - c if bc else c)
                peer = 4 * px + 2 * py + pc
                out_cp = pltpu.make_async_remote_copy(
                    src_ref=ins[t].at[peer] if scatter else ins[t], dst_ref=outs[t].at[me],
                    send_sem=send_sems.at[t, kk], recv_sem=recv_sems.at[t, kk],
                    device_id=(px, py, pc), device_id_type=pl.DeviceIdType.MESH)
                out_cp.start()
                copies.append(pltpu.make_async_remote_copy(
                    src_ref=ins[t].at[peer] if scatter else ins[t], dst_ref=outs[t].at[peer],
                    send_sem=send_sems.at[t, kk], recv_sem=recv_sems.at[t, kk],
                    device_id=(px, py, pc), device_id_type=pl.DeviceIdType.MESH))
        for cp in copies:
            cp.wait()

    any_spec = pl.BlockSpec(memory_space=pl.ANY)
    out_shape = [jax.ShapeDtypeStruct(t.shape if scatter else (N_DEV,) + t.shape, t.dtype) for t in tensors]
    return pl.pallas_call(
        body, name=name, in_specs=[any_spec] * n, out_specs=[any_spec] * n, out_shape=out_shape,
        scratch_shapes=[pltpu.SemaphoreType.DMA((n, n_peer)), pltpu.SemaphoreType.DMA((n, n_peer)), pltpu.SemaphoreType.DMA((n,))],
        compiler_params=pltpu.CompilerParams(has_side_effects=True),
    )(*tensors)


def all_gather(name, tensors):
    return _exchange(name, tensors, scatter=False)


def all_to_all(name, tensors):
    return _exchange(name, tensors, scatter=True)


def adamw(name, parts, w, m, v, tr=None):
    n_rows, n_cols = w.shape
    tr = _pick(n_rows, tr or TILES["adam"], SUBLANES)
    c1 = 1.0 - ADAM_B1 ** ADAM_STEP
    c2 = 1.0 - ADAM_B2 ** ADAM_STEP

    def body(p_ref, w_ref, m_ref, v_ref, g_ref, d_ref, mo_ref, vo_ref):
        g = p_ref[0]
        for s in range(1, N_DEV):
            g = g + p_ref[s]
        mn = ADAM_B1 * m_ref[...] + (1.0 - ADAM_B1) * g
        vn = ADAM_B2 * v_ref[...] + (1.0 - ADAM_B2) * (g * g)
        g_ref[...] = g
        mo_ref[...] = mn
        vo_ref[...] = vn
        d_ref[...] = -ADAM_LR * ((mn / c1) / (jnp.sqrt(vn / c2) + ADAM_EPS) + ADAM_WD * w_ref[...])

    spec = pl.BlockSpec((tr, n_cols), lambda i: (i, 0))
    return pl.pallas_call(
        body, name=name, grid=(n_rows // tr,),
        in_specs=[pl.BlockSpec((N_DEV, tr, n_cols), lambda i: (0, i, 0)), spec, spec, spec],
        out_specs=[spec] * 4, out_shape=[jax.ShapeDtypeStruct(w.shape, F32)] * 4,
        compiler_params=_params("parallel"),
    )(parts, w, m, v)


def _to_heads(x):
    n_rows, d = x.shape
    return x.reshape(n_rows, d // HEAD_DIM, HEAD_DIM).transpose(1, 0, 2)


def _from_heads(x):
    nh, n_rows, dh = x.shape
    return x.transpose(1, 0, 2).reshape(n_rows, nh * dh)


def _eye_mask():
    return jnp.eye(GROUPS_PER_BLOCK, dtype=F32)


def _b_blocks(bbr, bbi):
    nb = bbr.shape[0] // (GROUPS_PER_BLOCK * SSM_STATE)
    eye = _eye_mask()[None, :, None, :, None]

    def one(z):
        z = z.reshape(nb, GROUPS_PER_BLOCK, SSM_STATE, SSM_GROUP).transpose(0, 1, 3, 2)
        return z[:, :, :, None, :] * eye

    w = jnp.stack([one(bbr), one(bbi)], axis=3)
    return w.reshape(nb, LANES, BLOCK_COLS)


def _b_blocks_t(dw):
    nb = dw.shape[0]
    d6 = dw.reshape(nb, GROUPS_PER_BLOCK, SSM_GROUP, 2, GROUPS_PER_BLOCK, SSM_STATE)
    diag = jnp.sum(d6 * _eye_mask()[None, :, None, None, :, None], axis=4)
    diag = diag.transpose(3, 0, 1, 4, 2).reshape(2, nb * GROUPS_PER_BLOCK * SSM_STATE, SSM_GROUP)
    return diag[0], diag[1]


def _c_blocks(c_re, c_im):
    nb = c_re.shape[0] // GROUPS_PER_BLOCK
    eye = _eye_mask()[None, :, None, :, None]

    def one(z):
        z = z.reshape(nb, GROUPS_PER_BLOCK, SSM_GROUP, SSM_STATE).transpose(0, 1, 3, 2)
        return z[:, :, :, None, :] * eye

    w = jnp.stack([one(c_re), -one(c_im)], axis=1)
    return w.reshape(nb, BLOCK_COLS, LANES)


def _c_blocks_t(dw):
    nb = dw.shape[0]
    d6 = dw.reshape(nb, 2, GROUPS_PER_BLOCK, SSM_STATE, GROUPS_PER_BLOCK, SSM_GROUP)
    diag = jnp.sum(d6 * _eye_mask()[None, None, :, None, :, None], axis=4)
    diag = diag.transpose(1, 0, 2, 4, 3).reshape(2, nb * GROUPS_PER_BLOCK, SSM_GROUP, SSM_STATE)
    return diag[0], -diag[1]


def _unshard_cols(g):
    s, k, n = g.shape
    return g.transpose(1, 0, 2).reshape(k, s * n)


def _shard_cols(w):
    k, n = w.shape
    return w.reshape(k, N_DEV, n // N_DEV).transpose(1, 0, 2)


def _pack(arrays):
    chunks, offs, row = [], [], 0
    for a in arrays:
        flat = a.reshape(-1).astype(F32)
        rows = -(-flat.shape[0] // LANES)
        chunks.append(jnp.pad(flat, (0, rows * LANES - flat.shape[0])))
        offs.append((row, rows))
        row += rows
    pad_rows = (-row) % SUBLANES
    if pad_rows:
        chunks.append(jnp.zeros((pad_rows * LANES,), F32))
    return jnp.concatenate(chunks).reshape(row + pad_rows, LANES), offs


def _unpack(packed, offs, shapes):
    out = []
    for (row, rows), shp in zip(offs, shapes):
        size = 1
        for s in shp:
            size *= s
        out.append(packed[row: row + rows].reshape(-1)[:size].reshape(shp))
    return out


def kernel(x, mix_norm, mlp_norm, mlp_w1, mlp_w2, ssm_log_dt, ssm_a_re, ssm_a_im, ssm_b_re, ssm_b_im, ssm_c_re, ssm_c_im, ssm_d, ssm_w_glu, kv_norm, w_kvf, b_f, attn_wq, attn_wo, final_norm, loss_target, m_mix_norm, m_mlp_norm, m_mlp_w1, m_mlp_w2, m_ssm_log_dt, m_ssm_a_re, m_ssm_a_im, m_ssm_b_re, m_ssm_b_im, m_ssm_c_re, m_ssm_c_im, m_ssm_d, m_ssm_w_glu, m_kv_norm, m_w_kvf, m_b_f, m_attn_wq, m_attn_wo, m_final_norm, v_mix_norm, v_mlp_norm, v_mlp_w1, v_mlp_w2, v_ssm_log_dt, v_ssm_a_re, v_ssm_a_im, v_ssm_b_re, v_ssm_b_im, v_ssm_c_re, v_ssm_c_im, v_ssm_d, v_ssm_w_glu, v_kv_norm, v_w_kvf, v_b_f, v_attn_wq, v_attn_wo, v_final_norm):
    n_rows, d_model = x.shape[1], x.shape[2]
    depth = mix_norm.shape[0]
    n_a = ssm_log_dt.shape[0]
    n_b = depth - n_a
    n_heads = d_model // HEAD_DIM
    n_groups = d_model // SSM_GROUP
    nb = n_groups // GROUPS_PER_BLOCK
    kvf_cols = 2 * d_model + n_heads
    kvf_pad = 2 * d_model + LANES

    g_w1, g_w2, g_glu, g_kvf, g_wq, g_wo, g_d = all_gather(
        "gather_weights",
        [mlp_w1.astype(BF16), mlp_w2.astype(BF16), ssm_w_glu.astype(BF16), w_kvf.astype(BF16),
         attn_wq.astype(BF16), attn_wo.astype(BF16), ssm_d])
    w1 = [_unshard_cols(g_w1[:, i]) for i in range(depth)]
    w2 = [g_w2[:, i].reshape(-1, d_model) for i in range(depth)]
    wglu = [_unshard_cols(g_glu[:, i]) for i in range(n_a)]
    wkvf = _unshard_cols(g_kvf)
    wkvf = jnp.pad(wkvf, ((0, 0), (0, kvf_pad - kvf_cols)))
    wq = [g_wq[:, j].reshape(-1, d_model) for j in range(n_b)]
    wo = [g_wo[:, j].reshape(-1, d_model) for j in range(n_b)]
    d_skip = [g_d[:, i].reshape(1, d_model) for i in range(n_a)]

    row = lambda a: a.reshape(1, -1)
    col = lambda a: a.reshape(-1, 1)

    h = x[0]
    saved = []
    kh = vh = fq = fk = None
    for i in range(depth):
        sv = {"h": h}
        hn = rms_fwd(f"mix_norm_{i}", h, row(mix_norm[i]))
        sv["hn"] = hn
        if i < n_a:
            ldt = col(jnp.repeat(ssm_log_dt[i], SSM_STATE))
            prm = [ldt, col(ssm_a_re[i]), col(ssm_a_im[i]), ssm_b_re[i].reshape(-1, SSM_GROUP), ssm_b_im[i].reshape(-1, SSM_GROUP)]
            sv["prm"] = prm
            lr, li, bbr, bbi = rowcall(f"s5_prep_{i}", _s5_discretise, prm, [], [(1, F32), (1, F32), (SSM_GROUP, F32), (SSM_GROUP, F32)])
            lam = jnp.stack([lr.reshape(nb, BLOCK_STATE), li.reshape(nb, BLOCK_STATE)], axis=1)
            wb = _b_blocks(bbr, bbi)
            wc = _c_blocks(ssm_c_re[i], ssm_c_im[i])
            bu = bmm(f"s5_bu_{i}", hn, wb, "nn")
            st = s5_scan(f"s5_scan_{i}", bu, lam, reverse=False)
            yss = bmm(f"s5_y_{i}", st, wc, "nn")
            z = rowcall(f"s5_gelu_{i}", lambda ys, u, dsk: _gelu(ys + dsk * u), [yss, hn], [d_skip[i]], [(d_model, F32)])[0]
            zw = matmul(f"s5_glu_{i}", z, wglu[i])
            h1 = rowcall(f"s5_gate_{i}", lambda hh, zz: hh + zz[:, :d_model] * _sigmoid(zz[:, d_model:]), [h, zw], [], [(d_model, F32)])[0]
            sv.update(lam=lam, wb=wb, wc=wc, st=st, yss=yss, z=z, zw=zw)
        else:
            j = i - n_a
            q = matmul(f"attn_q_{j}", hn, wq[j], scale=HEAD_DIM ** -0.5)
            qh = _to_heads(q)
            oh, lse = fox_fwd(f"attn_fwd_{j}", qh, kh, vh, fq, fk)
            o2 = _from_heads(oh)
            h1 = matmul(f"attn_o_{j}", o2, wo[j], resid=h)
            sv.update(qh=qh, oh=oh, lse=lse, o2=o2)
        h2n = rms_fwd(f"mlp_norm_{i}", h1, row(mlp_norm[i]))
        ap = matmul(f"mlp_up_{i}", h2n, w1[i])
        h = matmul(f"mlp_down_{i}", ap, w2[i], a_fn=_sqrelu, resid=h1)
        sv.update(h1=h1, h2n=h2n, ap=ap)
        saved.append(sv)
        if i == n_a - 1:
            h_mid = h
            hk = rms_fwd("kv_norm", h, row(kv_norm))
            kvf = matmul("kvf_proj", hk, wkvf)
            kh, vh = _to_heads(kvf[:, :d_model]), _to_heads(kvf[:, d_model: 2 * d_model])
            fl = kvf[:, 2 * d_model:]
            bfp = jnp.pad(row(b_f), ((0, 0), (0, LANES - n_heads)))
            cum = cum_logf("cum_logf", fl, bfp)
            cum_t = cum[:, :n_heads].T
            fq, fk = cum_t[:, :, None], cum_t[:, None, :]

    def loss_fn(hh, tgt, g):
        y, vjp = jax.vjp(_rms, hh, g)
        err = y - tgt
        part = 0.5 * jnp.sum(jnp.mean(err * err, axis=-1, keepdims=True), axis=0, keepdims=True)
        dh, dg = vjp(err * (1.0 / d_model))
        return dh, jnp.broadcast_to(part, (1, LANES)), dg

    dh, loss_part, d_final = rowcall("loss_head", loss_fn, [h, loss_target[0]], [row(final_norm)],
                                     [(d_model, F32)], [(1, LANES), (1, d_model)])

    g_mix, g_mlpn = [None] * depth, [None] * depth
    g_w1f, g_w2f = [None] * depth, [None] * depth
    g_ssm = [None] * n_a
    g_wqf, g_wof = [None] * n_b, [None] * n_b
    dk_acc, dv_acc, df_plus, df_minus = [], [], [], []
    g_kv = None
    for i in reversed(range(depth)):
        sv = saved[i]
        if i == n_a - 1:
            dfl, db_f = cum_logf_bwd("cum_logf_bwd", fl, bfp, df_plus, df_minus)
            dk = rowcall("dk_sum", lambda a, b: a + b, [_from_heads(dk_acc[0]), _from_heads(dk_acc[1])], [], [(d_model, F32)])[0] if len(dk_acc) == 2 else _from_heads(dk_acc[0])
            dv = rowcall("dv_sum", lambda a, b: a + b, [_from_heads(dv_acc[0]), _from_heads(dv_acc[1])], [], [(d_model, F32)])[0] if len(dv_acc) == 2 else _from_heads(dv_acc[0])
            dkvf = jnp.concatenate([dk, dv, dfl], axis=1)
            dhk = matmul("kvf_dx", dkvf, wkvf, "nt")
            d_wkvf = matmul("kvf_dw", hk, dkvf, "tn")
            dh, d_kvn = rms_bwd("kv_norm_bwd", h_mid, row(kv_norm), dhk, add=dh)
            g_kv = (d_wkvf[:, :kvf_cols], d_kvn, db_f[:, :n_heads])
        dap = matmul(f"mlp_down_dx_{i}", dh, w2[i], "nt", post=lambda acc, apt: acc * (2.0 * jnp.maximum(apt, 0.0)), post_arg=sv["ap"])
        g_w2f[i] = matmul(f"mlp_down_dw_{i}", sv["ap"], dh, "tn", a_fn=_sqrelu)
        dh2n = matmul(f"mlp_up_dx_{i}", dap, w1[i], "nt")
        g_w1f[i] = matmul(f"mlp_up_dw_{i}", sv["h2n"], dap, "tn")
        dh1, g_mlpn[i] = rms_bwd(f"mlp_norm_bwd_{i}", sv["h1"], row(mlp_norm[i]), dh2n, add=dh)
        if i < n_a:
            def glu_bwd(zz, dd):
                val, gate = zz[:, :d_model], zz[:, d_model:]
                sg = _sigmoid(gate)
                return jnp.concatenate([dd * sg, dd * val * sg * (1.0 - sg)], axis=1)

            dzw = rowcall(f"s5_gate_bwd_{i}", glu_bwd, [sv["zw"], dh1], [], [(2 * d_model, F32)])[0]
            dz = matmul(f"s5_glu_dx_{i}", dzw, wglu[i], "nt")
            d_wglu = matmul(f"s5_glu_dw_{i}", sv["z"], dzw, "tn")

            def gelu_bwd(ys, u, dzz, dsk):
                _, vjp = jax.vjp(lambda yy: _gelu(yy), ys + dsk * u)
                dy = vjp(dzz)[0]
                return dy, dy * dsk, jnp.sum(dy * u, axis=0, keepdims=True)

            dy, du_skip, d_dskip = rowcall(f"s5_gelu_bwd_{i}", gelu_bwd, [sv["yss"], sv["hn"], dz], [d_skip[i]],
                                          [(d_model, F32), (d_model, F32)], [(1, d_model)])
            ds_out = bmm(f"s5_y_dx_{i}", dy, sv["wc"], "nt")
            d_wc = bmm_tn(f"s5_y_dw_{i}", sv["st"], dy, nb)
            gst = s5_scan(f"s5_scan_bwd_{i}", ds_out, sv["lam"], reverse=True)
            d_lam = s5_dlam(f"s5_dlam_{i}", gst, sv["st"], nb)
            du = bmm(f"s5_bu_dx_{i}", gst, sv["wb"], "nt")
            d_wb = bmm_tn(f"s5_bu_dw_{i}", sv["hn"], gst, nb)
            d_bbr, d_bbi = _b_blocks_t(d_wb)
            d_cre, d_cim = _c_blocks_t(d_wc)

            def prep_bwd(ldt, ar, ai, br, bi, dlr, dli, dbr, dbi):
                _, vjp = jax.vjp(_s5_discretise, ldt, ar, ai, br, bi)
                return vjp((dlr, dli, dbr, dbi))

            d_ldt, d_are, d_aim, d_bre, d_bim = rowcall(
                f"s5_prep_bwd_{i}", prep_bwd, sv["prm"] + [col(d_lam[:, 0]), col(d_lam[:, 1]), d_bbr, d_bbi], [],
                [(1, F32), (1, F32), (1, F32), (SSM_GROUP, F32), (SSM_GROUP, F32)])
            d_logdt = rowcall(f"s5_dlogdt_{i}", lambda a: jnp.sum(a, axis=1, keepdims=True), [d_ldt.reshape(n_groups, SSM_STATE)], [], [(1, F32)])[0]
            dhn = rowcall(f"s5_du_sum_{i}", lambda a, b: a + b, [du, du_skip], [], [(d_model, F32)])[0]
            g_ssm[i] = dict(log_dt=d_logdt.reshape(n_groups), a_re=d_are.reshape(n_groups, SSM_STATE), a_im=d_aim.reshape(n_groups, SSM_STATE),
                            b_re=d_bre.reshape(n_groups, SSM_STATE, SSM_GROUP), b_im=d_bim.reshape(n_groups, SSM_STATE, SSM_GROUP),
                            c_re=d_cre, c_im=d_cim, d=d_dskip, w_glu=d_wglu)
        else:
            j = i - n_a
            do2 = matmul(f"attn_o_dx_{j}", dh1, wo[j], "nt")
            g_wof[j] = matmul(f"attn_o_dw_{j}", sv["o2"], dh1, "tn")
            doh = _to_heads(do2)
            delta = rowcall(f"attn_delta_{j}", lambda a, b: jnp.sum(a * b, axis=1, keepdims=True),
                            [sv["oh"].reshape(-1, HEAD_DIM), doh.reshape(-1, HEAD_DIM)], [], [(1, F32)])[0].reshape(n_heads, n_rows, 1)
            dqh, dfq = fox_bwd_q(f"attn_bwd_q_{j}", sv["qh"], kh, vh, fq, fk, doh, sv["lse"], delta, HEAD_DIM ** -0.5)
            dkh, dvh, dfk = fox_bwd_kv(f"attn_bwd_kv_{j}", sv["qh"], kh, vh, fq, fk, doh, sv["lse"], delta)
            dk_acc.append(dkh)
            dv_acc.append(dvh)
            df_plus.append(jnp.pad(dfq[:, :, 0].T, ((0, 0), (0, LANES - n_heads))))
            df_minus.append(jnp.pad(dfk[:, 0, :].T, ((0, 0), (0, LANES - n_heads))))
            dq2 = _from_heads(dqh)
            dhn = matmul(f"attn_q_dx_{j}", dq2, wq[j], "nt")
            g_wqf[j] = matmul(f"attn_q_dw_{j}", sv["hn"], dq2, "tn")
        dh, g_mix[i] = rms_bwd(f"mix_norm_bwd_{i}", sv["h"], row(mix_norm[i]), dhn, add=dh1)
    grad_x = dh[None]

    stack = lambda xs: jnp.stack(xs, axis=1)
    parts = all_to_all("scatter_grads", [
        stack([_shard_cols(g) for g in g_w1f]),
        stack([g.reshape(N_DEV, -1, d_model) for g in g_w2f]),
        stack([_shard_cols(g["w_glu"]) for g in g_ssm]),
        _shard_cols(g_kv[0]),
        stack([g.reshape(N_DEV, -1, d_model) for g in g_wqf]),
        stack([g.reshape(N_DEV, -1, d_model) for g in g_wof]),
        stack([g["d"].reshape(N_DEV, -1) for g in g_ssm]),
    ])
    small_names = ["mix_norm", "mlp_norm", "ssm_log_dt", "ssm_a_re", "ssm_a_im", "ssm_b_re", "ssm_b_im", "ssm_c_re", "ssm_c_im",
                   "kv_norm", "b_f", "final_norm"]
    small_grads = [jnp.concatenate(g_mix, axis=0), jnp.concatenate(g_mlpn, axis=0)]
    small_grads += [jnp.stack([g[kk] for g in g_ssm]) for kk in ("log_dt", "a_re", "a_im", "b_re", "b_im", "c_re", "c_im")]
    small_grads += [g_kv[1], g_kv[2], d_final]
    small_w = [mix_norm, mlp_norm, ssm_log_dt, ssm_a_re, ssm_a_im, ssm_b_re, ssm_b_im, ssm_c_re, ssm_c_im, kv_norm, b_f, final_norm]
    small_m = [m_mix_norm, m_mlp_norm, m_ssm_log_dt, m_ssm_a_re, m_ssm_a_im, m_ssm_b_re, m_ssm_b_im, m_ssm_c_re, m_ssm_c_im, m_kv_norm, m_b_f, m_final_norm]
    small_v = [v_mix_norm, v_mlp_norm, v_ssm_log_dt, v_ssm_a_re, v_ssm_a_im, v_ssm_b_re, v_ssm_b_im, v_ssm_c_re, v_ssm_c_im, v_kv_norm, v_b_f, v_final_norm]
    loss_slot = jnp.zeros((LANES,), F32)
    packed_g, offs = _pack(small_grads + [loss_part])
    packed_w, _ = _pack(small_w + [loss_slot])
    packed_m, _ = _pack(small_m + [loss_slot])
    packed_v, _ = _pack(small_v + [loss_slot])
    (small_parts,) = all_gather("gather_small_grads", [packed_g])

    res = {}

    def update(nm, part, w, m, v):
        shp = w.shape
        as2d = lambda a: a.reshape(-1, shp[-1])
        outs = adamw(f"adamw_{nm}", part.reshape((N_DEV,) + as2d(w).shape), as2d(w), as2d(m), as2d(v))
        res[nm] = [o.reshape(shp) for o in outs]

    update("mlp_w1", parts[0], mlp_w1, m_mlp_w1, v_mlp_w1)
    update("mlp_w2", parts[1], mlp_w2, m_mlp_w2, v_mlp_w2)
    update("ssm_w_glu", parts[2], ssm_w_glu, m_ssm_w_glu, v_ssm_w_glu)
    update("w_kvf", parts[3], w_kvf, m_w_kvf, v_w_kvf)
    update("attn_wq", parts[4], attn_wq, m_attn_wq, v_attn_wq)
    update("attn_wo", parts[5], attn_wo, m_attn_wo, v_attn_wo)
    update("ssm_d", parts[6], ssm_d, m_ssm_d, v_ssm_d)
    small_out = adamw("adamw_small", small_parts, packed_w, packed_m, packed_v)
    shapes = [w.shape for w in small_w] + [(LANES,)]
    unpacked = [_unpack(o, offs, shapes) for o in small_out]
    for idx, nm in enumerate(small_names):
        res[nm] = [u[idx] for u in unpacked]
    loss = unpacked[0][-1][0]

    order = ["mix_norm", "mlp_norm", "mlp_w1", "mlp_w2", "ssm_log_dt", "ssm_a_re", "ssm_a_im", "ssm_b_re", "ssm_b_im", "ssm_c_re",
             "ssm_c_im", "ssm_d", "ssm_w_glu", "kv_norm", "w_kvf", "b_f", "attn_wq", "attn_wo", "final_norm"]
    out = [loss, grad_x]
    for kind in range(4):
        out += [res[nm][kind] for nm in order]
    return tuple(out)
```

```python
import functools

import jax
import jax.numpy as jnp
from jax import lax
from jax.experimental import pallas as pl
from jax.experimental.pallas import tpu as pltpu

F32 = jnp.float32
BF16 = jnp.bfloat16
HIGHEST = lax.Precision.HIGHEST

V7X_VMEM_BYTES = 64 << 20
VMEM_LIMIT_BYTES = (V7X_VMEM_BYTES * 3) // 4
LANES = 128
SUBLANES = 8

N_DEV = 8
RMS_EPS = 1e-6
SSM_GROUP = 16
SSM_STATE = 64
HEAD_DIM = 64
GROUPS_PER_BLOCK = LANES // SSM_GROUP
BLOCK_STATE = GROUPS_PER_BLOCK * SSM_STATE
BLOCK_COLS = 2 * BLOCK_STATE
NEG_BIG = -1e30
LOG2E = 1.4426950408889634

ADAM_LR = 0.001
ADAM_B1 = 0.9
ADAM_B2 = 0.999
ADAM_EPS = 1e-08
ADAM_WD = 0.01
ADAM_STEP = 10

TILES = {"row": 512, "mm": (1024, 512, 1024), "blk": 512, "scan": 256, "cum": 256, "attn": 512, "adam": 256}


def _pick(dim, pref, align=LANES):
    if dim <= pref:
        return dim
    for a in (align, SUBLANES):
        d = (pref // a) * a
        while d >= a:
            if dim % d == 0:
                return d
            d -= a
    return dim


def _params(*sem):
    return pltpu.CompilerParams(dimension_semantics=sem, vmem_limit_bytes=VMEM_LIMIT_BYTES)


def rowcall(name, fn, rows, consts, out_rows, out_accs=(), tm=None):
    n_rows = rows[0].shape[0]
    tm = _pick(n_rows, tm or TILES["row"], SUBLANES)
    nr, nc, no, na = len(rows), len(consts), len(out_rows), len(out_accs)

    def body(*refs):
        ins = [r[...] for r in refs[: nr + nc]]
        outs = fn(*ins)
        if not isinstance(outs, (tuple, list)):
            outs = (outs,)
        for r, o in zip(refs[nr + nc: nr + nc + no], outs[:no]):
            r[...] = o.astype(r.dtype)
        if na:
            i = pl.program_id(0)
            for r, o in zip(refs[nr + nc + no:], outs[no:]):
                @pl.when(i == 0)
                def _(r=r, o=o):
                    r[...] = o

                @pl.when(i > 0)
                def _(r=r, o=o):
                    r[...] += o

    in_specs = [pl.BlockSpec((tm, a.shape[1]), lambda i: (i, 0)) for a in rows]
    in_specs += [pl.BlockSpec(c.shape, lambda i, n=c.ndim: (0,) * n) for c in consts]
    out_shape = [jax.ShapeDtypeStruct((n_rows, c), dt) for c, dt in out_rows]
    out_specs = [pl.BlockSpec((tm, c), lambda i: (i, 0)) for c, _ in out_rows]
    out_shape += [jax.ShapeDtypeStruct(s, F32) for s in out_accs]
    out_specs += [pl.BlockSpec(s, lambda i, n=len(s): (0,) * n) for s in out_accs]
    res = pl.pallas_call(
        body, name=name, grid=(n_rows // tm,), in_specs=in_specs, out_specs=out_specs, out_shape=out_shape,
        compiler_params=_params("arbitrary" if na else "parallel"),
    )(*rows, *consts)
    return res


_DIMS = {"nn": (((1,), (0,)), ((), ())), "nt": (((1,), (1,)), ((), ())), "tn": (((0,), (0,)), ((), ()))}


def matmul(name, a, b, mode="nn", *, a_fn=None, scale=None, resid=None, post=None, post_arg=None,
           bm=None, bn=None, bk=None):
    if mode == "nn":
        (m, k), (k2, n) = a.shape, b.shape
    elif mode == "nt":
        (m, k), (n, k2) = a.shape, b.shape
    else:
        (k, m), (k2, n) = a.shape, b.shape
    assert k == k2, (name, a.shape, b.shape, mode)
    bm, bn, bk = _pick(m, bm or TILES["mm"][0]), _pick(n, bn or TILES["mm"][1]), _pick(k, bk or TILES["mm"][2])
    nk = k // bk
    a_spec = pl.BlockSpec((bk, bm), lambda i, j, kk: (kk, i)) if mode == "tn" else pl.BlockSpec((bm, bk), lambda i, j, kk: (i, kk))
    b_spec = pl.BlockSpec((bn, bk), lambda i, j, kk: (j, kk)) if mode == "nt" else pl.BlockSpec((bk, bn), lambda i, j, kk: (kk, j))
    mn_spec = pl.BlockSpec((bm, bn), lambda i, j, kk: (i, j))
    extra = [x for x in (resid, post_arg) if x is not None]
    has_resid, has_post = resid is not None, post is not None

    def body(*refs):
        a_ref, b_ref = refs[0], refs[1]
        ex = refs[2: 2 + len(extra)]
        o_ref = refs[2 + len(extra)]
        av = a_ref[...]
        if a_fn is not None:
            av = a_fn(av.astype(F32))
        p = lax.dot_general(av.astype(BF16), b_ref[...].astype(BF16), _DIMS[mode], preferred_element_type=F32)

        def finish(acc):
            if scale is not None:
                acc = acc * scale
            idx = 0
            if has_resid:
                acc = acc + ex[idx][...]
                idx += 1
            if has_post:
                acc = post(acc, ex[idx][...])
            o_ref[...] = acc.astype(o_ref.dtype)

        if nk == 1:
            finish(p)
        else:
            acc_ref = refs[-1]
            kk = pl.program_id(2)

            @pl.when(kk == 0)
            def _():
                acc_ref[...] = p

            @pl.when(kk > 0)
            def _():
                acc_ref[...] += p

            @pl.when(kk == nk - 1)
            def _():
                finish(acc_ref[...])

    return pl.pallas_call(
        body, name=name, grid=(m // bm, n // bn, nk),
        in_specs=[a_spec, b_spec] + [mn_spec] * len(extra), out_specs=mn_spec,
        out_shape=jax.ShapeDtypeStruct((m, n), F32),
        scratch_shapes=[pltpu.VMEM((bm, bn), F32)] if nk > 1 else [],
        compiler_params=_params("parallel", "parallel", "arbitrary"),
    )(a, b, *extra)


def bmm(name, a, w, mode, tm=None):
    n_rows = a.shape[0]
    nb, ka, kb = w.shape
    ca, co = (ka, kb) if mode == "nn" else (kb, ka)
    assert a.shape[1] == nb * ca
    tm = _pick(n_rows, tm or TILES["blk"], SUBLANES)

    def body(a_ref, w_ref, o_ref):
        o_ref[...] = lax.dot_general(a_ref[...], w_ref[...], _DIMS[mode], precision=HIGHEST, preferred_element_type=F32)

    return pl.pallas_call(
        body, name=name, grid=(n_rows // tm, nb),
        in_specs=[pl.BlockSpec((tm, ca), lambda i, b: (i, b)), pl.BlockSpec((None, ka, kb), lambda i, b: (b, 0, 0))],
        out_specs=pl.BlockSpec((tm, co), lambda i, b: (i, b)),
        out_shape=jax.ShapeDtypeStruct((n_rows, nb * co), F32),
        compiler_params=_params("parallel", "parallel"),
    )(a, w)


def bmm_tn(name, a, b, nb, tk=None):
    n_rows = a.shape[0]
    ka, kb = a.shape[1] // nb, b.shape[1] // nb
    tk = _pick(n_rows, tk or TILES["blk"], SUBLANES)

    def body(a_ref, b_ref, o_ref):
        p = lax.dot_general(a_ref[...], b_ref[...], _DIMS["tn"], precision=HIGHEST, preferred_element_type=F32)
        kk = pl.program_id(1)

        @pl.when(kk == 0)
        def _():
            o_ref[...] = p

        @pl.when(kk > 0)
        def _():
            o_ref[...] += p

    return pl.pallas_call(
        body, name=name, grid=(nb, n_rows // tk),
        in_specs=[pl.BlockSpec((tk, ka), lambda bb, kk: (kk, bb)), pl.BlockSpec((tk, kb), lambda bb, kk: (kk, bb))],
        out_specs=pl.BlockSpec((None, ka, kb), lambda bb, kk: (bb, 0, 0)),
        out_shape=jax.ShapeDtypeStruct((nb, ka, kb), F32),
        compiler_params=_params("parallel", "arbitrary"),
    )(a, b)


def _cmul(ar, ai, br, bi):
    return ar * br - ai * bi, ar * bi + ai * br


def _scan_tables(lam_ref, reverse):
    shape = (SUBLANES, BLOCK_STATE)
    lr = jnp.broadcast_to(lam_ref[0:1, :], shape)
    li = jnp.broadcast_to(lam_ref[1:2, :], shape)
    if reverse:
        li = -li
    row = lax.broadcasted_iota(jnp.int32, shape, 0)
    tt = (SUBLANES - 1 - row) if reverse else row
    l1 = (lr, li)
    l2 = _cmul(*l1, *l1)
    l4 = _cmul(*l2, *l2)
    pr, pi = l1
    for bit, lp in enumerate((l1, l2, l4)):
        qr, qi = _cmul(pr, pi, *lp)
        on = ((tt >> bit) & 1) == 1
        pr, pi = jnp.where(on, qr, pr), jnp.where(on, qi, pi)
    steps = []
    for d, lp in ((1, l1), (2, l2), (4, l4)):
        ok = tt >= d
        steps.append((d, jnp.where(ok, lp[0], 0.0), jnp.where(ok, lp[1], 0.0)))
    return steps, (pr, pi)


def s5_scan(name, x, lam, reverse, tm=None):
    n_rows = x.shape[0]
    nb = lam.shape[0]
    tm = _pick(n_rows, tm or TILES["scan"], SUBLANES)
    nt = n_rows // tm
    ng = tm // SUBLANES
    last = 0 if reverse else SUBLANES - 1

    def body(x_ref, lam_ref, o_ref, carry_ref):
        it = pl.program_id(1)
        steps, (pr, pi) = _scan_tables(lam_ref, reverse)

        @pl.when(it == 0)
        def _():
            carry_ref[...] = jnp.zeros_like(carry_ref)

        def group(r, carry):
            cr, ci = carry
            rr = (ng - 1 - r) if reverse else r
            off = pl.multiple_of(rr * SUBLANES, SUBLANES)
            xr = x_ref[pl.ds(off, SUBLANES), 0:BLOCK_STATE]
            xi = x_ref[pl.ds(off, SUBLANES), BLOCK_STATE:BLOCK_COLS]
            for d, mr, mi in steps:
                sh = (SUBLANES - d) if reverse else d
                yr, yi = pltpu.roll(xr, sh, 0), pltpu.roll(xi, sh, 0)
                xr, xi = xr + mr * yr - mi * yi, xi + mr * yi + mi * yr
            xr, xi = xr + pr * cr - pi * ci, xi + pr * ci + pi * cr
            o_ref[pl.ds(off, SUBLANES), 0:BLOCK_STATE] = xr
            o_ref[pl.ds(off, SUBLANES), BLOCK_STATE:BLOCK_COLS] = xi
            shape = (SUBLANES, BLOCK_STATE)
            return jnp.broadcast_to(xr[last:last + 1, :], shape), jnp.broadcast_to(xi[last:last + 1, :], shape)

        cr, ci = lax.fori_loop(0, ng, group, (carry_ref[0], carry_ref[1]))
        carry_ref[0] = cr
        carry_ref[1] = ci

    tile = (lambda b, it: (nt - 1 - it, b)) if reverse else (lambda b, it: (it, b))
    return pl.pallas_call(
        body, name=name, grid=(nb, nt),
        in_specs=[pl.BlockSpec((tm, BLOCK_COLS), tile), pl.BlockSpec((None, 2, BLOCK_STATE), lambda b, it: (b, 0, 0))],
        out_specs=pl.BlockSpec((tm, BLOCK_COLS), tile),
        out_shape=jax.ShapeDtypeStruct(x.shape, F32),
        scratch_shapes=[pltpu.VMEM((2, SUBLANES, BLOCK_STATE), F32)],
        compiler_params=_params("parallel", "arbitrary"),
    )(x, lam)


def s5_dlam(name, g, s, nb, tm=None):
    n_rows = g.shape[0]
    tm = _pick(n_rows, tm or TILES["scan"], SUBLANES)
    nt = n_rows // tm
    ng = tm // SUBLANES
    shape = (SUBLANES, BLOCK_STATE)

    def body(g_ref, s_ref, o_ref, carry_ref, acc_ref):
        it = pl.program_id(1)

        @pl.when(it == 0)
        def _():
            carry_ref[...] = jnp.zeros_like(carry_ref)
            acc_ref[...] = jnp.zeros_like(acc_ref)

        first = lax.broadcasted_iota(jnp.int32, shape, 0) == 0

        def group(r, carry):
            cr, ci, ar, ai = carry
            off = pl.multiple_of(r * SUBLANES, SUBLANES)
            sr = s_ref[pl.ds(off, SUBLANES), 0:BLOCK_STATE]
            si = s_ref[pl.ds(off, SUBLANES), BLOCK_STATE:BLOCK_COLS]
            gr = g_ref[pl.ds(off, SUBLANES), 0:BLOCK_STATE]
            gi = g_ref[pl.ds(off, SUBLANES), BLOCK_STATE:BLOCK_COLS]
            pr = jnp.where(first, cr, pltpu.roll(sr, 1, 0))
            pi = jnp.where(first, ci, pltpu.roll(si, 1, 0))
            ar = ar + gr * pr + gi * pi
            ai = ai + gi * pr - gr * pi
            return (jnp.broadcast_to(sr[SUBLANES - 1:SUBLANES, :], shape),
                    jnp.broadcast_to(si[SUBLANES - 1:SUBLANES, :], shape), ar, ai)

        cr, ci, ar, ai = lax.fori_loop(0, ng, group, (carry_ref[0], carry_ref[1], acc_ref[0], acc_ref[1]))
        carry_ref[0] = cr
        carry_ref[1] = ci
        acc_ref[0] = ar
        acc_ref[1] = ai

        @pl.when(it == nt - 1)
        def _():
            o_ref[0:1, :] = jnp.sum(ar, axis=0, keepdims=True)
            o_ref[1:2, :] = jnp.sum(ai, axis=0, keepdims=True)

    tile = lambda b, it: (it, b)
    return pl.pallas_call(
        body, name=name, grid=(nb, nt),
        in_specs=[pl.BlockSpec((tm, BLOCK_COLS), tile), pl.BlockSpec((tm, BLOCK_COLS), tile)],
        out_specs=pl.BlockSpec((None, 2, BLOCK_STATE), lambda b, it: (b, 0, 0)),
        out_shape=jax.ShapeDtypeStruct((nb, 2, BLOCK_STATE), F32),
        scratch_shapes=[pltpu.VMEM((2,) + shape, F32), pltpu.VMEM((2,) + shape, F32)],
        compiler_params=_params("parallel", "arbitrary"),
    )(g, s)


def _rms(x, g):
    return x * lax.rsqrt(jnp.mean(x * x, axis=-1, keepdims=True) + RMS_EPS) * g


def _sigmoid(x):
    return 1.0 / (1.0 + jnp.exp(-x))


def _gelu(x):
    return 0.5 * x * (1.0 + jnp.tanh(0.7978845608028654 * (x + 0.044715 * (x * x * x))))


def _log_sigmoid(x):
    return jnp.minimum(x, 0.0) - jnp.log(1.0 + jnp.exp(-jnp.abs(x)))


def _sqrelu(x):
    r = jnp.maximum(x, 0.0)
    return r * r


def _s5_discretise(ldt, ar, ai, br, bi):
    dt = jnp.exp(ldt)
    er = jnp.exp(ar * dt)
    lr, li = er * jnp.cos(ai * dt), er * jnp.sin(ai * dt)
    nr, ni = lr - 1.0, li
    den = ar * ar + ai * ai
    cr, ci = (nr * ar + ni * ai) / den, (ni * ar - nr * ai) / den
    return lr, li, cr * br - ci * bi, cr * bi + ci * br


def rms_fwd(name, x, g):
    return rowcall(name, _rms, [x], [g], [(x.shape[1], F32)])[0]


def rms_bwd(name, x, g, dy, add=None):
    def fn(x, dy, *rest):
        g = rest[-1]
        _, vjp = jax.vjp(_rms, x, g)
        dx, dg = vjp(dy)
        if add is not None:
            dx = dx + rest[0]
        return dx, dg

    rows = [x, dy] + ([add] if add is not None else [])
    return rowcall(name, fn, rows, [g], [(x.shape[1], F32)], [g.shape])


def _split3(x):
    hi = x.astype(BF16).astype(F32)
    r = x - hi
    mid = r.astype(BF16).astype(F32)
    return hi, mid, (r - mid).astype(BF16).astype(F32)


def cum_logf(name, fl, bf, tm=None):
    n_rows, w = fl.shape
    tm = _pick(n_rows, tm or TILES["cum"], SUBLANES)

    def body(fl_ref, bf_ref, hi_ref, mid_ref, lo_ref, carry_ref):
        it = pl.program_id(0)

        @pl.when(it == 0)
        def _():
            carry_ref[...] = jnp.zeros_like(carry_ref)

        ls = _log_sigmoid(fl_ref[...] + bf_ref[...])
        tri = (lax.broadcasted_iota(jnp.int32, (tm, tm), 0) >= lax.broadcasted_iota(jnp.int32, (tm, tm), 1)).astype(F32)
        c = jnp.dot(tri, ls, precision=HIGHEST, preferred_element_type=F32) + carry_ref[0:1, :]
        carry_ref[...] = jnp.broadcast_to(c[tm - 1:tm, :], carry_ref.shape)
        hi_ref[...], mid_ref[...], lo_ref[...] = _split3(c * (-LOG2E))

    spec = pl.BlockSpec((tm, w), lambda i: (i, 0))
    return pl.pallas_call(
        body, name=name, grid=(n_rows // tm,),
        in_specs=[spec, pl.BlockSpec((1, w), lambda i: (0, 0))],
        out_specs=[spec] * 3, out_shape=[jax.ShapeDtypeStruct((n_rows, w), F32)] * 3,
        scratch_shapes=[pltpu.VMEM((SUBLANES, w), F32)],
        compiler_params=_params("arbitrary"),
    )(fl, bf)


def cum_logf_bwd(name, fl, bf, plus, minus, tm=None):
    n_rows, w = fl.shape
    tm = _pick(n_rows, tm or TILES["cum"], SUBLANES)
    nt = n_rows // tm
    n_p, n_m = len(plus), len(minus)

    def body(*refs):
        fl_ref, bf_ref = refs[0], refs[1]
        d_refs = refs[2: 2 + n_p + n_m]
        o_ref, db_ref, carry_ref = refs[2 + n_p + n_m:]
        it = pl.program_id(0)

        @pl.when(it == 0)
        def _():
            carry_ref[...] = jnp.zeros_like(carry_ref)

        d = None
        for r in d_refs[:n_p]:
            d = r[...] if d is None else d + r[...]
        for r in d_refs[n_p:]:
            d = -r[...] if d is None else d - r[...]
        tri = (lax.broadcasted_iota(jnp.int32, (tm, tm), 0) <= lax.broadcasted_iota(jnp.int32, (tm, tm), 1)).astype(F32)
        c = jnp.dot(tri, d, precision=HIGHEST, preferred_element_type=F32) + carry_ref[0:1, :]
        carry_ref[...] = jnp.broadcast_to(c[0:1, :], carry_ref.shape)
        dfl = c * _sigmoid(-(fl_ref[...] + bf_ref[...]))
        o_ref[...] = dfl
        part = jnp.sum(dfl, axis=0, keepdims=True)

        @pl.when(it == 0)
        def _():
            db_ref[...] = part

        @pl.when(it > 0)
        def _():
            db_ref[...] += part

    rev = lambda i: (nt - 1 - i, 0)
    return pl.pallas_call(
        body, name=name, grid=(nt,),
        in_specs=[pl.BlockSpec((tm, w), rev), pl.BlockSpec((1, w), lambda i: (0, 0))] + [pl.BlockSpec((tm, w), rev)] * (n_p + n_m),
        out_specs=[pl.BlockSpec((tm, w), rev), pl.BlockSpec((1, w), lambda i: (0, 0))],
        out_shape=[jax.ShapeDtypeStruct((n_rows, w), F32), jax.ShapeDtypeStruct((1, w), F32)],
        scratch_shapes=[pltpu.VMEM((SUBLANES, w), F32)],
        compiler_params=_params("arbitrary"),
    )(fl, bf, *plus, *minus)


ROWSUM_LANE = HEAD_DIM + 6
def _ext(parts):
    x = jnp.concatenate(parts, axis=-1)
    pad = [(0, 0)] * (x.ndim - 1) + [(0, LANES - x.shape[-1])]
    return jnp.pad(x, pad).astype(BF16)


def fox_fwd(name, q_ext, k_ext, v_ext, t=None):
    nh, n_rows, w = q_ext.shape
    t = _pick(n_rows, t or TILES["attn"])
    nt = n_rows // t

    def body(q_ref, k_ref, v_ref, o_ref, m_ref, acc_ref):
        i = pl.program_id(1)
        m_ref[...] = jnp.full_like(m_ref, NEG_BIG)
        acc_ref[...] = jnp.zeros_like(acc_ref)
        q = q_ref[...]

        def block(j, diagonal):
            off = pl.multiple_of(j * t, t)
            s = lax.dot_general(q, k_ref[pl.ds(off, t), :], _DIMS["nt"], preferred_element_type=F32)
            if diagonal:
                keep = lax.broadcasted_iota(jnp.int32, (t, t), 0) >= lax.broadcasted_iota(jnp.int32, (t, t), 1)
                s = jnp.where(keep, s, NEG_BIG)
            m_prev = m_ref[...]
            m_new = jnp.maximum(m_prev, jnp.max(s, axis=1, keepdims=True))
            p = jnp.exp2(s - jnp.tile(m_new, (1, t // LANES)))
            acc_ref[...] = jnp.exp2(m_prev - m_new) * acc_ref[...] + jnp.dot(
                p.astype(BF16), v_ref[pl.ds(off, t), :], preferred_element_type=F32)
            m_ref[...] = m_new

        def off_diagonal(j, carry):
            block(j, False)
            return carry

        lax.fori_loop(0, i, off_diagonal, 0)
        block(i, True)
        acc = acc_ref[...]
        row_sum = acc[:, HEAD_DIM:HEAD_DIM + 1]
        hi, mid, lo = _split3(m_ref[:, 0:1] + jnp.log2(row_sum))
        lane = lax.broadcasted_iota(jnp.int32, (t, w), 1)
        tail = jnp.where(lane == HEAD_DIM, hi, jnp.where(lane == HEAD_DIM + 1, mid, jnp.where(lane == HEAD_DIM + 2, lo, 0.0)))
        o_ref[...] = jnp.where(lane < HEAD_DIM, acc / row_sum, tail)

    whole = pl.BlockSpec((None, n_rows, w), lambda h, i: (h, 0, 0))
    tile = pl.BlockSpec((None, t, w), lambda h, i: (h, i, 0))
    return pl.pallas_call(
        body, name=name, grid=(nh, nt), in_specs=[tile, whole, whole], out_specs=tile,
        out_shape=jax.ShapeDtypeStruct((nh, n_rows, w), F32),
        scratch_shapes=[pltpu.VMEM((t, w), F32), pltpu.VMEM((t, w), F32)],
        compiler_params=_params("parallel", "arbitrary"),
    )(q_ext, k_ext, v_ext)


def fox_bwd(name, q_ext, do_ext, k_ext, v_ext, t=None):
    nh, n_rows, w = q_ext.shape
    t = _pick(n_rows, t or TILES["attn"])
    nt = n_rows // t

    def body(q_ref, do_ref, k_ref, v_ref, dq_ref, dk_ref, dv_ref):
        j = pl.program_id(1)

        @pl.when(j == 0)
        def _():
            dq_ref[...] = jnp.zeros_like(dq_ref)

        dk_ref[...] = jnp.zeros_like(dk_ref)
        dv_ref[...] = jnp.zeros_like(dv_ref)
        kj, vj = k_ref[...], v_ref[...]

        def block(i, diagonal):
            off = pl.multiple_of(i * t, t)
            qi, doi = q_ref[pl.ds(off, t), :], do_ref[pl.ds(off, t), :]
            pt = jnp.exp2(lax.dot_general(kj, qi, _DIMS["nt"], preferred_element_type=F32))
            if diagonal:
                keep = lax.broadcasted_iota(jnp.int32, (t, t), 0) <= lax.broadcasted_iota(jnp.int32, (t, t), 1)
                pt = jnp.where(keep, pt, 0.0)
            dst = (pt * lax.dot_general(vj, doi, _DIMS["nt"], preferred_element_type=F32)).astype(BF16)
            dv_ref[...] += jnp.dot(pt.astype(BF16), doi, preferred_element_type=F32)
            dk_ref[...] += jnp.dot(dst, qi, preferred_element_type=F32)
            dq_ref[pl.ds(off, t), :] += lax.dot_general(dst, kj, _DIMS["tn"], preferred_element_type=F32)

        def off_diagonal(i, carry):
            block(i, False)
            return carry

        block(j, True)
        lax.fori_loop(j + 1, nt, off_diagonal, 0)
        lane = lax.broadcasted_iota(jnp.int32, (t, w), 1)
        dk_ref[...] = dk_ref[...] * jnp.where(lane < HEAD_DIM, 1.0 / LOG2E, 1.0)

    whole = pl.BlockSpec((None, n_rows, w), lambda h, j: (h, 0, 0))
    tile = pl.BlockSpec((None, t, w), lambda h, j: (h, j, 0))
    shape = jax.ShapeDtypeStruct((nh, n_rows, w), F32)
    return pl.pallas_call(
        body, name=name, grid=(nh, nt), in_specs=[whole, whole, tile, tile], out_specs=[whole, tile, tile],
        out_shape=[shape, shape, shape],
        compiler_params=_params("parallel", "arbitrary"),
    )(q_ext, do_ext, k_ext, v_ext)


def attn_delta(name, o, do):
    def fn(a, b):
        hi, mid, lo = _split3(jnp.sum(a * b, axis=1, keepdims=True))
        lane = lax.broadcasted_iota(jnp.int32, (a.shape[0], LANES), 1)
        return jnp.where(lane == 0, hi, jnp.where(lane == 1, mid, jnp.where(lane == 2, lo, 0.0)))

    return rowcall(name, fn, [o, do], [], [(LANES, F32)])[0]


_PEER_FLIPS = [(bx, by, bc) for bx in (0, 1) for by in (0, 1) for bc in (0, 1)][1:]


def _exchange(name, tensors, scatter):
    n = len(tensors)
    n_peer = len(_PEER_FLIPS)

    def body(*refs):
        ins, outs = refs[:n], refs[n: 2 * n]
        send_sems, recv_sems, local_sems = refs[2 * n:]
        x, y, c = lax.axis_index("x"), lax.axis_index("y"), lax.axis_index("c")
        me = 4 * x + 2 * y + c
        copies = []
        for t in range(n):
            src_me = ins[t].at[me] if scatter else ins[t]
            local = pltpu.make_async_copy(src_me, outs[t].at[me], local_sems.at[t])
            local.start()
            copies.append(local)
            for kk, (bx, by, bc) in enumerate(_PEER_FLIPS):
                px, py, pc = (1 - x if bx else x), (1 - y if by else y), (1 - c if bc else c)
                peer = 4 * px + 2 * py + pc
                out_cp = pltpu.make_async_remote_copy(
                    src_ref=ins[t].at[peer] if scatter else ins[t], dst_ref=outs[t].at[me],
                    send_sem=send_sems.at[t, kk], recv_sem=recv_sems.at[t, kk],
                    device_id=(px, py, pc), device_id_type=pl.DeviceIdType.MESH)
                out_cp.start()
                copies.append(pltpu.make_async_remote_copy(
                    src_ref=ins[t].at[peer] if scatter else ins[t], dst_ref=outs[t].at[peer],
                    send_sem=send_sems.at[t, kk], recv_sem=recv_sems.at[t, kk],
                    device_id=(px, py, pc), device_id_type=pl.DeviceIdType.MESH))
        for cp in copies:
            cp.wait()

    any_spec = pl.BlockSpec(memory_space=pl.ANY)
    out_shape = [jax.ShapeDtypeStruct(t.shape if scatter else (N_DEV,) + t.shape, t.dtype) for t in tensors]
    return pl.pallas_call(
        body, name=name, in_specs=[any_spec] * n, out_specs=[any_spec] * n, out_shape=out_shape,
        scratch_shapes=[pltpu.SemaphoreType.DMA((n, n_peer)), pltpu.SemaphoreType.DMA((n, n_peer)), pltpu.SemaphoreType.DMA((n,))],
        compiler_params=pltpu.CompilerParams(has_side_effects=True),
    )(*tensors)


def all_gather(name, tensors):
    return _exchange(name, tensors, scatter=False)


def all_to_all(name, tensors):
    return _exchange(name, tensors, scatter=True)


def adamw(name, parts, w, m, v, tr=None):
    n_rows, n_cols = w.shape
    tr = _pick(n_rows, tr or TILES["adam"], SUBLANES)
    c1 = 1.0 - ADAM_B1 ** ADAM_STEP
    c2 = 1.0 - ADAM_B2 ** ADAM_STEP

    def body(p_ref, w_ref, m_ref, v_ref, g_ref, d_ref, mo_ref, vo_ref):
        g = p_ref[0]
        for s in range(1, N_DEV):
            g = g + p_ref[s]
        mn = ADAM_B1 * m_ref[...] + (1.0 - ADAM_B1) * g
        vn = ADAM_B2 * v_ref[...] + (1.0 - ADAM_B2) * (g * g)
        g_ref[...] = g
        mo_ref[...] = mn
        vo_ref[...] = vn
        d_ref[...] = -ADAM_LR * ((mn / c1) / (jnp.sqrt(vn / c2) + ADAM_EPS) + ADAM_WD * w_ref[...])

    spec = pl.BlockSpec((tr, n_cols), lambda i: (i, 0))
    return pl.pallas_call(
        body, name=name, grid=(n_rows // tr,),
        in_specs=[pl.BlockSpec((N_DEV, tr, n_cols), lambda i: (0, i, 0)), spec, spec, spec],
        out_specs=[spec] * 4, out_shape=[jax.ShapeDtypeStruct(w.shape, F32)] * 4,
        compiler_params=_params("parallel"),
    )(parts, w, m, v)


def _to_heads(x):
    n_rows, d = x.shape
    return x.reshape(n_rows, d // HEAD_DIM, HEAD_DIM).transpose(1, 0, 2)


def _from_heads(x):
    nh, n_rows, dh = x.shape
    return x.transpose(1, 0, 2).reshape(n_rows, nh * dh)


def _eye_mask():
    return jnp.eye(GROUPS_PER_BLOCK, dtype=F32)


def _b_blocks(bbr, bbi):
    nb = bbr.shape[0] // (GROUPS_PER_BLOCK * SSM_STATE)
    eye = _eye_mask()[None, :, None, :, None]

    def one(z):
        z = z.reshape(nb, GROUPS_PER_BLOCK, SSM_STATE, SSM_GROUP).transpose(0, 1, 3, 2)
        return z[:, :, :, None, :] * eye

    w = jnp.stack([one(bbr), one(bbi)], axis=3)
    return w.reshape(nb, LANES, BLOCK_COLS)


def _b_blocks_t(dw):
    nb = dw.shape[0]
    d6 = dw.reshape(nb, GROUPS_PER_BLOCK, SSM_GROUP, 2, GROUPS_PER_BLOCK, SSM_STATE)
    diag = jnp.sum(d6 * _eye_mask()[None, :, None, None, :, None], axis=4)
    diag = diag.transpose(3, 0, 1, 4, 2).reshape(2, nb * GROUPS_PER_BLOCK * SSM_STATE, SSM_GROUP)
    return diag[0], diag[1]


def _c_blocks(c_re, c_im):
    nb = c_re.shape[0] // GROUPS_PER_BLOCK
    eye = _eye_mask()[None, :, None, :, None]

    def one(z):
        z = z.reshape(nb, GROUPS_PER_BLOCK, SSM_GROUP, SSM_STATE).transpose(0, 1, 3, 2)
        return z[:, :, :, None, :] * eye

    w = jnp.stack([one(c_re), -one(c_im)], axis=1)
    return w.reshape(nb, BLOCK_COLS, LANES)


def _c_blocks_t(dw):
    nb = dw.shape[0]
    d6 = dw.reshape(nb, 2, GROUPS_PER_BLOCK, SSM_STATE, GROUPS_PER_BLOCK, SSM_GROUP)
    diag = jnp.sum(d6 * _eye_mask()[None, None, :, None, :, None], axis=4)
    diag = diag.transpose(1, 0, 2, 4, 3).reshape(2, nb * GROUPS_PER_BLOCK, SSM_GROUP, SSM_STATE)
    return diag[0], -diag[1]


def _unshard_cols(g):
    s, k, n = g.shape
    return g.transpose(1, 0, 2).reshape(k, s * n)


def _shard_cols(w):
    k, n = w.shape
    return w.reshape(k, N_DEV, n // N_DEV).transpose(1, 0, 2)


def _pack(arrays):
    chunks, offs, row = [], [], 0
    for a in arrays:
        flat = a.reshape(-1).astype(F32)
        rows = -(-flat.shape[0] // LANES)
        chunks.append(jnp.pad(flat, (0, rows * LANES - flat.shape[0])))
        offs.append((row, rows))
        row += rows
    pad_rows = (-row) % SUBLANES
    if pad_rows:
        chunks.append(jnp.zeros((pad_rows * LANES,), F32))
    return jnp.concatenate(chunks).reshape(row + pad_rows, LANES), offs


def _unpack(packed, offs, shapes):
    out = []
    for (row, rows), shp in zip(offs, shapes):
        size = 1
        for s in shp:
            size *= s
        out.append(packed[row: row + rows].reshape(-1)[:size].reshape(shp))
    return out


def kernel(x, mix_norm, mlp_norm, mlp_w1, mlp_w2, ssm_log_dt, ssm_a_re, ssm_a_im, ssm_b_re, ssm_b_im, ssm_c_re, ssm_c_im, ssm_d, ssm_w_glu, kv_norm, w_kvf, b_f, attn_wq, attn_wo, final_norm, loss_target, m_mix_norm, m_mlp_norm, m_mlp_w1, m_mlp_w2, m_ssm_log_dt, m_ssm_a_re, m_ssm_a_im, m_ssm_b_re, m_ssm_b_im, m_ssm_c_re, m_ssm_c_im, m_ssm_d, m_ssm_w_glu, m_kv_norm, m_w_kvf, m_b_f, m_attn_wq, m_attn_wo, m_final_norm, v_mix_norm, v_mlp_norm, v_mlp_w1, v_mlp_w2, v_ssm_log_dt, v_ssm_a_re, v_ssm_a_im, v_ssm_b_re, v_ssm_b_im, v_ssm_c_re, v_ssm_c_im, v_ssm_d, v_ssm_w_glu, v_kv_norm, v_w_kvf, v_b_f, v_attn_wq, v_attn_wo, v_final_norm):
    n_rows, d_model = x.shape[1], x.shape[2]
    depth = mix_norm.shape[0]
    n_a = ssm_log_dt.shape[0]
    n_b = depth - n_a
    n_heads = d_model // HEAD_DIM
    n_groups = d_model // SSM_GROUP
    nb = n_groups // GROUPS_PER_BLOCK
    kvf_cols = 2 * d_model + n_heads
    kvf_pad = 2 * d_model + LANES

    g_w1, g_w2, g_glu, g_kvf, g_wq, g_wo, g_d = all_gather(
        "gather_weights",
        [mlp_w1.astype(BF16), mlp_w2.astype(BF16), ssm_w_glu.astype(BF16), w_kvf.astype(BF16),
         attn_wq.astype(BF16), attn_wo.astype(BF16), ssm_d])
    w1 = [_unshard_cols(g_w1[:, i]) for i in range(depth)]
    w2 = [g_w2[:, i].reshape(-1, d_model) for i in range(depth)]
    wglu = [_unshard_cols(g_glu[:, i]) for i in range(n_a)]
    wkvf = _unshard_cols(g_kvf)
    wkvf = jnp.pad(wkvf, ((0, 0), (0, kvf_pad - kvf_cols)))
    wq = [g_wq[:, j].reshape(-1, d_model) for j in range(n_b)]
    wo = [g_wo[:, j].reshape(-1, d_model) for j in range(n_b)]
    d_skip = [g_d[:, i].reshape(1, d_model) for i in range(n_a)]

    row = lambda a: a.reshape(1, -1)
    col = lambda a: a.reshape(-1, 1)

    h = x[0]
    saved = []
    k_ext = v_ext = None
    ones3 = jnp.ones((n_heads, n_rows, 3), F32)
    for i in range(depth):
        sv = {"h": h}
        hn = rms_fwd(f"mix_norm_{i}", h, row(mix_norm[i]))
        sv["hn"] = hn
        if i < n_a:
            ldt = col(jnp.repeat(ssm_log_dt[i], SSM_STATE))
            prm = [ldt, col(ssm_a_re[i]), col(ssm_a_im[i]), ssm_b_re[i].reshape(-1, SSM_GROUP), ssm_b_im[i].reshape(-1, SSM_GROUP)]
            sv["prm"] = prm
            lr, li, bbr, bbi = rowcall(f"s5_prep_{i}", _s5_discretise, prm, [], [(1, F32), (1, F32), (SSM_GROUP, F32), (SSM_GROUP, F32)])
            lam = jnp.stack([lr.reshape(nb, BLOCK_STATE), li.reshape(nb, BLOCK_STATE)], axis=1)
            wb = _b_blocks(bbr, bbi)
            wc = _c_blocks(ssm_c_re[i], ssm_c_im[i])
            bu = bmm(f"s5_bu_{i}", hn, wb, "nn")
            st = s5_scan(f"s5_scan_{i}", bu, lam, reverse=False)
            yss = bmm(f"s5_y_{i}", st, wc, "nn")
            z = rowcall(f"s5_gelu_{i}", lambda ys, u, dsk: _gelu(ys + dsk * u), [yss, hn], [d_skip[i]], [(d_model, F32)])[0]
            zw = matmul(f"s5_glu_{i}", z, wglu[i])
            h1 = rowcall(f"s5_gate_{i}", lambda hh, zz: hh + zz[:, :d_model] * _sigmoid(zz[:, d_model:]), [h, zw], [], [(d_model, F32)])[0]
            sv.update(lam=lam, wb=wb, wc=wc, st=st, yss=yss, z=z, zw=zw)
        else:
            j = i - n_a
            qh = _to_heads(matmul(f"attn_q_{j}", hn, wq[j], scale=LOG2E * HEAD_DIM ** -0.5))
            ol = fox_fwd(f"attn_fwd_{j}", _ext([qh, ones3]), k_ext, v_ext)
            oh, lse3 = ol[:, :, :HEAD_DIM], ol[:, :, HEAD_DIM: HEAD_DIM + 3]
            o2 = _from_heads(oh)
            h1 = matmul(f"attn_o_{j}", o2, wo[j], resid=h)
            sv.update(qh=qh, oh=oh, lse3=lse3, o2=o2)
        h2n = rms_fwd(f"mlp_norm_{i}", h1, row(mlp_norm[i]))
        ap = matmul(f"mlp_up_{i}", h2n, w1[i])
        h = matmul(f"mlp_down_{i}", ap, w2[i], a_fn=_sqrelu, resid=h1)
        sv.update(h1=h1, h2n=h2n, ap=ap)
        saved.append(sv)
        if i == n_a - 1:
            h_mid = h
            hk = rms_fwd("kv_norm", h, row(kv_norm))
            kvf = matmul("kvf_proj", hk, wkvf)
            fl = kvf[:, 2 * d_model:]
            bfp = jnp.pad(row(b_f), ((0, 0), (0, LANES - n_heads)))
            f_parts = jnp.stack([p[:, :n_heads].T for p in cum_logf("cum_logf", fl, bfp)], axis=-1)
            k_ext = _ext([_to_heads(kvf[:, :d_model]), f_parts, -ones3, ones3[:, :, :1]])
            v_ext = _ext([_to_heads(kvf[:, d_model: 2 * d_model]), ones3[:, :, :1], -ones3])

    def loss_fn(hh, tgt, g):
        y, vjp = jax.vjp(_rms, hh, g)
        err = y - tgt
        part = 0.5 * jnp.sum(jnp.mean(err * err, axis=-1, keepdims=True), axis=0, keepdims=True)
        dh, dg = vjp(err * (1.0 / d_model))
        return dh, jnp.broadcast_to(part, (1, LANES)), dg

    dh, loss_part, d_final = rowcall("loss_head", loss_fn, [h, loss_target[0]], [row(final_norm)],
                                     [(d_model, F32)], [(1, LANES), (1, d_model)])

    g_mix, g_mlpn = [None] * depth, [None] * depth
    g_w1f, g_w2f = [None] * depth, [None] * depth
    g_ssm = [None] * n_a
    g_wqf, g_wof = [None] * n_b, [None] * n_b
    dk_acc, dv_acc, df_plus, df_minus = [], [], [], []
    g_kv = None
    for i in reversed(range(depth)):
        sv = saved[i]
        if i == n_a - 1:
            dfl, db_f = cum_logf_bwd("cum_logf_bwd", fl, bfp, df_plus, df_minus)
            dk = rowcall("dk_sum", lambda a, b: a + b, [_from_heads(dk_acc[0]), _from_heads(dk_acc[1])], [], [(d_model, F32)])[0] if len(dk_acc) == 2 else _from_heads(dk_acc[0])
            dv = rowcall("dv_sum", lambda a, b: a + b, [_from_heads(dv_acc[0]), _from_heads(dv_acc[1])], [], [(d_model, F32)])[0] if len(dv_acc) == 2 else _from_heads(dv_acc[0])
            dkvf = jnp.concatenate([dk, dv, dfl], axis=1)
            dhk = matmul("kvf_dx", dkvf, wkvf, "nt")
            d_wkvf = matmul("kvf_dw", hk, dkvf, "tn")
            dh, d_kvn = rms_bwd("kv_norm_bwd", h_mid, row(kv_norm), dhk, add=dh)
            g_kv = (d_wkvf[:, :kvf_cols], d_kvn, db_f[:, :n_heads])
        dap = matmul(f"mlp_down_dx_{i}", dh, w2[i], "nt", post=lambda acc, apt: acc * (2.0 * jnp.maximum(apt, 0.0)), post_arg=sv["ap"])
        g_w2f[i] = matmul(f"mlp_down_dw_{i}", sv["ap"], dh, "tn", a_fn=_sqrelu)
        dh2n = matmul(f"mlp_up_dx_{i}", dap, w1[i], "nt")
        g_w1f[i] = matmul(f"mlp_up_dw_{i}", sv["h2n"], dap, "tn")
        dh1, g_mlpn[i] = rms_bwd(f"mlp_norm_bwd_{i}", sv["h1"], row(mlp_norm[i]), dh2n, add=dh)
        if i < n_a:
            def glu_bwd(zz, dd):
                val, gate = zz[:, :d_model], zz[:, d_model:]
                sg = _sigmoid(gate)
                return jnp.concatenate([dd * sg, dd * val * sg * (1.0 - sg)], axis=1)

            dzw = rowcall(f"s5_gate_bwd_{i}", glu_bwd, [sv["zw"], dh1], [], [(2 * d_model, F32)])[0]
            dz = matmul(f"s5_glu_dx_{i}", dzw, wglu[i], "nt")
            d_wglu = matmul(f"s5_glu_dw_{i}", sv["z"], dzw, "tn")

            def gelu_bwd(ys, u, dzz, dsk):
                _, vjp = jax.vjp(lambda yy: _gelu(yy), ys + dsk * u)
                dy = vjp(dzz)[0]
                return dy, dy * dsk, jnp.sum(dy * u, axis=0, keepdims=True)

            dy, du_skip, d_dskip = rowcall(f"s5_gelu_bwd_{i}", gelu_bwd, [sv["yss"], sv["hn"], dz], [d_skip[i]],
                                          [(d_model, F32), (d_model, F32)], [(1, d_model)])
            ds_out = bmm(f"s5_y_dx_{i}", dy, sv["wc"], "nt")
            d_wc = bmm_tn(f"s5_y_dw_{i}", sv["st"], dy, nb)
            gst = s5_scan(f"s5_scan_bwd_{i}", ds_out, sv["lam"], reverse=True)
            d_lam = s5_dlam(f"s5_dlam_{i}", gst, sv["st"], nb)
            du = bmm(f"s5_bu_dx_{i}", gst, sv["wb"], "nt")
            d_wb = bmm_tn(f"s5_bu_dw_{i}", sv["hn"], gst, nb)
            d_bbr, d_bbi = _b_blocks_t(d_wb)
            d_cre, d_cim = _c_blocks_t(d_wc)

            def prep_bwd(ldt, ar, ai, br, bi, dlr, dli, dbr, dbi):
                _, vjp = jax.vjp(_s5_discretise, ldt, ar, ai, br, bi)
                return vjp((dlr, dli, dbr, dbi))

            d_ldt, d_are, d_aim, d_bre, d_bim = rowcall(
                f"s5_prep_bwd_{i}", prep_bwd, sv["prm"] + [col(d_lam[:, 0]), col(d_lam[:, 1]), d_bbr, d_bbi], [],
                [(1, F32), (1, F32), (1, F32), (SSM_GROUP, F32), (SSM_GROUP, F32)])
            d_logdt = rowcall(f"s5_dlogdt_{i}", lambda a: jnp.sum(a, axis=1, keepdims=True), [d_ldt.reshape(n_groups, SSM_STATE)], [], [(1, F32)])[0]
            dhn = rowcall(f"s5_du_sum_{i}", lambda a, b: a + b, [du, du_skip], [], [(d_model, F32)])[0]
            g_ssm[i] = dict(log_dt=d_logdt.reshape(n_groups), a_re=d_are.reshape(n_groups, SSM_STATE), a_im=d_aim.reshape(n_groups, SSM_STATE),
                            b_re=d_bre.reshape(n_groups, SSM_STATE, SSM_GROUP), b_im=d_bim.reshape(n_groups, SSM_STATE, SSM_GROUP),
                            c_re=d_cre, c_im=d_cim, d=d_dskip, w_glu=d_wglu)
        else:
            j = i - n_a
            do2 = matmul(f"attn_o_dx_{j}", dh1, wo[j], "nt")
            g_wof[j] = matmul(f"attn_o_dw_{j}", sv["o2"], dh1, "tn")
            doh = _to_heads(do2)
            delta3 = attn_delta(f"attn_delta_{j}", sv["oh"].reshape(-1, HEAD_DIM), doh.reshape(-1, HEAD_DIM))
            delta3 = delta3.reshape(n_heads, n_rows, LANES)[:, :, :3]
            dq_ext, dk_ext, dv_ext = fox_bwd(f"attn_bwd_{j}", _ext([sv["qh"], ones3, sv["lse3"]]),
                                             _ext([doh, jnp.zeros_like(ones3[:, :, :1]), delta3]), k_ext, v_ext)
            dk_acc.append(dk_ext[:, :, :HEAD_DIM])
            dv_acc.append(dv_ext[:, :, :HEAD_DIM])
            df_plus.append(jnp.pad(dq_ext[:, :, ROWSUM_LANE].T, ((0, 0), (0, LANES - n_heads))))
            df_minus.append(jnp.pad(dk_ext[:, :, HEAD_DIM].T, ((0, 0), (0, LANES - n_heads))))
            dq2 = _from_heads(dq_ext[:, :, :HEAD_DIM])
            dhn = matmul(f"attn_q_dx_{j}", dq2, wq[j], "nt", scale=HEAD_DIM ** -0.5)
            g_wqf[j] = matmul(f"attn_q_dw_{j}", sv["hn"], dq2, "tn", scale=HEAD_DIM ** -0.5)
        dh, g_mix[i] = rms_bwd(f"mix_norm_bwd_{i}", sv["h"], row(mix_norm[i]), dhn, add=dh1)
    grad_x = dh[None]

    stack = lambda xs: jnp.stack(xs, axis=1)
    parts = all_to_all("scatter_grads", [
        stack([_shard_cols(g) for g in g_w1f]),
        stack([g.reshape(N_DEV, -1, d_model) for g in g_w2f]),
        stack([_shard_cols(g["w_glu"]) for g in g_ssm]),
        _shard_cols(g_kv[0]),
        stack([g.reshape(N_DEV, -1, d_model) for g in g_wqf]),
        stack([g.reshape(N_DEV, -1, d_model) for g in g_wof]),
        stack([g["d"].reshape(N_DEV, -1) for g in g_ssm]),
    ])
    small_names = ["mix_norm", "mlp_norm", "ssm_log_dt", "ssm_a_re", "ssm_a_im", "ssm_b_re", "ssm_b_im", "ssm_c_re", "ssm_c_im",
                   "kv_norm", "b_f", "final_norm"]
    small_grads = [jnp.concatenate(g_mix, axis=0), jnp.concatenate(g_mlpn, axis=0)]
    small_grads += [jnp.stack([g[kk] for g in g_ssm]) for kk in ("log_dt", "a_re", "a_im", "b_re", "b_im", "c_re", "c_im")]
    small_grads += [g_kv[1], g_kv[2], d_final]
    small_w = [mix_norm, mlp_norm, ssm_log_dt, ssm_a_re, ssm_a_im, ssm_b_re, ssm_b_im, ssm_c_re, ssm_c_im, kv_norm, b_f, final_norm]
    small_m = [m_mix_norm, m_mlp_norm, m_ssm_log_dt, m_ssm_a_re, m_ssm_a_im, m_ssm_b_re, m_ssm_b_im, m_ssm_c_re, m_ssm_c_im, m_kv_norm, m_b_f, m_final_norm]
    small_v = [v_mix_norm, v_mlp_norm, v_ssm_log_dt, v_ssm_a_re, v_ssm_a_im, v_ssm_b_re, v_ssm_b_im, v_ssm_c_re, v_ssm_c_im, v_kv_norm, v_b_f, v_final_norm]
    loss_slot = jnp.zeros((LANES,), F32)
    packed_g, offs = _pack(small_grads + [loss_part])
    packed_w, _ = _pack(small_w + [loss_slot])
    packed_m, _ = _pack(small_m + [loss_slot])
    packed_v, _ = _pack(small_v + [loss_slot])
    (small_parts,) = all_gather("gather_small_grads", [packed_g])

    res = {}

    def update(nm, part, w, m, v):
        shp = w.shape
        as2d = lambda a: a.reshape(-1, shp[-1])
        outs = adamw(f"adamw_{nm}", part.reshape((N_DEV,) + as2d(w).shape), as2d(w), as2d(m), as2d(v))
        res[nm] = [o.reshape(shp) for o in outs]

    update("mlp_w1", parts[0], mlp_w1, m_mlp_w1, v_mlp_w1)
    update("mlp_w2", parts[1], mlp_w2, m_mlp_w2, v_mlp_w2)
    update("ssm_w_glu", parts[2], ssm_w_glu, m_ssm_w_glu, v_ssm_w_glu)
    update("w_kvf", parts[3], w_kvf, m_w_kvf, v_w_kvf)
    update("attn_wq", parts[4], attn_wq, m_attn_wq, v_attn_wq)
    update("attn_wo", parts[5], attn_wo, m_attn_wo, v_attn_wo)
    update("ssm_d", parts[6], ssm_d, m_ssm_d, v_ssm_d)
    small_out = adamw("adamw_small", small_parts, packed_w, packed_m, packed_v)
    shapes = [w.shape for w in small_w] + [(LANES,)]
    unpacked = [_unpack(o, offs, shapes) for o in small_out]
    for idx, nm in enumerate(small_names):
        res[nm] = [u[idx] for u in unpacked]
    loss = unpacked[0][-1][0]

    order = ["mix_norm", "mlp_norm", "mlp_w1", "mlp_w2", "ssm_log_dt", "ssm_a_re", "ssm_a_im", "ssm_b_re", "ssm_b_im", "ssm_c_re",
             "ssm_c_im", "ssm_d", "ssm_w_glu", "kv_norm", "w_kvf", "b_f", "attn_wq", "attn_wo", "final_norm"]
    out = [loss, grad_x]
    for kind in range(4):
        out += [res[nm][kind] for nm in order]
    return tuple(out)
```

```python
import functools

import jax
import jax.numpy as jnp
from jax import lax
from jax.experimental import pallas as pl
from jax.experimental.pallas import tpu as pltpu

F32 = jnp.float32
BF16 = jnp.bfloat16
HIGHEST = lax.Precision.HIGHEST

V7X_VMEM_BYTES = 64 << 20
VMEM_LIMIT_BYTES = (V7X_VMEM_BYTES * 3) // 4
LANES = 128
SUBLANES = 8

N_DEV = 8
RMS_EPS = 1e-6
SSM_GROUP = 16
SSM_STATE = 64
HEAD_DIM = 64
GROUPS_PER_BLOCK = LANES // SSM_GROUP
BLOCK_STATE = GROUPS_PER_BLOCK * SSM_STATE
BLOCK_COLS = 2 * BLOCK_STATE
NEG_BIG = -1e30
LOG2E = 1.4426950408889634

ADAM_LR = 0.001
ADAM_B1 = 0.9
ADAM_B2 = 0.999
ADAM_EPS = 1e-08
ADAM_WD = 0.01
ADAM_STEP = 10

TILES = {"row": 512, "mm": (1024, 512, 1024), "blk": 512, "scan": 256, "cum": 256, "attn": 512, "adam": 256}


def _pick(dim, pref, align=LANES):
    if dim <= pref:
        return dim
    for a in (align, SUBLANES):
        d = (pref // a) * a
        while d >= a:
            if dim % d == 0:
                return d
            d -= a
    return dim


def _params(*sem):
    return pltpu.CompilerParams(dimension_semantics=sem, vmem_limit_bytes=VMEM_LIMIT_BYTES)


def rowcall(name, fn, rows, consts, out_rows, out_accs=(), tm=None):
    n_rows = rows[0].shape[0]
    tm = _pick(n_rows, tm or TILES["row"], SUBLANES)
    nr, nc, no, na = len(rows), len(consts), len(out_rows), len(out_accs)

    def body(*refs):
        ins = [r[...] for r in refs[: nr + nc]]
        outs = fn(*ins)
        if not isinstance(outs, (tuple, list)):
            outs = (outs,)
        for r, o in zip(refs[nr + nc: nr + nc + no], outs[:no]):
            r[...] = o.astype(r.dtype)
        if na:
            i = pl.program_id(0)
            for r, o in zip(refs[nr + nc + no:], outs[no:]):
                @pl.when(i == 0)
                def _(r=r, o=o):
                    r[...] = o

                @pl.when(i > 0)
                def _(r=r, o=o):
                    r[...] += o

    in_specs = [pl.BlockSpec((tm, a.shape[1]), lambda i: (i, 0)) for a in rows]
    in_specs += [pl.BlockSpec(c.shape, lambda i, n=c.ndim: (0,) * n) for c in consts]
    out_shape = [jax.ShapeDtypeStruct((n_rows, c), dt) for c, dt in out_rows]
    out_specs = [pl.BlockSpec((tm, c), lambda i: (i, 0)) for c, _ in out_rows]
    out_shape += [jax.ShapeDtypeStruct(s, F32) for s in out_accs]
    out_specs += [pl.BlockSpec(s, lambda i, n=len(s): (0,) * n) for s in out_accs]
    res = pl.pallas_call(
        body, name=name, grid=(n_rows // tm,), in_specs=in_specs, out_specs=out_specs, out_shape=out_shape,
        compiler_params=_params("arbitrary" if na else "parallel"),
    )(*rows, *consts)
    return res


_DIMS = {"nn": (((1,), (0,)), ((), ())), "nt": (((1,), (1,)), ((), ())), "tn": (((0,), (0,)), ((), ()))}


def matmul(name, a, b, mode="nn", *, a_fn=None, scale=None, resid=None, post=None, post_arg=None,
           bm=None, bn=None, bk=None):
    if mode == "nn":
        (m, k), (k2, n) = a.shape, b.shape
    elif mode == "nt":
        (m, k), (n, k2) = a.shape, b.shape
    else:
        (k, m), (k2, n) = a.shape, b.shape
    assert k == k2, (name, a.shape, b.shape, mode)
    bm, bn, bk = _pick(m, bm or TILES["mm"][0]), _pick(n, bn or TILES["mm"][1]), _pick(k, bk or TILES["mm"][2])
    nk = k // bk
    a_spec = pl.BlockSpec((bk, bm), lambda i, j, kk: (kk, i)) if mode == "tn" else pl.BlockSpec((bm, bk), lambda i, j, kk: (i, kk))
    b_spec = pl.BlockSpec((bn, bk), lambda i, j, kk: (j, kk)) if mode == "nt" else pl.BlockSpec((bk, bn), lambda i, j, kk: (kk, j))
    mn_spec = pl.BlockSpec((bm, bn), lambda i, j, kk: (i, j))
    extra = [x for x in (resid, post_arg) if x is not None]
    has_resid, has_post = resid is not None, post is not None

    def body(*refs):
        a_ref, b_ref = refs[0], refs[1]
        ex = refs[2: 2 + len(extra)]
        o_ref = refs[2 + len(extra)]
        av = a_ref[...]
        if a_fn is not None:
            av = a_fn(av.astype(F32))
        p = lax.dot_general(av.astype(BF16), b_ref[...].astype(BF16), _DIMS[mode], preferred_element_type=F32)

        def finish(acc):
            if scale is not None:
                acc = acc * scale
            idx = 0
            if has_resid:
                acc = acc + ex[idx][...]
                idx += 1
            if has_post:
                acc = post(acc, ex[idx][...])
            o_ref[...] = acc.astype(o_ref.dtype)

        if nk == 1:
            finish(p)
        else:
            acc_ref = refs[-1]
            kk = pl.program_id(2)

            @pl.when(kk == 0)
            def _():
                acc_ref[...] = p

            @pl.when(kk > 0)
            def _():
                acc_ref[...] += p

            @pl.when(kk == nk - 1)
            def _():
                finish(acc_ref[...])

    return pl.pallas_call(
        body, name=name, grid=(m // bm, n // bn, nk),
        in_specs=[a_spec, b_spec] + [mn_spec] * len(extra), out_specs=mn_spec,
        out_shape=jax.ShapeDtypeStruct((m, n), F32),
        scratch_shapes=[pltpu.VMEM((bm, bn), F32)] if nk > 1 else [],
        compiler_params=_params("parallel", "parallel", "arbitrary"),
    )(a, b, *extra)


S5_DOT_PASSES = 1


def _block_dot(a, b, mode):
    if S5_DOT_PASSES == 6:
        return lax.dot_general(a, b, _DIMS[mode], precision=HIGHEST, preferred_element_type=F32)
    dot = lambda x, y: lax.dot_general(x, y, _DIMS[mode], preferred_element_type=F32)
    a_hi, b_hi = a.astype(BF16), b.astype(BF16)
    if S5_DOT_PASSES == 1:
        return dot(a_hi, b_hi)
    a_lo, b_lo = (a - a_hi.astype(F32)).astype(BF16), (b - b_hi.astype(F32)).astype(BF16)
    return dot(a_hi, b_hi) + (dot(a_hi, b_lo) + dot(a_lo, b_hi))


def bmm(name, a, w, mode, tm=None):
    n_rows = a.shape[0]
    nb, ka, kb = w.shape
    ca, co = (ka, kb) if mode == "nn" else (kb, ka)
    assert a.shape[1] == nb * ca
    tm = _pick(n_rows, tm or TILES["blk"], SUBLANES)

    def body(a_ref, w_ref, o_ref):
        o_ref[...] = _block_dot(a_ref[...], w_ref[...], mode)

    return pl.pallas_call(
        body, name=name, grid=(n_rows // tm, nb),
        in_specs=[pl.BlockSpec((tm, ca), lambda i, b: (i, b)), pl.BlockSpec((None, ka, kb), lambda i, b: (b, 0, 0))],
        out_specs=pl.BlockSpec((tm, co), lambda i, b: (i, b)),
        out_shape=jax.ShapeDtypeStruct((n_rows, nb * co), F32),
        compiler_params=_params("parallel", "parallel"),
    )(a, w)


def bmm_tn(name, a, b, nb, tk=None):
    n_rows = a.shape[0]
    ka, kb = a.shape[1] // nb, b.shape[1] // nb
    tk = _pick(n_rows, tk or TILES["blk"], SUBLANES)

    def body(a_ref, b_ref, o_ref):
        p = _block_dot(a_ref[...], b_ref[...], "tn")
        kk = pl.program_id(1)

        @pl.when(kk == 0)
        def _():
            o_ref[...] = p

        @pl.when(kk > 0)
        def _():
            o_ref[...] += p

    return pl.pallas_call(
        body, name=name, grid=(nb, n_rows // tk),
        in_specs=[pl.BlockSpec((tk, ka), lambda bb, kk: (kk, bb)), pl.BlockSpec((tk, kb), lambda bb, kk: (kk, bb))],
        out_specs=pl.BlockSpec((None, ka, kb), lambda bb, kk: (bb, 0, 0)),
        out_shape=jax.ShapeDtypeStruct((nb, ka, kb), F32),
        compiler_params=_params("parallel", "arbitrary"),
    )(a, b)


def _cmul(ar, ai, br, bi):
    return ar * br - ai * bi, ar * bi + ai * br


def _scan_tables(lam_ref, reverse):
    shape = (SUBLANES, BLOCK_STATE)
    lr = jnp.broadcast_to(lam_ref[0:1, :], shape)
    li = jnp.broadcast_to(lam_ref[1:2, :], shape)
    if reverse:
        li = -li
    row = lax.broadcasted_iota(jnp.int32, shape, 0)
    tt = (SUBLANES - 1 - row) if reverse else row
    l1 = (lr, li)
    l2 = _cmul(*l1, *l1)
    l4 = _cmul(*l2, *l2)
    pr, pi = l1
    for bit, lp in enumerate((l1, l2, l4)):
        qr, qi = _cmul(pr, pi, *lp)
        on = ((tt >> bit) & 1) == 1
        pr, pi = jnp.where(on, qr, pr), jnp.where(on, qi, pi)
    steps = []
    for d, lp in ((1, l1), (2, l2), (4, l4)):
        ok = tt >= d
        steps.append((d, jnp.where(ok, lp[0], 0.0), jnp.where(ok, lp[1], 0.0)))
    return steps, (pr, pi)


def s5_scan(name, x, lam, reverse, tm=None):
    n_rows = x.shape[0]
    nb = lam.shape[0]
    tm = _pick(n_rows, tm or TILES["scan"], SUBLANES)
    nt = n_rows // tm
    ng = tm // SUBLANES
    last = 0 if reverse else SUBLANES - 1

    def body(x_ref, lam_ref, o_ref, carry_ref):
        it = pl.program_id(1)
        steps, (pr, pi) = _scan_tables(lam_ref, reverse)

        @pl.when(it == 0)
        def _():
            carry_ref[...] = jnp.zeros_like(carry_ref)

        def group(r, carry):
            cr, ci = carry
            rr = (ng - 1 - r) if reverse else r
            off = pl.multiple_of(rr * SUBLANES, SUBLANES)
            xr = x_ref[pl.ds(off, SUBLANES), 0:BLOCK_STATE]
            xi = x_ref[pl.ds(off, SUBLANES), BLOCK_STATE:BLOCK_COLS]
            for d, mr, mi in steps:
                sh = (SUBLANES - d) if reverse else d
                yr, yi = pltpu.roll(xr, sh, 0), pltpu.roll(xi, sh, 0)
                xr, xi = xr + mr * yr - mi * yi, xi + mr * yi + mi * yr
            xr, xi = xr + pr * cr - pi * ci, xi + pr * ci + pi * cr
            o_ref[pl.ds(off, SUBLANES), 0:BLOCK_STATE] = xr
            o_ref[pl.ds(off, SUBLANES), BLOCK_STATE:BLOCK_COLS] = xi
            shape = (SUBLANES, BLOCK_STATE)
            return jnp.broadcast_to(xr[last:last + 1, :], shape), jnp.broadcast_to(xi[last:last + 1, :], shape)

        cr, ci = lax.fori_loop(0, ng, group, (carry_ref[0], carry_ref[1]))
        carry_ref[0] = cr
        carry_ref[1] = ci

    tile = (lambda b, it: (nt - 1 - it, b)) if reverse else (lambda b, it: (it, b))
    return pl.pallas_call(
        body, name=name, grid=(nb, nt),
        in_specs=[pl.BlockSpec((tm, BLOCK_COLS), tile), pl.BlockSpec((None, 2, BLOCK_STATE), lambda b, it: (b, 0, 0))],
        out_specs=pl.BlockSpec((tm, BLOCK_COLS), tile),
        out_shape=jax.ShapeDtypeStruct(x.shape, F32),
        scratch_shapes=[pltpu.VMEM((2, SUBLANES, BLOCK_STATE), F32)],
        compiler_params=_params("parallel", "arbitrary"),
    )(x, lam)


def s5_dlam(name, g, s, nb, tm=None):
    n_rows = g.shape[0]
    tm = _pick(n_rows, tm or TILES["scan"], SUBLANES)
    nt = n_rows // tm
    ng = tm // SUBLANES
    shape = (SUBLANES, BLOCK_STATE)

    def body(g_ref, s_ref, o_ref, carry_ref, acc_ref):
        it = pl.program_id(1)

        @pl.when(it == 0)
        def _():
            carry_ref[...] = jnp.zeros_like(carry_ref)
            acc_ref[...] = jnp.zeros_like(acc_ref)

        first = lax.broadcasted_iota(jnp.int32, shape, 0) == 0

        def group(r, carry):
            cr, ci, ar, ai = carry
            off = pl.multiple_of(r * SUBLANES, SUBLANES)
            sr = s_ref[pl.ds(off, SUBLANES), 0:BLOCK_STATE]
            si = s_ref[pl.ds(off, SUBLANES), BLOCK_STATE:BLOCK_COLS]
            gr = g_ref[pl.ds(off, SUBLANES), 0:BLOCK_STATE]
            gi = g_ref[pl.ds(off, SUBLANES), BLOCK_STATE:BLOCK_COLS]
            pr = jnp.where(first, cr, pltpu.roll(sr, 1, 0))
            pi = jnp.where(first, ci, pltpu.roll(si, 1, 0))
            ar = ar + gr * pr + gi * pi
            ai = ai + gi * pr - gr * pi
            return (jnp.broadcast_to(sr[SUBLANES - 1:SUBLANES, :], shape),
                    jnp.broadcast_to(si[SUBLANES - 1:SUBLANES, :], shape), ar, ai)

        cr, ci, ar, ai = lax.fori_loop(0, ng, group, (carry_ref[0], carry_ref[1], acc_ref[0], acc_ref[1]))
        carry_ref[0] = cr
        carry_ref[1] = ci
        acc_ref[0] = ar
        acc_ref[1] = ai

        @pl.when(it == nt - 1)
        def _():
            o_ref[0:1, :] = jnp.sum(ar, axis=0, keepdims=True)
            o_ref[1:2, :] = jnp.sum(ai, axis=0, keepdims=True)

    tile = lambda b, it: (it, b)
    return pl.pallas_call(
        body, name=name, grid=(nb, nt),
        in_specs=[pl.BlockSpec((tm, BLOCK_COLS), tile), pl.BlockSpec((tm, BLOCK_COLS), tile)],
        out_specs=pl.BlockSpec((None, 2, BLOCK_STATE), lambda b, it: (b, 0, 0)),
        out_shape=jax.ShapeDtypeStruct((nb, 2, BLOCK_STATE), F32),
        scratch_shapes=[pltpu.VMEM((2,) + shape, F32), pltpu.VMEM((2,) + shape, F32)],
        compiler_params=_params("parallel", "arbitrary"),
    )(g, s)


def _rms(x, g):
    return x * lax.rsqrt(jnp.mean(x * x, axis=-1, keepdims=True) + RMS_EPS) * g


def _sigmoid(x):
    return 1.0 / (1.0 + jnp.exp(-x))


def _gelu(x):
    return 0.5 * x * (1.0 + jnp.tanh(0.7978845608028654 * (x + 0.044715 * (x * x * x))))


def _log_sigmoid(x):
    return jnp.minimum(x, 0.0) - jnp.log(1.0 + jnp.exp(-jnp.abs(x)))


def _sqrelu(x):
    r = jnp.maximum(x, 0.0)
    return r * r


def _s5_discretise(ldt, ar, ai, br, bi):
    dt = jnp.exp(ldt)
    er = jnp.exp(ar * dt)
    lr, li = er * jnp.cos(ai * dt), er * jnp.sin(ai * dt)
    nr, ni = lr - 1.0, li
    den = ar * ar + ai * ai
    cr, ci = (nr * ar + ni * ai) / den, (ni * ar - nr * ai) / den
    return lr, li, cr * br - ci * bi, cr * bi + ci * br


def rms_fwd(name, x, g):
    return rowcall(name, _rms, [x], [g], [(x.shape[1], F32)])[0]


def rms_bwd(name, x, g, dy, add=None):
    def fn(x, dy, *rest):
        g = rest[-1]
        _, vjp = jax.vjp(_rms, x, g)
        dx, dg = vjp(dy)
        if add is not None:
            dx = dx + rest[0]
        return dx, dg

    rows = [x, dy] + ([add] if add is not None else [])
    return rowcall(name, fn, rows, [g], [(x.shape[1], F32)], [g.shape])


def _split3(x):
    hi = x.astype(BF16).astype(F32)
    r = x - hi
    mid = r.astype(BF16).astype(F32)
    return hi, mid, (r - mid).astype(BF16).astype(F32)


def cum_logf(name, fl, bf, tm=None):
    n_rows, w = fl.shape
    tm = _pick(n_rows, tm or TILES["cum"], SUBLANES)

    def body(fl_ref, bf_ref, hi_ref, mid_ref, lo_ref, carry_ref):
        it = pl.program_id(0)

        @pl.when(it == 0)
        def _():
            carry_ref[...] = jnp.zeros_like(carry_ref)

        ls = _log_sigmoid(fl_ref[...] + bf_ref[...])
        tri = (lax.broadcasted_iota(jnp.int32, (tm, tm), 0) >= lax.broadcasted_iota(jnp.int32, (tm, tm), 1)).astype(F32)
        c = jnp.dot(tri, ls, precision=HIGHEST, preferred_element_type=F32) + carry_ref[0:1, :]
        carry_ref[...] = jnp.broadcast_to(c[tm - 1:tm, :], carry_ref.shape)
        hi_ref[...], mid_ref[...], lo_ref[...] = _split3(c * (-LOG2E))

    spec = pl.BlockSpec((tm, w), lambda i: (i, 0))
    return pl.pallas_call(
        body, name=name, grid=(n_rows // tm,),
        in_specs=[spec, pl.BlockSpec((1, w), lambda i: (0, 0))],
        out_specs=[spec] * 3, out_shape=[jax.ShapeDtypeStruct((n_rows, w), F32)] * 3,
        scratch_shapes=[pltpu.VMEM((SUBLANES, w), F32)],
        compiler_params=_params("arbitrary"),
    )(fl, bf)


def cum_logf_bwd(name, fl, bf, plus, minus, tm=None):
    n_rows, w = fl.shape
    tm = _pick(n_rows, tm or TILES["cum"], SUBLANES)
    nt = n_rows // tm
    n_p, n_m = len(plus), len(minus)

    def body(*refs):
        fl_ref, bf_ref = refs[0], refs[1]
        d_refs = refs[2: 2 + n_p + n_m]
        o_ref, db_ref, carry_ref = refs[2 + n_p + n_m:]
        it = pl.program_id(0)

        @pl.when(it == 0)
        def _():
            carry_ref[...] = jnp.zeros_like(carry_ref)

        d = None
        for r in d_refs[:n_p]:
            d = r[...] if d is None else d + r[...]
        for r in d_refs[n_p:]:
            d = -r[...] if d is None else d - r[...]
        tri = (lax.broadcasted_iota(jnp.int32, (tm, tm), 0) <= lax.broadcasted_iota(jnp.int32, (tm, tm), 1)).astype(F32)
        c = jnp.dot(tri, d, precision=HIGHEST, preferred_element_type=F32) + carry_ref[0:1, :]
        carry_ref[...] = jnp.broadcast_to(c[0:1, :], carry_ref.shape)
        dfl = c * _sigmoid(-(fl_ref[...] + bf_ref[...]))
        o_ref[...] = dfl
        part = jnp.sum(dfl, axis=0, keepdims=True)

        @pl.when(it == 0)
        def _():
            db_ref[...] = part

        @pl.when(it > 0)
        def _():
            db_ref[...] += part

    rev = lambda i: (nt - 1 - i, 0)
    return pl.pallas_call(
        body, name=name, grid=(nt,),
        in_specs=[pl.BlockSpec((tm, w), rev), pl.BlockSpec((1, w), lambda i: (0, 0))] + [pl.BlockSpec((tm, w), rev)] * (n_p + n_m),
        out_specs=[pl.BlockSpec((tm, w), rev), pl.BlockSpec((1, w), lambda i: (0, 0))],
        out_shape=[jax.ShapeDtypeStruct((n_rows, w), F32), jax.ShapeDtypeStruct((1, w), F32)],
        scratch_shapes=[pltpu.VMEM((SUBLANES, w), F32)],
        compiler_params=_params("arbitrary"),
    )(fl, bf, *plus, *minus)


ROWSUM_LANE = HEAD_DIM + 6
F_LANE = HEAD_DIM
LSE_LANE = HEAD_DIM + 3
SUM_LANE = HEAD_DIM
DELTA_LANE = HEAD_DIM + 1


def _lane_consts(pairs):
    lane = lax.broadcasted_iota(jnp.int32, (1, LANES), 1)
    out = jnp.zeros((1, LANES), F32)
    for lo, hi, v in pairs:
        out = jnp.where((lane >= lo) & (lane < hi), v, out)
    return out


def pack_heads(name, x, col_block, n_heads, consts, parts=(), parts_lane=0, tm=None):
    n_rows = x.shape[0]
    d = n_heads * HEAD_DIM
    assert n_heads % 2 == 0
    tm = _pick(n_rows, tm or TILES["row"], 2 * SUBLANES)
    n_parts = len(parts)

    def body(*refs):
        x_ref, c_ref = refs[0], refs[1]
        p_vals = [r[...] for r in refs[2: 2 + n_parts]]
        o_ref = refs[2 + n_parts]
        lane = lax.broadcasted_iota(jnp.int32, (tm, LANES), 1)
        tail0 = jnp.broadcast_to(c_ref[...], (tm, LANES))
        for h in range(n_heads):
            pair = x_ref[:, (h // 2) * LANES: (h // 2 + 1) * LANES].astype(F32)
            base = pair if h % 2 == 0 else pltpu.roll(pair, HEAD_DIM, 1)
            tail = tail0
            for kk, p in enumerate(p_vals):
                col = jnp.sum(jnp.where(lane == h, p, 0.0), axis=1, keepdims=True)
                tail = jnp.where(lane == parts_lane + kk, col, tail)
            o_ref[h] = jnp.where(lane < HEAD_DIM, base, tail).astype(BF16)

    return pl.pallas_call(
        body, name=name, grid=(n_rows // tm,),
        in_specs=[pl.BlockSpec((tm, d), lambda i: (i, col_block)), pl.BlockSpec((1, LANES), lambda i: (0, 0))]
        + [pl.BlockSpec((tm, LANES), lambda i: (i, 0))] * n_parts,
        out_specs=pl.BlockSpec((n_heads, tm, LANES), lambda i: (0, i, 0)),
        out_shape=jax.ShapeDtypeStruct((n_heads, n_rows, LANES), BF16),
        compiler_params=_params("parallel"),
    )(x, consts, *parts)


def unpack_heads(name, xs, extract_lane=None, tm=None):
    n_heads, n_rows, _ = xs[0].shape
    assert n_heads % 2 == 0
    tm = _pick(n_rows, tm or TILES["row"], SUBLANES)
    n = len(xs)

    def body(*refs):
        o_ref = refs[n]
        lane = lax.broadcasted_iota(jnp.int32, (tm, LANES), 1)
        picked = jnp.zeros((tm, LANES), F32)

        def head(h):
            v = refs[0][h]
            for r in refs[1:n]:
                v = v + r[h]
            return v

        for p in range(n_heads // 2):
            a, b = head(2 * p), head(2 * p + 1)
            o_ref[:, p * LANES: (p + 1) * LANES] = jnp.where(lane < HEAD_DIM, a, pltpu.roll(b, HEAD_DIM, 1))
            if extract_lane is not None:
                for hh, v in ((2 * p, a), (2 * p + 1, b)):
                    col = jnp.sum(jnp.where(lane == extract_lane, v, 0.0), axis=1, keepdims=True)
                    picked = jnp.where(lane == hh, col, picked)
        if extract_lane is not None:
            refs[n + 1][...] = picked

    d = n_heads * HEAD_DIM
    out_shape = [jax.ShapeDtypeStruct((n_rows, d), F32)]
    out_specs = [pl.BlockSpec((tm, d), lambda i: (i, 0))]
    if extract_lane is not None:
        out_shape.append(jax.ShapeDtypeStruct((n_rows, LANES), F32))
        out_specs.append(pl.BlockSpec((tm, LANES), lambda i: (i, 0)))
    return pl.pallas_call(
        body, name=name, grid=(n_rows // tm,),
        in_specs=[pl.BlockSpec((n_heads, tm, LANES), lambda i: (0, i, 0))] * n,
        out_specs=out_specs, out_shape=out_shape, compiler_params=_params("parallel"),
    )(*xs)


def fox_fwd(name, q_ext, k_ext, v_ext, t=None):
    nh, n_rows, w = q_ext.shape
    t = _pick(n_rows, t or TILES["attn"])
    nt = n_rows // t

    def body(q_ref, k_ref, v_ref, o_ref, qb_ref, m_ref, acc_ref):
        i = pl.program_id(1)
        m_ref[...] = jnp.full_like(m_ref, NEG_BIG)
        acc_ref[...] = jnp.zeros_like(acc_ref)
        q = q_ref[...]

        def block(j, diagonal):
            off = pl.multiple_of(j * t, t)
            s = lax.dot_general(q, k_ref[pl.ds(off, t), :], _DIMS["nt"], preferred_element_type=F32)
            if diagonal:
                keep = lax.broadcasted_iota(jnp.int32, (t, t), 0) >= lax.broadcasted_iota(jnp.int32, (t, t), 1)
                s = jnp.where(keep, s, NEG_BIG)
            m_prev = m_ref[...]
            m_new = jnp.maximum(m_prev, jnp.max(s, axis=1, keepdims=True))
            p = jnp.exp2(s - jnp.tile(m_new, (1, t // LANES)))
            acc_ref[...] = jnp.exp2(m_prev - m_new) * acc_ref[...] + jnp.dot(
                p.astype(BF16), v_ref[pl.ds(off, t), :], preferred_element_type=F32)
            m_ref[...] = m_new

        def off_diagonal(j, carry):
            block(j, False)
            return carry

        lax.fori_loop(0, i, off_diagonal, 0)
        block(i, True)
        acc = acc_ref[...]
        row_sum = acc[:, HEAD_DIM:HEAD_DIM + 1]
        hi, mid, lo = _split3(m_ref[:, 0:1] + jnp.log2(row_sum))
        lane = lax.broadcasted_iota(jnp.int32, (t, w), 1)
        o_ref[...] = acc / row_sum
        qb = jnp.where(lane == LSE_LANE, hi, jnp.where(lane == LSE_LANE + 1, mid, jnp.where(lane == LSE_LANE + 2, lo, q.astype(F32))))
        qb_ref[...] = qb.astype(BF16)

    whole = pl.BlockSpec((None, n_rows, w), lambda h, i: (h, 0, 0))
    tile = pl.BlockSpec((None, t, w), lambda h, i: (h, i, 0))
    return pl.pallas_call(
        body, name=name, grid=(nh, nt), in_specs=[tile, whole, whole], out_specs=[tile, tile],
        out_shape=[jax.ShapeDtypeStruct((nh, n_rows, w), F32), jax.ShapeDtypeStruct((nh, n_rows, w), BF16)],
        scratch_shapes=[pltpu.VMEM((t, w), F32), pltpu.VMEM((t, w), F32)],
        compiler_params=_params("parallel", "arbitrary"),
    )(q_ext, k_ext, v_ext)


def fox_bwd(name, q_ext, do_ext, k_ext, v_ext, t=None):
    nh, n_rows, w = q_ext.shape
    t = _pick(n_rows, t or TILES["attn"])
    nt = n_rows // t

    def body(q_ref, do_ref, k_ref, v_ref, dq_ref, dk_ref, dv_ref):
        j = pl.program_id(1)

        @pl.when(j == 0)
        def _():
            dq_ref[...] = jnp.zeros_like(dq_ref)

        dk_ref[...] = jnp.zeros_like(dk_ref)
        dv_ref[...] = jnp.zeros_like(dv_ref)
        kj, vj = k_ref[...], v_ref[...]

        def block(i, diagonal):
            off = pl.multiple_of(i * t, t)
            qi, doi = q_ref[pl.ds(off, t), :], do_ref[pl.ds(off, t), :]
            pt = jnp.exp2(lax.dot_general(kj, qi, _DIMS["nt"], preferred_element_type=F32))
            if diagonal:
                keep = lax.broadcasted_iota(jnp.int32, (t, t), 0) <= lax.broadcasted_iota(jnp.int32, (t, t), 1)
                pt = jnp.where(keep, pt, 0.0)
            dst = (pt * lax.dot_general(vj, doi, _DIMS["nt"], preferred_element_type=F32)).astype(BF16)
            dv_ref[...] += jnp.dot(pt.astype(BF16), doi, preferred_element_type=F32)
            dk_ref[...] += jnp.dot(dst, qi, preferred_element_type=F32)
            dq_ref[pl.ds(off, t), :] += lax.dot_general(dst, kj, _DIMS["tn"], preferred_element_type=F32)

        def off_diagonal(i, carry):
            block(i, False)
            return carry

        block(j, True)
        lax.fori_loop(j + 1, nt, off_diagonal, 0)
        lane = lax.broadcasted_iota(jnp.int32, (t, w), 1)
        dk_ref[...] = dk_ref[...] * jnp.where(lane < HEAD_DIM, 1.0 / LOG2E, 1.0)

    whole = pl.BlockSpec((None, n_rows, w), lambda h, j: (h, 0, 0))
    tile = pl.BlockSpec((None, t, w), lambda h, j: (h, j, 0))
    shape = jax.ShapeDtypeStruct((nh, n_rows, w), F32)
    return pl.pallas_call(
        body, name=name, grid=(nh, nt), in_specs=[whole, whole, tile, tile], out_specs=[whole, tile, tile],
        out_shape=[shape, shape, shape],
        compiler_params=_params("parallel", "arbitrary"),
    )(q_ext, do_ext, k_ext, v_ext)


def attn_delta(name, o, do):
    d_model = o.shape[1]
    head_of_col = lax.broadcasted_iota(jnp.int32, (d_model, LANES), 0) // HEAD_DIM
    sel = (head_of_col == lax.broadcasted_iota(jnp.int32, (d_model, LANES), 1)).astype(F32)

    def fn(a, b, s):
        return _split3(jnp.dot(a * b, s, precision=HIGHEST, preferred_element_type=F32))

    return rowcall(name, fn, [o, do], [sel], [(LANES, F32)] * 3)


_PEER_FLIPS = [(bx, by, bc) for bx in (0, 1) for by in (0, 1) for bc in (0, 1)][1:]


def _exchange(name, tensors, scatter):
    n = len(tensors)
    n_peer = len(_PEER_FLIPS)

    def body(*refs):
        ins, outs = refs[:n], refs[n: 2 * n]
        send_sems, recv_sems, local_sems = refs[2 * n:]
        x, y, c = lax.axis_index("x"), lax.axis_index("y"), lax.axis_index("c")
        me = 4 * x + 2 * y + c
        copies = []
        for t in range(n):
            src_me = ins[t].at[me] if scatter else ins[t]
            local = pltpu.make_async_copy(src_me, outs[t].at[me], local_sems.at[t])
            local.start()
            copies.append(local)
            for kk, (bx, by, bc) in enumerate(_PEER_FLIPS):
                px, py, pc = (1 - x if bx else x), (1 - y if by else y), (1 - c if bc else c)
                peer = 4 * px + 2 * py + pc
                out_cp = pltpu.make_async_remote_copy(
                    src_ref=ins[t].at[peer] if scatter else ins[t], dst_ref=outs[t].at[me],
                    send_sem=send_sems.at[t, kk], recv_sem=recv_sems.at[t, kk],
                    device_id=(px, py, pc), device_id_type=pl.DeviceIdType.MESH)
                out_cp.start()
                copies.append(pltpu.make_async_remote_copy(
                    src_ref=ins[t].at[peer] if scatter else ins[t], dst_ref=outs[t].at[peer],
                    send_sem=send_sems.at[t, kk], recv_sem=recv_sems.at[t, kk],
                    device_id=(px, py, pc), device_id_type=pl.DeviceIdType.MESH))
        for cp in copies:
            cp.wait()

    any_spec = pl.BlockSpec(memory_space=pl.ANY)
    out_shape = [jax.ShapeDtypeStruct(t.shape if scatter else (N_DEV,) + t.shape, t.dtype) for t in tensors]
    return pl.pallas_call(
        body, name=name, in_specs=[any_spec] * n, out_specs=[any_spec] * n, out_shape=out_shape,
        scratch_shapes=[pltpu.SemaphoreType.DMA((n, n_peer)), pltpu.SemaphoreType.DMA((n, n_peer)), pltpu.SemaphoreType.DMA((n,))],
        compiler_params=pltpu.CompilerParams(has_side_effects=True),
    )(*tensors)


def all_gather(name, tensors):
    return _exchange(name, tensors, scatter=False)


def all_to_all(name, tensors):
    return _exchange(name, tensors, scatter=True)


def adamw(name, parts, w, m, v, tr=None):
    n_rows, n_cols = w.shape
    tr = _pick(n_rows, tr or TILES["adam"], SUBLANES)
    c1 = 1.0 - ADAM_B1 ** ADAM_STEP
    c2 = 1.0 - ADAM_B2 ** ADAM_STEP

    def body(p_ref, w_ref, m_ref, v_ref, g_ref, d_ref, mo_ref, vo_ref):
        g = p_ref[0].astype(F32)
        for s in range(1, N_DEV):
            g = g + p_ref[s].astype(F32)
        mn = ADAM_B1 * m_ref[...] + (1.0 - ADAM_B1) * g
        vn = ADAM_B2 * v_ref[...] + (1.0 - ADAM_B2) * (g * g)
        g_ref[...] = g
        mo_ref[...] = mn
        vo_ref[...] = vn
        d_ref[...] = -ADAM_LR * ((mn / c1) / (jnp.sqrt(vn / c2) + ADAM_EPS) + ADAM_WD * w_ref[...])

    spec = pl.BlockSpec((tr, n_cols), lambda i: (i, 0))
    return pl.pallas_call(
        body, name=name, grid=(n_rows // tr,),
        in_specs=[pl.BlockSpec((N_DEV, tr, n_cols), lambda i: (0, i, 0)), spec, spec, spec],
        out_specs=[spec] * 4, out_shape=[jax.ShapeDtypeStruct(w.shape, F32)] * 4,
        compiler_params=_params("parallel"),
    )(parts, w, m, v)


def _eye_mask():
    return jnp.eye(GROUPS_PER_BLOCK, dtype=F32)


def _b_blocks(bbr, bbi):
    nb = bbr.shape[0] // (GROUPS_PER_BLOCK * SSM_STATE)
    eye = _eye_mask()[None, :, None, :, None]

    def one(z):
        z = z.reshape(nb, GROUPS_PER_BLOCK, SSM_STATE, SSM_GROUP).transpose(0, 1, 3, 2)
        return z[:, :, :, None, :] * eye

    w = jnp.stack([one(bbr), one(bbi)], axis=3)
    return w.reshape(nb, LANES, BLOCK_COLS)


def _b_blocks_t(dw):
    nb = dw.shape[0]
    d6 = dw.reshape(nb, GROUPS_PER_BLOCK, SSM_GROUP, 2, GROUPS_PER_BLOCK, SSM_STATE)
    diag = jnp.sum(d6 * _eye_mask()[None, :, None, None, :, None], axis=4)
    diag = diag.transpose(3, 0, 1, 4, 2).reshape(2, nb * GROUPS_PER_BLOCK * SSM_STATE, SSM_GROUP)
    return diag[0], diag[1]


def _c_blocks(c_re, c_im):
    nb = c_re.shape[0] // GROUPS_PER_BLOCK
    eye = _eye_mask()[None, :, None, :, None]

    def one(z):
        z = z.reshape(nb, GROUPS_PER_BLOCK, SSM_GROUP, SSM_STATE).transpose(0, 1, 3, 2)
        return z[:, :, :, None, :] * eye

    w = jnp.stack([one(c_re), -one(c_im)], axis=1)
    return w.reshape(nb, BLOCK_COLS, LANES)


def _c_blocks_t(dw):
    nb = dw.shape[0]
    d6 = dw.reshape(nb, 2, GROUPS_PER_BLOCK, SSM_STATE, GROUPS_PER_BLOCK, SSM_GROUP)
    diag = jnp.sum(d6 * _eye_mask()[None, None, :, None, :, None], axis=4)
    diag = diag.transpose(1, 0, 2, 4, 3).reshape(2, nb * GROUPS_PER_BLOCK, SSM_GROUP, SSM_STATE)
    return diag[0], -diag[1]


def _unshard_cols(g):
    s, k, n = g.shape
    return g.transpose(1, 0, 2).reshape(k, s * n)


def _shard_cols(w):
    k, n = w.shape
    return w.reshape(k, N_DEV, n // N_DEV).transpose(1, 0, 2)


def _pack(arrays):
    chunks, offs, row = [], [], 0
    for a in arrays:
        flat = a.reshape(-1).astype(F32)
        rows = -(-flat.shape[0] // LANES)
        chunks.append(jnp.pad(flat, (0, rows * LANES - flat.shape[0])))
        offs.append((row, rows))
        row += rows
    pad_rows = (-row) % SUBLANES
    if pad_rows:
        chunks.append(jnp.zeros((pad_rows * LANES,), F32))
    return jnp.concatenate(chunks).reshape(row + pad_rows, LANES), offs


def _unpack(packed, offs, shapes):
    out = []
    for (row, rows), shp in zip(offs, shapes):
        size = 1
        for s in shp:
            size *= s
        out.append(packed[row: row + rows].reshape(-1)[:size].reshape(shp))
    return out


def s5_fwd(tag, u, log_dt, a_re, a_im, b_re, b_im, c_re, c_im, d_row):
    d_model = u.shape[1]
    nb = d_model // LANES
    col = lambda a: a.reshape(-1, 1)
    prm = [col(jnp.repeat(log_dt, SSM_STATE)), col(a_re), col(a_im), b_re.reshape(-1, SSM_GROUP), b_im.reshape(-1, SSM_GROUP)]
    lr, li, bbr, bbi = rowcall(f"s5_prep_{tag}", _s5_discretise, prm, [], [(1, F32), (1, F32), (SSM_GROUP, F32), (SSM_GROUP, F32)])
    lam = jnp.stack([lr.reshape(nb, BLOCK_STATE), li.reshape(nb, BLOCK_STATE)], axis=1)
    wb = _b_blocks(bbr, bbi)
    wc = _c_blocks(c_re, c_im)
    st = s5_scan(f"s5_scan_{tag}", bmm(f"s5_bu_{tag}", u, wb, "nn"), lam, reverse=False)
    yss = bmm(f"s5_y_{tag}", st, wc, "nn")
    z = rowcall(f"s5_gelu_{tag}", lambda ys, uu, dsk: _gelu(ys + dsk * uu), [yss, u], [d_row], [(d_model, F32)])[0]
    return z, dict(prm=prm, lam=lam, wb=wb, wc=wc, st=st, yss=yss)


def s5_bwd(tag, dz, u, d_row, sv):
    d_model = u.shape[1]
    nb = d_model // LANES
    n_groups = d_model // SSM_GROUP
    col = lambda a: a.reshape(-1, 1)

    def gelu_bwd(ys, uu, dzz, dsk):
        _, vjp = jax.vjp(_gelu, ys + dsk * uu)
        dy = vjp(dzz)[0]
        return dy, dy * dsk, jnp.sum(dy * uu, axis=0, keepdims=True)

    dy, du_skip, d_dskip = rowcall(f"s5_gelu_bwd_{tag}", gelu_bwd, [sv["yss"], u, dz], [d_row],
                                  [(d_model, F32), (d_model, F32)], [(1, d_model)])
    ds_out = bmm(f"s5_y_dx_{tag}", dy, sv["wc"], "nt")
    d_wc = bmm_tn(f"s5_y_dw_{tag}", sv["st"], dy, nb)
    gst = s5_scan(f"s5_scan_bwd_{tag}", ds_out, sv["lam"], reverse=True)
    d_lam = s5_dlam(f"s5_dlam_{tag}", gst, sv["st"], nb)
    du = bmm(f"s5_bu_dx_{tag}", gst, sv["wb"], "nt")
    d_wb = bmm_tn(f"s5_bu_dw_{tag}", u, gst, nb)
    d_bbr, d_bbi = _b_blocks_t(d_wb)
    d_cre, d_cim = _c_blocks_t(d_wc)

    def prep_bwd(ldt, ar, ai, br, bi, dlr, dli, dbr, dbi):
        _, vjp = jax.vjp(_s5_discretise, ldt, ar, ai, br, bi)
        return vjp((dlr, dli, dbr, dbi))

    d_ldt, d_are, d_aim, d_bre, d_bim = rowcall(
        f"s5_prep_bwd_{tag}", prep_bwd, sv["prm"] + [col(d_lam[:, 0]), col(d_lam[:, 1]), d_bbr, d_bbi], [],
        [(1, F32), (1, F32), (1, F32), (SSM_GROUP, F32), (SSM_GROUP, F32)])
    d_logdt = rowcall(f"s5_dlogdt_{tag}", lambda a: jnp.sum(a, axis=1, keepdims=True), [d_ldt.reshape(n_groups, SSM_STATE)], [], [(1, F32)])[0]
    dhn = rowcall(f"s5_du_sum_{tag}", lambda a, b: a + b, [du, du_skip], [], [(d_model, F32)])[0]
    grads = dict(log_dt=d_logdt.reshape(n_groups), a_re=d_are.reshape(n_groups, SSM_STATE), a_im=d_aim.reshape(n_groups, SSM_STATE),
                 b_re=d_bre.reshape(n_groups, SSM_STATE, SSM_GROUP), b_im=d_bim.reshape(n_groups, SSM_STATE, SSM_GROUP),
                 c_re=d_cre, c_im=d_cim, d=d_dskip)
    return dhn, grads


def kernel(x, mix_norm, mlp_norm, mlp_w1, mlp_w2, ssm_log_dt, ssm_a_re, ssm_a_im, ssm_b_re, ssm_b_im, ssm_c_re, ssm_c_im, ssm_d, ssm_w_glu, kv_norm, w_kvf, b_f, attn_wq, attn_wo, final_norm, loss_target, m_mix_norm, m_mlp_norm, m_mlp_w1, m_mlp_w2, m_ssm_log_dt, m_ssm_a_re, m_ssm_a_im, m_ssm_b_re, m_ssm_b_im, m_ssm_c_re, m_ssm_c_im, m_ssm_d, m_ssm_w_glu, m_kv_norm, m_w_kvf, m_b_f, m_attn_wq, m_attn_wo, m_final_norm, v_mix_norm, v_mlp_norm, v_mlp_w1, v_mlp_w2, v_ssm_log_dt, v_ssm_a_re, v_ssm_a_im, v_ssm_b_re, v_ssm_b_im, v_ssm_c_re, v_ssm_c_im, v_ssm_d, v_ssm_w_glu, v_kv_norm, v_w_kvf, v_b_f, v_attn_wq, v_attn_wo, v_final_norm):
    n_rows, d_model = x.shape[1], x.shape[2]
    depth = mix_norm.shape[0]
    n_a = ssm_log_dt.shape[0]
    n_b = depth - n_a
    n_heads = d_model // HEAD_DIM
    n_groups = d_model // SSM_GROUP
    nb = n_groups // GROUPS_PER_BLOCK
    kvf_cols = 2 * d_model + n_heads
    kvf_pad = 2 * d_model + LANES

    g_w1, g_w2, g_glu, g_kvf, g_wq, g_wo, g_d = all_gather(
        "gather_weights",
        [mlp_w1.astype(BF16), mlp_w2.astype(BF16), ssm_w_glu.astype(BF16), w_kvf.astype(BF16),
         attn_wq.astype(BF16), attn_wo.astype(BF16), ssm_d])
    w1 = [_unshard_cols(g_w1[:, i]) for i in range(depth)]
    w2 = [g_w2[:, i].reshape(-1, d_model) for i in range(depth)]
    wglu = [_unshard_cols(g_glu[:, i]) for i in range(n_a)]
    wkvf = _unshard_cols(g_kvf)
    wkvf = jnp.pad(wkvf, ((0, 0), (0, kvf_pad - kvf_cols)))
    wq = [g_wq[:, j].reshape(-1, d_model) for j in range(n_b)]
    wo = [g_wo[:, j].reshape(-1, d_model) for j in range(n_b)]
    d_skip = [g_d[:, i].reshape(1, d_model) for i in range(n_a)]

    row = lambda a: a.reshape(1, -1)
    col = lambda a: a.reshape(-1, 1)

    h = x[0]
    saved = []
    k_ext = v_ext = None
    q_consts = _lane_consts([(F_LANE, F_LANE + 3, 1.0)])
    k_consts = _lane_consts([(LSE_LANE, LSE_LANE + 3, -1.0), (ROWSUM_LANE, ROWSUM_LANE + 1, 1.0)])
    v_consts = _lane_consts([(SUM_LANE, SUM_LANE + 1, 1.0), (DELTA_LANE, DELTA_LANE + 3, -1.0)])
    for i in range(depth):
        sv = {"h": h}
        hn = rms_fwd(f"mix_norm_{i}", h, row(mix_norm[i]))
        sv["hn"] = hn
        if i < n_a:
            z, s5_saved = s5_fwd(str(i), hn, ssm_log_dt[i], ssm_a_re[i], ssm_a_im[i], ssm_b_re[i], ssm_b_im[i],
                                 ssm_c_re[i], ssm_c_im[i], d_skip[i])
            zw = matmul(f"s5_glu_{i}", z, wglu[i])
            h1 = rowcall(f"s5_gate_{i}", lambda hh, zz: hh + zz[:, :d_model] * _sigmoid(zz[:, d_model:]), [h, zw], [], [(d_model, F32)])[0]
            sv.update(s5=s5_saved, z=z, zw=zw)
        else:
            j = i - n_a
            q = matmul(f"attn_q_{j}", hn, wq[j], scale=LOG2E * HEAD_DIM ** -0.5)
            o_ext, q_ext_b = fox_fwd(f"attn_fwd_{j}", pack_heads(f"pack_q_{j}", q, 0, n_heads, q_consts), k_ext, v_ext)
            o2 = unpack_heads(f"unpack_o_{j}", [o_ext])[0]
            h1 = matmul(f"attn_o_{j}", o2, wo[j], resid=h)
            sv.update(q_ext_b=q_ext_b, o2=o2)
        h2n = rms_fwd(f"mlp_norm_{i}", h1, row(mlp_norm[i]))
        ap = matmul(f"mlp_up_{i}", h2n, w1[i])
        h = matmul(f"mlp_down_{i}", ap, w2[i], a_fn=_sqrelu, resid=h1)
        sv.update(h1=h1, h2n=h2n, ap=ap)
        saved.append(sv)
        if i == n_a - 1:
            h_mid = h
            hk = rms_fwd("kv_norm", h, row(kv_norm))
            kvf = matmul("kvf_proj", hk, wkvf)
            fl = kvf[:, 2 * d_model:]
            bfp = jnp.pad(row(b_f), ((0, 0), (0, LANES - n_heads)))
            k_ext = pack_heads("pack_k", kvf, 0, n_heads, k_consts, cum_logf("cum_logf", fl, bfp), F_LANE)
            v_ext = pack_heads("pack_v", kvf, 1, n_heads, v_consts)

    def loss_fn(hh, tgt, g):
        y, vjp = jax.vjp(_rms, hh, g)
        err = y - tgt
        part = 0.5 * jnp.sum(jnp.mean(err * err, axis=-1, keepdims=True), axis=0, keepdims=True)
        dh, dg = vjp(err * (1.0 / d_model))
        return dh, jnp.broadcast_to(part, (1, LANES)), dg

    dh, loss_part, d_final = rowcall("loss_head", loss_fn, [h, loss_target[0]], [row(final_norm)],
                                     [(d_model, F32)], [(1, LANES), (1, d_model)])

    g_mix, g_mlpn = [None] * depth, [None] * depth
    g_w1f, g_w2f = [None] * depth, [None] * depth
    g_ssm = [None] * n_a
    g_wqf, g_wof = [None] * n_b, [None] * n_b
    dk_acc, dv_acc, df_plus = [], [], []
    g_kv = None
    for i in reversed(range(depth)):
        sv = saved[i]
        if i == n_a - 1:
            dk, col_sums = unpack_heads("unpack_dk", dk_acc, extract_lane=HEAD_DIM)
            dv = unpack_heads("unpack_dv", dv_acc)[0]
            dfl, db_f = cum_logf_bwd("cum_logf_bwd", fl, bfp, df_plus, [col_sums])
            dkvf = jnp.concatenate([dk, dv, dfl], axis=1)
            dhk = matmul("kvf_dx", dkvf, wkvf, "nt")
            d_wkvf = matmul("kvf_dw", hk, dkvf, "tn")
            dh, d_kvn = rms_bwd("kv_norm_bwd", h_mid, row(kv_norm), dhk, add=dh)
            g_kv = (d_wkvf[:, :kvf_cols], d_kvn, db_f[:, :n_heads])
        dap = matmul(f"mlp_down_dx_{i}", dh, w2[i], "nt", post=lambda acc, apt: acc * (2.0 * jnp.maximum(apt, 0.0)), post_arg=sv["ap"])
        g_w2f[i] = matmul(f"mlp_down_dw_{i}", sv["ap"], dh, "tn", a_fn=_sqrelu)
        dh2n = matmul(f"mlp_up_dx_{i}", dap, w1[i], "nt")
        g_w1f[i] = matmul(f"mlp_up_dw_{i}", sv["h2n"], dap, "tn")
        dh1, g_mlpn[i] = rms_bwd(f"mlp_norm_bwd_{i}", sv["h1"], row(mlp_norm[i]), dh2n, add=dh)
        if i < n_a:
            def glu_bwd(zz, dd):
                val, gate = zz[:, :d_model], zz[:, d_model:]
                sg = _sigmoid(gate)
                return jnp.concatenate([dd * sg, dd * val * sg * (1.0 - sg)], axis=1)

            dzw = rowcall(f"s5_gate_bwd_{i}", glu_bwd, [sv["zw"], dh1], [], [(2 * d_model, F32)])[0]
            dz = matmul(f"s5_glu_dx_{i}", dzw, wglu[i], "nt")
            d_wglu = matmul(f"s5_glu_dw_{i}", sv["z"], dzw, "tn")

            dhn, g_ssm[i] = s5_bwd(str(i), dz, sv["hn"], d_skip[i], sv["s5"])
            g_ssm[i]["w_glu"] = d_wglu
        else:
            j = i - n_a
            do2 = matmul(f"attn_o_dx_{j}", dh1, wo[j], "nt")
            g_wof[j] = matmul(f"attn_o_dw_{j}", sv["o2"], dh1, "tn")
            do_ext = pack_heads(f"pack_do_{j}", do2, 0, n_heads, jnp.zeros((1, LANES), F32),
                                attn_delta(f"attn_delta_{j}", sv["o2"], do2), DELTA_LANE)
            dq_ext, dk_ext, dv_ext = fox_bwd(f"attn_bwd_{j}", sv["q_ext_b"], do_ext, k_ext, v_ext)
            dk_acc.append(dk_ext)
            dv_acc.append(dv_ext)
            dq2, row_sums = unpack_heads(f"unpack_dq_{j}", [dq_ext], extract_lane=ROWSUM_LANE)
            df_plus.append(row_sums)
            dhn = matmul(f"attn_q_dx_{j}", dq2, wq[j], "nt", scale=HEAD_DIM ** -0.5)
            g_wqf[j] = matmul(f"attn_q_dw_{j}", sv["hn"], dq2, "tn", scale=HEAD_DIM ** -0.5)
        dh, g_mix[i] = rms_bwd(f"mix_norm_bwd_{i}", sv["h"], row(mix_norm[i]), dhn, add=dh1)
    grad_x = dh[None]

    stack = lambda xs: jnp.stack(xs, axis=1).astype(BF16 if xs[0].ndim == 3 else F32)
    parts = all_to_all("scatter_grads", [
        stack([_shard_cols(g) for g in g_w1f]),
        stack([g.reshape(N_DEV, -1, d_model) for g in g_w2f]),
        stack([_shard_cols(g["w_glu"]) for g in g_ssm]),
        _shard_cols(g_kv[0]).astype(BF16),
        stack([g.reshape(N_DEV, -1, d_model) for g in g_wqf]),
        stack([g.reshape(N_DEV, -1, d_model) for g in g_wof]),
        stack([g["d"].reshape(N_DEV, -1) for g in g_ssm]),
    ])
    small_names = ["mix_norm", "mlp_norm", "ssm_log_dt", "ssm_a_re", "ssm_a_im", "ssm_b_re", "ssm_b_im", "ssm_c_re", "ssm_c_im",
                   "kv_norm", "b_f", "final_norm"]
    small_grads = [jnp.concatenate(g_mix, axis=0), jnp.concatenate(g_mlpn, axis=0)]
    small_grads += [jnp.stack([g[kk] for g in g_ssm]) for kk in ("log_dt", "a_re", "a_im", "b_re", "b_im", "c_re", "c_im")]
    small_grads += [g_kv[1], g_kv[2], d_final]
    small_w = [mix_norm, mlp_norm, ssm_log_dt, ssm_a_re, ssm_a_im, ssm_b_re, ssm_b_im, ssm_c_re, ssm_c_im, kv_norm, b_f, final_norm]
    small_m = [m_mix_norm, m_mlp_norm, m_ssm_log_dt, m_ssm_a_re, m_ssm_a_im, m_ssm_b_re, m_ssm_b_im, m_ssm_c_re, m_ssm_c_im, m_kv_norm, m_b_f, m_final_norm]
    small_v = [v_mix_norm, v_mlp_norm, v_ssm_log_dt, v_ssm_a_re, v_ssm_a_im, v_ssm_b_re, v_ssm_b_im, v_ssm_c_re, v_ssm_c_im, v_kv_norm, v_b_f, v_final_norm]
    loss_slot = jnp.zeros((LANES,), F32)
    packed_g, offs = _pack(small_grads + [loss_part])
    packed_w, _ = _pack(small_w + [loss_slot])
    packed_m, _ = _pack(small_m + [loss_slot])
    packed_v, _ = _pack(small_v + [loss_slot])
    (small_parts,) = all_gather("gather_small_grads", [packed_g])

    res = {}

    def update(nm, part, w, m, v):
        shp = w.shape
        as2d = lambda a: a.reshape(-1, shp[-1])
        outs = adamw(f"adamw_{nm}", part.reshape((N_DEV,) + as2d(w).shape), as2d(w), as2d(m), as2d(v))
        res[nm] = [o.reshape(shp) for o in outs]

    update("mlp_w1", parts[0], mlp_w1, m_mlp_w1, v_mlp_w1)
    update("mlp_w2", parts[1], mlp_w2, m_mlp_w2, v_mlp_w2)
    update("ssm_w_glu", parts[2], ssm_w_glu, m_ssm_w_glu, v_ssm_w_glu)
    update("w_kvf", parts[3], w_kvf, m_w_kvf, v_w_kvf)
    update("attn_wq", parts[4], attn_wq, m_attn_wq, v_attn_wq)
    update("attn_wo", parts[5], attn_wo, m_attn_wo, v_attn_wo)
    update("ssm_d", parts[6], ssm_d, m_ssm_d, v_ssm_d)
    small_out = adamw("adamw_small", small_parts, packed_w, packed_m, packed_v)
    shapes = [w.shape for w in small_w] + [(LANES,)]
    unpacked = [_unpack(o, offs, shapes) for o in small_out]
    for idx, nm in enumerate(small_names):
        res[nm] = [u[idx] for u in unpacked]
    loss = unpacked[0][-1][0]

    order = ["mix_norm", "mlp_norm", "mlp_w1", "mlp_w2", "ssm_log_dt", "ssm_a_re", "ssm_a_im", "ssm_b_re", "ssm_b_im", "ssm_c_re",
             "ssm_c_im", "ssm_d", "ssm_w_glu", "kv_norm", "w_kvf", "b_f", "attn_wq", "attn_wo", "final_norm"]
    out = [loss, grad_x]
    for kind in range(4):
        out += [res[nm][kind] for nm in order]
    return tuple(out)
```

```python
import functools

import jax
import jax.numpy as jnp
from jax import lax
from jax.experimental import pallas as pl
from jax.experimental.pallas import tpu as pltpu

F32 = jnp.float32
BF16 = jnp.bfloat16
HIGHEST = lax.Precision.HIGHEST

V7X_VMEM_BYTES = 64 << 20
VMEM_LIMIT_BYTES = (V7X_VMEM_BYTES * 3) // 4
LANES = 128
SUBLANES = 8

N_DEV = 8
RMS_EPS = 1e-6
SSM_GROUP = 16
SSM_STATE = 64
HEAD_DIM = 64
GROUPS_PER_BLOCK = LANES // SSM_GROUP
BLOCK_STATE = GROUPS_PER_BLOCK * SSM_STATE
BLOCK_COLS = 2 * BLOCK_STATE
NEG_BIG = -1e30
LOG2E = 1.4426950408889634

ADAM_LR = 0.001
ADAM_B1 = 0.9
ADAM_B2 = 0.999
ADAM_EPS = 1e-08
ADAM_WD = 0.01
ADAM_STEP = 10

TILES = {"row": 512, "mm": (1024, 512, 1024), "blk": 512, "scan": 512, "cum": 256, "attn": 512, "adam": 256}


def _pick(dim, pref, align=LANES):
    if dim <= pref:
        return dim
    for a in (align, SUBLANES):
        d = (pref // a) * a
        while d >= a:
            if dim % d == 0:
                return d
            d -= a
    return dim


def _params(*sem):
    return pltpu.CompilerParams(dimension_semantics=sem, vmem_limit_bytes=VMEM_LIMIT_BYTES)


def rowcall(name, fn, rows, consts, out_rows, out_accs=(), tm=None):
    n_rows = rows[0].shape[0]
    tm = _pick(n_rows, tm or TILES["row"], SUBLANES)
    nr, nc, no, na = len(rows), len(consts), len(out_rows), len(out_accs)

    def body(*refs):
        ins = [r[...] for r in refs[: nr + nc]]
        outs = fn(*ins)
        if not isinstance(outs, (tuple, list)):
            outs = (outs,)
        for r, o in zip(refs[nr + nc: nr + nc + no], outs[:no]):
            r[...] = o.astype(r.dtype)
        if na:
            i = pl.program_id(0)
            for r, o in zip(refs[nr + nc + no:], outs[no:]):
                @pl.when(i == 0)
                def _(r=r, o=o):
                    r[...] = o

                @pl.when(i > 0)
                def _(r=r, o=o):
                    r[...] += o

    in_specs = [pl.BlockSpec((tm, a.shape[1]), lambda i: (i, 0)) for a in rows]
    in_specs += [pl.BlockSpec(c.shape, lambda i, n=c.ndim: (0,) * n) for c in consts]
    out_shape = [jax.ShapeDtypeStruct((n_rows, c), dt) for c, dt in out_rows]
    out_specs = [pl.BlockSpec((tm, c), lambda i: (i, 0)) for c, _ in out_rows]
    out_shape += [jax.ShapeDtypeStruct(s, F32) for s in out_accs]
    out_specs += [pl.BlockSpec(s, lambda i, n=len(s): (0,) * n) for s in out_accs]
    res = pl.pallas_call(
        body, name=name, grid=(n_rows // tm,), in_specs=in_specs, out_specs=out_specs, out_shape=out_shape,
        compiler_params=_params("arbitrary" if na else "parallel"),
    )(*rows, *consts)
    return res


_DIMS = {"nn": (((1,), (0,)), ((), ())), "nt": (((1,), (1,)), ((), ())), "tn": (((0,), (0,)), ((), ()))}


def matmul(name, a, b, mode="nn", *, a_fn=None, scale=None, resid=None, post=None, post_arg=None,
           bm=None, bn=None, bk=None):
    if mode == "nn":
        (m, k), (k2, n) = a.shape, b.shape
    elif mode == "nt":
        (m, k), (n, k2) = a.shape, b.shape
    else:
        (k, m), (k2, n) = a.shape, b.shape
    assert k == k2, (name, a.shape, b.shape, mode)
    bm, bn, bk = _pick(m, bm or TILES["mm"][0]), _pick(n, bn or TILES["mm"][1]), _pick(k, bk or TILES["mm"][2])
    nk = k // bk
    a_spec = pl.BlockSpec((bk, bm), lambda i, j, kk: (kk, i)) if mode == "tn" else pl.BlockSpec((bm, bk), lambda i, j, kk: (i, kk))
    b_spec = pl.BlockSpec((bn, bk), lambda i, j, kk: (j, kk)) if mode == "nt" else pl.BlockSpec((bk, bn), lambda i, j, kk: (kk, j))
    mn_spec = pl.BlockSpec((bm, bn), lambda i, j, kk: (i, j))
    extra = [x for x in (resid, post_arg) if x is not None]
    has_resid, has_post = resid is not None, post is not None

    def body(*refs):
        a_ref, b_ref = refs[0], refs[1]
        ex = refs[2: 2 + len(extra)]
        o_ref = refs[2 + len(extra)]
        av = a_ref[...]
        if a_fn is not None:
            av = a_fn(av.astype(F32))
        p = lax.dot_general(av.astype(BF16), b_ref[...].astype(BF16), _DIMS[mode], preferred_element_type=F32)

        def finish(acc):
            if scale is not None:
                acc = acc * scale
            idx = 0
            if has_resid:
                acc = acc + ex[idx][...]
                idx += 1
            if has_post:
                acc = post(acc, ex[idx][...])
            o_ref[...] = acc.astype(o_ref.dtype)

        if nk == 1:
            finish(p)
        else:
            acc_ref = refs[-1]
            kk = pl.program_id(2)

            @pl.when(kk == 0)
            def _():
                acc_ref[...] = p

            @pl.when(kk > 0)
            def _():
                acc_ref[...] += p

            @pl.when(kk == nk - 1)
            def _():
                finish(acc_ref[...])

    return pl.pallas_call(
        body, name=name, grid=(m // bm, n // bn, nk),
        in_specs=[a_spec, b_spec] + [mn_spec] * len(extra), out_specs=mn_spec,
        out_shape=jax.ShapeDtypeStruct((m, n), F32),
        scratch_shapes=[pltpu.VMEM((bm, bn), F32)] if nk > 1 else [],
        compiler_params=_params("parallel", "parallel", "arbitrary"),
    )(a, b, *extra)


S5_DOT_PASSES = 1


def _block_dot(a, b, mode):
    if S5_DOT_PASSES == 6:
        return lax.dot_general(a, b, _DIMS[mode], precision=HIGHEST, preferred_element_type=F32)
    dot = lambda x, y: lax.dot_general(x, y, _DIMS[mode], preferred_element_type=F32)
    a_hi, b_hi = a.astype(BF16), b.astype(BF16)
    if S5_DOT_PASSES == 1:
        return dot(a_hi, b_hi)
    a_lo, b_lo = (a - a_hi.astype(F32)).astype(BF16), (b - b_hi.astype(F32)).astype(BF16)
    return dot(a_hi, b_hi) + (dot(a_hi, b_lo) + dot(a_lo, b_hi))


def bmm(name, a, w, mode, tm=None):
    n_rows = a.shape[0]
    nb, ka, kb = w.shape
    ca, co = (ka, kb) if mode == "nn" else (kb, ka)
    assert a.shape[1] == nb * ca
    tm = _pick(n_rows, tm or TILES["blk"], SUBLANES)

    def body(a_ref, w_ref, o_ref):
        o_ref[...] = _block_dot(a_ref[...], w_ref[...], mode)

    return pl.pallas_call(
        body, name=name, grid=(n_rows // tm, nb),
        in_specs=[pl.BlockSpec((tm, ca), lambda i, b: (i, b)), pl.BlockSpec((None, ka, kb), lambda i, b: (b, 0, 0))],
        out_specs=pl.BlockSpec((tm, co), lambda i, b: (i, b)),
        out_shape=jax.ShapeDtypeStruct((n_rows, nb * co), F32),
        compiler_params=_params("parallel", "parallel"),
    )(a, w)


def bmm_tn(name, a, b, nb, tk=None):
    n_rows = a.shape[0]
    ka, kb = a.shape[1] // nb, b.shape[1] // nb
    tk = _pick(n_rows, tk or TILES["blk"], SUBLANES)

    def body(a_ref, b_ref, o_ref):
        p = _block_dot(a_ref[...], b_ref[...], "tn")
        kk = pl.program_id(1)

        @pl.when(kk == 0)
        def _():
            o_ref[...] = p

        @pl.when(kk > 0)
        def _():
            o_ref[...] += p

    return pl.pallas_call(
        body, name=name, grid=(nb, n_rows // tk),
        in_specs=[pl.BlockSpec((tk, ka), lambda bb, kk: (kk, bb)), pl.BlockSpec((tk, kb), lambda bb, kk: (kk, bb))],
        out_specs=pl.BlockSpec((None, ka, kb), lambda bb, kk: (bb, 0, 0)),
        out_shape=jax.ShapeDtypeStruct((nb, ka, kb), F32),
        compiler_params=_params("parallel", "arbitrary"),
    )(a, b)


def _cmul(ar, ai, br, bi):
    return ar * br - ai * bi, ar * bi + ai * br


def _scan_tables(lam_ref, reverse):
    shape = (SUBLANES, BLOCK_STATE)
    lr = jnp.broadcast_to(lam_ref[0:1, :], shape)
    li = jnp.broadcast_to(lam_ref[1:2, :], shape)
    if reverse:
        li = -li
    row = lax.broadcasted_iota(jnp.int32, shape, 0)
    tt = (SUBLANES - 1 - row) if reverse else row
    l1 = (lr, li)
    l2 = _cmul(*l1, *l1)
    l4 = _cmul(*l2, *l2)
    pr, pi = l1
    for bit, lp in enumerate((l1, l2, l4)):
        qr, qi = _cmul(pr, pi, *lp)
        on = ((tt >> bit) & 1) == 1
        pr, pi = jnp.where(on, qr, pr), jnp.where(on, qi, pi)
    steps = []
    for d, lp in ((1, l1), (2, l2), (4, l4)):
        ok = tt >= d
        steps.append((d, jnp.where(ok, lp[0], 0.0), jnp.where(ok, lp[1], 0.0)))
    return steps, (pr, pi)


def s5_scan(name, x, lam, reverse, tm=None):
    n_rows = x.shape[0]
    nb = lam.shape[0]
    tm = _pick(n_rows, tm or TILES["scan"], SUBLANES)
    nt = n_rows // tm
    ng = tm // SUBLANES
    last = 0 if reverse else SUBLANES - 1

    def body(x_ref, lam_ref, o_ref, carry_ref):
        it = pl.program_id(1)
        steps, (pr, pi) = _scan_tables(lam_ref, reverse)

        @pl.when(it == 0)
        def _():
            carry_ref[...] = jnp.zeros_like(carry_ref)

        def group(r, carry):
            cr, ci = carry
            rr = (ng - 1 - r) if reverse else r
            off = pl.multiple_of(rr * SUBLANES, SUBLANES)
            xr = x_ref[pl.ds(off, SUBLANES), 0:BLOCK_STATE]
            xi = x_ref[pl.ds(off, SUBLANES), BLOCK_STATE:BLOCK_COLS]
            for d, mr, mi in steps:
                sh = (SUBLANES - d) if reverse else d
                yr, yi = pltpu.roll(xr, sh, 0), pltpu.roll(xi, sh, 0)
                xr, xi = xr + mr * yr - mi * yi, xi + mr * yi + mi * yr
            xr, xi = xr + pr * cr - pi * ci, xi + pr * ci + pi * cr
            o_ref[pl.ds(off, SUBLANES), 0:BLOCK_STATE] = xr
            o_ref[pl.ds(off, SUBLANES), BLOCK_STATE:BLOCK_COLS] = xi
            shape = (SUBLANES, BLOCK_STATE)
            return jnp.broadcast_to(xr[last:last + 1, :], shape), jnp.broadcast_to(xi[last:last + 1, :], shape)

        cr, ci = lax.fori_loop(0, ng, group, (carry_ref[0], carry_ref[1]))
        carry_ref[0] = cr
        carry_ref[1] = ci

    tile = (lambda b, it: (nt - 1 - it, b)) if reverse else (lambda b, it: (it, b))
    return pl.pallas_call(
        body, name=name, grid=(nb, nt),
        in_specs=[pl.BlockSpec((tm, BLOCK_COLS), tile), pl.BlockSpec((None, 2, BLOCK_STATE), lambda b, it: (b, 0, 0))],
        out_specs=pl.BlockSpec((tm, BLOCK_COLS), tile),
        out_shape=jax.ShapeDtypeStruct(x.shape, F32),
        scratch_shapes=[pltpu.VMEM((2, SUBLANES, BLOCK_STATE), F32)],
        compiler_params=_params("parallel", "arbitrary"),
    )(x, lam)


def s5_dlam(name, g, s, nb, tm=None):
    n_rows = g.shape[0]
    tm = _pick(n_rows, tm or TILES["scan"], SUBLANES)
    nt = n_rows // tm
    ng = tm // SUBLANES
    shape = (SUBLANES, BLOCK_STATE)

    def body(g_ref, s_ref, o_ref, carry_ref, acc_ref):
        it = pl.program_id(1)

        @pl.when(it == 0)
        def _():
            carry_ref[...] = jnp.zeros_like(carry_ref)
            acc_ref[...] = jnp.zeros_like(acc_ref)

        first = lax.broadcasted_iota(jnp.int32, shape, 0) == 0

        def group(r, carry):
            cr, ci, ar, ai = carry
            off = pl.multiple_of(r * SUBLANES, SUBLANES)
            sr = s_ref[pl.ds(off, SUBLANES), 0:BLOCK_STATE]
            si = s_ref[pl.ds(off, SUBLANES), BLOCK_STATE:BLOCK_COLS]
            gr = g_ref[pl.ds(off, SUBLANES), 0:BLOCK_STATE]
            gi = g_ref[pl.ds(off, SUBLANES), BLOCK_STATE:BLOCK_COLS]
            pr = jnp.where(first, cr, pltpu.roll(sr, 1, 0))
            pi = jnp.where(first, ci, pltpu.roll(si, 1, 0))
            ar = ar + gr * pr + gi * pi
            ai = ai + gi * pr - gr * pi
            return (jnp.broadcast_to(sr[SUBLANES - 1:SUBLANES, :], shape),
                    jnp.broadcast_to(si[SUBLANES - 1:SUBLANES, :], shape), ar, ai)

        cr, ci, ar, ai = lax.fori_loop(0, ng, group, (carry_ref[0], carry_ref[1], acc_ref[0], acc_ref[1]))
        carry_ref[0] = cr
        carry_ref[1] = ci
        acc_ref[0] = ar
        acc_ref[1] = ai

        @pl.when(it == nt - 1)
        def _():
            o_ref[0:1, :] = jnp.sum(ar, axis=0, keepdims=True)
            o_ref[1:2, :] = jnp.sum(ai, axis=0, keepdims=True)

    tile = lambda b, it: (it, b)
    return pl.pallas_call(
        body, name=name, grid=(nb, nt),
        in_specs=[pl.BlockSpec((tm, BLOCK_COLS), tile), pl.BlockSpec((tm, BLOCK_COLS), tile)],
        out_specs=pl.BlockSpec((None, 2, BLOCK_STATE), lambda b, it: (b, 0, 0)),
        out_shape=jax.ShapeDtypeStruct((nb, 2, BLOCK_STATE), F32),
        scratch_shapes=[pltpu.VMEM((2,) + shape, F32), pltpu.VMEM((2,) + shape, F32)],
        compiler_params=_params("parallel", "arbitrary"),
    )(g, s)


def _rms(x, g):
    return x * lax.rsqrt(jnp.mean(x * x, axis=-1, keepdims=True) + RMS_EPS) * g


def _sigmoid(x):
    return 1.0 / (1.0 + jnp.exp(-x))


def _gelu(x):
    return 0.5 * x * (1.0 + jnp.tanh(0.7978845608028654 * (x + 0.044715 * (x * x * x))))


def _log_sigmoid(x):
    return jnp.minimum(x, 0.0) - jnp.log(1.0 + jnp.exp(-jnp.abs(x)))


def _sqrelu(x):
    r = jnp.maximum(x, 0.0)
    return r * r


def _s5_discretise(ldt, ar, ai, br, bi):
    dt = jnp.exp(ldt)
    er = jnp.exp(ar * dt)
    lr, li = er * jnp.cos(ai * dt), er * jnp.sin(ai * dt)
    nr, ni = lr - 1.0, li
    den = ar * ar + ai * ai
    cr, ci = (nr * ar + ni * ai) / den, (ni * ar - nr * ai) / den
    return lr, li, cr * br - ci * bi, cr * bi + ci * br


def rms_fwd(name, x, g):
    return rowcall(name, _rms, [x], [g], [(x.shape[1], F32)])[0]


def rms_bwd(name, x, g, dy, add=None):
    def fn(x, dy, *rest):
        g = rest[-1]
        _, vjp = jax.vjp(_rms, x, g)
        dx, dg = vjp(dy)
        if add is not None:
            dx = dx + rest[0]
        return dx, dg

    rows = [x, dy] + ([add] if add is not None else [])
    return rowcall(name, fn, rows, [g], [(x.shape[1], F32)], [g.shape])


def _split3(x):
    hi = x.astype(BF16).astype(F32)
    r = x - hi
    mid = r.astype(BF16).astype(F32)
    return hi, mid, (r - mid).astype(BF16).astype(F32)


def cum_logf(name, fl, bf, tm=None):
    n_rows, w = fl.shape
    tm = _pick(n_rows, tm or TILES["cum"], SUBLANES)

    def body(fl_ref, bf_ref, hi_ref, mid_ref, lo_ref, carry_ref):
        it = pl.program_id(0)

        @pl.when(it == 0)
        def _():
            carry_ref[...] = jnp.zeros_like(carry_ref)

        ls = _log_sigmoid(fl_ref[...] + bf_ref[...])
        tri = (lax.broadcasted_iota(jnp.int32, (tm, tm), 0) >= lax.broadcasted_iota(jnp.int32, (tm, tm), 1)).astype(F32)
        c = jnp.dot(tri, ls, precision=HIGHEST, preferred_element_type=F32) + carry_ref[0:1, :]
        carry_ref[...] = jnp.broadcast_to(c[tm - 1:tm, :], carry_ref.shape)
        hi_ref[...], mid_ref[...], lo_ref[...] = _split3(c * (-LOG2E))

    spec = pl.BlockSpec((tm, w), lambda i: (i, 0))
    return pl.pallas_call(
        body, name=name, grid=(n_rows // tm,),
        in_specs=[spec, pl.BlockSpec((1, w), lambda i: (0, 0))],
        out_specs=[spec] * 3, out_shape=[jax.ShapeDtypeStruct((n_rows, w), F32)] * 3,
        scratch_shapes=[pltpu.VMEM((SUBLANES, w), F32)],
        compiler_params=_params("arbitrary"),
    )(fl, bf)


def cum_logf_bwd(name, fl, bf, plus, minus, tm=None):
    n_rows, w = fl.shape
    tm = _pick(n_rows, tm or TILES["cum"], SUBLANES)
    nt = n_rows // tm
    n_p, n_m = len(plus), len(minus)

    def body(*refs):
        fl_ref, bf_ref = refs[0], refs[1]
        d_refs = refs[2: 2 + n_p + n_m]
        o_ref, db_ref, carry_ref = refs[2 + n_p + n_m:]
        it = pl.program_id(0)

        @pl.when(it == 0)
        def _():
            carry_ref[...] = jnp.zeros_like(carry_ref)

        d = None
        for r in d_refs[:n_p]:
            d = r[...] if d is None else d + r[...]
        for r in d_refs[n_p:]:
            d = -r[...] if d is None else d - r[...]
        tri = (lax.broadcasted_iota(jnp.int32, (tm, tm), 0) <= lax.broadcasted_iota(jnp.int32, (tm, tm), 1)).astype(F32)
        c = jnp.dot(tri, d, precision=HIGHEST, preferred_element_type=F32) + carry_ref[0:1, :]
        carry_ref[...] = jnp.broadcast_to(c[0:1, :], carry_ref.shape)
        dfl = c * _sigmoid(-(fl_ref[...] + bf_ref[...]))
        o_ref[...] = dfl
        part = jnp.sum(dfl, axis=0, keepdims=True)

        @pl.when(it == 0)
        def _():
            db_ref[...] = part

        @pl.when(it > 0)
        def _():
            db_ref[...] += part

    rev = lambda i: (nt - 1 - i, 0)
    return pl.pallas_call(
        body, name=name, grid=(nt,),
        in_specs=[pl.BlockSpec((tm, w), rev), pl.BlockSpec((1, w), lambda i: (0, 0))] + [pl.BlockSpec((tm, w), rev)] * (n_p + n_m),
        out_specs=[pl.BlockSpec((tm, w), rev), pl.BlockSpec((1, w), lambda i: (0, 0))],
        out_shape=[jax.ShapeDtypeStruct((n_rows, w), F32), jax.ShapeDtypeStruct((1, w), F32)],
        scratch_shapes=[pltpu.VMEM((SUBLANES, w), F32)],
        compiler_params=_params("arbitrary"),
    )(fl, bf, *plus, *minus)


ROWSUM_LANE = HEAD_DIM + 6
F_LANE = HEAD_DIM
LSE_LANE = HEAD_DIM + 3
SUM_LANE = HEAD_DIM
DELTA_LANE = HEAD_DIM + 1


def _lane_consts(pairs):
    lane = lax.broadcasted_iota(jnp.int32, (1, LANES), 1)
    out = jnp.zeros((1, LANES), F32)
    for lo, hi, v in pairs:
        out = jnp.where((lane >= lo) & (lane < hi), v, out)
    return out


def pack_heads(name, x, col_block, n_heads, consts, parts=(), parts_lane=0, tm=None):
    n_rows = x.shape[0]
    d = n_heads * HEAD_DIM
    assert n_heads % 2 == 0
    tm = _pick(n_rows, tm or TILES["row"], 2 * SUBLANES)
    n_parts = len(parts)

    def body(*refs):
        x_ref, c_ref = refs[0], refs[1]
        p_vals = [r[...] for r in refs[2: 2 + n_parts]]
        o_ref = refs[2 + n_parts]
        lane = lax.broadcasted_iota(jnp.int32, (tm, LANES), 1)
        tail0 = jnp.broadcast_to(c_ref[...], (tm, LANES))
        for h in range(n_heads):
            pair = x_ref[:, (h // 2) * LANES: (h // 2 + 1) * LANES].astype(F32)
            base = pair if h % 2 == 0 else pltpu.roll(pair, HEAD_DIM, 1)
            tail = tail0
            for kk, p in enumerate(p_vals):
                col = jnp.sum(jnp.where(lane == h, p, 0.0), axis=1, keepdims=True)
                tail = jnp.where(lane == parts_lane + kk, col, tail)
            o_ref[h] = jnp.where(lane < HEAD_DIM, base, tail).astype(BF16)

    return pl.pallas_call(
        body, name=name, grid=(n_rows // tm,),
        in_specs=[pl.BlockSpec((tm, d), lambda i: (i, col_block)), pl.BlockSpec((1, LANES), lambda i: (0, 0))]
        + [pl.BlockSpec((tm, LANES), lambda i: (i, 0))] * n_parts,
        out_specs=pl.BlockSpec((n_heads, tm, LANES), lambda i: (0, i, 0)),
        out_shape=jax.ShapeDtypeStruct((n_heads, n_rows, LANES), BF16),
        compiler_params=_params("parallel"),
    )(x, consts, *parts)


def unpack_heads(name, xs, extract_lane=None, tm=None):
    n_heads, n_rows, _ = xs[0].shape
    assert n_heads % 2 == 0
    tm = _pick(n_rows, tm or TILES["row"], SUBLANES)
    n = len(xs)

    def body(*refs):
        o_ref = refs[n]
        lane = lax.broadcasted_iota(jnp.int32, (tm, LANES), 1)
        picked = jnp.zeros((tm, LANES), F32)

        def head(h):
            v = refs[0][h]
            for r in refs[1:n]:
                v = v + r[h]
            return v

        for p in range(n_heads // 2):
            a, b = head(2 * p), head(2 * p + 1)
            o_ref[:, p * LANES: (p + 1) * LANES] = jnp.where(lane < HEAD_DIM, a, pltpu.roll(b, HEAD_DIM, 1))
            if extract_lane is not None:
                for hh, v in ((2 * p, a), (2 * p + 1, b)):
                    col = jnp.sum(jnp.where(lane == extract_lane, v, 0.0), axis=1, keepdims=True)
                    picked = jnp.where(lane == hh, col, picked)
        if extract_lane is not None:
            refs[n + 1][...] = picked

    d = n_heads * HEAD_DIM
    out_shape = [jax.ShapeDtypeStruct((n_rows, d), F32)]
    out_specs = [pl.BlockSpec((tm, d), lambda i: (i, 0))]
    if extract_lane is not None:
        out_shape.append(jax.ShapeDtypeStruct((n_rows, LANES), F32))
        out_specs.append(pl.BlockSpec((tm, LANES), lambda i: (i, 0)))
    return pl.pallas_call(
        body, name=name, grid=(n_rows // tm,),
        in_specs=[pl.BlockSpec((n_heads, tm, LANES), lambda i: (0, i, 0))] * n,
        out_specs=out_specs, out_shape=out_shape, compiler_params=_params("parallel"),
    )(*xs)


def fox_fwd(name, q_ext, k_ext, v_ext, t=None):
    nh, n_rows, w = q_ext.shape
    t = _pick(n_rows, t or TILES["attn"])
    nt = n_rows // t

    def body(q_ref, k_ref, v_ref, o_ref, qb_ref, m_ref, acc_ref):
        i = pl.program_id(1)
        m_ref[...] = jnp.full_like(m_ref, NEG_BIG)
        acc_ref[...] = jnp.zeros_like(acc_ref)
        q = q_ref[...]

        def block(j, diagonal):
            off = pl.multiple_of(j * t, t)
            s = lax.dot_general(q, k_ref[pl.ds(off, t), :], _DIMS["nt"], preferred_element_type=F32)
            if diagonal:
                keep = lax.broadcasted_iota(jnp.int32, (t, t), 0) >= lax.broadcasted_iota(jnp.int32, (t, t), 1)
                s = jnp.where(keep, s, NEG_BIG)
            m_prev = m_ref[...]
            m_new = jnp.maximum(m_prev, jnp.max(s, axis=1, keepdims=True))
            p = jnp.exp2(s - jnp.tile(m_new, (1, t // LANES)))
            acc_ref[...] = jnp.exp2(m_prev - m_new) * acc_ref[...] + jnp.dot(
                p.astype(BF16), v_ref[pl.ds(off, t), :], preferred_element_type=F32)
            m_ref[...] = m_new

        def off_diagonal(j, carry):
            block(j, False)
            return carry

        lax.fori_loop(0, i, off_diagonal, 0)
        block(i, True)
        acc = acc_ref[...]
        row_sum = acc[:, HEAD_DIM:HEAD_DIM + 1]
        hi, mid, lo = _split3(m_ref[:, 0:1] + jnp.log2(row_sum))
        lane = lax.broadcasted_iota(jnp.int32, (t, w), 1)
        o_ref[...] = acc / row_sum
        qb = jnp.where(lane == LSE_LANE, hi, jnp.where(lane == LSE_LANE + 1, mid, jnp.where(lane == LSE_LANE + 2, lo, q.astype(F32))))
        qb_ref[...] = qb.astype(BF16)

    whole = pl.BlockSpec((None, n_rows, w), lambda h, i: (h, 0, 0))
    tile = pl.BlockSpec((None, t, w), lambda h, i: (h, i, 0))
    return pl.pallas_call(
        body, name=name, grid=(nh, nt), in_specs=[tile, whole, whole], out_specs=[tile, tile],
        out_shape=[jax.ShapeDtypeStruct((nh, n_rows, w), F32), jax.ShapeDtypeStruct((nh, n_rows, w), BF16)],
        scratch_shapes=[pltpu.VMEM((t, w), F32), pltpu.VMEM((t, w), F32)],
        compiler_params=_params("parallel", "arbitrary"),
    )(q_ext, k_ext, v_ext)


def fox_bwd(name, q_ext, do_ext, k_ext, v_ext, t=None):
    nh, n_rows, w = q_ext.shape
    t = _pick(n_rows, t or TILES["attn"])
    nt = n_rows // t

    def body(q_ref, do_ref, k_ref, v_ref, dq_ref, dk_ref, dv_ref):
        j = pl.program_id(1)

        @pl.when(j == 0)
        def _():
            dq_ref[...] = jnp.zeros_like(dq_ref)

        dk_ref[...] = jnp.zeros_like(dk_ref)
        dv_ref[...] = jnp.zeros_like(dv_ref)
        kj, vj = k_ref[...], v_ref[...]

        def block(i, diagonal):
            off = pl.multiple_of(i * t, t)
            qi, doi = q_ref[pl.ds(off, t), :], do_ref[pl.ds(off, t), :]
            pt = jnp.exp2(lax.dot_general(kj, qi, _DIMS["nt"], preferred_element_type=F32))
            if diagonal:
                keep = lax.broadcasted_iota(jnp.int32, (t, t), 0) <= lax.broadcasted_iota(jnp.int32, (t, t), 1)
                pt = jnp.where(keep, pt, 0.0)
            dst = (pt * lax.dot_general(vj, doi, _DIMS["nt"], preferred_element_type=F32)).astype(BF16)
            dv_ref[...] += jnp.dot(pt.astype(BF16), doi, preferred_element_type=F32)
            dk_ref[...] += jnp.dot(dst, qi, preferred_element_type=F32)
            dq_ref[pl.ds(off, t), :] += lax.dot_general(dst, kj, _DIMS["tn"], preferred_element_type=F32)

        def off_diagonal(i, carry):
            block(i, False)
            return carry

        block(j, True)
        lax.fori_loop(j + 1, nt, off_diagonal, 0)
        lane = lax.broadcasted_iota(jnp.int32, (t, w), 1)
        dk_ref[...] = dk_ref[...] * jnp.where(lane < HEAD_DIM, 1.0 / LOG2E, 1.0)

    whole = pl.BlockSpec((None, n_rows, w), lambda h, j: (h, 0, 0))
    tile = pl.BlockSpec((None, t, w), lambda h, j: (h, j, 0))
    shape = jax.ShapeDtypeStruct((nh, n_rows, w), F32)
    return pl.pallas_call(
        body, name=name, grid=(nh, nt), in_specs=[whole, whole, tile, tile], out_specs=[whole, tile, tile],
        out_shape=[shape, shape, shape],
        compiler_params=_params("parallel", "arbitrary"),
    )(q_ext, do_ext, k_ext, v_ext)


def attn_delta(name, o, do):
    d_model = o.shape[1]
    head_of_col = lax.broadcasted_iota(jnp.int32, (d_model, LANES), 0) // HEAD_DIM
    sel = (head_of_col == lax.broadcasted_iota(jnp.int32, (d_model, LANES), 1)).astype(F32)

    def fn(a, b, s):
        return _split3(jnp.dot(a * b, s, precision=HIGHEST, preferred_element_type=F32))

    return rowcall(name, fn, [o, do], [sel], [(LANES, F32)] * 3)


_PEER_FLIPS = [(bx, by, bc) for bx in (0, 1) for by in (0, 1) for bc in (0, 1)][1:]


def _exchange(name, tensors, scatter):
    n = len(tensors)
    n_peer = len(_PEER_FLIPS)

    def body(*refs):
        ins, outs = refs[:n], refs[n: 2 * n]
        send_sems, recv_sems, local_sems = refs[2 * n:]
        x, y, c = lax.axis_index("x"), lax.axis_index("y"), lax.axis_index("c")
        me = 4 * x + 2 * y + c
        copies = []
        for t in range(n):
            src_me = ins[t].at[me] if scatter else ins[t]
            local = pltpu.make_async_copy(src_me, outs[t].at[me], local_sems.at[t])
            local.start()
            copies.append(local)
            for kk, (bx, by, bc) in enumerate(_PEER_FLIPS):
                px, py, pc = (1 - x if bx else x), (1 - y if by else y), (1 - c if bc else c)
                peer = 4 * px + 2 * py + pc
                out_cp = pltpu.make_async_remote_copy(
                    src_ref=ins[t].at[peer] if scatter else ins[t], dst_ref=outs[t].at[me],
                    send_sem=send_sems.at[t, kk], recv_sem=recv_sems.at[t, kk],
                    device_id=(px, py, pc), device_id_type=pl.DeviceIdType.MESH)
                out_cp.start()
                copies.append(pltpu.make_async_remote_copy(
                    src_ref=ins[t].at[peer] if scatter else ins[t], dst_ref=outs[t].at[peer],
                    send_sem=send_sems.at[t, kk], recv_sem=recv_sems.at[t, kk],
                    device_id=(px, py, pc), device_id_type=pl.DeviceIdType.MESH))
        for cp in copies:
            cp.wait()

    any_spec = pl.BlockSpec(memory_space=pl.ANY)
    out_shape = [jax.ShapeDtypeStruct(t.shape if scatter else (N_DEV,) + t.shape, t.dtype) for t in tensors]
    return pl.pallas_call(
        body, name=name, in_specs=[any_spec] * n, out_specs=[any_spec] * n, out_shape=out_shape,
        scratch_shapes=[pltpu.SemaphoreType.DMA((n, n_peer)), pltpu.SemaphoreType.DMA((n, n_peer)), pltpu.SemaphoreType.DMA((n,))],
        compiler_params=pltpu.CompilerParams(has_side_effects=True),
    )(*tensors)


def all_gather(name, tensors):
    return _exchange(name, tensors, scatter=False)


def all_to_all(name, tensors):
    return _exchange(name, tensors, scatter=True)


def adamw(name, parts, w, m, v, tr=None):
    n_rows, n_cols = w.shape
    tr = _pick(n_rows, tr or TILES["adam"], SUBLANES)
    c1 = 1.0 - ADAM_B1 ** ADAM_STEP
    c2 = 1.0 - ADAM_B2 ** ADAM_STEP

    def body(p_ref, w_ref, m_ref, v_ref, g_ref, d_ref, mo_ref, vo_ref):
        g = p_ref[0].astype(F32)
        for s in range(1, N_DEV):
            g = g + p_ref[s].astype(F32)
        mn = ADAM_B1 * m_ref[...] + (1.0 - ADAM_B1) * g
        vn = ADAM_B2 * v_ref[...] + (1.0 - ADAM_B2) * (g * g)
        g_ref[...] = g
        mo_ref[...] = mn
        vo_ref[...] = vn
        d_ref[...] = -ADAM_LR * ((mn / c1) / (jnp.sqrt(vn / c2) + ADAM_EPS) + ADAM_WD * w_ref[...])

    spec = pl.BlockSpec((tr, n_cols), lambda i: (i, 0))
    return pl.pallas_call(
        body, name=name, grid=(n_rows // tr,),
        in_specs=[pl.BlockSpec((N_DEV, tr, n_cols), lambda i: (0, i, 0)), spec, spec, spec],
        out_specs=[spec] * 4, out_shape=[jax.ShapeDtypeStruct(w.shape, F32)] * 4,
        compiler_params=_params("parallel"),
    )(parts, w, m, v)


def _eye_mask():
    return jnp.eye(GROUPS_PER_BLOCK, dtype=F32)


def _b_blocks(bbr, bbi):
    nb = bbr.shape[0] // (GROUPS_PER_BLOCK * SSM_STATE)
    eye = _eye_mask()[None, :, None, :, None]

    def one(z):
        z = z.reshape(nb, GROUPS_PER_BLOCK, SSM_STATE, SSM_GROUP).transpose(0, 1, 3, 2)
        return z[:, :, :, None, :] * eye

    w = jnp.stack([one(bbr), one(bbi)], axis=3)
    return w.reshape(nb, LANES, BLOCK_COLS)


def _b_blocks_t(dw):
    nb = dw.shape[0]
    d6 = dw.reshape(nb, GROUPS_PER_BLOCK, SSM_GROUP, 2, GROUPS_PER_BLOCK, SSM_STATE)
    diag = jnp.sum(d6 * _eye_mask()[None, :, None, None, :, None], axis=4)
    diag = diag.transpose(3, 0, 1, 4, 2).reshape(2, nb * GROUPS_PER_BLOCK * SSM_STATE, SSM_GROUP)
    return diag[0], diag[1]


def _c_blocks(c_re, c_im):
    nb = c_re.shape[0] // GROUPS_PER_BLOCK
    eye = _eye_mask()[None, :, None, :, None]

    def one(z):
        z = z.reshape(nb, GROUPS_PER_BLOCK, SSM_GROUP, SSM_STATE).transpose(0, 1, 3, 2)
        return z[:, :, :, None, :] * eye

    w = jnp.stack([one(c_re), -one(c_im)], axis=1)
    return w.reshape(nb, BLOCK_COLS, LANES)


def _c_blocks_t(dw):
    nb = dw.shape[0]
    d6 = dw.reshape(nb, 2, GROUPS_PER_BLOCK, SSM_STATE, GROUPS_PER_BLOCK, SSM_GROUP)
    diag = jnp.sum(d6 * _eye_mask()[None, None, :, None, :, None], axis=4)
    diag = diag.transpose(1, 0, 2, 4, 3).reshape(2, nb * GROUPS_PER_BLOCK, SSM_GROUP, SSM_STATE)
    return diag[0], -diag[1]


def _unshard_cols(g):
    s, k, n = g.shape
    return g.transpose(1, 0, 2).reshape(k, s * n)


def _shard_cols(w):
    k, n = w.shape
    return w.reshape(k, N_DEV, n // N_DEV).transpose(1, 0, 2)


def _pack(arrays):
    chunks, offs, row = [], [], 0
    for a in arrays:
        flat = a.reshape(-1).astype(F32)
        rows = -(-flat.shape[0] // LANES)
        chunks.append(jnp.pad(flat, (0, rows * LANES - flat.shape[0])))
        offs.append((row, rows))
        row += rows
    pad_rows = (-row) % SUBLANES
    if pad_rows:
        chunks.append(jnp.zeros((pad_rows * LANES,), F32))
    return jnp.concatenate(chunks).reshape(row + pad_rows, LANES), offs


def _unpack(packed, offs, shapes):
    out = []
    for (row, rows), shp in zip(offs, shapes):
        size = 1
        for s in shp:
            size *= s
        out.append(packed[row: row + rows].reshape(-1)[:size].reshape(shp))
    return out


SCAN_LANES = 256


def _scan_chunk_tables(lam_ref, c0, reverse):
    shape = (SUBLANES, SCAN_LANES)
    lr = jnp.broadcast_to(lam_ref[0:1, c0:c0 + SCAN_LANES], shape)
    li = jnp.broadcast_to(lam_ref[1:2, c0:c0 + SCAN_LANES], shape)
    if reverse:
        li = -li
    row = lax.broadcasted_iota(jnp.int32, shape, 0)
    tt = (SUBLANES - 1 - row) if reverse else row
    l1 = (lr, li)
    l2 = _cmul(*l1, *l1)
    l4 = _cmul(*l2, *l2)
    pr, pi = l1
    for bit, lp in enumerate((l1, l2, l4)):
        qr, qi = _cmul(pr, pi, *lp)
        on = ((tt >> bit) & 1) == 1
        pr, pi = jnp.where(on, qr, pr), jnp.where(on, qi, pi)
    steps = []
    for d, lp in ((1, l1), (2, l2), (4, l4)):
        ok = tt >= d
        steps.append((d, jnp.where(ok, lp[0], 0.0), jnp.where(ok, lp[1], 0.0)))
    return steps, (pr, pi)


def _scan_in_place(buf_ref, lam_ref, carry_ref, ng, reverse, prev_ref=None, acc_ref=None):
    shape = (SUBLANES, SCAN_LANES)
    last = 0 if reverse else SUBLANES - 1
    first_row = lax.broadcasted_iota(jnp.int32, shape, 0) == 0
    for c0 in range(0, BLOCK_STATE, SCAN_LANES):
        re_cols, im_cols = pl.ds(c0, SCAN_LANES), pl.ds(BLOCK_STATE + c0, SCAN_LANES)
        steps, (pr, pi) = _scan_chunk_tables(lam_ref, c0, reverse)

        def group(r, carry):
            cr, ci = carry[0], carry[1]
            rr = (ng - 1 - r) if reverse else r
            off = pl.multiple_of(rr * SUBLANES, SUBLANES)
            xr, xi = buf_ref[pl.ds(off, SUBLANES), re_cols], buf_ref[pl.ds(off, SUBLANES), im_cols]
            for d, mr, mi in steps:
                sh = (SUBLANES - d) if reverse else d
                yr, yi = pltpu.roll(xr, sh, 0), pltpu.roll(xi, sh, 0)
                xr, xi = xr + mr * yr - mi * yi, xi + mr * yi + mi * yr
            xr, xi = xr + pr * cr - pi * ci, xi + pr * ci + pi * cr
            buf_ref[pl.ds(off, SUBLANES), re_cols] = xr
            buf_ref[pl.ds(off, SUBLANES), im_cols] = xi
            out = (jnp.broadcast_to(xr[last:last + 1, :], shape), jnp.broadcast_to(xi[last:last + 1, :], shape))
            if prev_ref is not None:
                off8 = pl.multiple_of(off + SUBLANES, SUBLANES)
                before_r, before_i = prev_ref[pl.ds(off, SUBLANES), re_cols], prev_ref[pl.ds(off, SUBLANES), im_cols]
                same_r, same_i = prev_ref[pl.ds(off8, SUBLANES), re_cols], prev_ref[pl.ds(off8, SUBLANES), im_cols]
                sr = jnp.where(first_row, jnp.broadcast_to(before_r[SUBLANES - 1:, :], shape), pltpu.roll(same_r, 1, 0))
                si = jnp.where(first_row, jnp.broadcast_to(before_i[SUBLANES - 1:, :], shape), pltpu.roll(same_i, 1, 0))
                out += (carry[2] + xr * sr + xi * si, carry[3] + xi * sr - xr * si)
            return out

        init = (carry_ref[0, :, c0:c0 + SCAN_LANES], carry_ref[1, :, c0:c0 + SCAN_LANES])
        if prev_ref is not None:
            init += (acc_ref[0, :, c0:c0 + SCAN_LANES], acc_ref[1, :, c0:c0 + SCAN_LANES])
        res = lax.fori_loop(0, ng, group, init)
        carry_ref[0, :, c0:c0 + SCAN_LANES] = res[0]
        carry_ref[1, :, c0:c0 + SCAN_LANES] = res[1]
        if prev_ref is not None:
            acc_ref[0, :, c0:c0 + SCAN_LANES] = res[2]
            acc_ref[1, :, c0:c0 + SCAN_LANES] = res[3]


def s5_fused_fwd(name, u, wb, wc, lam, d_row, tm=None):
    n_rows, d_model = u.shape
    nb = lam.shape[0]
    tm = _pick(n_rows, tm or TILES["scan"], 2 * SUBLANES)
    nt = n_rows // tm

    def body(u_ref, wb_ref, wc_ref, lam_ref, d_ref, z_ref, st_ref, buf_ref, carry_ref):
        @pl.when(pl.program_id(1) == 0)
        def _():
            carry_ref[...] = jnp.zeros_like(carry_ref)

        uu = u_ref[...]
        buf_ref[...] = jnp.dot(uu.astype(BF16), wb_ref[...], preferred_element_type=F32)
        _scan_in_place(buf_ref, lam_ref, carry_ref, tm // SUBLANES, False)
        st = buf_ref[...].astype(BF16)
        st_ref[...] = st
        z_ref[...] = _gelu(jnp.dot(st, wc_ref[...], preferred_element_type=F32) + d_ref[...] * uu)

    tile = lambda b, it: (it, b)
    blk = lambda b, it: (b, 0, 0)
    return pl.pallas_call(
        body, name=name, grid=(nb, nt),
        in_specs=[pl.BlockSpec((tm, LANES), tile), pl.BlockSpec((None, LANES, BLOCK_COLS), blk),
                  pl.BlockSpec((None, BLOCK_COLS, LANES), blk), pl.BlockSpec((None, 2, BLOCK_STATE), blk),
                  pl.BlockSpec((1, LANES), lambda b, it: (0, b))],
        out_specs=[pl.BlockSpec((tm, LANES), tile), pl.BlockSpec((tm, BLOCK_COLS), tile)],
        out_shape=[jax.ShapeDtypeStruct((n_rows, d_model), F32), jax.ShapeDtypeStruct((n_rows, nb * BLOCK_COLS), BF16)],
        scratch_shapes=[pltpu.VMEM((tm, BLOCK_COLS), F32), pltpu.VMEM((2, SUBLANES, BLOCK_STATE), F32)],
        compiler_params=_params("parallel", "arbitrary"),
    )(u, wb, wc, lam, d_row)


def s5_fused_bwd(name, dz, u, st, wb, wc, lam, d_row, tm=None):
    n_rows, d_model = u.shape
    nb = lam.shape[0]
    tm = _pick(n_rows, tm or TILES["scan"], 2 * SUBLANES)
    nt = n_rows // tm
    tail_rows = 2 * SUBLANES

    def body(dz_ref, u_ref, st_ref, tail_ref, wb_ref, wc_ref, lam_ref, d_ref,
             du_ref, dwb_ref, dwc_ref, dlam_ref, dd_ref, buf_ref, prev_ref, carry_ref, acc_ref):
        it = pl.program_id(1)

        @pl.when(it == 0)
        def _():
            carry_ref[...] = jnp.zeros_like(carry_ref)
            acc_ref[...] = jnp.zeros_like(acc_ref)
            dwb_ref[...] = jnp.zeros_like(dwb_ref)
            dwc_ref[...] = jnp.zeros_like(dwc_ref)
            dd_ref[...] = jnp.zeros_like(dd_ref)

        uu, st_b = u_ref[...], st_ref[...]
        prev_ref[SUBLANES:, :] = st_b.astype(F32)
        before = tail_ref[...].astype(F32)[SUBLANES:, :]
        prev_ref[:SUBLANES, :] = jnp.where(it == nt - 1, 0.0, before)
        y = jnp.dot(st_b, wc_ref[...], preferred_element_type=F32) + d_ref[...] * uu
        _, vjp = jax.vjp(_gelu, y)
        dy = vjp(dz_ref[...])[0]
        dyb = dy.astype(BF16)
        dd_ref[...] += jnp.sum(dy * uu, axis=0, keepdims=True)
        dwc_ref[...] += lax.dot_general(st_b, dyb, _DIMS["tn"], preferred_element_type=F32)
        buf_ref[...] = lax.dot_general(dyb, wc_ref[...], _DIMS["nt"], preferred_element_type=F32)
        _scan_in_place(buf_ref, lam_ref, carry_ref, tm // SUBLANES, True, prev_ref, acc_ref)
        gb = buf_ref[...].astype(BF16)
        du_ref[...] = lax.dot_general(gb, wb_ref[...], _DIMS["nt"], preferred_element_type=F32) + dy * d_ref[...]
        dwb_ref[...] += lax.dot_general(uu.astype(BF16), gb, _DIMS["tn"], preferred_element_type=F32)

        @pl.when(it == nt - 1)
        def _():
            dlam_ref[0:1, :] = jnp.sum(acc_ref[0], axis=0, keepdims=True)
            dlam_ref[1:2, :] = jnp.sum(acc_ref[1], axis=0, keepdims=True)

    tile = lambda b, it: (nt - 1 - it, b)
    blk = lambda b, it: (b, 0, 0)
    per_tile = tm // tail_rows
    tail = lambda b, it: (jnp.maximum((nt - 1 - it) * per_tile - 1, 0), b)
    return pl.pallas_call(
        body, name=name, grid=(nb, nt),
        in_specs=[pl.BlockSpec((tm, LANES), tile), pl.BlockSpec((tm, LANES), tile), pl.BlockSpec((tm, BLOCK_COLS), tile),
                  pl.BlockSpec((tail_rows, BLOCK_COLS), tail), pl.BlockSpec((None, LANES, BLOCK_COLS), blk),
                  pl.BlockSpec((None, BLOCK_COLS, LANES), blk), pl.BlockSpec((None, 2, BLOCK_STATE), blk),
                  pl.BlockSpec((1, LANES), lambda b, it: (0, b))],
        out_specs=[pl.BlockSpec((tm, LANES), tile), pl.BlockSpec((None, LANES, BLOCK_COLS), blk),
                   pl.BlockSpec((None, BLOCK_COLS, LANES), blk), pl.BlockSpec((None, 2, BLOCK_STATE), blk),
                   pl.BlockSpec((1, LANES), lambda b, it: (0, b))],
        out_shape=[jax.ShapeDtypeStruct((n_rows, d_model), F32), jax.ShapeDtypeStruct((nb, LANES, BLOCK_COLS), F32),
                   jax.ShapeDtypeStruct((nb, BLOCK_COLS, LANES), F32), jax.ShapeDtypeStruct((nb, 2, BLOCK_STATE), F32),
                   jax.ShapeDtypeStruct((1, d_model), F32)],
        scratch_shapes=[pltpu.VMEM((tm, BLOCK_COLS), F32), pltpu.VMEM((tm + SUBLANES, BLOCK_COLS), F32),
                        pltpu.VMEM((2, SUBLANES, BLOCK_STATE), F32), pltpu.VMEM((2, SUBLANES, BLOCK_STATE), F32)],
        compiler_params=_params("parallel", "arbitrary"),
    )(dz, u, st, st, wb, wc, lam, d_row)


def s5_fwd(tag, u, log_dt, a_re, a_im, b_re, b_im, c_re, c_im, d_row):
    d_model = u.shape[1]
    nb = d_model // LANES
    col = lambda a: a.reshape(-1, 1)
    prm = [col(jnp.repeat(log_dt, SSM_STATE)), col(a_re), col(a_im), b_re.reshape(-1, SSM_GROUP), b_im.reshape(-1, SSM_GROUP)]
    lr, li, bbr, bbi = rowcall(f"s5_prep_{tag}", _s5_discretise, prm, [], [(1, F32), (1, F32), (SSM_GROUP, F32), (SSM_GROUP, F32)])
    lam = jnp.stack([lr.reshape(nb, BLOCK_STATE), li.reshape(nb, BLOCK_STATE)], axis=1)
    wb = _b_blocks(bbr, bbi).astype(BF16)
    wc = _c_blocks(c_re, c_im).astype(BF16)
    z, st = s5_fused_fwd(f"s5_fwd_{tag}", u, wb, wc, lam, d_row)
    return z, dict(prm=prm, lam=lam, wb=wb, wc=wc, st=st)


def s5_bwd(tag, dz, u, d_row, sv):
    d_model = u.shape[1]
    nb = d_model // LANES
    n_groups = d_model // SSM_GROUP
    col = lambda a: a.reshape(-1, 1)

    dhn, d_wb, d_wc, d_lam, d_dskip = s5_fused_bwd(f"s5_bwd_{tag}", dz, u, sv["st"], sv["wb"], sv["wc"], sv["lam"], d_row)
    d_bbr, d_bbi = _b_blocks_t(d_wb)
    d_cre, d_cim = _c_blocks_t(d_wc)

    def prep_bwd(ldt, ar, ai, br, bi, dlr, dli, dbr, dbi):
        _, vjp = jax.vjp(_s5_discretise, ldt, ar, ai, br, bi)
        return vjp((dlr, dli, dbr, dbi))

    d_ldt, d_are, d_aim, d_bre, d_bim = rowcall(
        f"s5_prep_bwd_{tag}", prep_bwd, sv["prm"] + [col(d_lam[:, 0]), col(d_lam[:, 1]), d_bbr, d_bbi], [],
        [(1, F32), (1, F32), (1, F32), (SSM_GROUP, F32), (SSM_GROUP, F32)])
    d_logdt = rowcall(f"s5_dlogdt_{tag}", lambda a: jnp.sum(a, axis=1, keepdims=True), [d_ldt.reshape(n_groups, SSM_STATE)], [], [(1, F32)])[0]
    grads = dict(log_dt=d_logdt.reshape(n_groups), a_re=d_are.reshape(n_groups, SSM_STATE), a_im=d_aim.reshape(n_groups, SSM_STATE),
                 b_re=d_bre.reshape(n_groups, SSM_STATE, SSM_GROUP), b_im=d_bim.reshape(n_groups, SSM_STATE, SSM_GROUP),
                 c_re=d_cre, c_im=d_cim, d=d_dskip)
    return dhn, grads


def kernel(x, mix_norm, mlp_norm, mlp_w1, mlp_w2, ssm_log_dt, ssm_a_re, ssm_a_im, ssm_b_re, ssm_b_im, ssm_c_re, ssm_c_im, ssm_d, ssm_w_glu, kv_norm, w_kvf, b_f, attn_wq, attn_wo, final_norm, loss_target, m_mix_norm, m_mlp_norm, m_mlp_w1, m_mlp_w2, m_ssm_log_dt, m_ssm_a_re, m_ssm_a_im, m_ssm_b_re, m_ssm_b_im, m_ssm_c_re, m_ssm_c_im, m_ssm_d, m_ssm_w_glu, m_kv_norm, m_w_kvf, m_b_f, m_attn_wq, m_attn_wo, m_final_norm, v_mix_norm, v_mlp_norm, v_mlp_w1, v_mlp_w2, v_ssm_log_dt, v_ssm_a_re, v_ssm_a_im, v_ssm_b_re, v_ssm_b_im, v_ssm_c_re, v_ssm_c_im, v_ssm_d, v_ssm_w_glu, v_kv_norm, v_w_kvf, v_b_f, v_attn_wq, v_attn_wo, v_final_norm):
    n_rows, d_model = x.shape[1], x.shape[2]
    depth = mix_norm.shape[0]
    n_a = ssm_log_dt.shape[0]
    n_b = depth - n_a
    n_heads = d_model // HEAD_DIM
    n_groups = d_model // SSM_GROUP
    nb = n_groups // GROUPS_PER_BLOCK
    kvf_cols = 2 * d_model + n_heads
    kvf_pad = 2 * d_model + LANES

    g_w1, g_w2, g_glu, g_kvf, g_wq, g_wo, g_d = all_gather(
        "gather_weights",
        [mlp_w1.astype(BF16), mlp_w2.astype(BF16), ssm_w_glu.astype(BF16), w_kvf.astype(BF16),
         attn_wq.astype(BF16), attn_wo.astype(BF16), ssm_d])
    w1 = [_unshard_cols(g_w1[:, i]) for i in range(depth)]
    w2 = [g_w2[:, i].reshape(-1, d_model) for i in range(depth)]
    wglu = [_unshard_cols(g_glu[:, i]) for i in range(n_a)]
    wkvf = _unshard_cols(g_kvf)
    wkvf = jnp.pad(wkvf, ((0, 0), (0, kvf_pad - kvf_cols)))
    wq = [g_wq[:, j].reshape(-1, d_model) for j in range(n_b)]
    wo = [g_wo[:, j].reshape(-1, d_model) for j in range(n_b)]
    d_skip = [g_d[:, i].reshape(1, d_model) for i in range(n_a)]

    row = lambda a: a.reshape(1, -1)
    col = lambda a: a.reshape(-1, 1)

    h = x[0]
    saved = []
    k_ext = v_ext = None
    q_consts = _lane_consts([(F_LANE, F_LANE + 3, 1.0)])
    k_consts = _lane_consts([(LSE_LANE, LSE_LANE + 3, -1.0), (ROWSUM_LANE, ROWSUM_LANE + 1, 1.0)])
    v_consts = _lane_consts([(SUM_LANE, SUM_LANE + 1, 1.0), (DELTA_LANE, DELTA_LANE + 3, -1.0)])
    for i in range(depth):
        sv = {"h": h}
        hn = rms_fwd(f"mix_norm_{i}", h, row(mix_norm[i]))
        sv["hn"] = hn
        if i < n_a:
            z, s5_saved = s5_fwd(str(i), hn, ssm_log_dt[i], ssm_a_re[i], ssm_a_im[i], ssm_b_re[i], ssm_b_im[i],
                                 ssm_c_re[i], ssm_c_im[i], d_skip[i])
            zw = matmul(f"s5_glu_{i}", z, wglu[i])
            h1 = rowcall(f"s5_gate_{i}", lambda hh, zz: hh + zz[:, :d_model] * _sigmoid(zz[:, d_model:]), [h, zw], [], [(d_model, F32)])[0]
            sv.update(s5=s5_saved, z=z, zw=zw)
        else:
            j = i - n_a
            q = matmul(f"attn_q_{j}", hn, wq[j], scale=LOG2E * HEAD_DIM ** -0.5)
            o_ext, q_ext_b = fox_fwd(f"attn_fwd_{j}", pack_heads(f"pack_q_{j}", q, 0, n_heads, q_consts), k_ext, v_ext)
            o2 = unpack_heads(f"unpack_o_{j}", [o_ext])[0]
            h1 = matmul(f"attn_o_{j}", o2, wo[j], resid=h)
            sv.update(q_ext_b=q_ext_b, o2=o2)
        h2n = rms_fwd(f"mlp_norm_{i}", h1, row(mlp_norm[i]))
        ap = matmul(f"mlp_up_{i}", h2n, w1[i])
        h = matmul(f"mlp_down_{i}", ap, w2[i], a_fn=_sqrelu, resid=h1)
        sv.update(h1=h1, h2n=h2n, ap=ap)
        saved.append(sv)
        if i == n_a - 1:
            h_mid = h
            hk = rms_fwd("kv_norm", h, row(kv_norm))
            kvf = matmul("kvf_proj", hk, wkvf)
            fl = kvf[:, 2 * d_model:]
            bfp = jnp.pad(row(b_f), ((0, 0), (0, LANES - n_heads)))
            k_ext = pack_heads("pack_k", kvf, 0, n_heads, k_consts, cum_logf("cum_logf", fl, bfp), F_LANE)
            v_ext = pack_heads("pack_v", kvf, 1, n_heads, v_consts)

    def loss_fn(hh, tgt, g):
        y, vjp = jax.vjp(_rms, hh, g)
        err = y - tgt
        part = 0.5 * jnp.sum(jnp.mean(err * err, axis=-1, keepdims=True), axis=0, keepdims=True)
        dh, dg = vjp(err * (1.0 / d_model))
        return dh, jnp.broadcast_to(part, (1, LANES)), dg

    dh, loss_part, d_final = rowcall("loss_head", loss_fn, [h, loss_target[0]], [row(final_norm)],
                                     [(d_model, F32)], [(1, LANES), (1, d_model)])

    g_mix, g_mlpn = [None] * depth, [None] * depth
    g_w1f, g_w2f = [None] * depth, [None] * depth
    g_ssm = [None] * n_a
    g_wqf, g_wof = [None] * n_b, [None] * n_b
    dk_acc, dv_acc, df_plus = [], [], []
    g_kv = None
    for i in reversed(range(depth)):
        sv = saved[i]
        if i == n_a - 1:
            dk, col_sums = unpack_heads("unpack_dk", dk_acc, extract_lane=HEAD_DIM)
            dv = unpack_heads("unpack_dv", dv_acc)[0]
            dfl, db_f = cum_logf_bwd("cum_logf_bwd", fl, bfp, df_plus, [col_sums])
            dkvf = jnp.concatenate([dk, dv, dfl], axis=1)
            dhk = matmul("kvf_dx", dkvf, wkvf, "nt")
            d_wkvf = matmul("kvf_dw", hk, dkvf, "tn")
            dh, d_kvn = rms_bwd("kv_norm_bwd", h_mid, row(kv_norm), dhk, add=dh)
            g_kv = (d_wkvf[:, :kvf_cols], d_kvn, db_f[:, :n_heads])
        dap = matmul(f"mlp_down_dx_{i}", dh, w2[i], "nt", post=lambda acc, apt: acc * (2.0 * jnp.maximum(apt, 0.0)), post_arg=sv["ap"])
        g_w2f[i] = matmul(f"mlp_down_dw_{i}", sv["ap"], dh, "tn", a_fn=_sqrelu)
        dh2n = matmul(f"mlp_up_dx_{i}", dap, w1[i], "nt")
        g_w1f[i] = matmul(f"mlp_up_dw_{i}", sv["h2n"], dap, "tn")
        dh1, g_mlpn[i] = rms_bwd(f"mlp_norm_bwd_{i}", sv["h1"], row(mlp_norm[i]), dh2n, add=dh)
        if i < n_a:
            def glu_bwd(zz, dd):
                val, gate = zz[:, :d_model], zz[:, d_model:]
                sg = _sigmoid(gate)
                return jnp.concatenate([dd * sg, dd * val * sg * (1.0 - sg)], axis=1)

            dzw = rowcall(f"s5_gate_bwd_{i}", glu_bwd, [sv["zw"], dh1], [], [(2 * d_model, F32)])[0]
            dz = matmul(f"s5_glu_dx_{i}", dzw, wglu[i], "nt")
            d_wglu = matmul(f"s5_glu_dw_{i}", sv["z"], dzw, "tn")

            dhn, g_ssm[i] = s5_bwd(str(i), dz, sv["hn"], d_skip[i], sv["s5"])
            g_ssm[i]["w_glu"] = d_wglu
        else:
            j = i - n_a
            do2 = matmul(f"attn_o_dx_{j}", dh1, wo[j], "nt")
            g_wof[j] = matmul(f"attn_o_dw_{j}", sv["o2"], dh1, "tn")
            do_ext = pack_heads(f"pack_do_{j}", do2, 0, n_heads, jnp.zeros((1, LANES), F32),
                                attn_delta(f"attn_delta_{j}", sv["o2"], do2), DELTA_LANE)
            dq_ext, dk_ext, dv_ext = fox_bwd(f"attn_bwd_{j}", sv["q_ext_b"], do_ext, k_ext, v_ext)
            dk_acc.append(dk_ext)
            dv_acc.append(dv_ext)
            dq2, row_sums = unpack_heads(f"unpack_dq_{j}", [dq_ext], extract_lane=ROWSUM_LANE)
            df_plus.append(row_sums)
            dhn = matmul(f"attn_q_dx_{j}", dq2, wq[j], "nt", scale=HEAD_DIM ** -0.5)
            g_wqf[j] = matmul(f"attn_q_dw_{j}", sv["hn"], dq2, "tn", scale=HEAD_DIM ** -0.5)
        dh, g_mix[i] = rms_bwd(f"mix_norm_bwd_{i}", sv["h"], row(mix_norm[i]), dhn, add=dh1)
    grad_x = dh[None]

    stack = lambda xs: jnp.stack(xs, axis=1).astype(BF16 if xs[0].ndim == 3 else F32)
    parts = all_to_all("scatter_grads", [
        stack([_shard_cols(g) for g in g_w1f]),
        stack([g.reshape(N_DEV, -1, d_model) for g in g_w2f]),
        stack([_shard_cols(g["w_glu"]) for g in g_ssm]),
        _shard_cols(g_kv[0]).astype(BF16),
        stack([g.reshape(N_DEV, -1, d_model) for g in g_wqf]),
        stack([g.reshape(N_DEV, -1, d_model) for g in g_wof]),
        stack([g["d"].reshape(N_DEV, -1) for g in g_ssm]),
    ])
    small_names = ["mix_norm", "mlp_norm", "ssm_log_dt", "ssm_a_re", "ssm_a_im", "ssm_b_re", "ssm_b_im", "ssm_c_re", "ssm_c_im",
                   "kv_norm", "b_f", "final_norm"]
    small_grads = [jnp.concatenate(g_mix, axis=0), jnp.concatenate(g_mlpn, axis=0)]
    small_grads += [jnp.stack([g[kk] for g in g_ssm]) for kk in ("log_dt", "a_re", "a_im", "b_re", "b_im", "c_re", "c_im")]
    small_grads += [g_kv[1], g_kv[2], d_final]
    small_w = [mix_norm, mlp_norm, ssm_log_dt, ssm_a_re, ssm_a_im, ssm_b_re, ssm_b_im, ssm_c_re, ssm_c_im, kv_norm, b_f, final_norm]
    small_m = [m_mix_norm, m_mlp_norm, m_ssm_log_dt, m_ssm_a_re, m_ssm_a_im, m_ssm_b_re, m_ssm_b_im, m_ssm_c_re, m_ssm_c_im, m_kv_norm, m_b_f, m_final_norm]
    small_v = [v_mix_norm, v_mlp_norm, v_ssm_log_dt, v_ssm_a_re, v_ssm_a_im, v_ssm_b_re, v_ssm_b_im, v_ssm_c_re, v_ssm_c_im, v_kv_norm, v_b_f, v_final_norm]
    loss_slot = jnp.zeros((LANES,), F32)
    packed_g, offs = _pack(small_grads + [loss_part])
    packed_w, _ = _pack(small_w + [loss_slot])
    packed_m, _ = _pack(small_m + [loss_slot])
    packed_v, _ = _pack(small_v + [loss_slot])
    (small_parts,) = all_gather("gather_small_grads", [packed_g])

    res = {}

    def update(nm, part, w, m, v):
        shp = w.shape
        as2d = lambda a: a.reshape(-1, shp[-1])
        outs = adamw(f"adamw_{nm}", part.reshape((N_DEV,) + as2d(w).shape), as2d(w), as2d(m), as2d(v))
        res[nm] = [o.reshape(shp) for o in outs]

    update("mlp_w1", parts[0], mlp_w1, m_mlp_w1, v_mlp_w1)
    update("mlp_w2", parts[1], mlp_w2, m_mlp_w2, v_mlp_w2)
    update("ssm_w_glu", parts[2], ssm_w_glu, m_ssm_w_glu, v_ssm_w_glu)
    update("w_kvf", parts[3], w_kvf, m_w_kvf, v_w_kvf)
    update("attn_wq", parts[4], attn_wq, m_attn_wq, v_attn_wq)
    update("attn_wo", parts[5], attn_wo, m_attn_wo, v_attn_wo)
    update("ssm_d", parts[6], ssm_d, m_ssm_d, v_ssm_d)
    small_out = adamw("adamw_small", small_parts, packed_w, packed_m, packed_v)
    shapes = [w.shape for w in small_w] + [(LANES,)]
    unpacked = [_unpack(o, offs, shapes) for o in small_out]
    for idx, nm in enumerate(small_names):
        res[nm] = [u[idx] for u in unpacked]
    loss = unpacked[0][-1][0]

    order = ["mix_norm", "mlp_norm", "mlp_w1", "mlp_w2", "ssm_log_dt", "ssm_a_re", "ssm_a_im", "ssm_b_re", "ssm_b_im", "ssm_c_re",
             "ssm_c_im", "ssm_d", "ssm_w_glu", "kv_norm", "w_kvf", "b_f", "attn_wq", "attn_wo", "final_norm"]
    out = [loss, grad_x]
    for kind in range(4):
        out += [res[nm][kind] for nm in order]
    return tuple(out)
```

```python
import functools

import jax
import jax.numpy as jnp
from jax import lax
from jax.experimental import pallas as pl
from jax.experimental.pallas import tpu as pltpu

F32 = jnp.float32
BF16 = jnp.bfloat16
HIGHEST = lax.Precision.HIGHEST

V7X_VMEM_BYTES = 64 << 20
VMEM_LIMIT_BYTES = (V7X_VMEM_BYTES * 3) // 4
LANES = 128
SUBLANES = 8

N_DEV = 8
RMS_EPS = 1e-6
SSM_GROUP = 16
SSM_STATE = 64
HEAD_DIM = 64
GROUPS_PER_BLOCK = LANES // SSM_GROUP
BLOCK_STATE = GROUPS_PER_BLOCK * SSM_STATE
BLOCK_COLS = 2 * BLOCK_STATE
NEG_BIG = -1e30
LOG2E = 1.4426950408889634

ADAM_LR = 0.001
ADAM_B1 = 0.9
ADAM_B2 = 0.999
ADAM_EPS = 1e-08
ADAM_WD = 0.01
ADAM_STEP = 10

TILES = {"row": 512, "mm": (1024, 1024, 1024), "blk": 512, "scan": 512, "cum": 256, "attn": 1024, "adam": 256}


def _pick(dim, pref, align=LANES):
    if dim <= pref:
        return dim
    for a in (align, SUBLANES):
        d = (pref // a) * a
        while d >= a:
            if dim % d == 0:
                return d
            d -= a
    return dim


def _params(*sem):
    return pltpu.CompilerParams(dimension_semantics=sem, vmem_limit_bytes=VMEM_LIMIT_BYTES)


def rowcall(name, fn, rows, consts, out_rows, out_accs=(), tm=None):
    n_rows = rows[0].shape[0]
    tm = _pick(n_rows, tm or TILES["row"], SUBLANES)
    nr, nc, no, na = len(rows), len(consts), len(out_rows), len(out_accs)

    def body(*refs):
        ins = [r[...] for r in refs[: nr + nc]]
        outs = fn(*ins)
        if not isinstance(outs, (tuple, list)):
            outs = (outs,)
        for r, o in zip(refs[nr + nc: nr + nc + no], outs[:no]):
            r[...] = o.astype(r.dtype)
        if na:
            i = pl.program_id(0)
            for r, o in zip(refs[nr + nc + no:], outs[no:]):
                @pl.when(i == 0)
                def _(r=r, o=o):
                    r[...] = o

                @pl.when(i > 0)
                def _(r=r, o=o):
                    r[...] += o

    in_specs = [pl.BlockSpec((tm, a.shape[1]), lambda i: (i, 0)) for a in rows]
    in_specs += [pl.BlockSpec(c.shape, lambda i, n=c.ndim: (0,) * n) for c in consts]
    out_shape = [jax.ShapeDtypeStruct((n_rows, c), dt) for c, dt in out_rows]
    out_specs = [pl.BlockSpec((tm, c), lambda i: (i, 0)) for c, _ in out_rows]
    out_shape += [jax.ShapeDtypeStruct(s, F32) for s in out_accs]
    out_specs += [pl.BlockSpec(s, lambda i, n=len(s): (0,) * n) for s in out_accs]
    res = pl.pallas_call(
        body, name=name, grid=(n_rows // tm,), in_specs=in_specs, out_specs=out_specs, out_shape=out_shape,
        compiler_params=_params("arbitrary" if na else "parallel"),
    )(*rows, *consts)
    return res


_DIMS = {"nn": (((1,), (0,)), ((), ())), "nt": (((1,), (1,)), ((), ())), "tn": (((0,), (0,)), ((), ()))}


def matmul(name, a, b, mode="nn", *, a_fn=None, scale=None, resid=None, post=None, post_arg=None,
           out_dtype=F32, bm=None, bn=None, bk=None):
    if mode == "nn":
        (m, k), (k2, n) = a.shape, b.shape
    elif mode == "nt":
        (m, k), (n, k2) = a.shape, b.shape
    else:
        (k, m), (k2, n) = a.shape, b.shape
    assert k == k2, (name, a.shape, b.shape, mode)
    bm, bn, bk = _pick(m, bm or TILES["mm"][0]), _pick(n, bn or TILES["mm"][1]), _pick(k, bk or TILES["mm"][2])
    nk = k // bk
    a_spec = pl.BlockSpec((bk, bm), lambda i, j, kk: (kk, i)) if mode == "tn" else pl.BlockSpec((bm, bk), lambda i, j, kk: (i, kk))
    b_spec = pl.BlockSpec((bn, bk), lambda i, j, kk: (j, kk)) if mode == "nt" else pl.BlockSpec((bk, bn), lambda i, j, kk: (kk, j))
    mn_spec = pl.BlockSpec((bm, bn), lambda i, j, kk: (i, j))
    extra = [x for x in (resid, post_arg) if x is not None]
    has_resid, has_post = resid is not None, post is not None

    def body(*refs):
        a_ref, b_ref = refs[0], refs[1]
        ex = refs[2: 2 + len(extra)]
        o_ref = refs[2 + len(extra)]
        av = a_ref[...]
        if a_fn is not None:
            av = a_fn(av.astype(F32))
        p = lax.dot_general(av.astype(BF16), b_ref[...].astype(BF16), _DIMS[mode], preferred_element_type=F32)

        def finish(acc):
            if scale is not None:
                acc = acc * scale
            idx = 0
            if has_resid:
                acc = acc + ex[idx][...]
                idx += 1
            if has_post:
                acc = post(acc, ex[idx][...])
            o_ref[...] = acc.astype(o_ref.dtype)

        if nk == 1:
            finish(p)
        else:
            acc_ref = refs[-1]
            kk = pl.program_id(2)

            @pl.when(kk == 0)
            def _():
                acc_ref[...] = p

            @pl.when(kk > 0)
            def _():
                acc_ref[...] += p

            @pl.when(kk == nk - 1)
            def _():
                finish(acc_ref[...])

    return pl.pallas_call(
        body, name=name, grid=(m // bm, n // bn, nk),
        in_specs=[a_spec, b_spec] + [mn_spec] * len(extra), out_specs=mn_spec,
        out_shape=jax.ShapeDtypeStruct((m, n), out_dtype),
        scratch_shapes=[pltpu.VMEM((bm, bn), F32)] if nk > 1 else [],
        compiler_params=_params("parallel", "parallel", "arbitrary"),
    )(a, b, *extra)


S5_DOT_PASSES = 1


def _block_dot(a, b, mode):
    if S5_DOT_PASSES == 6:
        return lax.dot_general(a, b, _DIMS[mode], precision=HIGHEST, preferred_element_type=F32)
    dot = lambda x, y: lax.dot_general(x, y, _DIMS[mode], preferred_element_type=F32)
    a_hi, b_hi = a.astype(BF16), b.astype(BF16)
    if S5_DOT_PASSES == 1:
        return dot(a_hi, b_hi)
    a_lo, b_lo = (a - a_hi.astype(F32)).astype(BF16), (b - b_hi.astype(F32)).astype(BF16)
    return dot(a_hi, b_hi) + (dot(a_hi, b_lo) + dot(a_lo, b_hi))


def bmm(name, a, w, mode, tm=None):
    n_rows = a.shape[0]
    nb, ka, kb = w.shape
    ca, co = (ka, kb) if mode == "nn" else (kb, ka)
    assert a.shape[1] == nb * ca
    tm = _pick(n_rows, tm or TILES["blk"], SUBLANES)

    def body(a_ref, w_ref, o_ref):
        o_ref[...] = _block_dot(a_ref[...], w_ref[...], mode)

    return pl.pallas_call(
        body, name=name, grid=(n_rows // tm, nb),
        in_specs=[pl.BlockSpec((tm, ca), lambda i, b: (i, b)), pl.BlockSpec((None, ka, kb), lambda i, b: (b, 0, 0))],
        out_specs=pl.BlockSpec((tm, co), lambda i, b: (i, b)),
        out_shape=jax.ShapeDtypeStruct((n_rows, nb * co), F32),
        compiler_params=_params("parallel", "parallel"),
    )(a, w)


def bmm_tn(name, a, b, nb, tk=None):
    n_rows = a.shape[0]
    ka, kb = a.shape[1] // nb, b.shape[1] // nb
    tk = _pick(n_rows, tk or TILES["blk"], SUBLANES)

    def body(a_ref, b_ref, o_ref):
        p = _block_dot(a_ref[...], b_ref[...], "tn")
        kk = pl.program_id(1)

        @pl.when(kk == 0)
        def _():
            o_ref[...] = p

        @pl.when(kk > 0)
        def _():
            o_ref[...] += p

    return pl.pallas_call(
        body, name=name, grid=(nb, n_rows // tk),
        in_specs=[pl.BlockSpec((tk, ka), lambda bb, kk: (kk, bb)), pl.BlockSpec((tk, kb), lambda bb, kk: (kk, bb))],
        out_specs=pl.BlockSpec((None, ka, kb), lambda bb, kk: (bb, 0, 0)),
        out_shape=jax.ShapeDtypeStruct((nb, ka, kb), F32),
        compiler_params=_params("parallel", "arbitrary"),
    )(a, b)


def _cmul(ar, ai, br, bi):
    return ar * br - ai * bi, ar * bi + ai * br


def _scan_tables(lam_ref, reverse):
    shape = (SUBLANES, BLOCK_STATE)
    lr = jnp.broadcast_to(lam_ref[0:1, :], shape)
    li = jnp.broadcast_to(lam_ref[1:2, :], shape)
    if reverse:
        li = -li
    row = lax.broadcasted_iota(jnp.int32, shape, 0)
    tt = (SUBLANES - 1 - row) if reverse else row
    l1 = (lr, li)
    l2 = _cmul(*l1, *l1)
    l4 = _cmul(*l2, *l2)
    pr, pi = l1
    for bit, lp in enumerate((l1, l2, l4)):
        qr, qi = _cmul(pr, pi, *lp)
        on = ((tt >> bit) & 1) == 1
        pr, pi = jnp.where(on, qr, pr), jnp.where(on, qi, pi)
    steps = []
    for d, lp in ((1, l1), (2, l2), (4, l4)):
        ok = tt >= d
        steps.append((d, jnp.where(ok, lp[0], 0.0), jnp.where(ok, lp[1], 0.0)))
    return steps, (pr, pi)


def s5_scan(name, x, lam, reverse, tm=None):
    n_rows = x.shape[0]
    nb = lam.shape[0]
    tm = _pick(n_rows, tm or TILES["scan"], SUBLANES)
    nt = n_rows // tm
    ng = tm // SUBLANES
    last = 0 if reverse else SUBLANES - 1

    def body(x_ref, lam_ref, o_ref, carry_ref):
        it = pl.program_id(1)
        steps, (pr, pi) = _scan_tables(lam_ref, reverse)

        @pl.when(it == 0)
        def _():
            carry_ref[...] = jnp.zeros_like(carry_ref)

        def group(r, carry):
            cr, ci = carry
            rr = (ng - 1 - r) if reverse else r
            off = pl.multiple_of(rr * SUBLANES, SUBLANES)
            xr = x_ref[pl.ds(off, SUBLANES), 0:BLOCK_STATE]
            xi = x_ref[pl.ds(off, SUBLANES), BLOCK_STATE:BLOCK_COLS]
            for d, mr, mi in steps:
                sh = (SUBLANES - d) if reverse else d
                yr, yi = pltpu.roll(xr, sh, 0), pltpu.roll(xi, sh, 0)
                xr, xi = xr + mr * yr - mi * yi, xi + mr * yi + mi * yr
            xr, xi = xr + pr * cr - pi * ci, xi + pr * ci + pi * cr
            o_ref[pl.ds(off, SUBLANES), 0:BLOCK_STATE] = xr
            o_ref[pl.ds(off, SUBLANES), BLOCK_STATE:BLOCK_COLS] = xi
            shape = (SUBLANES, BLOCK_STATE)
            return jnp.broadcast_to(xr[last:last + 1, :], shape), jnp.broadcast_to(xi[last:last + 1, :], shape)

        cr, ci = lax.fori_loop(0, ng, group, (carry_ref[0], carry_ref[1]))
        carry_ref[0] = cr
        carry_ref[1] = ci

    tile = (lambda b, it: (nt - 1 - it, b)) if reverse else (lambda b, it: (it, b))
    return pl.pallas_call(
        body, name=name, grid=(nb, nt),
        in_specs=[pl.BlockSpec((tm, BLOCK_COLS), tile), pl.BlockSpec((None, 2, BLOCK_STATE), lambda b, it: (b, 0, 0))],
        out_specs=pl.BlockSpec((tm, BLOCK_COLS), tile),
        out_shape=jax.ShapeDtypeStruct(x.shape, F32),
        scratch_shapes=[pltpu.VMEM((2, SUBLANES, BLOCK_STATE), F32)],
        compiler_params=_params("parallel", "arbitrary"),
    )(x, lam)


def s5_dlam(name, g, s, nb, tm=None):
    n_rows = g.shape[0]
    tm = _pick(n_rows, tm or TILES["scan"], SUBLANES)
    nt = n_rows // tm
    ng = tm // SUBLANES
    shape = (SUBLANES, BLOCK_STATE)

    def body(g_ref, s_ref, o_ref, carry_ref, acc_ref):
        it = pl.program_id(1)

        @pl.when(it == 0)
        def _():
            carry_ref[...] = jnp.zeros_like(carry_ref)
            acc_ref[...] = jnp.zeros_like(acc_ref)

        first = lax.broadcasted_iota(jnp.int32, shape, 0) == 0

        def group(r, carry):
            cr, ci, ar, ai = carry
            off = pl.multiple_of(r * SUBLANES, SUBLANES)
            sr = s_ref[pl.ds(off, SUBLANES), 0:BLOCK_STATE]
            si = s_ref[pl.ds(off, SUBLANES), BLOCK_STATE:BLOCK_COLS]
            gr = g_ref[pl.ds(off, SUBLANES), 0:BLOCK_STATE]
            gi = g_ref[pl.ds(off, SUBLANES), BLOCK_STATE:BLOCK_COLS]
            pr = jnp.where(first, cr, pltpu.roll(sr, 1, 0))
            pi = jnp.where(first, ci, pltpu.roll(si, 1, 0))
            ar = ar + gr * pr + gi * pi
            ai = ai + gi * pr - gr * pi
            return (jnp.broadcast_to(sr[SUBLANES - 1:SUBLANES, :], shape),
                    jnp.broadcast_to(si[SUBLANES - 1:SUBLANES, :], shape), ar, ai)

        cr, ci, ar, ai = lax.fori_loop(0, ng, group, (carry_ref[0], carry_ref[1], acc_ref[0], acc_ref[1]))
        carry_ref[0] = cr
        carry_ref[1] = ci
        acc_ref[0] = ar
        acc_ref[1] = ai

        @pl.when(it == nt - 1)
        def _():
            o_ref[0:1, :] = jnp.sum(ar, axis=0, keepdims=True)
            o_ref[1:2, :] = jnp.sum(ai, axis=0, keepdims=True)

    tile = lambda b, it: (it, b)
    return pl.pallas_call(
        body, name=name, grid=(nb, nt),
        in_specs=[pl.BlockSpec((tm, BLOCK_COLS), tile), pl.BlockSpec((tm, BLOCK_COLS), tile)],
        out_specs=pl.BlockSpec((None, 2, BLOCK_STATE), lambda b, it: (b, 0, 0)),
        out_shape=jax.ShapeDtypeStruct((nb, 2, BLOCK_STATE), F32),
        scratch_shapes=[pltpu.VMEM((2,) + shape, F32), pltpu.VMEM((2,) + shape, F32)],
        compiler_params=_params("parallel", "arbitrary"),
    )(g, s)


def _rms(x, g):
    return x * lax.rsqrt(jnp.mean(x * x, axis=-1, keepdims=True) + RMS_EPS) * g


def _sigmoid(x):
    return 1.0 / (1.0 + jnp.exp(-x))


def _gelu(x):
    return 0.5 * x * (1.0 + jnp.tanh(0.7978845608028654 * (x + 0.044715 * (x * x * x))))


def _log_sigmoid(x):
    return jnp.minimum(x, 0.0) - jnp.log(1.0 + jnp.exp(-jnp.abs(x)))


def _sqrelu(x):
    r = jnp.maximum(x, 0.0)
    return r * r


def _s5_discretise(ldt, ar, ai, br, bi):
    dt = jnp.exp(ldt)
    er = jnp.exp(ar * dt)
    lr, li = er * jnp.cos(ai * dt), er * jnp.sin(ai * dt)
    nr, ni = lr - 1.0, li
    den = ar * ar + ai * ai
    cr, ci = (nr * ar + ni * ai) / den, (ni * ar - nr * ai) / den
    return lr, li, cr * br - ci * bi, cr * bi + ci * br


def rms_fwd(name, x, g, dtype=F32):
    return rowcall(name, _rms, [x], [g], [(x.shape[1], dtype)])[0]


def rms_bwd(name, x, g, dy, add=None):
    def fn(x, dy, *rest):
        g = rest[-1]
        _, vjp = jax.vjp(_rms, x, g)
        dx, dg = vjp(dy)
        if add is not None:
            dx = dx + rest[0]
        return dx, dg

    rows = [x, dy] + ([add] if add is not None else [])
    return rowcall(name, fn, rows, [g], [(x.shape[1], F32)], [g.shape])


def _split3(x):
    hi = x.astype(BF16).astype(F32)
    r = x - hi
    mid = r.astype(BF16).astype(F32)
    return hi, mid, (r - mid).astype(BF16).astype(F32)


def cum_logf(name, fl, bf, tm=None):
    n_rows, w = fl.shape
    tm = _pick(n_rows, tm or TILES["cum"], SUBLANES)

    def body(fl_ref, bf_ref, hi_ref, mid_ref, lo_ref, carry_ref):
        it = pl.program_id(0)

        @pl.when(it == 0)
        def _():
            carry_ref[...] = jnp.zeros_like(carry_ref)

        ls = _log_sigmoid(fl_ref[...] + bf_ref[...])
        tri = (lax.broadcasted_iota(jnp.int32, (tm, tm), 0) >= lax.broadcasted_iota(jnp.int32, (tm, tm), 1)).astype(F32)
        c = jnp.dot(tri, ls, precision=HIGHEST, preferred_element_type=F32) + carry_ref[0:1, :]
        carry_ref[...] = jnp.broadcast_to(c[tm - 1:tm, :], carry_ref.shape)
        hi_ref[...], mid_ref[...], lo_ref[...] = _split3(c * (-LOG2E))

    spec = pl.BlockSpec((tm, w), lambda i: (i, 0))
    return pl.pallas_call(
        body, name=name, grid=(n_rows // tm,),
        in_specs=[spec, pl.BlockSpec((1, w), lambda i: (0, 0))],
        out_specs=[spec] * 3, out_shape=[jax.ShapeDtypeStruct((n_rows, w), F32)] * 3,
        scratch_shapes=[pltpu.VMEM((SUBLANES, w), F32)],
        compiler_params=_params("arbitrary"),
    )(fl, bf)


def cum_logf_bwd(name, fl, bf, plus, minus, tm=None):
    n_rows, w = fl.shape
    tm = _pick(n_rows, tm or TILES["cum"], SUBLANES)
    nt = n_rows // tm
    n_p, n_m = len(plus), len(minus)

    def body(*refs):
        fl_ref, bf_ref = refs[0], refs[1]
        d_refs = refs[2: 2 + n_p + n_m]
        o_ref, db_ref, carry_ref = refs[2 + n_p + n_m:]
        it = pl.program_id(0)

        @pl.when(it == 0)
        def _():
            carry_ref[...] = jnp.zeros_like(carry_ref)

        d = None
        for r in d_refs[:n_p]:
            d = r[...] if d is None else d + r[...]
        for r in d_refs[n_p:]:
            d = -r[...] if d is None else d - r[...]
        tri = (lax.broadcasted_iota(jnp.int32, (tm, tm), 0) <= lax.broadcasted_iota(jnp.int32, (tm, tm), 1)).astype(F32)
        c = jnp.dot(tri, d, precision=HIGHEST, preferred_element_type=F32) + carry_ref[0:1, :]
        carry_ref[...] = jnp.broadcast_to(c[0:1, :], carry_ref.shape)
        dfl = c * _sigmoid(-(fl_ref[...] + bf_ref[...]))
        o_ref[...] = dfl
        part = jnp.sum(dfl, axis=0, keepdims=True)

        @pl.when(it == 0)
        def _():
            db_ref[...] = part

        @pl.when(it > 0)
        def _():
            db_ref[...] += part

    rev = lambda i: (nt - 1 - i, 0)
    return pl.pallas_call(
        body, name=name, grid=(nt,),
        in_specs=[pl.BlockSpec((tm, w), rev), pl.BlockSpec((1, w), lambda i: (0, 0))] + [pl.BlockSpec((tm, w), rev)] * (n_p + n_m),
        out_specs=[pl.BlockSpec((tm, w), rev), pl.BlockSpec((1, w), lambda i: (0, 0))],
        out_shape=[jax.ShapeDtypeStruct((n_rows, w), F32), jax.ShapeDtypeStruct((1, w), F32)],
        scratch_shapes=[pltpu.VMEM((SUBLANES, w), F32)],
        compiler_params=_params("arbitrary"),
    )(fl, bf, *plus, *minus)


ROWSUM_LANE = HEAD_DIM + 6
F_LANE = HEAD_DIM
LSE_LANE = HEAD_DIM + 3
SUM_LANE = HEAD_DIM
DELTA_LANE = HEAD_DIM + 1


def _lane_consts(pairs):
    lane = lax.broadcasted_iota(jnp.int32, (1, LANES), 1)
    out = jnp.zeros((1, LANES), F32)
    for lo, hi, v in pairs:
        out = jnp.where((lane >= lo) & (lane < hi), v, out)
    return out


def pack_heads(name, x, col_block, n_heads, consts, parts=(), parts_lane=0, tm=None):
    n_rows = x.shape[0]
    d = n_heads * HEAD_DIM
    assert n_heads % 2 == 0
    tm = _pick(n_rows, tm or TILES["row"], 2 * SUBLANES)
    n_parts = len(parts)

    def body(*refs):
        x_ref, c_ref = refs[0], refs[1]
        p_vals = [r[...] for r in refs[2: 2 + n_parts]]
        o_ref = refs[2 + n_parts]
        lane = lax.broadcasted_iota(jnp.int32, (tm, LANES), 1)
        tail0 = jnp.broadcast_to(c_ref[...], (tm, LANES))
        for h in range(n_heads):
            pair = x_ref[:, (h // 2) * LANES: (h // 2 + 1) * LANES].astype(F32)
            base = pair if h % 2 == 0 else pltpu.roll(pair, HEAD_DIM, 1)
            tail = tail0
            for kk, p in enumerate(p_vals):
                col = jnp.sum(jnp.where(lane == h, p, 0.0), axis=1, keepdims=True)
                tail = jnp.where(lane == parts_lane + kk, col, tail)
            o_ref[h] = jnp.where(lane < HEAD_DIM, base, tail).astype(BF16)

    return pl.pallas_call(
        body, name=name, grid=(n_rows // tm,),
        in_specs=[pl.BlockSpec((tm, d), lambda i: (i, col_block)), pl.BlockSpec((1, LANES), lambda i: (0, 0))]
        + [pl.BlockSpec((tm, LANES), lambda i: (i, 0))] * n_parts,
        out_specs=pl.BlockSpec((n_heads, tm, LANES), lambda i: (0, i, 0)),
        out_shape=jax.ShapeDtypeStruct((n_heads, n_rows, LANES), BF16),
        compiler_params=_params("parallel"),
    )(x, consts, *parts)


def unpack_heads(name, xs, extract_lane=None, tm=None):
    n_heads, n_rows, _ = xs[0].shape
    assert n_heads % 2 == 0
    tm = _pick(n_rows, tm or TILES["row"], SUBLANES)
    n = len(xs)

    def body(*refs):
        o_ref = refs[n]
        lane = lax.broadcasted_iota(jnp.int32, (tm, LANES), 1)
        picked = jnp.zeros((tm, LANES), F32)

        def head(h):
            v = refs[0][h]
            for r in refs[1:n]:
                v = v + r[h]
            return v

        for p in range(n_heads // 2):
            a, b = head(2 * p), head(2 * p + 1)
            o_ref[:, p * LANES: (p + 1) * LANES] = jnp.where(lane < HEAD_DIM, a, pltpu.roll(b, HEAD_DIM, 1))
            if extract_lane is not None:
                for hh, v in ((2 * p, a), (2 * p + 1, b)):
                    col = jnp.sum(jnp.where(lane == extract_lane, v, 0.0), axis=1, keepdims=True)
                    picked = jnp.where(lane == hh, col, picked)
        if extract_lane is not None:
            refs[n + 1][...] = picked

    d = n_heads * HEAD_DIM
    out_shape = [jax.ShapeDtypeStruct((n_rows, d), F32)]
    out_specs = [pl.BlockSpec((tm, d), lambda i: (i, 0))]
    if extract_lane is not None:
        out_shape.append(jax.ShapeDtypeStruct((n_rows, LANES), F32))
        out_specs.append(pl.BlockSpec((tm, LANES), lambda i: (i, 0)))
    return pl.pallas_call(
        body, name=name, grid=(n_rows // tm,),
        in_specs=[pl.BlockSpec((n_heads, tm, LANES), lambda i: (0, i, 0))] * n,
        out_specs=out_specs, out_shape=out_shape, compiler_params=_params("parallel"),
    )(*xs)


def fox_fwd(name, q_ext, k_ext, v_ext, t=None):
    nh, n_rows, w = q_ext.shape
    t = _pick(n_rows, t or TILES["attn"])
    nt = n_rows // t

    def body(q_ref, k_ref, v_ref, o_ref, qb_ref, m_ref, acc_ref):
        i = pl.program_id(1)
        m_ref[...] = jnp.full_like(m_ref, NEG_BIG)
        acc_ref[...] = jnp.zeros_like(acc_ref)
        q = q_ref[...]

        def block(j, diagonal):
            off = pl.multiple_of(j * t, t)
            s = lax.dot_general(q, k_ref[pl.ds(off, t), :], _DIMS["nt"], preferred_element_type=F32)
            if diagonal:
                keep = lax.broadcasted_iota(jnp.int32, (t, t), 0) >= lax.broadcasted_iota(jnp.int32, (t, t), 1)
                s = jnp.where(keep, s, NEG_BIG)
            m_prev = m_ref[...]
            m_new = jnp.maximum(m_prev, jnp.max(s, axis=1, keepdims=True))
            p = jnp.exp2(s - jnp.tile(m_new, (1, t // LANES)))
            acc_ref[...] = jnp.exp2(m_prev - m_new) * acc_ref[...] + jnp.dot(
                p.astype(BF16), v_ref[pl.ds(off, t), :], preferred_element_type=F32)
            m_ref[...] = m_new

        def off_diagonal(j, carry):
            block(j, False)
            return carry

        lax.fori_loop(0, i, off_diagonal, 0)
        block(i, True)
        acc = acc_ref[...]
        row_sum = acc[:, HEAD_DIM:HEAD_DIM + 1]
        hi, mid, lo = _split3(m_ref[:, 0:1] + jnp.log2(row_sum))
        lane = lax.broadcasted_iota(jnp.int32, (t, w), 1)
        o_ref[...] = acc / row_sum
        qb = jnp.where(lane == LSE_LANE, hi, jnp.where(lane == LSE_LANE + 1, mid, jnp.where(lane == LSE_LANE + 2, lo, q.astype(F32))))
        qb_ref[...] = qb.astype(BF16)

    whole = pl.BlockSpec((None, n_rows, w), lambda h, i: (h, 0, 0))
    tile = pl.BlockSpec((None, t, w), lambda h, i: (h, i, 0))
    return pl.pallas_call(
        body, name=name, grid=(nh, nt), in_specs=[tile, whole, whole], out_specs=[tile, tile],
        out_shape=[jax.ShapeDtypeStruct((nh, n_rows, w), F32), jax.ShapeDtypeStruct((nh, n_rows, w), BF16)],
        scratch_shapes=[pltpu.VMEM((t, w), F32), pltpu.VMEM((t, w), F32)],
        compiler_params=_params("parallel", "arbitrary"),
    )(q_ext, k_ext, v_ext)


def fox_bwd(name, q_ext, do_ext, k_ext, v_ext, t=None):
    nh, n_rows, w = q_ext.shape
    t = _pick(n_rows, t or TILES["attn"])
    nt = n_rows // t

    def body(q_ref, do_ref, k_ref, v_ref, dq_ref, dk_ref, dv_ref):
        j = pl.program_id(1)

        @pl.when(j == 0)
        def _():
            dq_ref[...] = jnp.zeros_like(dq_ref)

        dk_ref[...] = jnp.zeros_like(dk_ref)
        dv_ref[...] = jnp.zeros_like(dv_ref)
        kj, vj = k_ref[...], v_ref[...]

        def block(i, diagonal):
            off = pl.multiple_of(i * t, t)
            qi, doi = q_ref[pl.ds(off, t), :], do_ref[pl.ds(off, t), :]
            pt = jnp.exp2(lax.dot_general(kj, qi, _DIMS["nt"], preferred_element_type=F32))
            if diagonal:
                keep = lax.broadcasted_iota(jnp.int32, (t, t), 0) <= lax.broadcasted_iota(jnp.int32, (t, t), 1)
                pt = jnp.where(keep, pt, 0.0)
            dst = (pt * lax.dot_general(vj, doi, _DIMS["nt"], preferred_element_type=F32)).astype(BF16)
            dv_ref[...] += jnp.dot(pt.astype(BF16), doi, preferred_element_type=F32)
            dk_ref[...] += jnp.dot(dst, qi, preferred_element_type=F32)
            dq_ref[pl.ds(off, t), :] += lax.dot_general(dst, kj, _DIMS["tn"], preferred_element_type=F32)

        def off_diagonal(i, carry):
            block(i, False)
            return carry

        block(j, True)
        lax.fori_loop(j + 1, nt, off_diagonal, 0)
        lane = lax.broadcasted_iota(jnp.int32, (t, w), 1)
        dk_ref[...] = dk_ref[...] * jnp.where(lane < HEAD_DIM, 1.0 / LOG2E, 1.0)

    whole = pl.BlockSpec((None, n_rows, w), lambda h, j: (h, 0, 0))
    tile = pl.BlockSpec((None, t, w), lambda h, j: (h, j, 0))
    shape = jax.ShapeDtypeStruct((nh, n_rows, w), F32)
    return pl.pallas_call(
        body, name=name, grid=(nh, nt), in_specs=[whole, whole, tile, tile], out_specs=[whole, tile, tile],
        out_shape=[shape, shape, shape],
        compiler_params=_params("parallel", "arbitrary"),
    )(q_ext, do_ext, k_ext, v_ext)


def attn_delta(name, o, do):
    d_model = o.shape[1]
    head_of_col = lax.broadcasted_iota(jnp.int32, (d_model, LANES), 0) // HEAD_DIM
    sel = (head_of_col == lax.broadcasted_iota(jnp.int32, (d_model, LANES), 1)).astype(F32)

    def fn(a, b, s):
        return _split3(jnp.dot(a * b, s, precision=HIGHEST, preferred_element_type=F32))

    return rowcall(name, fn, [o, do], [sel], [(LANES, F32)] * 3)


_PEER_FLIPS = [(bx, by, bc) for bx in (0, 1) for by in (0, 1) for bc in (0, 1)][1:]


def _exchange(name, tensors, scatter):
    n = len(tensors)
    n_peer = len(_PEER_FLIPS)

    def body(*refs):
        ins, outs = refs[:n], refs[n: 2 * n]
        send_sems, recv_sems, local_sems = refs[2 * n:]
        x, y, c = lax.axis_index("x"), lax.axis_index("y"), lax.axis_index("c")
        me = 4 * x + 2 * y + c
        copies = []
        for t in range(n):
            src_me = ins[t].at[me] if scatter else ins[t]
            local = pltpu.make_async_copy(src_me, outs[t].at[me], local_sems.at[t])
            local.start()
            copies.append(local)
            for kk, (bx, by, bc) in enumerate(_PEER_FLIPS):
                px, py, pc = (1 - x if bx else x), (1 - y if by else y), (1 - c if bc else c)
                peer = 4 * px + 2 * py + pc
                out_cp = pltpu.make_async_remote_copy(
                    src_ref=ins[t].at[peer] if scatter else ins[t], dst_ref=outs[t].at[me],
                    send_sem=send_sems.at[t, kk], recv_sem=recv_sems.at[t, kk],
                    device_id=(px, py, pc), device_id_type=pl.DeviceIdType.MESH)
                out_cp.start()
                copies.append(pltpu.make_async_remote_copy(
                    src_ref=ins[t].at[peer] if scatter else ins[t], dst_ref=outs[t].at[peer],
                    send_sem=send_sems.at[t, kk], recv_sem=recv_sems.at[t, kk],
                    device_id=(px, py, pc), device_id_type=pl.DeviceIdType.MESH))
        for cp in copies:
            cp.wait()

    any_spec = pl.BlockSpec(memory_space=pl.ANY)
    out_shape = [jax.ShapeDtypeStruct(t.shape if scatter else (N_DEV,) + t.shape, t.dtype) for t in tensors]
    return pl.pallas_call(
        body, name=name, in_specs=[any_spec] * n, out_specs=[any_spec] * n, out_shape=out_shape,
        scratch_shapes=[pltpu.SemaphoreType.DMA((n, n_peer)), pltpu.SemaphoreType.DMA((n, n_peer)), pltpu.SemaphoreType.DMA((n,))],
        compiler_params=pltpu.CompilerParams(has_side_effects=True),
    )(*tensors)


def all_gather(name, tensors):
    return _exchange(name, tensors, scatter=False)


def all_to_all(name, tensors):
    return _exchange(name, tensors, scatter=True)


def adamw(name, parts, w, m, v, tr=None):
    n_rows, n_cols = w.shape
    tr = _pick(n_rows, tr or TILES["adam"], SUBLANES)
    c1 = 1.0 - ADAM_B1 ** ADAM_STEP
    c2 = 1.0 - ADAM_B2 ** ADAM_STEP

    def body(p_ref, w_ref, m_ref, v_ref, g_ref, d_ref, mo_ref, vo_ref):
        g = p_ref[0].astype(F32)
        for s in range(1, N_DEV):
            g = g + p_ref[s].astype(F32)
        mn = ADAM_B1 * m_ref[...] + (1.0 - ADAM_B1) * g
        vn = ADAM_B2 * v_ref[...] + (1.0 - ADAM_B2) * (g * g)
        g_ref[...] = g
        mo_ref[...] = mn
        vo_ref[...] = vn
        d_ref[...] = -ADAM_LR * ((mn / c1) / (jnp.sqrt(vn / c2) + ADAM_EPS) + ADAM_WD * w_ref[...])

    spec = pl.BlockSpec((tr, n_cols), lambda i: (i, 0))
    return pl.pallas_call(
        body, name=name, grid=(n_rows // tr,),
        in_specs=[pl.BlockSpec((N_DEV, tr, n_cols), lambda i: (0, i, 0)), spec, spec, spec],
        out_specs=[spec] * 4, out_shape=[jax.ShapeDtypeStruct(w.shape, F32)] * 4,
        compiler_params=_params("parallel"),
    )(parts, w, m, v)


def _eye_mask():
    return jnp.eye(GROUPS_PER_BLOCK, dtype=F32)


def _b_blocks(bbr, bbi):
    nb = bbr.shape[0] // (GROUPS_PER_BLOCK * SSM_STATE)
    eye = _eye_mask()[None, :, None, :, None]

    def one(z):
        z = z.reshape(nb, GROUPS_PER_BLOCK, SSM_STATE, SSM_GROUP).transpose(0, 1, 3, 2)
        return z[:, :, :, None, :] * eye

    w = jnp.stack([one(bbr), one(bbi)], axis=3)
    return w.reshape(nb, LANES, BLOCK_COLS)


def _b_blocks_t(dw):
    nb = dw.shape[0]
    d6 = dw.reshape(nb, GROUPS_PER_BLOCK, SSM_GROUP, 2, GROUPS_PER_BLOCK, SSM_STATE)
    diag = jnp.sum(d6 * _eye_mask()[None, :, None, None, :, None], axis=4)
    diag = diag.transpose(3, 0, 1, 4, 2).reshape(2, nb * GROUPS_PER_BLOCK * SSM_STATE, SSM_GROUP)
    return diag[0], diag[1]


def _c_blocks(c_re, c_im):
    nb = c_re.shape[0] // GROUPS_PER_BLOCK
    eye = _eye_mask()[None, :, None, :, None]

    def one(z):
        z = z.reshape(nb, GROUPS_PER_BLOCK, SSM_GROUP, SSM_STATE).transpose(0, 1, 3, 2)
        return z[:, :, :, None, :] * eye

    w = jnp.stack([one(c_re), -one(c_im)], axis=1)
    return w.reshape(nb, BLOCK_COLS, LANES)


def _c_blocks_t(dw):
    nb = dw.shape[0]
    d6 = dw.reshape(nb, 2, GROUPS_PER_BLOCK, SSM_STATE, GROUPS_PER_BLOCK, SSM_GROUP)
    diag = jnp.sum(d6 * _eye_mask()[None, None, :, None, :, None], axis=4)
    diag = diag.transpose(1, 0, 2, 4, 3).reshape(2, nb * GROUPS_PER_BLOCK, SSM_GROUP, SSM_STATE)
    return diag[0], -diag[1]


def _unshard_cols(g):
    s, k, n = g.shape
    return g.transpose(1, 0, 2).reshape(k, s * n)


def _shard_cols(w):
    k, n = w.shape
    return w.reshape(k, N_DEV, n // N_DEV).transpose(1, 0, 2)


def _pack(arrays):
    chunks, offs, row = [], [], 0
    for a in arrays:
        flat = a.reshape(-1).astype(F32)
        rows = -(-flat.shape[0] // LANES)
        chunks.append(jnp.pad(flat, (0, rows * LANES - flat.shape[0])))
        offs.append((row, rows))
        row += rows
    pad_rows = (-row) % SUBLANES
    if pad_rows:
        chunks.append(jnp.zeros((pad_rows * LANES,), F32))
    return jnp.concatenate(chunks).reshape(row + pad_rows, LANES), offs


def _unpack(packed, offs, shapes):
    out = []
    for (row, rows), shp in zip(offs, shapes):
        size = 1
        for s in shp:
            size *= s
        out.append(packed[row: row + rows].reshape(-1)[:size].reshape(shp))
    return out


SCAN_LANES = 256


def _scan_chunk_tables(lam_ref, c0, reverse):
    shape = (SUBLANES, SCAN_LANES)
    lr = jnp.broadcast_to(lam_ref[0:1, c0:c0 + SCAN_LANES], shape)
    li = jnp.broadcast_to(lam_ref[1:2, c0:c0 + SCAN_LANES], shape)
    if reverse:
        li = -li
    row = lax.broadcasted_iota(jnp.int32, shape, 0)
    tt = (SUBLANES - 1 - row) if reverse else row
    l1 = (lr, li)
    l2 = _cmul(*l1, *l1)
    l4 = _cmul(*l2, *l2)
    pr, pi = l1
    for bit, lp in enumerate((l1, l2, l4)):
        qr, qi = _cmul(pr, pi, *lp)
        on = ((tt >> bit) & 1) == 1
        pr, pi = jnp.where(on, qr, pr), jnp.where(on, qi, pi)
    steps = []
    for d, lp in ((1, l1), (2, l2), (4, l4)):
        ok = tt >= d
        steps.append((d, jnp.where(ok, lp[0], 0.0), jnp.where(ok, lp[1], 0.0)))
    return steps, (pr, pi)


def _scan_in_place(buf_ref, lam_ref, carry_ref, ng, reverse, prev_ref=None, acc_ref=None):
    shape = (SUBLANES, SCAN_LANES)
    last = 0 if reverse else SUBLANES - 1
    first_row = lax.broadcasted_iota(jnp.int32, shape, 0) == 0
    for c0 in range(0, BLOCK_STATE, SCAN_LANES):
        re_cols, im_cols = pl.ds(c0, SCAN_LANES), pl.ds(BLOCK_STATE + c0, SCAN_LANES)
        steps, (pr, pi) = _scan_chunk_tables(lam_ref, c0, reverse)

        def group(r, carry):
            cr, ci = carry[0], carry[1]
            rr = (ng - 1 - r) if reverse else r
            off = pl.multiple_of(rr * SUBLANES, SUBLANES)
            xr, xi = buf_ref[pl.ds(off, SUBLANES), re_cols], buf_ref[pl.ds(off, SUBLANES), im_cols]
            for d, mr, mi in steps:
                sh = (SUBLANES - d) if reverse else d
                yr, yi = pltpu.roll(xr, sh, 0), pltpu.roll(xi, sh, 0)
                xr, xi = xr + mr * yr - mi * yi, xi + mr * yi + mi * yr
            xr, xi = xr + pr * cr - pi * ci, xi + pr * ci + pi * cr
            buf_ref[pl.ds(off, SUBLANES), re_cols] = xr
            buf_ref[pl.ds(off, SUBLANES), im_cols] = xi
            out = (jnp.broadcast_to(xr[last:last + 1, :], shape), jnp.broadcast_to(xi[last:last + 1, :], shape))
            if prev_ref is not None:
                off8 = pl.multiple_of(off + SUBLANES, SUBLANES)
                before_r, before_i = prev_ref[pl.ds(off, SUBLANES), re_cols], prev_ref[pl.ds(off, SUBLANES), im_cols]
                same_r, same_i = prev_ref[pl.ds(off8, SUBLANES), re_cols], prev_ref[pl.ds(off8, SUBLANES), im_cols]
                sr = jnp.where(first_row, jnp.broadcast_to(before_r[SUBLANES - 1:, :], shape), pltpu.roll(same_r, 1, 0))
                si = jnp.where(first_row, jnp.broadcast_to(before_i[SUBLANES - 1:, :], shape), pltpu.roll(same_i, 1, 0))
                out += (carry[2] + xr * sr + xi * si, carry[3] + xi * sr - xr * si)
            return out

        init = (carry_ref[0, :, c0:c0 + SCAN_LANES], carry_ref[1, :, c0:c0 + SCAN_LANES])
        if prev_ref is not None:
            init += (acc_ref[0, :, c0:c0 + SCAN_LANES], acc_ref[1, :, c0:c0 + SCAN_LANES])
        res = lax.fori_loop(0, ng, group, init)
        carry_ref[0, :, c0:c0 + SCAN_LANES] = res[0]
        carry_ref[1, :, c0:c0 + SCAN_LANES] = res[1]
        if prev_ref is not None:
            acc_ref[0, :, c0:c0 + SCAN_LANES] = res[2]
            acc_ref[1, :, c0:c0 + SCAN_LANES] = res[3]


def s5_fused_fwd(name, u, wb, wc, lam, d_row, tm=None):
    n_rows, d_model = u.shape
    nb = lam.shape[0]
    tm = _pick(n_rows, tm or TILES["scan"], 2 * SUBLANES)
    nt = n_rows // tm

    def body(u_ref, wb_ref, wc_ref, lam_ref, d_ref, z_ref, st_ref, buf_ref, carry_ref):
        @pl.when(pl.program_id(1) == 0)
        def _():
            carry_ref[...] = jnp.zeros_like(carry_ref)

        uu = u_ref[...]
        buf_ref[...] = jnp.dot(uu.astype(BF16), wb_ref[...], preferred_element_type=F32)
        _scan_in_place(buf_ref, lam_ref, carry_ref, tm // SUBLANES, False)
        st = buf_ref[...].astype(BF16)
        st_ref[...] = st
        z_ref[...] = _gelu(jnp.dot(st, wc_ref[...], preferred_element_type=F32) + d_ref[...] * uu).astype(z_ref.dtype)

    tile = lambda b, it: (it, b)
    blk = lambda b, it: (b, 0, 0)
    return pl.pallas_call(
        body, name=name, grid=(nb, nt),
        in_specs=[pl.BlockSpec((tm, LANES), tile), pl.BlockSpec((None, LANES, BLOCK_COLS), blk),
                  pl.BlockSpec((None, BLOCK_COLS, LANES), blk), pl.BlockSpec((None, 2, BLOCK_STATE), blk),
                  pl.BlockSpec((1, LANES), lambda b, it: (0, b))],
        out_specs=[pl.BlockSpec((tm, LANES), tile), pl.BlockSpec((tm, BLOCK_COLS), tile)],
        out_shape=[jax.ShapeDtypeStruct((n_rows, d_model), BF16), jax.ShapeDtypeStruct((n_rows, nb * BLOCK_COLS), BF16)],
        scratch_shapes=[pltpu.VMEM((tm, BLOCK_COLS), F32), pltpu.VMEM((2, SUBLANES, BLOCK_STATE), F32)],
        compiler_params=_params("parallel", "arbitrary"),
    )(u, wb, wc, lam, d_row)


def s5_fused_bwd(name, dz, u, st, wb, wc, lam, d_row, tm=None):
    n_rows, d_model = u.shape
    nb = lam.shape[0]
    tm = _pick(n_rows, tm or TILES["scan"], 2 * SUBLANES)
    nt = n_rows // tm
    tail_rows = 2 * SUBLANES

    def body(dz_ref, u_ref, st_ref, tail_ref, wb_ref, wc_ref, lam_ref, d_ref,
             du_ref, dwb_ref, dwc_ref, dlam_ref, dd_ref, buf_ref, prev_ref, carry_ref, acc_ref):
        it = pl.program_id(1)

        @pl.when(it == 0)
        def _():
            carry_ref[...] = jnp.zeros_like(carry_ref)
            acc_ref[...] = jnp.zeros_like(acc_ref)
            dwb_ref[...] = jnp.zeros_like(dwb_ref)
            dwc_ref[...] = jnp.zeros_like(dwc_ref)
            dd_ref[...] = jnp.zeros_like(dd_ref)

        uu, st_b = u_ref[...], st_ref[...]
        prev_ref[SUBLANES:, :] = st_b.astype(F32)
        before = tail_ref[...].astype(F32)[SUBLANES:, :]
        prev_ref[:SUBLANES, :] = jnp.where(it == nt - 1, 0.0, before)
        y = jnp.dot(st_b, wc_ref[...], preferred_element_type=F32) + d_ref[...] * uu
        _, vjp = jax.vjp(_gelu, y)
        dy = vjp(dz_ref[...])[0]
        dyb = dy.astype(BF16)
        dd_ref[...] += jnp.sum(dy * uu, axis=0, keepdims=True)
        dwc_ref[...] += lax.dot_general(st_b, dyb, _DIMS["tn"], preferred_element_type=F32)
        buf_ref[...] = lax.dot_general(dyb, wc_ref[...], _DIMS["nt"], preferred_element_type=F32)
        _scan_in_place(buf_ref, lam_ref, carry_ref, tm // SUBLANES, True, prev_ref, acc_ref)
        gb = buf_ref[...].astype(BF16)
        du_ref[...] = lax.dot_general(gb, wb_ref[...], _DIMS["nt"], preferred_element_type=F32) + dy * d_ref[...]
        dwb_ref[...] += lax.dot_general(uu.astype(BF16), gb, _DIMS["tn"], preferred_element_type=F32)

        @pl.when(it == nt - 1)
        def _():
            dlam_ref[0:1, :] = jnp.sum(acc_ref[0], axis=0, keepdims=True)
            dlam_ref[1:2, :] = jnp.sum(acc_ref[1], axis=0, keepdims=True)

    tile = lambda b, it: (nt - 1 - it, b)
    blk = lambda b, it: (b, 0, 0)
    per_tile = tm // tail_rows
    tail = lambda b, it: (jnp.maximum((nt - 1 - it) * per_tile - 1, 0), b)
    return pl.pallas_call(
        body, name=name, grid=(nb, nt),
        in_specs=[pl.BlockSpec((tm, LANES), tile), pl.BlockSpec((tm, LANES), tile), pl.BlockSpec((tm, BLOCK_COLS), tile),
                  pl.BlockSpec((tail_rows, BLOCK_COLS), tail), pl.BlockSpec((None, LANES, BLOCK_COLS), blk),
                  pl.BlockSpec((None, BLOCK_COLS, LANES), blk), pl.BlockSpec((None, 2, BLOCK_STATE), blk),
                  pl.BlockSpec((1, LANES), lambda b, it: (0, b))],
        out_specs=[pl.BlockSpec((tm, LANES), tile), pl.BlockSpec((None, LANES, BLOCK_COLS), blk),
                   pl.BlockSpec((None, BLOCK_COLS, LANES), blk), pl.BlockSpec((None, 2, BLOCK_STATE), blk),
                   pl.BlockSpec((1, LANES), lambda b, it: (0, b))],
        out_shape=[jax.ShapeDtypeStruct((n_rows, d_model), F32), jax.ShapeDtypeStruct((nb, LANES, BLOCK_COLS), F32),
                   jax.ShapeDtypeStruct((nb, BLOCK_COLS, LANES), F32), jax.ShapeDtypeStruct((nb, 2, BLOCK_STATE), F32),
                   jax.ShapeDtypeStruct((1, d_model), F32)],
        scratch_shapes=[pltpu.VMEM((tm, BLOCK_COLS), F32), pltpu.VMEM((tm + SUBLANES, BLOCK_COLS), F32),
                        pltpu.VMEM((2, SUBLANES, BLOCK_STATE), F32), pltpu.VMEM((2, SUBLANES, BLOCK_STATE), F32)],
        compiler_params=_params("parallel", "arbitrary"),
    )(dz, u, st, st, wb, wc, lam, d_row)


def s5_fwd(tag, u, log_dt, a_re, a_im, b_re, b_im, c_re, c_im, d_row):
    d_model = u.shape[1]
    nb = d_model // LANES
    col = lambda a: a.reshape(-1, 1)
    prm = [col(jnp.repeat(log_dt, SSM_STATE)), col(a_re), col(a_im), b_re.reshape(-1, SSM_GROUP), b_im.reshape(-1, SSM_GROUP)]
    lr, li, bbr, bbi = rowcall(f"s5_prep_{tag}", _s5_discretise, prm, [], [(1, F32), (1, F32), (SSM_GROUP, F32), (SSM_GROUP, F32)])
    lam = jnp.stack([lr.reshape(nb, BLOCK_STATE), li.reshape(nb, BLOCK_STATE)], axis=1)
    wb = _b_blocks(bbr, bbi).astype(BF16)
    wc = _c_blocks(c_re, c_im).astype(BF16)
    z, st = s5_fused_fwd(f"s5_fwd_{tag}", u, wb, wc, lam, d_row)
    return z, dict(prm=prm, lam=lam, wb=wb, wc=wc, st=st)


def s5_bwd(tag, dz, u, d_row, sv):
    d_model = u.shape[1]
    nb = d_model // LANES
    n_groups = d_model // SSM_GROUP
    col = lambda a: a.reshape(-1, 1)

    dhn, d_wb, d_wc, d_lam, d_dskip = s5_fused_bwd(f"s5_bwd_{tag}", dz, u, sv["st"], sv["wb"], sv["wc"], sv["lam"], d_row)
    d_bbr, d_bbi = _b_blocks_t(d_wb)
    d_cre, d_cim = _c_blocks_t(d_wc)

    def prep_bwd(ldt, ar, ai, br, bi, dlr, dli, dbr, dbi):
        _, vjp = jax.vjp(_s5_discretise, ldt, ar, ai, br, bi)
        return vjp((dlr, dli, dbr, dbi))

    d_ldt, d_are, d_aim, d_bre, d_bim = rowcall(
        f"s5_prep_bwd_{tag}", prep_bwd, sv["prm"] + [col(d_lam[:, 0]), col(d_lam[:, 1]), d_bbr, d_bbi], [],
        [(1, F32), (1, F32), (1, F32), (SSM_GROUP, F32), (SSM_GROUP, F32)])
    d_logdt = rowcall(f"s5_dlogdt_{tag}", lambda a: jnp.sum(a, axis=1, keepdims=True), [d_ldt.reshape(n_groups, SSM_STATE)], [], [(1, F32)])[0]
    grads = dict(log_dt=d_logdt.reshape(n_groups), a_re=d_are.reshape(n_groups, SSM_STATE), a_im=d_aim.reshape(n_groups, SSM_STATE),
                 b_re=d_bre.reshape(n_groups, SSM_STATE, SSM_GROUP), b_im=d_bim.reshape(n_groups, SSM_STATE, SSM_GROUP),
                 c_re=d_cre, c_im=d_cim, d=d_dskip)
    return dhn, grads


def kernel(x, mix_norm, mlp_norm, mlp_w1, mlp_w2, ssm_log_dt, ssm_a_re, ssm_a_im, ssm_b_re, ssm_b_im, ssm_c_re, ssm_c_im, ssm_d, ssm_w_glu, kv_norm, w_kvf, b_f, attn_wq, attn_wo, final_norm, loss_target, m_mix_norm, m_mlp_norm, m_mlp_w1, m_mlp_w2, m_ssm_log_dt, m_ssm_a_re, m_ssm_a_im, m_ssm_b_re, m_ssm_b_im, m_ssm_c_re, m_ssm_c_im, m_ssm_d, m_ssm_w_glu, m_kv_norm, m_w_kvf, m_b_f, m_attn_wq, m_attn_wo, m_final_norm, v_mix_norm, v_mlp_norm, v_mlp_w1, v_mlp_w2, v_ssm_log_dt, v_ssm_a_re, v_ssm_a_im, v_ssm_b_re, v_ssm_b_im, v_ssm_c_re, v_ssm_c_im, v_ssm_d, v_ssm_w_glu, v_kv_norm, v_w_kvf, v_b_f, v_attn_wq, v_attn_wo, v_final_norm):
    n_rows, d_model = x.shape[1], x.shape[2]
    depth = mix_norm.shape[0]
    n_a = ssm_log_dt.shape[0]
    n_b = depth - n_a
    n_heads = d_model // HEAD_DIM
    n_groups = d_model // SSM_GROUP
    nb = n_groups // GROUPS_PER_BLOCK
    kvf_cols = 2 * d_model + n_heads
    kvf_pad = 2 * d_model + LANES

    g_w1, g_w2, g_glu, g_kvf, g_wq, g_wo, g_d = all_gather(
        "gather_weights",
        [mlp_w1.astype(BF16), mlp_w2.astype(BF16), ssm_w_glu.astype(BF16), w_kvf.astype(BF16),
         attn_wq.astype(BF16), attn_wo.astype(BF16), ssm_d])
    w1 = [_unshard_cols(g_w1[:, i]) for i in range(depth)]
    w2 = [g_w2[:, i].reshape(-1, d_model) for i in range(depth)]
    wglu = [_unshard_cols(g_glu[:, i]) for i in range(n_a)]
    wkvf = _unshard_cols(g_kvf)
    wkvf = jnp.pad(wkvf, ((0, 0), (0, kvf_pad - kvf_cols)))
    wq = [g_wq[:, j].reshape(-1, d_model) for j in range(n_b)]
    wo = [g_wo[:, j].reshape(-1, d_model) for j in range(n_b)]
    d_skip = [g_d[:, i].reshape(1, d_model) for i in range(n_a)]

    row = lambda a: a.reshape(1, -1)
    col = lambda a: a.reshape(-1, 1)

    h = x[0]
    saved = []
    k_ext = v_ext = None
    q_consts = _lane_consts([(F_LANE, F_LANE + 3, 1.0)])
    k_consts = _lane_consts([(LSE_LANE, LSE_LANE + 3, -1.0), (ROWSUM_LANE, ROWSUM_LANE + 1, 1.0)])
    v_consts = _lane_consts([(SUM_LANE, SUM_LANE + 1, 1.0), (DELTA_LANE, DELTA_LANE + 3, -1.0)])
    for i in range(depth):
        sv = {"h": h}
        hn = rms_fwd(f"mix_norm_{i}", h, row(mix_norm[i]), F32 if i < n_a else BF16)
        sv["hn"] = hn
        if i < n_a:
            z, s5_saved = s5_fwd(str(i), hn, ssm_log_dt[i], ssm_a_re[i], ssm_a_im[i], ssm_b_re[i], ssm_b_im[i],
                                 ssm_c_re[i], ssm_c_im[i], d_skip[i])
            zw = matmul(f"s5_glu_{i}", z, wglu[i])
            h1 = rowcall(f"s5_gate_{i}", lambda hh, zz: hh + zz[:, :d_model] * _sigmoid(zz[:, d_model:]), [h, zw], [], [(d_model, F32)])[0]
            sv.update(s5=s5_saved, z=z, zw=zw)
        else:
            j = i - n_a
            q = matmul(f"attn_q_{j}", hn, wq[j], scale=LOG2E * HEAD_DIM ** -0.5, out_dtype=BF16)
            o_ext, q_ext_b = fox_fwd(f"attn_fwd_{j}", pack_heads(f"pack_q_{j}", q, 0, n_heads, q_consts), k_ext, v_ext)
            o2 = unpack_heads(f"unpack_o_{j}", [o_ext])[0]
            h1 = matmul(f"attn_o_{j}", o2, wo[j], resid=h)
            sv.update(q_ext_b=q_ext_b, o2=o2)
        h2n = rms_fwd(f"mlp_norm_{i}", h1, row(mlp_norm[i]), BF16)
        ap = matmul(f"mlp_up_{i}", h2n, w1[i], out_dtype=BF16)
        h = matmul(f"mlp_down_{i}", ap, w2[i], a_fn=_sqrelu, resid=h1)
        sv.update(h1=h1, h2n=h2n, ap=ap)
        saved.append(sv)
        if i == n_a - 1:
            h_mid = h
            hk = rms_fwd("kv_norm", h, row(kv_norm), BF16)
            kvf = matmul("kvf_proj", hk, wkvf, bn=kvf_pad)
            fl = kvf[:, 2 * d_model:]
            bfp = jnp.pad(row(b_f), ((0, 0), (0, LANES - n_heads)))
            k_ext = pack_heads("pack_k", kvf, 0, n_heads, k_consts, cum_logf("cum_logf", fl, bfp), F_LANE)
            v_ext = pack_heads("pack_v", kvf, 1, n_heads, v_consts)

    def loss_fn(hh, tgt, g):
        y, vjp = jax.vjp(_rms, hh, g)
        err = y - tgt
        part = 0.5 * jnp.sum(jnp.mean(err * err, axis=-1, keepdims=True), axis=0, keepdims=True)
        dh, dg = vjp(err * (1.0 / d_model))
        return dh, jnp.broadcast_to(part, (1, LANES)), dg

    dh, loss_part, d_final = rowcall("loss_head", loss_fn, [h, loss_target[0]], [row(final_norm)],
                                     [(d_model, F32)], [(1, LANES), (1, d_model)])

    g_mix, g_mlpn = [None] * depth, [None] * depth
    g_w1f, g_w2f = [None] * depth, [None] * depth
    g_ssm = [None] * n_a
    g_wqf, g_wof = [None] * n_b, [None] * n_b
    dk_acc, dv_acc, df_plus = [], [], []
    g_kv = None
    for i in reversed(range(depth)):
        sv = saved[i]
        if i == n_a - 1:
            dk, col_sums = unpack_heads("unpack_dk", dk_acc, extract_lane=HEAD_DIM)
            dv = unpack_heads("unpack_dv", dv_acc)[0]
            dfl, db_f = cum_logf_bwd("cum_logf_bwd", fl, bfp, df_plus, [col_sums])
            dkvf = jnp.concatenate([dk, dv, dfl], axis=1).astype(BF16)
            dhk = matmul("kvf_dx", dkvf, wkvf, "nt", bk=kvf_pad)
            d_wkvf = matmul("kvf_dw", hk, dkvf, "tn", bm=512, bn=kvf_pad)
            dh, d_kvn = rms_bwd("kv_norm_bwd", h_mid, row(kv_norm), dhk, add=dh)
            g_kv = (d_wkvf[:, :kvf_cols], d_kvn, db_f[:, :n_heads])
        dap = matmul(f"mlp_down_dx_{i}", dh, w2[i], "nt", post=lambda acc, apt: acc * (2.0 * jnp.maximum(apt.astype(F32), 0.0)),
                     post_arg=sv["ap"], out_dtype=BF16)
        g_w2f[i] = matmul(f"mlp_down_dw_{i}", sv["ap"], dh, "tn", a_fn=_sqrelu)
        dh2n = matmul(f"mlp_up_dx_{i}", dap, w1[i], "nt")
        g_w1f[i] = matmul(f"mlp_up_dw_{i}", sv["h2n"], dap, "tn")
        dh1, g_mlpn[i] = rms_bwd(f"mlp_norm_bwd_{i}", sv["h1"], row(mlp_norm[i]), dh2n, add=dh)
        if i < n_a:
            def glu_bwd(zz, dd):
                val, gate = zz[:, :d_model], zz[:, d_model:]
                sg = _sigmoid(gate)
                return jnp.concatenate([dd * sg, dd * val * sg * (1.0 - sg)], axis=1)

            dzw = rowcall(f"s5_gate_bwd_{i}", glu_bwd, [sv["zw"], dh1], [], [(2 * d_model, BF16)])[0]
            dz = matmul(f"s5_glu_dx_{i}", dzw, wglu[i], "nt")
            d_wglu = matmul(f"s5_glu_dw_{i}", sv["z"], dzw, "tn")

            dhn, g_ssm[i] = s5_bwd(str(i), dz, sv["hn"], d_skip[i], sv["s5"])
            g_ssm[i]["w_glu"] = d_wglu
        else:
            j = i - n_a
            do2 = matmul(f"attn_o_dx_{j}", dh1, wo[j], "nt")
            g_wof[j] = matmul(f"attn_o_dw_{j}", sv["o2"], dh1, "tn")
            do_ext = pack_heads(f"pack_do_{j}", do2, 0, n_heads, jnp.zeros((1, LANES), F32),
                                attn_delta(f"attn_delta_{j}", sv["o2"], do2), DELTA_LANE)
            dq_ext, dk_ext, dv_ext = fox_bwd(f"attn_bwd_{j}", sv["q_ext_b"], do_ext, k_ext, v_ext)
            dk_acc.append(dk_ext)
            dv_acc.append(dv_ext)
            dq2, row_sums = unpack_heads(f"unpack_dq_{j}", [dq_ext], extract_lane=ROWSUM_LANE)
            df_plus.append(row_sums)
            dhn = matmul(f"attn_q_dx_{j}", dq2, wq[j], "nt", scale=HEAD_DIM ** -0.5)
            g_wqf[j] = matmul(f"attn_q_dw_{j}", sv["hn"], dq2, "tn", scale=HEAD_DIM ** -0.5)
        dh, g_mix[i] = rms_bwd(f"mix_norm_bwd_{i}", sv["h"], row(mix_norm[i]), dhn, add=dh1)
    grad_x = dh[None]

    stack = lambda xs: jnp.stack(xs, axis=1).astype(BF16 if xs[0].ndim == 3 else F32)
    parts = all_to_all("scatter_grads", [
        stack([_shard_cols(g) for g in g_w1f]),
        stack([g.reshape(N_DEV, -1, d_model) for g in g_w2f]),
        stack([_shard_cols(g["w_glu"]) for g in g_ssm]),
        _shard_cols(g_kv[0]).astype(BF16),
        stack([g.reshape(N_DEV, -1, d_model) for g in g_wqf]),
        stack([g.reshape(N_DEV, -1, d_model) for g in g_wof]),
        stack([g["d"].reshape(N_DEV, -1) for g in g_ssm]),
    ])
    small_names = ["mix_norm", "mlp_norm", "ssm_log_dt", "ssm_a_re", "ssm_a_im", "ssm_b_re", "ssm_b_im", "ssm_c_re", "ssm_c_im",
                   "kv_norm", "b_f", "final_norm"]
    small_grads = [jnp.concatenate(g_mix, axis=0), jnp.concatenate(g_mlpn, axis=0)]
    small_grads += [jnp.stack([g[kk] for g in g_ssm]) for kk in ("log_dt", "a_re", "a_im", "b_re", "b_im", "c_re", "c_im")]
    small_grads += [g_kv[1], g_kv[2], d_final]
    small_w = [mix_norm, mlp_norm, ssm_log_dt, ssm_a_re, ssm_a_im, ssm_b_re, ssm_b_im, ssm_c_re, ssm_c_im, kv_norm, b_f, final_norm]
    small_m = [m_mix_norm, m_mlp_norm, m_ssm_log_dt, m_ssm_a_re, m_ssm_a_im, m_ssm_b_re, m_ssm_b_im, m_ssm_c_re, m_ssm_c_im, m_kv_norm, m_b_f, m_final_norm]
    small_v = [v_mix_norm, v_mlp_norm, v_ssm_log_dt, v_ssm_a_re, v_ssm_a_im, v_ssm_b_re, v_ssm_b_im, v_ssm_c_re, v_ssm_c_im, v_kv_norm, v_b_f, v_final_norm]
    loss_slot = jnp.zeros((LANES,), F32)
    packed_g, offs = _pack(small_grads + [loss_part])
    packed_w, _ = _pack(small_w + [loss_slot])
    packed_m, _ = _pack(small_m + [loss_slot])
    packed_v, _ = _pack(small_v + [loss_slot])
    (small_parts,) = all_gather("gather_small_grads", [packed_g])

    res = {}

    def update(nm, part, w, m, v):
        shp = w.shape
        as2d = lambda a: a.reshape(-1, shp[-1])
        outs = adamw(f"adamw_{nm}", part.reshape((N_DEV,) + as2d(w).shape), as2d(w), as2d(m), as2d(v))
        res[nm] = [o.reshape(shp) for o in outs]

    update("mlp_w1", parts[0], mlp_w1, m_mlp_w1, v_mlp_w1)
    update("mlp_w2", parts[1], mlp_w2, m_mlp_w2, v_mlp_w2)
    update("ssm_w_glu", parts[2], ssm_w_glu, m_ssm_w_glu, v_ssm_w_glu)
    update("w_kvf", parts[3], w_kvf, m_w_kvf, v_w_kvf)
    update("attn_wq", parts[4], attn_wq, m_attn_wq, v_attn_wq)
    update("attn_wo", parts[5], attn_wo, m_attn_wo, v_attn_wo)
    update("ssm_d", parts[6], ssm_d, m_ssm_d, v_ssm_d)
    small_out = adamw("adamw_small", small_parts, packed_w, packed_m, packed_v)
    shapes = [w.shape for w in small_w] + [(LANES,)]
    unpacked = [_unpack(o, offs, shapes) for o in small_out]
    for idx, nm in enumerate(small_names):
        res[nm] = [u[idx] for u in unpacked]
    loss = unpacked[0][-1][0]

    order = ["mix_norm", "mlp_norm", "mlp_w1", "mlp_w2", "ssm_log_dt", "ssm_a_re", "ssm_a_im", "ssm_b_re", "ssm_b_im", "ssm_c_re",
             "ssm_c_im", "ssm_d", "ssm_w_glu", "kv_norm", "w_kvf", "b_f", "attn_wq", "attn_wo", "final_norm"]
    out = [loss, grad_x]
    for kind in range(4):
        out += [res[nm][kind] for nm in order]
    return tuple(out)
```

```python
import functools

import jax
import jax.numpy as jnp
from jax import lax
from jax.experimental import pallas as pl
from jax.experimental.pallas import tpu as pltpu

F32 = jnp.float32
BF16 = jnp.bfloat16
HIGHEST = lax.Precision.HIGHEST

V7X_VMEM_BYTES = 64 << 20
VMEM_LIMIT_BYTES = (V7X_VMEM_BYTES * 3) // 4
LANES = 128
SUBLANES = 8

N_DEV = 8
RMS_EPS = 1e-6
SSM_GROUP = 16
SSM_STATE = 64
HEAD_DIM = 64
GROUPS_PER_BLOCK = LANES // SSM_GROUP
BLOCK_STATE = GROUPS_PER_BLOCK * SSM_STATE
BLOCK_COLS = 2 * BLOCK_STATE
NEG_BIG = -1e30
LOG2E = 1.4426950408889634

ADAM_LR = 0.001
ADAM_B1 = 0.9
ADAM_B2 = 0.999
ADAM_EPS = 1e-08
ADAM_WD = 0.01
ADAM_STEP = 10

TILES = {"row": 512, "mm": (1024, 1024, 1024), "blk": 512, "scan": 512, "cum": 256, "attn": 1024, "adam": 256}


def _pick(dim, pref, align=LANES):
    if dim <= pref:
        return dim
    for a in (align, SUBLANES):
        d = (pref // a) * a
        while d >= a:
            if dim % d == 0:
                return d
            d -= a
    return dim


def _params(*sem):
    return pltpu.CompilerParams(dimension_semantics=sem, vmem_limit_bytes=VMEM_LIMIT_BYTES)


def rowcall(name, fn, rows, consts, out_rows, out_accs=(), tm=None):
    n_rows = rows[0].shape[0]
    tm = _pick(n_rows, tm or TILES["row"], SUBLANES)
    nr, nc, no, na = len(rows), len(consts), len(out_rows), len(out_accs)

    def body(*refs):
        ins = [r[...] for r in refs[: nr + nc]]
        outs = fn(*ins)
        if not isinstance(outs, (tuple, list)):
            outs = (outs,)
        for r, o in zip(refs[nr + nc: nr + nc + no], outs[:no]):
            r[...] = o.astype(r.dtype)
        if na:
            i = pl.program_id(0)
            for r, o in zip(refs[nr + nc + no:], outs[no:]):
                @pl.when(i == 0)
                def _(r=r, o=o):
                    r[...] = o

                @pl.when(i > 0)
                def _(r=r, o=o):
                    r[...] += o

    in_specs = [pl.BlockSpec((tm, a.shape[1]), lambda i: (i, 0)) for a in rows]
    in_specs += [pl.BlockSpec(c.shape, lambda i, n=c.ndim: (0,) * n) for c in consts]
    out_shape = [jax.ShapeDtypeStruct((n_rows, c), dt) for c, dt in out_rows]
    out_specs = [pl.BlockSpec((tm, c), lambda i: (i, 0)) for c, _ in out_rows]
    out_shape += [jax.ShapeDtypeStruct(s, F32) for s in out_accs]
    out_specs += [pl.BlockSpec(s, lambda i, n=len(s): (0,) * n) for s in out_accs]
    res = pl.pallas_call(
        body, name=name, grid=(n_rows // tm,), in_specs=in_specs, out_specs=out_specs, out_shape=out_shape,
        compiler_params=_params("arbitrary" if na else "parallel"),
    )(*rows, *consts)
    return res


_DIMS = {"nn": (((1,), (0,)), ((), ())), "nt": (((1,), (1,)), ((), ())), "tn": (((0,), (0,)), ((), ()))}


def matmul(name, a, b, mode="nn", *, a_fn=None, scale=None, resid=None, post=None, post_arg=None,
           out_dtype=F32, bm=None, bn=None, bk=None):
    if mode == "nn":
        (m, k), (k2, n) = a.shape, b.shape
    elif mode == "nt":
        (m, k), (n, k2) = a.shape, b.shape
    else:
        (k, m), (k2, n) = a.shape, b.shape
    assert k == k2, (name, a.shape, b.shape, mode)
    bm, bn, bk = _pick(m, bm or TILES["mm"][0]), _pick(n, bn or TILES["mm"][1]), _pick(k, bk or TILES["mm"][2])
    nk = k // bk
    a_spec = pl.BlockSpec((bk, bm), lambda i, j, kk: (kk, i)) if mode == "tn" else pl.BlockSpec((bm, bk), lambda i, j, kk: (i, kk))
    b_spec = pl.BlockSpec((bn, bk), lambda i, j, kk: (j, kk)) if mode == "nt" else pl.BlockSpec((bk, bn), lambda i, j, kk: (kk, j))
    mn_spec = pl.BlockSpec((bm, bn), lambda i, j, kk: (i, j))
    extra = [x for x in (resid, post_arg) if x is not None]
    has_resid, has_post = resid is not None, post is not None

    def body(*refs):
        a_ref, b_ref = refs[0], refs[1]
        ex = refs[2: 2 + len(extra)]
        o_ref = refs[2 + len(extra)]
        av = a_ref[...]
        if a_fn is not None:
            av = a_fn(av.astype(F32))
        p = lax.dot_general(av.astype(BF16), b_ref[...].astype(BF16), _DIMS[mode], preferred_element_type=F32)

        def finish(acc):
            if scale is not None:
                acc = acc * scale
            idx = 0
            if has_resid:
                acc = acc + ex[idx][...]
                idx += 1
            if has_post:
                acc = post(acc, ex[idx][...])
            o_ref[...] = acc.astype(o_ref.dtype)

        if nk == 1:
            finish(p)
        else:
            acc_ref = refs[-1]
            kk = pl.program_id(2)

            @pl.when(kk == 0)
            def _():
                acc_ref[...] = p

            @pl.when(kk > 0)
            def _():
                acc_ref[...] += p

            @pl.when(kk == nk - 1)
            def _():
                finish(acc_ref[...])

    return pl.pallas_call(
        body, name=name, grid=(m // bm, n // bn, nk),
        in_specs=[a_spec, b_spec] + [mn_spec] * len(extra), out_specs=mn_spec,
        out_shape=jax.ShapeDtypeStruct((m, n), out_dtype),
        scratch_shapes=[pltpu.VMEM((bm, bn), F32)] if nk > 1 else [],
        compiler_params=_params("parallel", "parallel", "arbitrary"),
    )(a, b, *extra)


S5_DOT_PASSES = 1


def _block_dot(a, b, mode):
    if S5_DOT_PASSES == 6:
        return lax.dot_general(a, b, _DIMS[mode], precision=HIGHEST, preferred_element_type=F32)
    dot = lambda x, y: lax.dot_general(x, y, _DIMS[mode], preferred_element_type=F32)
    a_hi, b_hi = a.astype(BF16), b.astype(BF16)
    if S5_DOT_PASSES == 1:
        return dot(a_hi, b_hi)
    a_lo, b_lo = (a - a_hi.astype(F32)).astype(BF16), (b - b_hi.astype(F32)).astype(BF16)
    return dot(a_hi, b_hi) + (dot(a_hi, b_lo) + dot(a_lo, b_hi))


def bmm(name, a, w, mode, tm=None):
    n_rows = a.shape[0]
    nb, ka, kb = w.shape
    ca, co = (ka, kb) if mode == "nn" else (kb, ka)
    assert a.shape[1] == nb * ca
    tm = _pick(n_rows, tm or TILES["blk"], SUBLANES)

    def body(a_ref, w_ref, o_ref):
        o_ref[...] = _block_dot(a_ref[...], w_ref[...], mode)

    return pl.pallas_call(
        body, name=name, grid=(n_rows // tm, nb),
        in_specs=[pl.BlockSpec((tm, ca), lambda i, b: (i, b)), pl.BlockSpec((None, ka, kb), lambda i, b: (b, 0, 0))],
        out_specs=pl.BlockSpec((tm, co), lambda i, b: (i, b)),
        out_shape=jax.ShapeDtypeStruct((n_rows, nb * co), F32),
        compiler_params=_params("parallel", "parallel"),
    )(a, w)


def bmm_tn(name, a, b, nb, tk=None):
    n_rows = a.shape[0]
    ka, kb = a.shape[1] // nb, b.shape[1] // nb
    tk = _pick(n_rows, tk or TILES["blk"], SUBLANES)

    def body(a_ref, b_ref, o_ref):
        p = _block_dot(a_ref[...], b_ref[...], "tn")
        kk = pl.program_id(1)

        @pl.when(kk == 0)
        def _():
            o_ref[...] = p

        @pl.when(kk > 0)
        def _():
            o_ref[...] += p

    return pl.pallas_call(
        body, name=name, grid=(nb, n_rows // tk),
        in_specs=[pl.BlockSpec((tk, ka), lambda bb, kk: (kk, bb)), pl.BlockSpec((tk, kb), lambda bb, kk: (kk, bb))],
        out_specs=pl.BlockSpec((None, ka, kb), lambda bb, kk: (bb, 0, 0)),
        out_shape=jax.ShapeDtypeStruct((nb, ka, kb), F32),
        compiler_params=_params("parallel", "arbitrary"),
    )(a, b)


def _cmul(ar, ai, br, bi):
    return ar * br - ai * bi, ar * bi + ai * br


def _scan_tables(lam_ref, reverse):
    shape = (SUBLANES, BLOCK_STATE)
    lr = jnp.broadcast_to(lam_ref[0:1, :], shape)
    li = jnp.broadcast_to(lam_ref[1:2, :], shape)
    if reverse:
        li = -li
    row = lax.broadcasted_iota(jnp.int32, shape, 0)
    tt = (SUBLANES - 1 - row) if reverse else row
    l1 = (lr, li)
    l2 = _cmul(*l1, *l1)
    l4 = _cmul(*l2, *l2)
    pr, pi = l1
    for bit, lp in enumerate((l1, l2, l4)):
        qr, qi = _cmul(pr, pi, *lp)
        on = ((tt >> bit) & 1) == 1
        pr, pi = jnp.where(on, qr, pr), jnp.where(on, qi, pi)
    steps = []
    for d, lp in ((1, l1), (2, l2), (4, l4)):
        ok = tt >= d
        steps.append((d, jnp.where(ok, lp[0], 0.0), jnp.where(ok, lp[1], 0.0)))
    return steps, (pr, pi)


def s5_scan(name, x, lam, reverse, tm=None):
    n_rows = x.shape[0]
    nb = lam.shape[0]
    tm = _pick(n_rows, tm or TILES["scan"], SUBLANES)
    nt = n_rows // tm
    ng = tm // SUBLANES
    last = 0 if reverse else SUBLANES - 1

    def body(x_ref, lam_ref, o_ref, carry_ref):
        it = pl.program_id(1)
        steps, (pr, pi) = _scan_tables(lam_ref, reverse)

        @pl.when(it == 0)
        def _():
            carry_ref[...] = jnp.zeros_like(carry_ref)

        def group(r, carry):
            cr, ci = carry
            rr = (ng - 1 - r) if reverse else r
            off = pl.multiple_of(rr * SUBLANES, SUBLANES)
            xr = x_ref[pl.ds(off, SUBLANES), 0:BLOCK_STATE]
            xi = x_ref[pl.ds(off, SUBLANES), BLOCK_STATE:BLOCK_COLS]
            for d, mr, mi in steps:
                sh = (SUBLANES - d) if reverse else d
                yr, yi = pltpu.roll(xr, sh, 0), pltpu.roll(xi, sh, 0)
                xr, xi = xr + mr * yr - mi * yi, xi + mr * yi + mi * yr
            xr, xi = xr + pr * cr - pi * ci, xi + pr * ci + pi * cr
            o_ref[pl.ds(off, SUBLANES), 0:BLOCK_STATE] = xr
            o_ref[pl.ds(off, SUBLANES), BLOCK_STATE:BLOCK_COLS] = xi
            shape = (SUBLANES, BLOCK_STATE)
            return jnp.broadcast_to(xr[last:last + 1, :], shape), jnp.broadcast_to(xi[last:last + 1, :], shape)

        cr, ci = lax.fori_loop(0, ng, group, (carry_ref[0], carry_ref[1]))
        carry_ref[0] = cr
        carry_ref[1] = ci

    tile = (lambda b, it: (nt - 1 - it, b)) if reverse else (lambda b, it: (it, b))
    return pl.pallas_call(
        body, name=name, grid=(nb, nt),
        in_specs=[pl.BlockSpec((tm, BLOCK_COLS), tile), pl.BlockSpec((None, 2, BLOCK_STATE), lambda b, it: (b, 0, 0))],
        out_specs=pl.BlockSpec((tm, BLOCK_COLS), tile),
        out_shape=jax.ShapeDtypeStruct(x.shape, F32),
        scratch_shapes=[pltpu.VMEM((2, SUBLANES, BLOCK_STATE), F32)],
        compiler_params=_params("parallel", "arbitrary"),
    )(x, lam)


def s5_dlam(name, g, s, nb, tm=None):
    n_rows = g.shape[0]
    tm = _pick(n_rows, tm or TILES["scan"], SUBLANES)
    nt = n_rows // tm
    ng = tm // SUBLANES
    shape = (SUBLANES, BLOCK_STATE)

    def body(g_ref, s_ref, o_ref, carry_ref, acc_ref):
        it = pl.program_id(1)

        @pl.when(it == 0)
        def _():
            carry_ref[...] = jnp.zeros_like(carry_ref)
            acc_ref[...] = jnp.zeros_like(acc_ref)

        first = lax.broadcasted_iota(jnp.int32, shape, 0) == 0

        def group(r, carry):
            cr, ci, ar, ai = carry
            off = pl.multiple_of(r * SUBLANES, SUBLANES)
            sr = s_ref[pl.ds(off, SUBLANES), 0:BLOCK_STATE]
            si = s_ref[pl.ds(off, SUBLANES), BLOCK_STATE:BLOCK_COLS]
            gr = g_ref[pl.ds(off, SUBLANES), 0:BLOCK_STATE]
            gi = g_ref[pl.ds(off, SUBLANES), BLOCK_STATE:BLOCK_COLS]
            pr = jnp.where(first, cr, pltpu.roll(sr, 1, 0))
            pi = jnp.where(first, ci, pltpu.roll(si, 1, 0))
            ar = ar + gr * pr + gi * pi
            ai = ai + gi * pr - gr * pi
            return (jnp.broadcast_to(sr[SUBLANES - 1:SUBLANES, :], shape),
                    jnp.broadcast_to(si[SUBLANES - 1:SUBLANES, :], shape), ar, ai)

        cr, ci, ar, ai = lax.fori_loop(0, ng, group, (carry_ref[0], carry_ref[1], acc_ref[0], acc_ref[1]))
        carry_ref[0] = cr
        carry_ref[1] = ci
        acc_ref[0] = ar
        acc_ref[1] = ai

        @pl.when(it == nt - 1)
        def _():
            o_ref[0:1, :] = jnp.sum(ar, axis=0, keepdims=True)
            o_ref[1:2, :] = jnp.sum(ai, axis=0, keepdims=True)

    tile = lambda b, it: (it, b)
    return pl.pallas_call(
        body, name=name, grid=(nb, nt),
        in_specs=[pl.BlockSpec((tm, BLOCK_COLS), tile), pl.BlockSpec((tm, BLOCK_COLS), tile)],
        out_specs=pl.BlockSpec((None, 2, BLOCK_STATE), lambda b, it: (b, 0, 0)),
        out_shape=jax.ShapeDtypeStruct((nb, 2, BLOCK_STATE), F32),
        scratch_shapes=[pltpu.VMEM((2,) + shape, F32), pltpu.VMEM((2,) + shape, F32)],
        compiler_params=_params("parallel", "arbitrary"),
    )(g, s)


def _rms(x, g):
    return x * lax.rsqrt(jnp.mean(x * x, axis=-1, keepdims=True) + RMS_EPS) * g


def _sigmoid(x):
    return 1.0 / (1.0 + jnp.exp(-x))


def _gelu(x):
    return 0.5 * x * (1.0 + jnp.tanh(0.7978845608028654 * (x + 0.044715 * (x * x * x))))


def _log_sigmoid(x):
    return jnp.minimum(x, 0.0) - jnp.log(1.0 + jnp.exp(-jnp.abs(x)))


def _sqrelu(x):
    r = jnp.maximum(x, 0.0)
    return r * r


def _s5_discretise(ldt, ar, ai, br, bi):
    dt = jnp.exp(ldt)
    er = jnp.exp(ar * dt)
    lr, li = er * jnp.cos(ai * dt), er * jnp.sin(ai * dt)
    nr, ni = lr - 1.0, li
    den = ar * ar + ai * ai
    cr, ci = (nr * ar + ni * ai) / den, (ni * ar - nr * ai) / den
    return lr, li, cr * br - ci * bi, cr * bi + ci * br


def rms_fwd(name, x, g, dtype=F32):
    return rowcall(name, _rms, [x], [g], [(x.shape[1], dtype)])[0]


def rms_bwd(name, x, g, dy, add=None):
    def fn(x, dy, *rest):
        g = rest[-1]
        _, vjp = jax.vjp(_rms, x, g)
        dx, dg = vjp(dy)
        if add is not None:
            dx = dx + rest[0]
        return dx, dg

    rows = [x, dy] + ([add] if add is not None else [])
    return rowcall(name, fn, rows, [g], [(x.shape[1], F32)], [g.shape])


def _split3(x):
    hi = x.astype(BF16).astype(F32)
    r = x - hi
    mid = r.astype(BF16).astype(F32)
    return hi, mid, (r - mid).astype(BF16).astype(F32)


def cum_logf(name, fl, bf, tm=None):
    n_rows, w = fl.shape
    tm = _pick(n_rows, tm or TILES["cum"], SUBLANES)

    def body(fl_ref, bf_ref, hi_ref, mid_ref, lo_ref, carry_ref):
        it = pl.program_id(0)

        @pl.when(it == 0)
        def _():
            carry_ref[...] = jnp.zeros_like(carry_ref)

        ls = _log_sigmoid(fl_ref[...] + bf_ref[...])
        tri = (lax.broadcasted_iota(jnp.int32, (tm, tm), 0) >= lax.broadcasted_iota(jnp.int32, (tm, tm), 1)).astype(F32)
        c = jnp.dot(tri, ls, precision=HIGHEST, preferred_element_type=F32) + carry_ref[0:1, :]
        carry_ref[...] = jnp.broadcast_to(c[tm - 1:tm, :], carry_ref.shape)
        hi_ref[...], mid_ref[...], lo_ref[...] = _split3(c * (-LOG2E))

    spec = pl.BlockSpec((tm, w), lambda i: (i, 0))
    return pl.pallas_call(
        body, name=name, grid=(n_rows // tm,),
        in_specs=[spec, pl.BlockSpec((1, w), lambda i: (0, 0))],
        out_specs=[spec] * 3, out_shape=[jax.ShapeDtypeStruct((n_rows, w), F32)] * 3,
        scratch_shapes=[pltpu.VMEM((SUBLANES, w), F32)],
        compiler_params=_params("arbitrary"),
    )(fl, bf)


def cum_logf_bwd(name, fl, bf, plus, minus, tm=None):
    n_rows, w = fl.shape
    tm = _pick(n_rows, tm or TILES["cum"], SUBLANES)
    nt = n_rows // tm
    n_p, n_m = len(plus), len(minus)

    def body(*refs):
        fl_ref, bf_ref = refs[0], refs[1]
        d_refs = refs[2: 2 + n_p + n_m]
        o_ref, db_ref, carry_ref = refs[2 + n_p + n_m:]
        it = pl.program_id(0)

        @pl.when(it == 0)
        def _():
            carry_ref[...] = jnp.zeros_like(carry_ref)

        d = None
        for r in d_refs[:n_p]:
            d = r[...] if d is None else d + r[...]
        for r in d_refs[n_p:]:
            d = -r[...] if d is None else d - r[...]
        tri = (lax.broadcasted_iota(jnp.int32, (tm, tm), 0) <= lax.broadcasted_iota(jnp.int32, (tm, tm), 1)).astype(F32)
        c = jnp.dot(tri, d, precision=HIGHEST, preferred_element_type=F32) + carry_ref[0:1, :]
        carry_ref[...] = jnp.broadcast_to(c[0:1, :], carry_ref.shape)
        dfl = c * _sigmoid(-(fl_ref[...] + bf_ref[...]))
        o_ref[...] = dfl
        part = jnp.sum(dfl, axis=0, keepdims=True)

        @pl.when(it == 0)
        def _():
            db_ref[...] = part

        @pl.when(it > 0)
        def _():
            db_ref[...] += part

    rev = lambda i: (nt - 1 - i, 0)
    return pl.pallas_call(
        body, name=name, grid=(nt,),
        in_specs=[pl.BlockSpec((tm, w), rev), pl.BlockSpec((1, w), lambda i: (0, 0))] + [pl.BlockSpec((tm, w), rev)] * (n_p + n_m),
        out_specs=[pl.BlockSpec((tm, w), rev), pl.BlockSpec((1, w), lambda i: (0, 0))],
        out_shape=[jax.ShapeDtypeStruct((n_rows, w), F32), jax.ShapeDtypeStruct((1, w), F32)],
        scratch_shapes=[pltpu.VMEM((SUBLANES, w), F32)],
        compiler_params=_params("arbitrary"),
    )(fl, bf, *plus, *minus)


ROWSUM_LANE = HEAD_DIM + 6
F_LANE = HEAD_DIM
LSE_LANE = HEAD_DIM + 3
SUM_LANE = HEAD_DIM
DELTA_LANE = HEAD_DIM + 1


def _lane_consts(pairs):
    lane = lax.broadcasted_iota(jnp.int32, (1, LANES), 1)
    out = jnp.zeros((1, LANES), F32)
    for lo, hi, v in pairs:
        out = jnp.where((lane >= lo) & (lane < hi), v, out)
    return out


def pack_heads(name, x, col_block, n_heads, consts, parts=(), parts_lane=0, tm=None):
    n_rows = x.shape[0]
    d = n_heads * HEAD_DIM
    assert n_heads % 2 == 0
    tm = _pick(n_rows, tm or TILES["row"], 2 * SUBLANES)
    n_parts = len(parts)

    def body(*refs):
        x_ref, c_ref = refs[0], refs[1]
        p_vals = [r[...] for r in refs[2: 2 + n_parts]]
        o_ref = refs[2 + n_parts]
        lane = lax.broadcasted_iota(jnp.int32, (tm, LANES), 1)
        tail0 = jnp.broadcast_to(c_ref[...], (tm, LANES))
        for h in range(n_heads):
            pair = x_ref[:, (h // 2) * LANES: (h // 2 + 1) * LANES].astype(F32)
            base = pair if h % 2 == 0 else pltpu.roll(pair, HEAD_DIM, 1)
            tail = tail0
            for kk, p in enumerate(p_vals):
                col = jnp.sum(jnp.where(lane == h, p, 0.0), axis=1, keepdims=True)
                tail = jnp.where(lane == parts_lane + kk, col, tail)
            o_ref[h] = jnp.where(lane < HEAD_DIM, base, tail).astype(BF16)

    return pl.pallas_call(
        body, name=name, grid=(n_rows // tm,),
        in_specs=[pl.BlockSpec((tm, d), lambda i: (i, col_block)), pl.BlockSpec((1, LANES), lambda i: (0, 0))]
        + [pl.BlockSpec((tm, LANES), lambda i: (i, 0))] * n_parts,
        out_specs=pl.BlockSpec((n_heads, tm, LANES), lambda i: (0, i, 0)),
        out_shape=jax.ShapeDtypeStruct((n_heads, n_rows, LANES), BF16),
        compiler_params=_params("parallel"),
    )(x, consts, *parts)


def unpack_heads(name, xs, extract_lane=None, tm=None):
    n_heads, n_rows, _ = xs[0].shape
    assert n_heads % 2 == 0
    tm = _pick(n_rows, tm or TILES["row"], SUBLANES)
    n = len(xs)

    def body(*refs):
        o_ref = refs[n]
        lane = lax.broadcasted_iota(jnp.int32, (tm, LANES), 1)
        picked = jnp.zeros((tm, LANES), F32)

        def head(h):
            v = refs[0][h]
            for r in refs[1:n]:
                v = v + r[h]
            return v

        for p in range(n_heads // 2):
            a, b = head(2 * p), head(2 * p + 1)
            o_ref[:, p * LANES: (p + 1) * LANES] = jnp.where(lane < HEAD_DIM, a, pltpu.roll(b, HEAD_DIM, 1))
            if extract_lane is not None:
                for hh, v in ((2 * p, a), (2 * p + 1, b)):
                    col = jnp.sum(jnp.where(lane == extract_lane, v, 0.0), axis=1, keepdims=True)
                    picked = jnp.where(lane == hh, col, picked)
        if extract_lane is not None:
            refs[n + 1][...] = picked

    d = n_heads * HEAD_DIM
    out_shape = [jax.ShapeDtypeStruct((n_rows, d), F32)]
    out_specs = [pl.BlockSpec((tm, d), lambda i: (i, 0))]
    if extract_lane is not None:
        out_shape.append(jax.ShapeDtypeStruct((n_rows, LANES), F32))
        out_specs.append(pl.BlockSpec((tm, LANES), lambda i: (i, 0)))
    return pl.pallas_call(
        body, name=name, grid=(n_rows // tm,),
        in_specs=[pl.BlockSpec((n_heads, tm, LANES), lambda i: (0, i, 0))] * n,
        out_specs=out_specs, out_shape=out_shape, compiler_params=_params("parallel"),
    )(*xs)


def fox_fwd(name, q_ext, k_ext, v_ext, t=None):
    nh, n_rows, w = q_ext.shape
    t = _pick(n_rows, t or TILES["attn"])
    nt = n_rows // t

    def body(q_ref, k_ref, v_ref, o_ref, qb_ref, m_ref, acc_ref):
        i = pl.program_id(1)
        m_ref[...] = jnp.full_like(m_ref, NEG_BIG)
        acc_ref[...] = jnp.zeros_like(acc_ref)
        q = q_ref[...]

        def block(j, diagonal):
            off = pl.multiple_of(j * t, t)
            s = lax.dot_general(q, k_ref[pl.ds(off, t), :], _DIMS["nt"], preferred_element_type=F32)
            if diagonal:
                keep = lax.broadcasted_iota(jnp.int32, (t, t), 0) >= lax.broadcasted_iota(jnp.int32, (t, t), 1)
                s = jnp.where(keep, s, NEG_BIG)
            m_prev = m_ref[...]
            m_new = jnp.maximum(m_prev, jnp.max(s, axis=1, keepdims=True))
            p = jnp.exp2(s - jnp.tile(m_new, (1, t // LANES)))
            acc_ref[...] = jnp.exp2(m_prev - m_new) * acc_ref[...] + jnp.dot(
                p.astype(BF16), v_ref[pl.ds(off, t), :], preferred_element_type=F32)
            m_ref[...] = m_new

        def off_diagonal(j, carry):
            block(j, False)
            return carry

        lax.fori_loop(0, i, off_diagonal, 0)
        block(i, True)
        acc = acc_ref[...]
        row_sum = acc[:, HEAD_DIM:HEAD_DIM + 1]
        hi, mid, lo = _split3(m_ref[:, 0:1] + jnp.log2(row_sum))
        lane = lax.broadcasted_iota(jnp.int32, (t, w), 1)
        o_ref[...] = acc / row_sum
        qb = jnp.where(lane == LSE_LANE, hi, jnp.where(lane == LSE_LANE + 1, mid, jnp.where(lane == LSE_LANE + 2, lo, q.astype(F32))))
        qb_ref[...] = qb.astype(BF16)

    whole = pl.BlockSpec((None, n_rows, w), lambda h, i: (h, 0, 0))
    tile = pl.BlockSpec((None, t, w), lambda h, i: (h, i, 0))
    return pl.pallas_call(
        body, name=name, grid=(nh, nt), in_specs=[tile, whole, whole], out_specs=[tile, tile],
        out_shape=[jax.ShapeDtypeStruct((nh, n_rows, w), F32), jax.ShapeDtypeStruct((nh, n_rows, w), BF16)],
        scratch_shapes=[pltpu.VMEM((t, w), F32), pltpu.VMEM((t, w), F32)],
        compiler_params=_params("parallel", "arbitrary"),
    )(q_ext, k_ext, v_ext)


def fox_bwd(name, q_ext, do_ext, k_ext, v_ext, t=None):
    nh, n_rows, w = q_ext.shape
    t = _pick(n_rows, t or TILES["attn"])
    nt = n_rows // t

    def body(q_ref, do_ref, k_ref, v_ref, dq_ref, dk_ref, dv_ref):
        j = pl.program_id(1)

        @pl.when(j == 0)
        def _():
            dq_ref[...] = jnp.zeros_like(dq_ref)

        dk_ref[...] = jnp.zeros_like(dk_ref)
        dv_ref[...] = jnp.zeros_like(dv_ref)
        kj, vj = k_ref[...], v_ref[...]

        def block(i, diagonal):
            off = pl.multiple_of(i * t, t)
            qi, doi = q_ref[pl.ds(off, t), :], do_ref[pl.ds(off, t), :]
            pt = jnp.exp2(lax.dot_general(kj, qi, _DIMS["nt"], preferred_element_type=F32))
            if diagonal:
                keep = lax.broadcasted_iota(jnp.int32, (t, t), 0) <= lax.broadcasted_iota(jnp.int32, (t, t), 1)
                pt = jnp.where(keep, pt, 0.0)
            dst = (pt * lax.dot_general(vj, doi, _DIMS["nt"], preferred_element_type=F32)).astype(BF16)
            dv_ref[...] += jnp.dot(pt.astype(BF16), doi, preferred_element_type=F32)
            dk_ref[...] += jnp.dot(dst, qi, preferred_element_type=F32)
            dq_ref[pl.ds(off, t), :] += lax.dot_general(dst, kj, _DIMS["tn"], preferred_element_type=F32)

        def off_diagonal(i, carry):
            block(i, False)
            return carry

        block(j, True)
        lax.fori_loop(j + 1, nt, off_diagonal, 0)
        lane = lax.broadcasted_iota(jnp.int32, (t, w), 1)
        dk_ref[...] = dk_ref[...] * jnp.where(lane < HEAD_DIM, 1.0 / LOG2E, 1.0)

    whole = pl.BlockSpec((None, n_rows, w), lambda h, j: (h, 0, 0))
    tile = pl.BlockSpec((None, t, w), lambda h, j: (h, j, 0))
    shape = jax.ShapeDtypeStruct((nh, n_rows, w), F32)
    return pl.pallas_call(
        body, name=name, grid=(nh, nt), in_specs=[whole, whole, tile, tile], out_specs=[whole, tile, tile],
        out_shape=[shape, shape, shape],
        compiler_params=_params("parallel", "arbitrary"),
    )(q_ext, do_ext, k_ext, v_ext)


def attn_delta(name, o, do):
    d_model = o.shape[1]
    head_of_col = lax.broadcasted_iota(jnp.int32, (d_model, LANES), 0) // HEAD_DIM
    sel = (head_of_col == lax.broadcasted_iota(jnp.int32, (d_model, LANES), 1)).astype(F32)

    def fn(a, b, s):
        return _split3(jnp.dot(a * b, s, precision=HIGHEST, preferred_element_type=F32))

    return rowcall(name, fn, [o, do], [sel], [(LANES, F32)] * 3)


_PEER_FLIPS = [(bx, by, bc) for bx in (0, 1) for by in (0, 1) for bc in (0, 1)][1:]


def _exchange(name, tensors, scatter):
    n = len(tensors)
    n_peer = len(_PEER_FLIPS)

    def body(*refs):
        ins, outs = refs[:n], refs[n: 2 * n]
        send_sems, recv_sems, local_sems = refs[2 * n:]
        x, y, c = lax.axis_index("x"), lax.axis_index("y"), lax.axis_index("c")
        me = 4 * x + 2 * y + c
        copies = []
        for t in range(n):
            src_me = ins[t].at[me] if scatter else ins[t]
            local = pltpu.make_async_copy(src_me, outs[t].at[me], local_sems.at[t])
            local.start()
            copies.append(local)
            for kk, (bx, by, bc) in enumerate(_PEER_FLIPS):
                px, py, pc = (1 - x if bx else x), (1 - y if by else y), (1 - c if bc else c)
                peer = 4 * px + 2 * py + pc
                out_cp = pltpu.make_async_remote_copy(
                    src_ref=ins[t].at[peer] if scatter else ins[t], dst_ref=outs[t].at[me],
                    send_sem=send_sems.at[t, kk], recv_sem=recv_sems.at[t, kk],
                    device_id=(px, py, pc), device_id_type=pl.DeviceIdType.MESH)
                out_cp.start()
                copies.append(pltpu.make_async_remote_copy(
                    src_ref=ins[t].at[peer] if scatter else ins[t], dst_ref=outs[t].at[peer],
                    send_sem=send_sems.at[t, kk], recv_sem=recv_sems.at[t, kk],
                    device_id=(px, py, pc), device_id_type=pl.DeviceIdType.MESH))
        for cp in copies:
            cp.wait()

    any_spec = pl.BlockSpec(memory_space=pl.ANY)
    out_shape = [jax.ShapeDtypeStruct(t.shape if scatter else (N_DEV,) + t.shape, t.dtype) for t in tensors]
    return pl.pallas_call(
        body, name=name, in_specs=[any_spec] * n, out_specs=[any_spec] * n, out_shape=out_shape,
        scratch_shapes=[pltpu.SemaphoreType.DMA((n, n_peer)), pltpu.SemaphoreType.DMA((n, n_peer)), pltpu.SemaphoreType.DMA((n,))],
        compiler_params=pltpu.CompilerParams(has_side_effects=True),
    )(*tensors)


def all_gather(name, tensors):
    n = len(tensors)

    def body(*refs):
        ins, outs = refs[:n], refs[n: 2 * n]
        send_sems, recv_sems, local_sems = refs[2 * n:]
        x, y, c = lax.axis_index("x"), lax.axis_index("y"), lax.axis_index("c")
        sibling = (x, y, 1 - c)
        chips = [(1 - x, y), (x, 1 - y), (1 - x, 1 - y)]
        slot = lambda px, py, pc: 4 * px + 2 * py + pc

        def copy(t, k, block, to, src=None):
            dst = outs[t].at[slot(*block)]
            return pltpu.make_async_remote_copy(
                src_ref=dst if src is None else src, dst_ref=dst, send_sem=send_sems.at[t, k], recv_sem=recv_sems.at[t, k],
                device_id=to, device_id_type=pl.DeviceIdType.MESH)

        sends, locals_ = [], []
        for t in range(n):
            own = pltpu.make_async_copy(ins[t], outs[t].at[slot(x, y, c)], local_sems.at[t])
            own.start()
            locals_.append(own)
            first = [copy(t, 0, (x, y, c), sibling, ins[t])]
            first += [copy(t, 1 + j, (x, y, c), (*chip, c), ins[t]) for j, chip in enumerate(chips)]
            for cp in first:
                cp.start()
            sends += first
        for t in range(n):
            for j, chip in enumerate(chips):
                copy(t, 1 + j, (*chip, c), (x, y, c)).wait_recv()
                passed = copy(t, 4 + j, (*chip, c), sibling)
                passed.start()
                sends.append(passed)
        for t in range(n):
            copy(t, 0, sibling, (x, y, c)).wait_recv()
            for j, chip in enumerate(chips):
                copy(t, 4 + j, (*chip, 1 - c), (x, y, c)).wait_recv()
        for cp in sends:
            cp.wait_send()
        for cp in locals_:
            cp.wait()

    any_spec = pl.BlockSpec(memory_space=pl.ANY)
    return pl.pallas_call(
        body, name=name, in_specs=[any_spec] * n, out_specs=[any_spec] * n,
        out_shape=[jax.ShapeDtypeStruct((N_DEV,) + t.shape, t.dtype) for t in tensors],
        scratch_shapes=[pltpu.SemaphoreType.DMA((n, 7)), pltpu.SemaphoreType.DMA((n, 7)), pltpu.SemaphoreType.DMA((n,))],
        compiler_params=pltpu.CompilerParams(has_side_effects=True),
    )(*tensors)


def all_to_all(name, tensors):
    return _exchange(name, tensors, scatter=True)


def _remote_call(name, body, tensors, out_shapes, n_copies):
    n = len(tensors)
    any_spec = pl.BlockSpec(memory_space=pl.ANY)
    return pl.pallas_call(
        body, name=name, in_specs=[any_spec] * n, out_specs=[any_spec] * n, out_shape=out_shapes,
        scratch_shapes=[pltpu.SemaphoreType.DMA((n, n_copies)), pltpu.SemaphoreType.DMA((n, n_copies)), pltpu.SemaphoreType.DMA((n,))],
        compiler_params=pltpu.CompilerParams(has_side_effects=True),
    )(*tensors)


N_CHIPS = 4
_CHIPS = [(0, 0), (0, 1), (1, 0), (1, 1)]


def pair_swap(name, tensors):
    n = len(tensors)

    def body(*refs):
        ins, outs = refs[:n], refs[n: 2 * n]
        send_sems, recv_sems, _ = refs[2 * n:]
        x, y, c = lax.axis_index("x"), lax.axis_index("y"), lax.axis_index("c")
        copies = []
        for t in range(n):
            for k in range(N_CHIPS):
                cp = pltpu.make_async_remote_copy(
                    src_ref=ins[t].at[2 * k + (1 - c)], dst_ref=outs[t].at[k], send_sem=send_sems.at[t, k], recv_sem=recv_sems.at[t, k],
                    device_id=(x, y, 1 - c), device_id_type=pl.DeviceIdType.MESH)
                cp.start()
                copies.append(cp)
        for cp in copies:
            cp.wait()

    return _remote_call(name, body, tensors, [jax.ShapeDtypeStruct((N_CHIPS,) + t.shape[1:], t.dtype) for t in tensors], N_CHIPS)


def chip_exchange(name, tensors):
    n = len(tensors)
    flips = [(1, 0), (0, 1), (1, 1)]

    def body(*refs):
        ins, outs = refs[:n], refs[n: 2 * n]
        send_sems, recv_sems, local_sems = refs[2 * n:]
        x, y, c = lax.axis_index("x"), lax.axis_index("y"), lax.axis_index("c")
        me = 2 * x + y
        copies = []
        for t in range(n):
            own = pltpu.make_async_copy(ins[t].at[me], outs[t].at[me], local_sems.at[t])
            own.start()
            copies.append(own)
            for j, (bx, by) in enumerate(flips):
                px, py = (1 - x if bx else x), (1 - y if by else y)
                peer = 2 * px + py
                pltpu.make_async_remote_copy(
                    src_ref=ins[t].at[peer], dst_ref=outs[t].at[me], send_sem=send_sems.at[t, j], recv_sem=recv_sems.at[t, j],
                    device_id=(px, py, c), device_id_type=pl.DeviceIdType.MESH).start()
                copies.append(pltpu.make_async_remote_copy(
                    src_ref=ins[t].at[peer], dst_ref=outs[t].at[peer], send_sem=send_sems.at[t, j], recv_sem=recv_sems.at[t, j],
                    device_id=(px, py, c), device_id_type=pl.DeviceIdType.MESH))
        for cp in copies:
            cp.wait()

    return _remote_call(name, body, tensors, [jax.ShapeDtypeStruct(t.shape, t.dtype) for t in tensors], len(flips))


def adamw(name, parts, w, m, v, tr=None):
    n_parts = parts.shape[0]
    n_rows, n_cols = w.shape
    tr = _pick(n_rows, tr or TILES["adam"], SUBLANES)
    c1 = 1.0 - ADAM_B1 ** ADAM_STEP
    c2 = 1.0 - ADAM_B2 ** ADAM_STEP

    def body(p_ref, w_ref, m_ref, v_ref, g_ref, d_ref, mo_ref, vo_ref):
        g = p_ref[0].astype(F32)
        for s in range(1, n_parts):
            g = g + p_ref[s].astype(F32)
        mn = ADAM_B1 * m_ref[...] + (1.0 - ADAM_B1) * g
        vn = ADAM_B2 * v_ref[...] + (1.0 - ADAM_B2) * (g * g)
        g_ref[...] = g
        mo_ref[...] = mn
        vo_ref[...] = vn
        d_ref[...] = -ADAM_LR * ((mn / c1) / (jnp.sqrt(vn / c2) + ADAM_EPS) + ADAM_WD * w_ref[...])

    spec = pl.BlockSpec((tr, n_cols), lambda i: (i, 0))
    return pl.pallas_call(
        body, name=name, grid=(n_rows // tr,),
        in_specs=[pl.BlockSpec((n_parts, tr, n_cols), lambda i: (0, i, 0)), spec, spec, spec],
        out_specs=[spec] * 4, out_shape=[jax.ShapeDtypeStruct(w.shape, F32)] * 4,
        compiler_params=_params("parallel"),
    )(parts, w, m, v)


def _eye_mask():
    return jnp.eye(GROUPS_PER_BLOCK, dtype=F32)


def _b_blocks(bbr, bbi):
    nb = bbr.shape[0] // (GROUPS_PER_BLOCK * SSM_STATE)
    eye = _eye_mask()[None, :, None, :, None]

    def one(z):
        z = z.reshape(nb, GROUPS_PER_BLOCK, SSM_STATE, SSM_GROUP).transpose(0, 1, 3, 2)
        return z[:, :, :, None, :] * eye

    w = jnp.stack([one(bbr), one(bbi)], axis=3)
    return w.reshape(nb, LANES, BLOCK_COLS)


def _b_blocks_t(dw):
    nb = dw.shape[0]
    d6 = dw.reshape(nb, GROUPS_PER_BLOCK, SSM_GROUP, 2, GROUPS_PER_BLOCK, SSM_STATE)
    diag = jnp.sum(d6 * _eye_mask()[None, :, None, None, :, None], axis=4)
    diag = diag.transpose(3, 0, 1, 4, 2).reshape(2, nb * GROUPS_PER_BLOCK * SSM_STATE, SSM_GROUP)
    return diag[0], diag[1]


def _c_blocks(c_re, c_im):
    nb = c_re.shape[0] // GROUPS_PER_BLOCK
    eye = _eye_mask()[None, :, None, :, None]

    def one(z):
        z = z.reshape(nb, GROUPS_PER_BLOCK, SSM_GROUP, SSM_STATE).transpose(0, 1, 3, 2)
        return z[:, :, :, None, :] * eye

    w = jnp.stack([one(c_re), -one(c_im)], axis=1)
    return w.reshape(nb, BLOCK_COLS, LANES)


def _c_blocks_t(dw):
    nb = dw.shape[0]
    d6 = dw.reshape(nb, 2, GROUPS_PER_BLOCK, SSM_STATE, GROUPS_PER_BLOCK, SSM_GROUP)
    diag = jnp.sum(d6 * _eye_mask()[None, None, :, None, :, None], axis=4)
    diag = diag.transpose(1, 0, 2, 4, 3).reshape(2, nb * GROUPS_PER_BLOCK, SSM_GROUP, SSM_STATE)
    return diag[0], -diag[1]


def _unshard_cols(g):
    s, k, n = g.shape
    return g.transpose(1, 0, 2).reshape(k, s * n)


def _shard_cols(w):
    k, n = w.shape
    return w.reshape(k, N_DEV, n // N_DEV).transpose(1, 0, 2)


def _pack(arrays):
    chunks, offs, row = [], [], 0
    for a in arrays:
        flat = a.reshape(-1).astype(F32)
        rows = -(-flat.shape[0] // LANES)
        chunks.append(jnp.pad(flat, (0, rows * LANES - flat.shape[0])))
        offs.append((row, rows))
        row += rows
    pad_rows = (-row) % SUBLANES
    if pad_rows:
        chunks.append(jnp.zeros((pad_rows * LANES,), F32))
    return jnp.concatenate(chunks).reshape(row + pad_rows, LANES), offs


def _unpack(packed, offs, shapes):
    out = []
    for (row, rows), shp in zip(offs, shapes):
        size = 1
        for s in shp:
            size *= s
        out.append(packed[row: row + rows].reshape(-1)[:size].reshape(shp))
    return out


def _scan_tables_into(tab_ref, lam_ref, reverse):
    steps, (pr, pi) = _scan_tables(lam_ref, reverse)
    for kk, (_, mr, mi) in enumerate(steps):
        tab_ref[2 * kk] = mr
        tab_ref[2 * kk + 1] = mi
    tab_ref[6] = pr
    tab_ref[7] = pi


def _scan_in_place(buf_ref, tab_ref, carry_ref, ng, reverse, prev_ref=None, acc_ref=None):
    shape = (SUBLANES, BLOCK_STATE)
    last = 0 if reverse else SUBLANES - 1
    first_row = lax.broadcasted_iota(jnp.int32, shape, 0) == 0
    re_cols, im_cols = pl.ds(0, BLOCK_STATE), pl.ds(BLOCK_STATE, BLOCK_STATE)

    def group(r, carry):
        cr, ci = carry[0], carry[1]
        rr = (ng - 1 - r) if reverse else r
        off = pl.multiple_of(rr * SUBLANES, SUBLANES)
        xr, xi = buf_ref[pl.ds(off, SUBLANES), re_cols], buf_ref[pl.ds(off, SUBLANES), im_cols]
        for kk, d in enumerate((1, 2, 4)):
            sh = (SUBLANES - d) if reverse else d
            mr, mi = tab_ref[2 * kk], tab_ref[2 * kk + 1]
            yr, yi = pltpu.roll(xr, sh, 0), pltpu.roll(xi, sh, 0)
            xr, xi = xr + mr * yr - mi * yi, xi + mr * yi + mi * yr
        pr, pi = tab_ref[6], tab_ref[7]
        xr, xi = xr + pr * cr - pi * ci, xi + pr * ci + pi * cr
        buf_ref[pl.ds(off, SUBLANES), re_cols] = xr
        buf_ref[pl.ds(off, SUBLANES), im_cols] = xi
        out = (jnp.broadcast_to(xr[last:last + 1, :], shape), jnp.broadcast_to(xi[last:last + 1, :], shape))
        if prev_ref is not None:
            off8 = pl.multiple_of(off + SUBLANES, SUBLANES)
            before_r, before_i = prev_ref[pl.ds(off, SUBLANES), re_cols], prev_ref[pl.ds(off, SUBLANES), im_cols]
            same_r, same_i = prev_ref[pl.ds(off8, SUBLANES), re_cols], prev_ref[pl.ds(off8, SUBLANES), im_cols]
            sr = jnp.where(first_row, jnp.broadcast_to(before_r[SUBLANES - 1:, :], shape), pltpu.roll(same_r, 1, 0))
            si = jnp.where(first_row, jnp.broadcast_to(before_i[SUBLANES - 1:, :], shape), pltpu.roll(same_i, 1, 0))
            out += (carry[2] + xr * sr + xi * si, carry[3] + xi * sr - xr * si)
        return out

    init = (carry_ref[0], carry_ref[1])
    if prev_ref is not None:
        init += (acc_ref[0], acc_ref[1])
    res = lax.fori_loop(0, ng, group, init)
    carry_ref[0] = res[0]
    carry_ref[1] = res[1]
    if prev_ref is not None:
        acc_ref[0] = res[2]
        acc_ref[1] = res[3]


def s5_fused_fwd(name, u, wb, wc, lam, d_row, tm=None):
    n_rows, d_model = u.shape
    nb = lam.shape[0]
    tm = _pick(n_rows, tm or TILES["scan"], 2 * SUBLANES)
    nt = n_rows // tm

    def body(u_ref, wb_ref, wc_ref, lam_ref, d_ref, z_ref, st_ref, buf_ref, carry_ref, tab_ref):
        @pl.when(pl.program_id(1) == 0)
        def _():
            carry_ref[...] = jnp.zeros_like(carry_ref)
            _scan_tables_into(tab_ref, lam_ref, False)

        uu = u_ref[...]
        buf_ref[...] = jnp.dot(uu.astype(BF16), wb_ref[...], preferred_element_type=F32)
        _scan_in_place(buf_ref, tab_ref, carry_ref, tm // SUBLANES, False)
        st = buf_ref[...].astype(BF16)
        st_ref[...] = st
        z_ref[...] = _gelu(jnp.dot(st, wc_ref[...], preferred_element_type=F32) + d_ref[...] * uu).astype(z_ref.dtype)

    tile = lambda b, it: (it, b)
    blk = lambda b, it: (b, 0, 0)
    return pl.pallas_call(
        body, name=name, grid=(nb, nt),
        in_specs=[pl.BlockSpec((tm, LANES), tile), pl.BlockSpec((None, LANES, BLOCK_COLS), blk),
                  pl.BlockSpec((None, BLOCK_COLS, LANES), blk), pl.BlockSpec((None, 2, BLOCK_STATE), blk),
                  pl.BlockSpec((1, LANES), lambda b, it: (0, b))],
        out_specs=[pl.BlockSpec((tm, LANES), tile), pl.BlockSpec((tm, BLOCK_COLS), tile)],
        out_shape=[jax.ShapeDtypeStruct((n_rows, d_model), BF16), jax.ShapeDtypeStruct((n_rows, nb * BLOCK_COLS), BF16)],
        scratch_shapes=[pltpu.VMEM((tm, BLOCK_COLS), F32), pltpu.VMEM((2, SUBLANES, BLOCK_STATE), F32),
                        pltpu.VMEM((8, SUBLANES, BLOCK_STATE), F32)],
        compiler_params=_params("parallel", "arbitrary"),
    )(u, wb, wc, lam, d_row)


def s5_fused_bwd(name, dz, u, st, wb, wc, lam, d_row, tm=None):
    n_rows, d_model = u.shape
    nb = lam.shape[0]
    tm = _pick(n_rows, tm or TILES["scan"], 2 * SUBLANES)
    nt = n_rows // tm
    tail_rows = 2 * SUBLANES

    def body(dz_ref, u_ref, st_ref, tail_ref, wb_ref, wc_ref, lam_ref, d_ref,
             du_ref, dwb_ref, dwc_ref, dlam_ref, dd_ref, buf_ref, prev_ref, carry_ref, acc_ref, tab_ref):
        it = pl.program_id(1)

        @pl.when(it == 0)
        def _():
            carry_ref[...] = jnp.zeros_like(carry_ref)
            acc_ref[...] = jnp.zeros_like(acc_ref)
            dwb_ref[...] = jnp.zeros_like(dwb_ref)
            dwc_ref[...] = jnp.zeros_like(dwc_ref)
            dd_ref[...] = jnp.zeros_like(dd_ref)
            _scan_tables_into(tab_ref, lam_ref, True)

        uu, st_b = u_ref[...], st_ref[...]
        prev_ref[SUBLANES:, :] = st_b.astype(F32)
        before = tail_ref[...].astype(F32)[SUBLANES:, :]
        prev_ref[:SUBLANES, :] = jnp.where(it == nt - 1, 0.0, before)
        y = jnp.dot(st_b, wc_ref[...], preferred_element_type=F32) + d_ref[...] * uu
        _, vjp = jax.vjp(_gelu, y)
        dy = vjp(dz_ref[...])[0]
        dyb = dy.astype(BF16)
        dd_ref[...] += jnp.sum(dy * uu, axis=0, keepdims=True)
        dwc_ref[...] += lax.dot_general(st_b, dyb, _DIMS["tn"], preferred_element_type=F32)
        buf_ref[...] = lax.dot_general(dyb, wc_ref[...], _DIMS["nt"], preferred_element_type=F32)
        _scan_in_place(buf_ref, tab_ref, carry_ref, tm // SUBLANES, True, prev_ref, acc_ref)
        gb = buf_ref[...].astype(BF16)
        du_ref[...] = lax.dot_general(gb, wb_ref[...], _DIMS["nt"], preferred_element_type=F32) + dy * d_ref[...]
        dwb_ref[...] += lax.dot_general(uu.astype(BF16), gb, _DIMS["tn"], preferred_element_type=F32)

        @pl.when(it == nt - 1)
        def _():
            dlam_ref[0:1, :] = jnp.sum(acc_ref[0], axis=0, keepdims=True)
            dlam_ref[1:2, :] = jnp.sum(acc_ref[1], axis=0, keepdims=True)

    tile = lambda b, it: (nt - 1 - it, b)
    blk = lambda b, it: (b, 0, 0)
    per_tile = tm // tail_rows
    tail = lambda b, it: (jnp.maximum((nt - 1 - it) * per_tile - 1, 0), b)
    return pl.pallas_call(
        body, name=name, grid=(nb, nt),
        in_specs=[pl.BlockSpec((tm, LANES), tile), pl.BlockSpec((tm, LANES), tile), pl.BlockSpec((tm, BLOCK_COLS), tile),
                  pl.BlockSpec((tail_rows, BLOCK_COLS), tail), pl.BlockSpec((None, LANES, BLOCK_COLS), blk),
                  pl.BlockSpec((None, BLOCK_COLS, LANES), blk), pl.BlockSpec((None, 2, BLOCK_STATE), blk),
                  pl.BlockSpec((1, LANES), lambda b, it: (0, b))],
        out_specs=[pl.BlockSpec((tm, LANES), tile), pl.BlockSpec((None, LANES, BLOCK_COLS), blk),
                   pl.BlockSpec((None, BLOCK_COLS, LANES), blk), pl.BlockSpec((None, 2, BLOCK_STATE), blk),
                   pl.BlockSpec((1, LANES), lambda b, it: (0, b))],
        out_shape=[jax.ShapeDtypeStruct((n_rows, d_model), F32), jax.ShapeDtypeStruct((nb, LANES, BLOCK_COLS), F32),
                   jax.ShapeDtypeStruct((nb, BLOCK_COLS, LANES), F32), jax.ShapeDtypeStruct((nb, 2, BLOCK_STATE), F32),
                   jax.ShapeDtypeStruct((1, d_model), F32)],
        scratch_shapes=[pltpu.VMEM((tm, BLOCK_COLS), F32), pltpu.VMEM((tm + SUBLANES, BLOCK_COLS), F32),
                        pltpu.VMEM((2, SUBLANES, BLOCK_STATE), F32), pltpu.VMEM((2, SUBLANES, BLOCK_STATE), F32),
                        pltpu.VMEM((8, SUBLANES, BLOCK_STATE), F32)],
        compiler_params=_params("parallel", "arbitrary"),
    )(dz, u, st, st, wb, wc, lam, d_row)


def s5_fwd(tag, u, log_dt, a_re, a_im, b_re, b_im, c_re, c_im, d_row):
    d_model = u.shape[1]
    nb = d_model // LANES
    col = lambda a: a.reshape(-1, 1)
    prm = [col(jnp.repeat(log_dt, SSM_STATE)), col(a_re), col(a_im), b_re.reshape(-1, SSM_GROUP), b_im.reshape(-1, SSM_GROUP)]
    lr, li, bbr, bbi = rowcall(f"s5_prep_{tag}", _s5_discretise, prm, [], [(1, F32), (1, F32), (SSM_GROUP, F32), (SSM_GROUP, F32)])
    lam = jnp.stack([lr.reshape(nb, BLOCK_STATE), li.reshape(nb, BLOCK_STATE)], axis=1)
    wb = _b_blocks(bbr, bbi).astype(BF16)
    wc = _c_blocks(c_re, c_im).astype(BF16)
    z, st = s5_fused_fwd(f"s5_fwd_{tag}", u, wb, wc, lam, d_row)
    return z, dict(prm=prm, lam=lam, wb=wb, wc=wc, st=st)


def s5_bwd(tag, dz, u, d_row, sv):
    d_model = u.shape[1]
    nb = d_model // LANES
    n_groups = d_model // SSM_GROUP
    col = lambda a: a.reshape(-1, 1)

    dhn, d_wb, d_wc, d_lam, d_dskip = s5_fused_bwd(f"s5_bwd_{tag}", dz, u, sv["st"], sv["wb"], sv["wc"], sv["lam"], d_row)
    d_bbr, d_bbi = _b_blocks_t(d_wb)
    d_cre, d_cim = _c_blocks_t(d_wc)

    def prep_bwd(ldt, ar, ai, br, bi, dlr, dli, dbr, dbi):
        _, vjp = jax.vjp(_s5_discretise, ldt, ar, ai, br, bi)
        return vjp((dlr, dli, dbr, dbi))

    d_ldt, d_are, d_aim, d_bre, d_bim = rowcall(
        f"s5_prep_bwd_{tag}", prep_bwd, sv["prm"] + [col(d_lam[:, 0]), col(d_lam[:, 1]), d_bbr, d_bbi], [],
        [(1, F32), (1, F32), (1, F32), (SSM_GROUP, F32), (SSM_GROUP, F32)])
    d_logdt = rowcall(f"s5_dlogdt_{tag}", lambda a: jnp.sum(a, axis=1, keepdims=True), [d_ldt.reshape(n_groups, SSM_STATE)], [], [(1, F32)])[0]
    grads = dict(log_dt=d_logdt.reshape(n_groups), a_re=d_are.reshape(n_groups, SSM_STATE), a_im=d_aim.reshape(n_groups, SSM_STATE),
                 b_re=d_bre.reshape(n_groups, SSM_STATE, SSM_GROUP), b_im=d_bim.reshape(n_groups, SSM_STATE, SSM_GROUP),
                 c_re=d_cre, c_im=d_cim, d=d_dskip)
    return dhn, grads


def kernel(x, mix_norm, mlp_norm, mlp_w1, mlp_w2, ssm_log_dt, ssm_a_re, ssm_a_im, ssm_b_re, ssm_b_im, ssm_c_re, ssm_c_im, ssm_d, ssm_w_glu, kv_norm, w_kvf, b_f, attn_wq, attn_wo, final_norm, loss_target, m_mix_norm, m_mlp_norm, m_mlp_w1, m_mlp_w2, m_ssm_log_dt, m_ssm_a_re, m_ssm_a_im, m_ssm_b_re, m_ssm_b_im, m_ssm_c_re, m_ssm_c_im, m_ssm_d, m_ssm_w_glu, m_kv_norm, m_w_kvf, m_b_f, m_attn_wq, m_attn_wo, m_final_norm, v_mix_norm, v_mlp_norm, v_mlp_w1, v_mlp_w2, v_ssm_log_dt, v_ssm_a_re, v_ssm_a_im, v_ssm_b_re, v_ssm_b_im, v_ssm_c_re, v_ssm_c_im, v_ssm_d, v_ssm_w_glu, v_kv_norm, v_w_kvf, v_b_f, v_attn_wq, v_attn_wo, v_final_norm):
    n_rows, d_model = x.shape[1], x.shape[2]
    depth = mix_norm.shape[0]
    n_a = ssm_log_dt.shape[0]
    n_b = depth - n_a
    n_heads = d_model // HEAD_DIM
    n_groups = d_model // SSM_GROUP
    nb = n_groups // GROUPS_PER_BLOCK
    kvf_cols = 2 * d_model + n_heads
    kvf_pad = 2 * d_model + LANES

    g_w1, g_w2, g_glu, g_kvf, g_wq, g_wo, g_d = all_gather(
        "gather_weights",
        [mlp_w1.astype(BF16), mlp_w2.astype(BF16), ssm_w_glu.astype(BF16), w_kvf.astype(BF16),
         attn_wq.astype(BF16), attn_wo.astype(BF16), ssm_d])
    w1 = [_unshard_cols(g_w1[:, i]) for i in range(depth)]
    w2 = [g_w2[:, i].reshape(-1, d_model) for i in range(depth)]
    wglu = [_unshard_cols(g_glu[:, i]) for i in range(n_a)]
    wkvf = _unshard_cols(g_kvf)
    wkvf = jnp.pad(wkvf, ((0, 0), (0, kvf_pad - kvf_cols)))
    wq = [g_wq[:, j].reshape(-1, d_model) for j in range(n_b)]
    wo = [g_wo[:, j].reshape(-1, d_model) for j in range(n_b)]
    d_skip = [g_d[:, i].reshape(1, d_model) for i in range(n_a)]

    row = lambda a: a.reshape(1, -1)
    col = lambda a: a.reshape(-1, 1)

    h = x[0]
    saved = []
    k_ext = v_ext = None
    q_consts = _lane_consts([(F_LANE, F_LANE + 3, 1.0)])
    k_consts = _lane_consts([(LSE_LANE, LSE_LANE + 3, -1.0), (ROWSUM_LANE, ROWSUM_LANE + 1, 1.0)])
    v_consts = _lane_consts([(SUM_LANE, SUM_LANE + 1, 1.0), (DELTA_LANE, DELTA_LANE + 3, -1.0)])
    for i in range(depth):
        sv = {"h": h}
        hn = rms_fwd(f"mix_norm_{i}", h, row(mix_norm[i]), F32 if i < n_a else BF16)
        sv["hn"] = hn
        if i < n_a:
            z, s5_saved = s5_fwd(str(i), hn, ssm_log_dt[i], ssm_a_re[i], ssm_a_im[i], ssm_b_re[i], ssm_b_im[i],
                                 ssm_c_re[i], ssm_c_im[i], d_skip[i])
            zw = matmul(f"s5_glu_{i}", z, wglu[i])
            h1 = rowcall(f"s5_gate_{i}", lambda hh, zz: hh + zz[:, :d_model] * _sigmoid(zz[:, d_model:]), [h, zw], [], [(d_model, F32)])[0]
            sv.update(s5=s5_saved, z=z, zw=zw)
        else:
            j = i - n_a
            q = matmul(f"attn_q_{j}", hn, wq[j], scale=LOG2E * HEAD_DIM ** -0.5, out_dtype=BF16)
            o_ext, q_ext_b = fox_fwd(f"attn_fwd_{j}", pack_heads(f"pack_q_{j}", q, 0, n_heads, q_consts), k_ext, v_ext)
            o2 = unpack_heads(f"unpack_o_{j}", [o_ext])[0]
            h1 = matmul(f"attn_o_{j}", o2, wo[j], resid=h)
            sv.update(q_ext_b=q_ext_b, o2=o2)
        h2n = rms_fwd(f"mlp_norm_{i}", h1, row(mlp_norm[i]), BF16)
        ap = matmul(f"mlp_up_{i}", h2n, w1[i], out_dtype=BF16)
        h = matmul(f"mlp_down_{i}", ap, w2[i], a_fn=_sqrelu, resid=h1)
        sv.update(h1=h1, h2n=h2n, ap=ap)
        saved.append(sv)
        if i == n_a - 1:
            h_mid = h
            hk = rms_fwd("kv_norm", h, row(kv_norm), BF16)
            kvf = matmul("kvf_proj", hk, wkvf, bn=kvf_pad)
            fl = kvf[:, 2 * d_model:]
            bfp = jnp.pad(row(b_f), ((0, 0), (0, LANES - n_heads)))
            k_ext = pack_heads("pack_k", kvf, 0, n_heads, k_consts, cum_logf("cum_logf", fl, bfp), F_LANE)
            v_ext = pack_heads("pack_v", kvf, 1, n_heads, v_consts)

    def loss_fn(hh, tgt, g):
        y, vjp = jax.vjp(_rms, hh, g)
        err = y - tgt
        part = 0.5 * jnp.sum(jnp.mean(err * err, axis=-1, keepdims=True), axis=0, keepdims=True)
        dh, dg = vjp(err * (1.0 / d_model))
        return dh, jnp.broadcast_to(part, (1, LANES)), dg

    dh, loss_part, d_final = rowcall("loss_head", loss_fn, [h, loss_target[0]], [row(final_norm)],
                                     [(d_model, F32)], [(1, LANES), (1, d_model)])

    g_mix, g_mlpn = [None] * depth, [None] * depth
    g_w1f, g_w2f = [None] * depth, [None] * depth
    g_ssm = [None] * n_a
    g_wqf, g_wof = [None] * n_b, [None] * n_b
    dk_acc, dv_acc, df_plus = [], [], []
    g_kv = None
    for i in reversed(range(depth)):
        sv = saved[i]
        if i == n_a - 1:
            dk, col_sums = unpack_heads("unpack_dk", dk_acc, extract_lane=HEAD_DIM)
            dv = unpack_heads("unpack_dv", dv_acc)[0]
            dfl, db_f = cum_logf_bwd("cum_logf_bwd", fl, bfp, df_plus, [col_sums])
            dkvf = jnp.concatenate([dk, dv, dfl], axis=1).astype(BF16)
            dhk = matmul("kvf_dx", dkvf, wkvf, "nt", bk=kvf_pad)
            d_wkvf = matmul("kvf_dw", hk, dkvf, "tn", bm=512, bn=kvf_pad)
            dh, d_kvn = rms_bwd("kv_norm_bwd", h_mid, row(kv_norm), dhk, add=dh)
            g_kv = (d_wkvf[:, :kvf_cols], d_kvn, db_f[:, :n_heads])
        dap = matmul(f"mlp_down_dx_{i}", dh, w2[i], "nt", post=lambda acc, apt: acc * (2.0 * jnp.maximum(apt.astype(F32), 0.0)),
                     post_arg=sv["ap"], out_dtype=BF16)
        g_w2f[i] = matmul(f"mlp_down_dw_{i}", sv["ap"], dh, "tn", a_fn=_sqrelu)
        dh2n = matmul(f"mlp_up_dx_{i}", dap, w1[i], "nt")
        g_w1f[i] = matmul(f"mlp_up_dw_{i}", sv["h2n"], dap, "tn")
        dh1, g_mlpn[i] = rms_bwd(f"mlp_norm_bwd_{i}", sv["h1"], row(mlp_norm[i]), dh2n, add=dh)
        if i < n_a:
            def glu_bwd(zz, dd):
                val, gate = zz[:, :d_model], zz[:, d_model:]
                sg = _sigmoid(gate)
                return jnp.concatenate([dd * sg, dd * val * sg * (1.0 - sg)], axis=1)

            dzw = rowcall(f"s5_gate_bwd_{i}", glu_bwd, [sv["zw"], dh1], [], [(2 * d_model, BF16)])[0]
            dz = matmul(f"s5_glu_dx_{i}", dzw, wglu[i], "nt")
            d_wglu = matmul(f"s5_glu_dw_{i}", sv["z"], dzw, "tn")

            dhn, g_ssm[i] = s5_bwd(str(i), dz, sv["hn"], d_skip[i], sv["s5"])
            g_ssm[i]["w_glu"] = d_wglu
        else:
            j = i - n_a
            do2 = matmul(f"attn_o_dx_{j}", dh1, wo[j], "nt")
            g_wof[j] = matmul(f"attn_o_dw_{j}", sv["o2"], dh1, "tn")
            do_ext = pack_heads(f"pack_do_{j}", do2, 0, n_heads, jnp.zeros((1, LANES), F32),
                                attn_delta(f"attn_delta_{j}", sv["o2"], do2), DELTA_LANE)
            dq_ext, dk_ext, dv_ext = fox_bwd(f"attn_bwd_{j}", sv["q_ext_b"], do_ext, k_ext, v_ext)
            dk_acc.append(dk_ext)
            dv_acc.append(dv_ext)
            dq2, row_sums = unpack_heads(f"unpack_dq_{j}", [dq_ext], extract_lane=ROWSUM_LANE)
            df_plus.append(row_sums)
            dhn = matmul(f"attn_q_dx_{j}", dq2, wq[j], "nt", scale=HEAD_DIM ** -0.5)
            g_wqf[j] = matmul(f"attn_q_dw_{j}", sv["hn"], dq2, "tn", scale=HEAD_DIM ** -0.5)
        dh, g_mix[i] = rms_bwd(f"mix_norm_bwd_{i}", sv["h"], row(mix_norm[i]), dhn, add=dh1)
    grad_x = dh[None]

    stack = lambda xs: jnp.stack(xs, axis=1).astype(BF16 if xs[0].ndim == 3 else F32)
    contributions = [
        stack([_shard_cols(g) for g in g_w1f]),
        stack([g.reshape(N_DEV, -1, d_model) for g in g_w2f]),
        stack([_shard_cols(g["w_glu"]) for g in g_ssm]),
        _shard_cols(g_kv[0]).astype(BF16),
        stack([g.reshape(N_DEV, -1, d_model) for g in g_wqf]),
        stack([g.reshape(N_DEV, -1, d_model) for g in g_wof]),
        stack([g["d"].reshape(N_DEV, -1) for g in g_ssm]),
    ]
    from_sibling = pair_swap("pair_swap_grads", contributions)
    core = lax.axis_index("c")
    pair_sums = []
    for k, (mine, theirs) in enumerate(zip(contributions, from_sibling)):
        mine = lax.dynamic_index_in_dim(mine.reshape((N_CHIPS, 2) + mine.shape[1:]), core, axis=1, keepdims=False)
        cols = mine.shape[-1]
        pair_sums.append(rowcall(f"pair_add_{k}", lambda a, b: a.astype(F32) + b.astype(F32),
                                 [mine.reshape(-1, cols), theirs.reshape(-1, cols)], [], [(cols, mine.dtype)])[0].reshape(mine.shape))
    parts = chip_exchange("chip_exchange_grads", pair_sums)
    ssm_g = lambda kk: jnp.stack([g[kk] for g in g_ssm])
    loss_slot = jnp.zeros((LANES,), F32)
    small = {
        "f32": (["mix_norm", "mlp_norm", "ssm_log_dt", "ssm_a_re", "ssm_a_im", "kv_norm", "b_f", "final_norm"],
                [jnp.concatenate(g_mix, axis=0), jnp.concatenate(g_mlpn, axis=0), ssm_g("log_dt"), ssm_g("a_re"), ssm_g("a_im"),
                 g_kv[1], g_kv[2], d_final, loss_part],
                [mix_norm, mlp_norm, ssm_log_dt, ssm_a_re, ssm_a_im, kv_norm, b_f, final_norm, loss_slot],
                [m_mix_norm, m_mlp_norm, m_ssm_log_dt, m_ssm_a_re, m_ssm_a_im, m_kv_norm, m_b_f, m_final_norm, loss_slot],
                [v_mix_norm, v_mlp_norm, v_ssm_log_dt, v_ssm_a_re, v_ssm_a_im, v_kv_norm, v_b_f, v_final_norm, loss_slot]),
        "bf16": (["ssm_b_re", "ssm_b_im", "ssm_c_re", "ssm_c_im"],
                 [ssm_g("b_re"), ssm_g("b_im"), ssm_g("c_re"), ssm_g("c_im")],
                 [ssm_b_re, ssm_b_im, ssm_c_re, ssm_c_im], [m_ssm_b_re, m_ssm_b_im, m_ssm_c_re, m_ssm_c_im],
                 [v_ssm_b_re, v_ssm_b_im, v_ssm_c_re, v_ssm_c_im]),
    }
    packed = {kk: [_pack(arrs) for arrs in grp[1:]] for kk, grp in small.items()}
    small_parts = all_gather("gather_small_grads", [packed["f32"][0][0], packed["bf16"][0][0].astype(BF16)])

    res = {}

    def update(nm, part, w, m, v):
        shp = w.shape
        as2d = lambda a: a.reshape(-1, shp[-1])
        outs = adamw(f"adamw_{nm}", part.reshape(part.shape[:1] + as2d(w).shape), as2d(w), as2d(m), as2d(v))
        res[nm] = [o.reshape(shp) for o in outs]

    update("mlp_w1", parts[0], mlp_w1, m_mlp_w1, v_mlp_w1)
    update("mlp_w2", parts[1], mlp_w2, m_mlp_w2, v_mlp_w2)
    update("ssm_w_glu", parts[2], ssm_w_glu, m_ssm_w_glu, v_ssm_w_glu)
    update("w_kvf", parts[3], w_kvf, m_w_kvf, v_w_kvf)
    update("attn_wq", parts[4], attn_wq, m_attn_wq, v_attn_wq)
    update("attn_wo", parts[5], attn_wo, m_attn_wo, v_attn_wo)
    update("ssm_d", parts[6], ssm_d, m_ssm_d, v_ssm_d)
    loss = None
    for (kk, (names, _, ws, _, _)), part in zip(small.items(), small_parts):
        (_, offs), (pw, _), (pm, _), (pv, _) = packed[kk]
        small_out = adamw(f"adamw_small_{kk}", part, pw, pm, pv)
        unpacked = [_unpack(o, offs, [w.shape for w in ws]) for o in small_out]
        for idx, nm in enumerate(names):
            res[nm] = [u[idx] for u in unpacked]
        if kk == "f32":
            loss = unpacked[0][-1][0]

    order = ["mix_norm", "mlp_norm", "mlp_w1", "mlp_w2", "ssm_log_dt", "ssm_a_re", "ssm_a_im", "ssm_b_re", "ssm_b_im", "ssm_c_re",
             "ssm_c_im", "ssm_d", "ssm_w_glu", "kv_norm", "w_kvf", "b_f", "attn_wq", "attn_wo", "final_norm"]
    out = [loss, grad_x]
    for kind in range(4):
        out += [res[nm][kind] for nm in order]
    return tuple(out)
```

```python
import functools

import jax
import jax.numpy as jnp
from jax import lax
from jax.experimental import pallas as pl
from jax.experimental.pallas import tpu as pltpu

F32 = jnp.float32
BF16 = jnp.bfloat16
HIGHEST = lax.Precision.HIGHEST

V7X_VMEM_BYTES = 64 << 20
VMEM_LIMIT_BYTES = (V7X_VMEM_BYTES * 3) // 4
LANES = 128
SUBLANES = 8

N_DEV = 8
RMS_EPS = 1e-6
SSM_GROUP = 16
SSM_STATE = 64
HEAD_DIM = 64
GROUPS_PER_BLOCK = LANES // SSM_GROUP
BLOCK_STATE = GROUPS_PER_BLOCK * SSM_STATE
BLOCK_COLS = 2 * BLOCK_STATE
NEG_BIG = -1e30
LOG2E = 1.4426950408889634

ADAM_LR = 0.001
ADAM_B1 = 0.9
ADAM_B2 = 0.999
ADAM_EPS = 1e-08
ADAM_WD = 0.01
ADAM_STEP = 10

TILES = {"row": 512, "mm": (1024, 1024, 1024), "blk": 512, "scan": 512, "cum": 256, "attn": 1024, "adam": 256}


def _pick(dim, pref, align=LANES):
    if dim <= pref:
        return dim
    for a in (align, SUBLANES):
        d = (pref // a) * a
        while d >= a:
            if dim % d == 0:
                return d
            d -= a
    return dim


def _params(*sem):
    return pltpu.CompilerParams(dimension_semantics=sem, vmem_limit_bytes=VMEM_LIMIT_BYTES)


def rowcall(name, fn, rows, consts, out_rows, out_accs=(), tm=None):
    n_rows = rows[0].shape[0]
    tm = _pick(n_rows, tm or TILES["row"], SUBLANES)
    nr, nc, no, na = len(rows), len(consts), len(out_rows), len(out_accs)

    def body(*refs):
        ins = [r[...] for r in refs[: nr + nc]]
        outs = fn(*ins)
        if not isinstance(outs, (tuple, list)):
            outs = (outs,)
        for r, o in zip(refs[nr + nc: nr + nc + no], outs[:no]):
            r[...] = o.astype(r.dtype)
        if na:
            i = pl.program_id(0)
            for r, o in zip(refs[nr + nc + no:], outs[no:]):
                @pl.when(i == 0)
                def _(r=r, o=o):
                    r[...] = o

                @pl.when(i > 0)
                def _(r=r, o=o):
                    r[...] += o

    in_specs = [pl.BlockSpec((tm, a.shape[1]), lambda i: (i, 0)) for a in rows]
    in_specs += [pl.BlockSpec(c.shape, lambda i, n=c.ndim: (0,) * n) for c in consts]
    out_shape = [jax.ShapeDtypeStruct((n_rows, c), dt) for c, dt in out_rows]
    out_specs = [pl.BlockSpec((tm, c), lambda i: (i, 0)) for c, _ in out_rows]
    out_shape += [jax.ShapeDtypeStruct(s, F32) for s in out_accs]
    out_specs += [pl.BlockSpec(s, lambda i, n=len(s): (0,) * n) for s in out_accs]
    res = pl.pallas_call(
        body, name=name, grid=(n_rows // tm,), in_specs=in_specs, out_specs=out_specs, out_shape=out_shape,
        compiler_params=_params("arbitrary" if na else "parallel"),
    )(*rows, *consts)
    return res


_DIMS = {"nn": (((1,), (0,)), ((), ())), "nt": (((1,), (1,)), ((), ())), "tn": (((0,), (0,)), ((), ()))}


def matmul(name, a, b, mode="nn", *, a_fn=None, scale=None, resid=None, post=None, post_arg=None,
           out_dtype=F32, bm=None, bn=None, bk=None):
    if mode == "nn":
        (m, k), (k2, n) = a.shape, b.shape
    elif mode == "nt":
        (m, k), (n, k2) = a.shape, b.shape
    else:
        (k, m), (k2, n) = a.shape, b.shape
    assert k == k2, (name, a.shape, b.shape, mode)
    bm, bn, bk = _pick(m, bm or TILES["mm"][0]), _pick(n, bn or TILES["mm"][1]), _pick(k, bk or TILES["mm"][2])
    nk = k // bk
    a_spec = pl.BlockSpec((bk, bm), lambda i, j, kk: (kk, i)) if mode == "tn" else pl.BlockSpec((bm, bk), lambda i, j, kk: (i, kk))
    b_spec = pl.BlockSpec((bn, bk), lambda i, j, kk: (j, kk)) if mode == "nt" else pl.BlockSpec((bk, bn), lambda i, j, kk: (kk, j))
    mn_spec = pl.BlockSpec((bm, bn), lambda i, j, kk: (i, j))
    extra = [x for x in (resid, post_arg) if x is not None]
    has_resid, has_post = resid is not None, post is not None

    def body(*refs):
        a_ref, b_ref = refs[0], refs[1]
        ex = refs[2: 2 + len(extra)]
        o_ref = refs[2 + len(extra)]
        av = a_ref[...]
        if a_fn is not None:
            av = a_fn(av.astype(F32))
        p = lax.dot_general(av.astype(BF16), b_ref[...].astype(BF16), _DIMS[mode], preferred_element_type=F32)

        def finish(acc):
            if scale is not None:
                acc = acc * scale
            idx = 0
            if has_resid:
                acc = acc + ex[idx][...]
                idx += 1
            if has_post:
                acc = post(acc, ex[idx][...])
            o_ref[...] = acc.astype(o_ref.dtype)

        if nk == 1:
            finish(p)
        else:
            acc_ref = refs[-1]
            kk = pl.program_id(2)

            @pl.when(kk == 0)
            def _():
                acc_ref[...] = p

            @pl.when(kk > 0)
            def _():
                acc_ref[...] += p

            @pl.when(kk == nk - 1)
            def _():
                finish(acc_ref[...])

    return pl.pallas_call(
        body, name=name, grid=(m // bm, n // bn, nk),
        in_specs=[a_spec, b_spec] + [mn_spec] * len(extra), out_specs=mn_spec,
        out_shape=jax.ShapeDtypeStruct((m, n), out_dtype),
        scratch_shapes=[pltpu.VMEM((bm, bn), F32)] if nk > 1 else [],
        compiler_params=_params("parallel", "parallel", "arbitrary"),
    )(a, b, *extra)


def _cmul(ar, ai, br, bi):
    return ar * br - ai * bi, ar * bi + ai * br


def _scan_tables(lam_ref, reverse):
    shape = (SUBLANES, BLOCK_STATE)
    lr = jnp.broadcast_to(lam_ref[0:1, :], shape)
    li = jnp.broadcast_to(lam_ref[1:2, :], shape)
    if reverse:
        li = -li
    row = lax.broadcasted_iota(jnp.int32, shape, 0)
    tt = (SUBLANES - 1 - row) if reverse else row
    l1 = (lr, li)
    l2 = _cmul(*l1, *l1)
    l4 = _cmul(*l2, *l2)
    pr, pi = l1
    for bit, lp in enumerate((l1, l2, l4)):
        qr, qi = _cmul(pr, pi, *lp)
        on = ((tt >> bit) & 1) == 1
        pr, pi = jnp.where(on, qr, pr), jnp.where(on, qi, pi)
    steps = []
    for d, lp in ((1, l1), (2, l2), (4, l4)):
        ok = tt >= d
        steps.append((d, jnp.where(ok, lp[0], 0.0), jnp.where(ok, lp[1], 0.0)))
    return steps, (pr, pi)


def _rms(x, g):
    return x * lax.rsqrt(jnp.mean(x * x, axis=-1, keepdims=True) + RMS_EPS) * g


def _sigmoid(x):
    return 1.0 / (1.0 + jnp.exp(-x))


def _gelu(x):
    return 0.5 * x * (1.0 + jnp.tanh(0.7978845608028654 * (x + 0.044715 * (x * x * x))))


def _log_sigmoid(x):
    return jnp.minimum(x, 0.0) - jnp.log(1.0 + jnp.exp(-jnp.abs(x)))


def _sqrelu(x):
    r = jnp.maximum(x, 0.0)
    return r * r


def _s5_discretise(ldt, ar, ai, br, bi):
    dt = jnp.exp(ldt)
    er = jnp.exp(ar * dt)
    lr, li = er * jnp.cos(ai * dt), er * jnp.sin(ai * dt)
    nr, ni = lr - 1.0, li
    den = ar * ar + ai * ai
    cr, ci = (nr * ar + ni * ai) / den, (ni * ar - nr * ai) / den
    return lr, li, cr * br - ci * bi, cr * bi + ci * br


def rms_fwd(name, x, g, dtype=F32):
    return rowcall(name, _rms, [x], [g], [(x.shape[1], dtype)])[0]


def rms_bwd(name, x, g, dy, add=None):
    def fn(x, dy, *rest):
        g = rest[-1]
        _, vjp = jax.vjp(_rms, x, g)
        dx, dg = vjp(dy)
        if add is not None:
            dx = dx + rest[0]
        return dx, dg

    rows = [x, dy] + ([add] if add is not None else [])
    return rowcall(name, fn, rows, [g], [(x.shape[1], F32)], [g.shape])


def _split3(x):
    hi = x.astype(BF16).astype(F32)
    r = x - hi
    mid = r.astype(BF16).astype(F32)
    return hi, mid, (r - mid).astype(BF16).astype(F32)


def cum_logf(name, fl, bf, tm=None):
    n_rows, w = fl.shape
    tm = _pick(n_rows, tm or TILES["cum"], SUBLANES)

    def body(fl_ref, bf_ref, hi_ref, mid_ref, lo_ref, carry_ref):
        it = pl.program_id(0)

        @pl.when(it == 0)
        def _():
            carry_ref[...] = jnp.zeros_like(carry_ref)

        ls = _log_sigmoid(fl_ref[...] + bf_ref[...])
        tri = (lax.broadcasted_iota(jnp.int32, (tm, tm), 0) >= lax.broadcasted_iota(jnp.int32, (tm, tm), 1)).astype(F32)
        c = jnp.dot(tri, ls, precision=HIGHEST, preferred_element_type=F32) + carry_ref[0:1, :]
        carry_ref[...] = jnp.broadcast_to(c[tm - 1:tm, :], carry_ref.shape)
        hi_ref[...], mid_ref[...], lo_ref[...] = _split3(c * (-LOG2E))

    spec = pl.BlockSpec((tm, w), lambda i: (i, 0))
    return pl.pallas_call(
        body, name=name, grid=(n_rows // tm,),
        in_specs=[spec, pl.BlockSpec((1, w), lambda i: (0, 0))],
        out_specs=[spec] * 3, out_shape=[jax.ShapeDtypeStruct((n_rows, w), F32)] * 3,
        scratch_shapes=[pltpu.VMEM((SUBLANES, w), F32)],
        compiler_params=_params("arbitrary"),
    )(fl, bf)


def cum_logf_bwd(name, fl, bf, plus, minus, tm=None):
    n_rows, w = fl.shape
    tm = _pick(n_rows, tm or TILES["cum"], SUBLANES)
    nt = n_rows // tm
    n_p, n_m = len(plus), len(minus)

    def body(*refs):
        fl_ref, bf_ref = refs[0], refs[1]
        d_refs = refs[2: 2 + n_p + n_m]
        o_ref, db_ref, carry_ref = refs[2 + n_p + n_m:]
        it = pl.program_id(0)

        @pl.when(it == 0)
        def _():
            carry_ref[...] = jnp.zeros_like(carry_ref)

        d = None
        for r in d_refs[:n_p]:
            d = r[...] if d is None else d + r[...]
        for r in d_refs[n_p:]:
            d = -r[...] if d is None else d - r[...]
        tri = (lax.broadcasted_iota(jnp.int32, (tm, tm), 0) <= lax.broadcasted_iota(jnp.int32, (tm, tm), 1)).astype(F32)
        c = jnp.dot(tri, d, precision=HIGHEST, preferred_element_type=F32) + carry_ref[0:1, :]
        carry_ref[...] = jnp.broadcast_to(c[0:1, :], carry_ref.shape)
        dfl = c * _sigmoid(-(fl_ref[...] + bf_ref[...]))
        o_ref[...] = dfl
        part = jnp.sum(dfl, axis=0, keepdims=True)

        @pl.when(it == 0)
        def _():
            db_ref[...] = part

        @pl.when(it > 0)
        def _():
            db_ref[...] += part

    rev = lambda i: (nt - 1 - i, 0)
    return pl.pallas_call(
        body, name=name, grid=(nt,),
        in_specs=[pl.BlockSpec((tm, w), rev), pl.BlockSpec((1, w), lambda i: (0, 0))] + [pl.BlockSpec((tm, w), rev)] * (n_p + n_m),
        out_specs=[pl.BlockSpec((tm, w), rev), pl.BlockSpec((1, w), lambda i: (0, 0))],
        out_shape=[jax.ShapeDtypeStruct((n_rows, w), F32), jax.ShapeDtypeStruct((1, w), F32)],
        scratch_shapes=[pltpu.VMEM((SUBLANES, w), F32)],
        compiler_params=_params("arbitrary"),
    )(fl, bf, *plus, *minus)


ROWSUM_LANE = HEAD_DIM + 6
F_LANE = HEAD_DIM
LSE_LANE = HEAD_DIM + 3
SUM_LANE = HEAD_DIM
DELTA_LANE = HEAD_DIM + 1


def _lane_consts(pairs):
    lane = lax.broadcasted_iota(jnp.int32, (1, LANES), 1)
    out = jnp.zeros((1, LANES), F32)
    for lo, hi, v in pairs:
        out = jnp.where((lane >= lo) & (lane < hi), v, out)
    return out


def pack_heads(name, x, col_block, n_heads, consts, parts=(), parts_lane=0, tm=None):
    n_rows = x.shape[0]
    d = n_heads * HEAD_DIM
    assert n_heads % 2 == 0
    tm = _pick(n_rows, tm or TILES["row"], 2 * SUBLANES)
    n_parts = len(parts)

    def body(*refs):
        x_ref, c_ref = refs[0], refs[1]
        p_vals = [r[...] for r in refs[2: 2 + n_parts]]
        o_ref = refs[2 + n_parts]
        lane = lax.broadcasted_iota(jnp.int32, (tm, LANES), 1)
        tail0 = jnp.broadcast_to(c_ref[...], (tm, LANES))
        for h in range(n_heads):
            pair = x_ref[:, (h // 2) * LANES: (h // 2 + 1) * LANES].astype(F32)
            base = pair if h % 2 == 0 else pltpu.roll(pair, HEAD_DIM, 1)
            tail = tail0
            for kk, p in enumerate(p_vals):
                col = jnp.sum(jnp.where(lane == h, p, 0.0), axis=1, keepdims=True)
                tail = jnp.where(lane == parts_lane + kk, col, tail)
            o_ref[h] = jnp.where(lane < HEAD_DIM, base, tail).astype(BF16)

    return pl.pallas_call(
        body, name=name, grid=(n_rows // tm,),
        in_specs=[pl.BlockSpec((tm, d), lambda i: (i, col_block)), pl.BlockSpec((1, LANES), lambda i: (0, 0))]
        + [pl.BlockSpec((tm, LANES), lambda i: (i, 0))] * n_parts,
        out_specs=pl.BlockSpec((n_heads, tm, LANES), lambda i: (0, i, 0)),
        out_shape=jax.ShapeDtypeStruct((n_heads, n_rows, LANES), BF16),
        compiler_params=_params("parallel"),
    )(x, consts, *parts)


def unpack_heads(name, xs, extract_lane=None, tm=None):
    n_heads, n_rows, _ = xs[0].shape
    assert n_heads % 2 == 0
    tm = _pick(n_rows, tm or TILES["row"], SUBLANES)
    n = len(xs)

    def body(*refs):
        o_ref = refs[n]
        lane = lax.broadcasted_iota(jnp.int32, (tm, LANES), 1)
        picked = jnp.zeros((tm, LANES), F32)

        def head(h):
            v = refs[0][h]
            for r in refs[1:n]:
                v = v + r[h]
            return v

        for p in range(n_heads // 2):
            a, b = head(2 * p), head(2 * p + 1)
            o_ref[:, p * LANES: (p + 1) * LANES] = jnp.where(lane < HEAD_DIM, a, pltpu.roll(b, HEAD_DIM, 1))
            if extract_lane is not None:
                for hh, v in ((2 * p, a), (2 * p + 1, b)):
                    col = jnp.sum(jnp.where(lane == extract_lane, v, 0.0), axis=1, keepdims=True)
                    picked = jnp.where(lane == hh, col, picked)
        if extract_lane is not None:
            refs[n + 1][...] = picked

    d = n_heads * HEAD_DIM
    out_shape = [jax.ShapeDtypeStruct((n_rows, d), F32)]
    out_specs = [pl.BlockSpec((tm, d), lambda i: (i, 0))]
    if extract_lane is not None:
        out_shape.append(jax.ShapeDtypeStruct((n_rows, LANES), F32))
        out_specs.append(pl.BlockSpec((tm, LANES), lambda i: (i, 0)))
    return pl.pallas_call(
        body, name=name, grid=(n_rows // tm,),
        in_specs=[pl.BlockSpec((n_heads, tm, LANES), lambda i: (0, i, 0))] * n,
        out_specs=out_specs, out_shape=out_shape, compiler_params=_params("parallel"),
    )(*xs)


def fox_fwd(name, q_ext, k_ext, v_ext, t=None):
    nh, n_rows, w = q_ext.shape
    t = _pick(n_rows, t or TILES["attn"])
    nt = n_rows // t
    half = t // 2

    def body(q_ref, k_ref, v_ref, o_ref, qb_ref, m_ref, acc_ref):
        i = pl.program_id(1)
        m_ref[...] = jnp.full_like(m_ref, NEG_BIG)
        acc_ref[...] = jnp.zeros_like(acc_ref)
        q = q_ref[...]

        def piece(rows, kv_off, size, diagonal):
            kv = pl.ds(pl.multiple_of(kv_off, size), size)
            s = lax.dot_general(q_ref[rows, :], k_ref[kv, :], _DIMS["nt"], preferred_element_type=F32)
            if diagonal:
                keep = lax.broadcasted_iota(jnp.int32, s.shape, 0) >= lax.broadcasted_iota(jnp.int32, s.shape, 1)
                s = jnp.where(keep, s, NEG_BIG)
            m_prev = m_ref[rows, :]
            m_new = jnp.maximum(m_prev, jnp.max(s, axis=1, keepdims=True))
            p = jnp.exp2(s - jnp.tile(m_new, (1, size // LANES)))
            acc_ref[rows, :] = jnp.exp2(m_prev - m_new) * acc_ref[rows, :] + jnp.dot(
                p.astype(BF16), v_ref[kv, :], preferred_element_type=F32)
            m_ref[rows, :] = m_new

        def off_diagonal(j, carry):
            piece(slice(0, t), j * t, t, False)
            return carry

        lax.fori_loop(0, i, off_diagonal, 0)
        if half % LANES == 0:
            piece(slice(0, half), i * t, half, True)
            piece(slice(half, t), i * t, half, False)
            piece(slice(half, t), i * t + half, half, True)
        else:
            piece(slice(0, t), i * t, t, True)
        acc = acc_ref[...]
        row_sum = acc[:, HEAD_DIM:HEAD_DIM + 1]
        hi, mid, lo = _split3(m_ref[:, 0:1] + jnp.log2(row_sum))
        lane = lax.broadcasted_iota(jnp.int32, (t, w), 1)
        o_ref[...] = acc / row_sum
        qb = jnp.where(lane == LSE_LANE, hi, jnp.where(lane == LSE_LANE + 1, mid, jnp.where(lane == LSE_LANE + 2, lo, q.astype(F32))))
        qb_ref[...] = qb.astype(BF16)

    whole = pl.BlockSpec((None, n_rows, w), lambda h, i: (h, 0, 0))
    tile = pl.BlockSpec((None, t, w), lambda h, i: (h, i, 0))
    return pl.pallas_call(
        body, name=name, grid=(nh, nt), in_specs=[tile, whole, whole], out_specs=[tile, tile],
        out_shape=[jax.ShapeDtypeStruct((nh, n_rows, w), F32), jax.ShapeDtypeStruct((nh, n_rows, w), BF16)],
        scratch_shapes=[pltpu.VMEM((t, w), F32), pltpu.VMEM((t, w), F32)],
        compiler_params=_params("parallel", "arbitrary"),
    )(q_ext, k_ext, v_ext)


def fox_bwd(name, q_ext, do_ext, k_ext, v_ext, t=None):
    nh, n_rows, w = q_ext.shape
    t = _pick(n_rows, t or TILES["attn"])
    nt = n_rows // t
    half = t // 2

    def body(q_ref, do_ref, k_ref, v_ref, dq_ref, dk_ref, dv_ref):
        j = pl.program_id(1)

        @pl.when(j == 0)
        def _():
            dq_ref[...] = jnp.zeros_like(dq_ref)

        dk_ref[...] = jnp.zeros_like(dk_ref)
        dv_ref[...] = jnp.zeros_like(dv_ref)
        def piece(keys, q_off, size, diagonal):
            qs = pl.ds(pl.multiple_of(q_off, size), size)
            kj, vj, qi, doi = k_ref[keys, :], v_ref[keys, :], q_ref[qs, :], do_ref[qs, :]
            pt = jnp.exp2(lax.dot_general(kj, qi, _DIMS["nt"], preferred_element_type=F32))
            if diagonal:
                keep = lax.broadcasted_iota(jnp.int32, pt.shape, 0) <= lax.broadcasted_iota(jnp.int32, pt.shape, 1)
                pt = jnp.where(keep, pt, 0.0)
            dst = (pt * lax.dot_general(vj, doi, _DIMS["nt"], preferred_element_type=F32)).astype(BF16)
            dv_ref[keys, :] += jnp.dot(pt.astype(BF16), doi, preferred_element_type=F32)
            dk_ref[keys, :] += jnp.dot(dst, qi, preferred_element_type=F32)
            dq_ref[qs, :] += lax.dot_general(dst, kj, _DIMS["tn"], preferred_element_type=F32)

        def off_diagonal(i, carry):
            piece(slice(0, t), i * t, t, False)
            return carry

        if half % LANES == 0:
            piece(slice(0, half), j * t, half, True)
            piece(slice(0, half), j * t + half, half, False)
            piece(slice(half, t), j * t + half, half, True)
        else:
            piece(slice(0, t), j * t, t, True)
        lax.fori_loop(j + 1, nt, off_diagonal, 0)
        lane = lax.broadcasted_iota(jnp.int32, (t, w), 1)
        dk_ref[...] = dk_ref[...] * jnp.where(lane < HEAD_DIM, 1.0 / LOG2E, 1.0)

    whole = pl.BlockSpec((None, n_rows, w), lambda h, j: (h, 0, 0))
    tile = pl.BlockSpec((None, t, w), lambda h, j: (h, j, 0))
    shape = jax.ShapeDtypeStruct((nh, n_rows, w), F32)
    return pl.pallas_call(
        body, name=name, grid=(nh, nt), in_specs=[whole, whole, tile, tile], out_specs=[whole, tile, tile],
        out_shape=[shape, shape, shape],
        compiler_params=_params("parallel", "arbitrary"),
    )(q_ext, do_ext, k_ext, v_ext)


def attn_delta(name, o, do):
    d_model = o.shape[1]
    head_of_col = lax.broadcasted_iota(jnp.int32, (d_model, LANES), 0) // HEAD_DIM
    sel = (head_of_col == lax.broadcasted_iota(jnp.int32, (d_model, LANES), 1)).astype(F32)

    def fn(a, b, s):
        return _split3(jnp.dot(a * b, s, precision=HIGHEST, preferred_element_type=F32))

    return rowcall(name, fn, [o, do], [sel], [(LANES, F32)] * 3)


def all_gather(name, tensors):
    n = len(tensors)

    def body(*refs):
        ins, outs = refs[:n], refs[n: 2 * n]
        send_sems, recv_sems, local_sems = refs[2 * n:]
        x, y, c = lax.axis_index("x"), lax.axis_index("y"), lax.axis_index("c")
        sibling = (x, y, 1 - c)
        chips = [(1 - x, y), (x, 1 - y), (1 - x, 1 - y)]
        slot = lambda px, py, pc: 4 * px + 2 * py + pc

        def copy(t, k, block, to, src=None):
            dst = outs[t].at[slot(*block)]
            return pltpu.make_async_remote_copy(
                src_ref=dst if src is None else src, dst_ref=dst, send_sem=send_sems.at[t, k], recv_sem=recv_sems.at[t, k],
                device_id=to, device_id_type=pl.DeviceIdType.MESH)

        sends, locals_ = [], []
        for t in range(n):
            own = pltpu.make_async_copy(ins[t], outs[t].at[slot(x, y, c)], local_sems.at[t])
            own.start()
            locals_.append(own)
            first = [copy(t, 0, (x, y, c), sibling, ins[t])]
            first += [copy(t, 1 + j, (x, y, c), (*chip, c), ins[t]) for j, chip in enumerate(chips)]
            for cp in first:
                cp.start()
            sends += first
        for t in range(n):
            for j, chip in enumerate(chips):
                copy(t, 1 + j, (*chip, c), (x, y, c)).wait_recv()
                passed = copy(t, 4 + j, (*chip, c), sibling)
                passed.start()
                sends.append(passed)
        for t in range(n):
            copy(t, 0, sibling, (x, y, c)).wait_recv()
            for j, chip in enumerate(chips):
                copy(t, 4 + j, (*chip, 1 - c), (x, y, c)).wait_recv()
        for cp in sends:
            cp.wait_send()
        for cp in locals_:
            cp.wait()

    any_spec = pl.BlockSpec(memory_space=pl.ANY)
    return pl.pallas_call(
        body, name=name, in_specs=[any_spec] * n, out_specs=[any_spec] * n,
        out_shape=[jax.ShapeDtypeStruct((N_DEV,) + t.shape, t.dtype) for t in tensors],
        scratch_shapes=[pltpu.SemaphoreType.DMA((n, 7)), pltpu.SemaphoreType.DMA((n, 7)), pltpu.SemaphoreType.DMA((n,))],
        compiler_params=pltpu.CompilerParams(has_side_effects=True),
    )(*tensors)


def _remote_call(name, body, tensors, out_shapes, n_copies):
    n = len(tensors)
    any_spec = pl.BlockSpec(memory_space=pl.ANY)
    return pl.pallas_call(
        body, name=name, in_specs=[any_spec] * n, out_specs=[any_spec] * n, out_shape=out_shapes,
        scratch_shapes=[pltpu.SemaphoreType.DMA((n, n_copies)), pltpu.SemaphoreType.DMA((n, n_copies)), pltpu.SemaphoreType.DMA((n,))],
        compiler_params=pltpu.CompilerParams(has_side_effects=True),
    )(*tensors)


N_CHIPS = 4
_CHIPS = [(0, 0), (0, 1), (1, 0), (1, 1)]


def pair_swap(name, tensors):
    n = len(tensors)

    def body(*refs):
        ins, outs = refs[:n], refs[n: 2 * n]
        send_sems, recv_sems, _ = refs[2 * n:]
        x, y, c = lax.axis_index("x"), lax.axis_index("y"), lax.axis_index("c")
        copies = []
        for t in range(n):
            for k in range(N_CHIPS):
                cp = pltpu.make_async_remote_copy(
                    src_ref=ins[t].at[2 * k + (1 - c)], dst_ref=outs[t].at[k], send_sem=send_sems.at[t, k], recv_sem=recv_sems.at[t, k],
                    device_id=(x, y, 1 - c), device_id_type=pl.DeviceIdType.MESH)
                cp.start()
                copies.append(cp)
        for cp in copies:
            cp.wait()

    return _remote_call(name, body, tensors, [jax.ShapeDtypeStruct((N_CHIPS,) + t.shape[1:], t.dtype) for t in tensors], N_CHIPS)


def chip_exchange(name, tensors):
    n = len(tensors)
    flips = [(1, 0), (0, 1), (1, 1)]

    def body(*refs):
        ins, outs = refs[:n], refs[n: 2 * n]
        send_sems, recv_sems, local_sems = refs[2 * n:]
        x, y, c = lax.axis_index("x"), lax.axis_index("y"), lax.axis_index("c")
        me = 2 * x + y
        copies = []
        for t in range(n):
            own = pltpu.make_async_copy(ins[t].at[me], outs[t].at[me], local_sems.at[t])
            own.start()
            copies.append(own)
            for j, (bx, by) in enumerate(flips):
                px, py = (1 - x if bx else x), (1 - y if by else y)
                peer = 2 * px + py
                pltpu.make_async_remote_copy(
                    src_ref=ins[t].at[peer], dst_ref=outs[t].at[me], send_sem=send_sems.at[t, j], recv_sem=recv_sems.at[t, j],
                    device_id=(px, py, c), device_id_type=pl.DeviceIdType.MESH).start()
                copies.append(pltpu.make_async_remote_copy(
                    src_ref=ins[t].at[peer], dst_ref=outs[t].at[peer], send_sem=send_sems.at[t, j], recv_sem=recv_sems.at[t, j],
                    device_id=(px, py, c), device_id_type=pl.DeviceIdType.MESH))
        for cp in copies:
            cp.wait()

    return _remote_call(name, body, tensors, [jax.ShapeDtypeStruct(t.shape, t.dtype) for t in tensors], len(flips))


def adamw(name, parts, w, m, v, tr=None):
    n_parts = parts.shape[0]
    n_rows, n_cols = w.shape
    tr = _pick(n_rows, tr or TILES["adam"], SUBLANES)
    c1 = 1.0 - ADAM_B1 ** ADAM_STEP
    c2 = 1.0 - ADAM_B2 ** ADAM_STEP

    def body(p_ref, w_ref, m_ref, v_ref, g_ref, d_ref, mo_ref, vo_ref):
        g = p_ref[0].astype(F32)
        for s in range(1, n_parts):
            g = g + p_ref[s].astype(F32)
        mn = ADAM_B1 * m_ref[...] + (1.0 - ADAM_B1) * g
        vn = ADAM_B2 * v_ref[...] + (1.0 - ADAM_B2) * (g * g)
        g_ref[...] = g
        mo_ref[...] = mn
        vo_ref[...] = vn
        d_ref[...] = -ADAM_LR * ((mn / c1) / (jnp.sqrt(vn / c2) + ADAM_EPS) + ADAM_WD * w_ref[...])

    spec = pl.BlockSpec((tr, n_cols), lambda i: (i, 0))
    return pl.pallas_call(
        body, name=name, grid=(n_rows // tr,),
        in_specs=[pl.BlockSpec((n_parts, tr, n_cols), lambda i: (0, i, 0)), spec, spec, spec],
        out_specs=[spec] * 4, out_shape=[jax.ShapeDtypeStruct(w.shape, F32)] * 4,
        compiler_params=_params("parallel"),
    )(parts, w, m, v)


def _eye_mask():
    return jnp.eye(GROUPS_PER_BLOCK, dtype=F32)


def _b_blocks(bbr, bbi):
    nb = bbr.shape[0] // (GROUPS_PER_BLOCK * SSM_STATE)
    eye = _eye_mask()[None, :, None, :, None]

    def one(z):
        z = z.reshape(nb, GROUPS_PER_BLOCK, SSM_STATE, SSM_GROUP).transpose(0, 1, 3, 2)
        return z[:, :, :, None, :] * eye

    w = jnp.stack([one(bbr), one(bbi)], axis=3)
    return w.reshape(nb, LANES, BLOCK_COLS)


def _b_blocks_t(dw):
    nb = dw.shape[0]
    d6 = dw.reshape(nb, GROUPS_PER_BLOCK, SSM_GROUP, 2, GROUPS_PER_BLOCK, SSM_STATE)
    diag = jnp.sum(d6 * _eye_mask()[None, :, None, None, :, None], axis=4)
    diag = diag.transpose(3, 0, 1, 4, 2).reshape(2, nb * GROUPS_PER_BLOCK * SSM_STATE, SSM_GROUP)
    return diag[0], diag[1]


def _c_blocks(c_re, c_im):
    nb = c_re.shape[0] // GROUPS_PER_BLOCK
    eye = _eye_mask()[None, :, None, :, None]

    def one(z):
        z = z.reshape(nb, GROUPS_PER_BLOCK, SSM_GROUP, SSM_STATE).transpose(0, 1, 3, 2)
        return z[:, :, :, None, :] * eye

    w = jnp.stack([one(c_re), -one(c_im)], axis=1)
    return w.reshape(nb, BLOCK_COLS, LANES)


def _c_blocks_t(dw):
    nb = dw.shape[0]
    d6 = dw.reshape(nb, 2, GROUPS_PER_BLOCK, SSM_STATE, GROUPS_PER_BLOCK, SSM_GROUP)
    diag = jnp.sum(d6 * _eye_mask()[None, None, :, None, :, None], axis=4)
    diag = diag.transpose(1, 0, 2, 4, 3).reshape(2, nb * GROUPS_PER_BLOCK, SSM_GROUP, SSM_STATE)
    return diag[0], -diag[1]


def _unshard_cols(g):
    s, k, n = g.shape
    return g.transpose(1, 0, 2).reshape(k, s * n)


def _shard_cols(w):
    k, n = w.shape
    return w.reshape(k, N_DEV, n // N_DEV).transpose(1, 0, 2)


def _pack(arrays):
    chunks, offs, row = [], [], 0
    for a in arrays:
        flat = a.reshape(-1).astype(F32)
        rows = -(-flat.shape[0] // LANES)
        chunks.append(jnp.pad(flat, (0, rows * LANES - flat.shape[0])))
        offs.append((row, rows))
        row += rows
    pad_rows = (-row) % SUBLANES
    if pad_rows:
        chunks.append(jnp.zeros((pad_rows * LANES,), F32))
    return jnp.concatenate(chunks).reshape(row + pad_rows, LANES), offs


def _unpack(packed, offs, shapes):
    out = []
    for (row, rows), shp in zip(offs, shapes):
        size = 1
        for s in shp:
            size *= s
        out.append(packed[row: row + rows].reshape(-1)[:size].reshape(shp))
    return out


SCAN_SUB_ROWS = 128


def _scan_tables_into(tab_ref, lam_ref, reverse):
    steps, (pr, pi) = _scan_tables(lam_ref, reverse)
    for kk, (_, mr, mi) in enumerate(steps):
        tab_ref[2 * kk] = mr
        tab_ref[2 * kk + 1] = mi
    tab_ref[6] = pr
    tab_ref[7] = pi


def _scan_in_place(buf_ref, tab_ref, carry_ref, ng, reverse, prev_ref=None, acc_ref=None, row0=0, unroll=False):
    shape = (SUBLANES, BLOCK_STATE)
    last = 0 if reverse else SUBLANES - 1
    first_row = lax.broadcasted_iota(jnp.int32, shape, 0) == 0
    re_cols, im_cols = pl.ds(0, BLOCK_STATE), pl.ds(BLOCK_STATE, BLOCK_STATE)

    def group(r, carry):
        cr, ci = carry[0], carry[1]
        rr = (ng - 1 - r) if reverse else r
        off = row0 + rr * SUBLANES
        off = off if unroll else pl.multiple_of(off, SUBLANES)
        xr, xi = buf_ref[pl.ds(off, SUBLANES), re_cols], buf_ref[pl.ds(off, SUBLANES), im_cols]
        for kk, d in enumerate((1, 2, 4)):
            sh = (SUBLANES - d) if reverse else d
            mr, mi = tab_ref[2 * kk], tab_ref[2 * kk + 1]
            yr, yi = pltpu.roll(xr, sh, 0), pltpu.roll(xi, sh, 0)
            xr, xi = xr + mr * yr - mi * yi, xi + mr * yi + mi * yr
        pr, pi = tab_ref[6], tab_ref[7]
        xr, xi = xr + pr * cr - pi * ci, xi + pr * ci + pi * cr
        buf_ref[pl.ds(off, SUBLANES), re_cols] = xr
        buf_ref[pl.ds(off, SUBLANES), im_cols] = xi
        out = (jnp.broadcast_to(xr[last:last + 1, :], shape), jnp.broadcast_to(xi[last:last + 1, :], shape))
        if prev_ref is not None:
            off8 = off + SUBLANES if unroll else pl.multiple_of(off + SUBLANES, SUBLANES)
            before_r, before_i = prev_ref[pl.ds(off, SUBLANES), re_cols], prev_ref[pl.ds(off, SUBLANES), im_cols]
            same_r, same_i = prev_ref[pl.ds(off8, SUBLANES), re_cols], prev_ref[pl.ds(off8, SUBLANES), im_cols]
            sr = jnp.where(first_row, jnp.broadcast_to(before_r[SUBLANES - 1:, :], shape), pltpu.roll(same_r, 1, 0))
            si = jnp.where(first_row, jnp.broadcast_to(before_i[SUBLANES - 1:, :], shape), pltpu.roll(same_i, 1, 0))
            out += (carry[2] + xr * sr + xi * si, carry[3] + xi * sr - xr * si)
        return out

    init = (carry_ref[0], carry_ref[1])
    if prev_ref is not None:
        init += (acc_ref[0], acc_ref[1])
    if unroll:
        res = init
        for r in range(ng):
            res = group(r, res)
    else:
        res = lax.fori_loop(0, ng, group, init)
    carry_ref[0] = res[0]
    carry_ref[1] = res[1]
    if prev_ref is not None:
        acc_ref[0] = res[2]
        acc_ref[1] = res[3]


def s5_fused_fwd(name, u, wb, wc, lam, d_row, tm=None):
    n_rows, d_model = u.shape
    nb = lam.shape[0]
    tm = _pick(n_rows, tm or TILES["scan"], 2 * SUBLANES)
    nt = n_rows // tm
    sub = _pick(tm, SCAN_SUB_ROWS, 2 * SUBLANES)

    def body(u_ref, wb_ref, wc_ref, lam_ref, d_ref, z_ref, st_ref, buf_ref, carry_ref, tab_ref):
        @pl.when(pl.program_id(1) == 0)
        def _():
            carry_ref[...] = jnp.zeros_like(carry_ref)
            _scan_tables_into(tab_ref, lam_ref, False)

        for r0 in range(0, tm, sub):
            rows = slice(r0, r0 + sub)
            uu = u_ref[rows, :]
            buf_ref[rows, :] = jnp.dot(uu.astype(BF16), wb_ref[...], preferred_element_type=F32)
            _scan_in_place(buf_ref, tab_ref, carry_ref, sub // SUBLANES, False, row0=r0, unroll=True)
            st = buf_ref[rows, :].astype(BF16)
            st_ref[rows, :] = st
            z_ref[rows, :] = _gelu(jnp.dot(st, wc_ref[...], preferred_element_type=F32) + d_ref[...] * uu).astype(z_ref.dtype)

    tile = lambda b, it: (it, b)
    blk = lambda b, it: (b, 0, 0)
    return pl.pallas_call(
        body, name=name, grid=(nb, nt),
        in_specs=[pl.BlockSpec((tm, LANES), tile), pl.BlockSpec((None, LANES, BLOCK_COLS), blk),
                  pl.BlockSpec((None, BLOCK_COLS, LANES), blk), pl.BlockSpec((None, 2, BLOCK_STATE), blk),
                  pl.BlockSpec((1, LANES), lambda b, it: (0, b))],
        out_specs=[pl.BlockSpec((tm, LANES), tile), pl.BlockSpec((tm, BLOCK_COLS), tile)],
        out_shape=[jax.ShapeDtypeStruct((n_rows, d_model), BF16), jax.ShapeDtypeStruct((n_rows, nb * BLOCK_COLS), BF16)],
        scratch_shapes=[pltpu.VMEM((tm, BLOCK_COLS), F32), pltpu.VMEM((2, SUBLANES, BLOCK_STATE), F32),
                        pltpu.VMEM((8, SUBLANES, BLOCK_STATE), F32)],
        compiler_params=_params("parallel", "arbitrary"),
    )(u, wb, wc, lam, d_row)


def s5_fused_bwd(name, dz, u, st, wb, wc, lam, d_row, tm=None):
    n_rows, d_model = u.shape
    nb = lam.shape[0]
    tm = _pick(n_rows, tm or TILES["scan"], 2 * SUBLANES)
    nt = n_rows // tm
    tail_rows = 2 * SUBLANES
    sub = _pick(tm, SCAN_SUB_ROWS, 2 * SUBLANES)

    def body(dz_ref, u_ref, st_ref, tail_ref, wb_ref, wc_ref, lam_ref, d_ref,
             du_ref, dwb_ref, dwc_ref, dlam_ref, dd_ref, buf_ref, prev_ref, carry_ref, acc_ref, tab_ref):
        it = pl.program_id(1)

        @pl.when(it == 0)
        def _():
            carry_ref[...] = jnp.zeros_like(carry_ref)
            acc_ref[...] = jnp.zeros_like(acc_ref)
            dwb_ref[...] = jnp.zeros_like(dwb_ref)
            dwc_ref[...] = jnp.zeros_like(dwc_ref)
            dd_ref[...] = jnp.zeros_like(dd_ref)
            _scan_tables_into(tab_ref, lam_ref, True)

        prev_ref[SUBLANES:, :] = st_ref[...].astype(F32)
        before = tail_ref[...].astype(F32)[SUBLANES:, :]
        prev_ref[:SUBLANES, :] = jnp.where(it == nt - 1, 0.0, before)
        for r0 in range(tm - sub, -1, -sub):
            rows = slice(r0, r0 + sub)
            uu, st_b = u_ref[rows, :], st_ref[rows, :]
            y = jnp.dot(st_b, wc_ref[...], preferred_element_type=F32) + d_ref[...] * uu
            _, vjp = jax.vjp(_gelu, y)
            dy = vjp(dz_ref[rows, :])[0]
            dyb = dy.astype(BF16)
            dd_ref[...] += jnp.sum(dy * uu, axis=0, keepdims=True)
            dwc_ref[...] += lax.dot_general(st_b, dyb, _DIMS["tn"], preferred_element_type=F32)
            buf_ref[rows, :] = lax.dot_general(dyb, wc_ref[...], _DIMS["nt"], preferred_element_type=F32)
            _scan_in_place(buf_ref, tab_ref, carry_ref, sub // SUBLANES, True, prev_ref, acc_ref, row0=r0, unroll=True)
            gb = buf_ref[rows, :].astype(BF16)
            du_ref[rows, :] = lax.dot_general(gb, wb_ref[...], _DIMS["nt"], preferred_element_type=F32) + dy * d_ref[...]
            dwb_ref[...] += lax.dot_general(uu.astype(BF16), gb, _DIMS["tn"], preferred_element_type=F32)

        @pl.when(it == nt - 1)
        def _():
            dlam_ref[0:1, :] = jnp.sum(acc_ref[0], axis=0, keepdims=True)
            dlam_ref[1:2, :] = jnp.sum(acc_ref[1], axis=0, keepdims=True)

    tile = lambda b, it: (nt - 1 - it, b)
    blk = lambda b, it: (b, 0, 0)
    per_tile = tm // tail_rows
    tail = lambda b, it: (jnp.maximum((nt - 1 - it) * per_tile - 1, 0), b)
    return pl.pallas_call(
        body, name=name, grid=(nb, nt),
        in_specs=[pl.BlockSpec((tm, LANES), tile), pl.BlockSpec((tm, LANES), tile), pl.BlockSpec((tm, BLOCK_COLS), tile),
                  pl.BlockSpec((tail_rows, BLOCK_COLS), tail), pl.BlockSpec((None, LANES, BLOCK_COLS), blk),
                  pl.BlockSpec((None, BLOCK_COLS, LANES), blk), pl.BlockSpec((None, 2, BLOCK_STATE), blk),
                  pl.BlockSpec((1, LANES), lambda b, it: (0, b))],
        out_specs=[pl.BlockSpec((tm, LANES), tile), pl.BlockSpec((None, LANES, BLOCK_COLS), blk),
                   pl.BlockSpec((None, BLOCK_COLS, LANES), blk), pl.BlockSpec((None, 2, BLOCK_STATE), blk),
                   pl.BlockSpec((1, LANES), lambda b, it: (0, b))],
        out_shape=[jax.ShapeDtypeStruct((n_rows, d_model), F32), jax.ShapeDtypeStruct((nb, LANES, BLOCK_COLS), F32),
                   jax.ShapeDtypeStruct((nb, BLOCK_COLS, LANES), F32), jax.ShapeDtypeStruct((nb, 2, BLOCK_STATE), F32),
                   jax.ShapeDtypeStruct((1, d_model), F32)],
        scratch_shapes=[pltpu.VMEM((tm, BLOCK_COLS), F32), pltpu.VMEM((tm + SUBLANES, BLOCK_COLS), F32),
                        pltpu.VMEM((2, SUBLANES, BLOCK_STATE), F32), pltpu.VMEM((2, SUBLANES, BLOCK_STATE), F32),
                        pltpu.VMEM((8, SUBLANES, BLOCK_STATE), F32)],
        compiler_params=_params("parallel", "arbitrary"),
    )(dz, u, st, st, wb, wc, lam, d_row)


def s5_fwd(tag, u, log_dt, a_re, a_im, b_re, b_im, c_re, c_im, d_row):
    d_model = u.shape[1]
    nb = d_model // LANES
    col = lambda a: a.reshape(-1, 1)
    prm = [col(jnp.repeat(log_dt, SSM_STATE)), col(a_re), col(a_im), b_re.reshape(-1, SSM_GROUP), b_im.reshape(-1, SSM_GROUP)]
    lr, li, bbr, bbi = rowcall(f"s5_prep_{tag}", _s5_discretise, prm, [], [(1, F32), (1, F32), (SSM_GROUP, F32), (SSM_GROUP, F32)])
    lam = jnp.stack([lr.reshape(nb, BLOCK_STATE), li.reshape(nb, BLOCK_STATE)], axis=1)
    wb = _b_blocks(bbr, bbi).astype(BF16)
    wc = _c_blocks(c_re, c_im).astype(BF16)
    z, st = s5_fused_fwd(f"s5_fwd_{tag}", u, wb, wc, lam, d_row)
    return z, dict(prm=prm, lam=lam, wb=wb, wc=wc, st=st)


def s5_bwd(tag, dz, u, d_row, sv):
    d_model = u.shape[1]
    nb = d_model // LANES
    n_groups = d_model // SSM_GROUP
    col = lambda a: a.reshape(-1, 1)

    dhn, d_wb, d_wc, d_lam, d_dskip = s5_fused_bwd(f"s5_bwd_{tag}", dz, u, sv["st"], sv["wb"], sv["wc"], sv["lam"], d_row)
    d_bbr, d_bbi = _b_blocks_t(d_wb)
    d_cre, d_cim = _c_blocks_t(d_wc)

    def prep_bwd(ldt, ar, ai, br, bi, dlr, dli, dbr, dbi):
        _, vjp = jax.vjp(_s5_discretise, ldt, ar, ai, br, bi)
        return vjp((dlr, dli, dbr, dbi))

    d_ldt, d_are, d_aim, d_bre, d_bim = rowcall(
        f"s5_prep_bwd_{tag}", prep_bwd, sv["prm"] + [col(d_lam[:, 0]), col(d_lam[:, 1]), d_bbr, d_bbi], [],
        [(1, F32), (1, F32), (1, F32), (SSM_GROUP, F32), (SSM_GROUP, F32)])
    d_logdt = rowcall(f"s5_dlogdt_{tag}", lambda a: jnp.sum(a, axis=1, keepdims=True), [d_ldt.reshape(n_groups, SSM_STATE)], [], [(1, F32)])[0]
    grads = dict(log_dt=d_logdt.reshape(n_groups), a_re=d_are.reshape(n_groups, SSM_STATE), a_im=d_aim.reshape(n_groups, SSM_STATE),
                 b_re=d_bre.reshape(n_groups, SSM_STATE, SSM_GROUP), b_im=d_bim.reshape(n_groups, SSM_STATE, SSM_GROUP),
                 c_re=d_cre, c_im=d_cim, d=d_dskip)
    return dhn, grads


def kernel(x, mix_norm, mlp_norm, mlp_w1, mlp_w2, ssm_log_dt, ssm_a_re, ssm_a_im, ssm_b_re, ssm_b_im, ssm_c_re, ssm_c_im, ssm_d, ssm_w_glu, kv_norm, w_kvf, b_f, attn_wq, attn_wo, final_norm, loss_target, m_mix_norm, m_mlp_norm, m_mlp_w1, m_mlp_w2, m_ssm_log_dt, m_ssm_a_re, m_ssm_a_im, m_ssm_b_re, m_ssm_b_im, m_ssm_c_re, m_ssm_c_im, m_ssm_d, m_ssm_w_glu, m_kv_norm, m_w_kvf, m_b_f, m_attn_wq, m_attn_wo, m_final_norm, v_mix_norm, v_mlp_norm, v_mlp_w1, v_mlp_w2, v_ssm_log_dt, v_ssm_a_re, v_ssm_a_im, v_ssm_b_re, v_ssm_b_im, v_ssm_c_re, v_ssm_c_im, v_ssm_d, v_ssm_w_glu, v_kv_norm, v_w_kvf, v_b_f, v_attn_wq, v_attn_wo, v_final_norm):
    n_rows, d_model = x.shape[1], x.shape[2]
    depth = mix_norm.shape[0]
    n_a = ssm_log_dt.shape[0]
    n_b = depth - n_a
    n_heads = d_model // HEAD_DIM
    n_groups = d_model // SSM_GROUP
    nb = n_groups // GROUPS_PER_BLOCK
    kvf_cols = 2 * d_model + n_heads
    kvf_pad = 2 * d_model + LANES

    g_w1, g_w2, g_glu, g_kvf, g_wq, g_wo, g_d = all_gather(
        "gather_weights",
        [mlp_w1.astype(BF16), mlp_w2.astype(BF16), ssm_w_glu.astype(BF16), w_kvf.astype(BF16),
         attn_wq.astype(BF16), attn_wo.astype(BF16), ssm_d])
    w1 = [_unshard_cols(g_w1[:, i]) for i in range(depth)]
    w2 = [g_w2[:, i].reshape(-1, d_model) for i in range(depth)]
    wglu = [_unshard_cols(g_glu[:, i]) for i in range(n_a)]
    wkvf = _unshard_cols(g_kvf)
    wkvf = jnp.pad(wkvf, ((0, 0), (0, kvf_pad - kvf_cols)))
    wq = [g_wq[:, j].reshape(-1, d_model) for j in range(n_b)]
    wo = [g_wo[:, j].reshape(-1, d_model) for j in range(n_b)]
    d_skip = [g_d[:, i].reshape(1, d_model) for i in range(n_a)]

    row = lambda a: a.reshape(1, -1)
    col = lambda a: a.reshape(-1, 1)

    h = x[0]
    saved = []
    k_ext = v_ext = None
    q_consts = _lane_consts([(F_LANE, F_LANE + 3, 1.0)])
    k_consts = _lane_consts([(LSE_LANE, LSE_LANE + 3, -1.0), (ROWSUM_LANE, ROWSUM_LANE + 1, 1.0)])
    v_consts = _lane_consts([(SUM_LANE, SUM_LANE + 1, 1.0), (DELTA_LANE, DELTA_LANE + 3, -1.0)])
    for i in range(depth):
        sv = {"h": h}
        hn = rms_fwd(f"mix_norm_{i}", h, row(mix_norm[i]), F32 if i < n_a else BF16)
        sv["hn"] = hn
        if i < n_a:
            z, s5_saved = s5_fwd(str(i), hn, ssm_log_dt[i], ssm_a_re[i], ssm_a_im[i], ssm_b_re[i], ssm_b_im[i],
                                 ssm_c_re[i], ssm_c_im[i], d_skip[i])
            zw = matmul(f"s5_glu_{i}", z, wglu[i])
            h1 = rowcall(f"s5_gate_{i}", lambda hh, zz: hh + zz[:, :d_model] * _sigmoid(zz[:, d_model:]), [h, zw], [], [(d_model, F32)])[0]
            sv.update(s5=s5_saved, z=z, zw=zw)
        else:
            j = i - n_a
            q = matmul(f"attn_q_{j}", hn, wq[j], scale=LOG2E * HEAD_DIM ** -0.5, out_dtype=BF16)
            o_ext, q_ext_b = fox_fwd(f"attn_fwd_{j}", pack_heads(f"pack_q_{j}", q, 0, n_heads, q_consts), k_ext, v_ext)
            o2 = unpack_heads(f"unpack_o_{j}", [o_ext])[0]
            h1 = matmul(f"attn_o_{j}", o2, wo[j], resid=h)
            sv.update(q_ext_b=q_ext_b, o2=o2)
        h2n = rms_fwd(f"mlp_norm_{i}", h1, row(mlp_norm[i]), BF16)
        ap = matmul(f"mlp_up_{i}", h2n, w1[i], out_dtype=BF16)
        h = matmul(f"mlp_down_{i}", ap, w2[i], a_fn=_sqrelu, resid=h1, bk=2048)
        sv.update(h1=h1, h2n=h2n, ap=ap)
        saved.append(sv)
        if i == n_a - 1:
            h_mid = h
            hk = rms_fwd("kv_norm", h, row(kv_norm), BF16)
            kvf = matmul("kvf_proj", hk, wkvf, bn=kvf_pad)
            fl = kvf[:, 2 * d_model:]
            bfp = jnp.pad(row(b_f), ((0, 0), (0, LANES - n_heads)))
            k_ext = pack_heads("pack_k", kvf, 0, n_heads, k_consts, cum_logf("cum_logf", fl, bfp), F_LANE)
            v_ext = pack_heads("pack_v", kvf, 1, n_heads, v_consts)

    def loss_fn(hh, tgt, g):
        y, vjp = jax.vjp(_rms, hh, g)
        err = y - tgt
        part = 0.5 * jnp.sum(jnp.mean(err * err, axis=-1, keepdims=True), axis=0, keepdims=True)
        dh, dg = vjp(err * (1.0 / d_model))
        return dh, jnp.broadcast_to(part, (1, LANES)), dg

    dh, loss_part, d_final = rowcall("loss_head", loss_fn, [h, loss_target[0]], [row(final_norm)],
                                     [(d_model, F32)], [(1, LANES), (1, d_model)])

    g_mix, g_mlpn = [None] * depth, [None] * depth
    g_w1f, g_w2f = [None] * depth, [None] * depth
    g_ssm = [None] * n_a
    g_wqf, g_wof = [None] * n_b, [None] * n_b
    dk_acc, dv_acc, df_plus = [], [], []
    g_kv = None
    for i in reversed(range(depth)):
        sv = saved[i]
        if i == n_a - 1:
            dk, col_sums = unpack_heads("unpack_dk", dk_acc, extract_lane=HEAD_DIM)
            dv = unpack_heads("unpack_dv", dv_acc)[0]
            dfl, db_f = cum_logf_bwd("cum_logf_bwd", fl, bfp, df_plus, [col_sums])
            dkvf = jnp.concatenate([dk, dv, dfl], axis=1).astype(BF16)
            dhk = matmul("kvf_dx", dkvf, wkvf, "nt", bk=kvf_pad)
            d_wkvf = matmul("kvf_dw", hk, dkvf, "tn", bm=512, bn=kvf_pad)
            dh, d_kvn = rms_bwd("kv_norm_bwd", h_mid, row(kv_norm), dhk, add=dh)
            g_kv = (d_wkvf[:, :kvf_cols], d_kvn, db_f[:, :n_heads])
        dap = matmul(f"mlp_down_dx_{i}", dh, w2[i], "nt", post=lambda acc, apt: acc * (2.0 * jnp.maximum(apt.astype(F32), 0.0)),
                     post_arg=sv["ap"], out_dtype=BF16)
        g_w2f[i] = matmul(f"mlp_down_dw_{i}", sv["ap"], dh, "tn", a_fn=_sqrelu)
        dh2n = matmul(f"mlp_up_dx_{i}", dap, w1[i], "nt", bk=2048)
        g_w1f[i] = matmul(f"mlp_up_dw_{i}", sv["h2n"], dap, "tn")
        dh1, g_mlpn[i] = rms_bwd(f"mlp_norm_bwd_{i}", sv["h1"], row(mlp_norm[i]), dh2n, add=dh)
        if i < n_a:
            def glu_bwd(zz, dd):
                val, gate = zz[:, :d_model], zz[:, d_model:]
                sg = _sigmoid(gate)
                return jnp.concatenate([dd * sg, dd * val * sg * (1.0 - sg)], axis=1)

            dzw = rowcall(f"s5_gate_bwd_{i}", glu_bwd, [sv["zw"], dh1], [], [(2 * d_model, BF16)])[0]
            dz = matmul(f"s5_glu_dx_{i}", dzw, wglu[i], "nt")
            d_wglu = matmul(f"s5_glu_dw_{i}", sv["z"], dzw, "tn")

            dhn, g_ssm[i] = s5_bwd(str(i), dz, sv["hn"], d_skip[i], sv["s5"])
            g_ssm[i]["w_glu"] = d_wglu
        else:
            j = i - n_a
            do2 = matmul(f"attn_o_dx_{j}", dh1, wo[j], "nt")
            g_wof[j] = matmul(f"attn_o_dw_{j}", sv["o2"], dh1, "tn")
            do_ext = pack_heads(f"pack_do_{j}", do2, 0, n_heads, jnp.zeros((1, LANES), F32),
                                attn_delta(f"attn_delta_{j}", sv["o2"], do2), DELTA_LANE)
            dq_ext, dk_ext, dv_ext = fox_bwd(f"attn_bwd_{j}", sv["q_ext_b"], do_ext, k_ext, v_ext)
            dk_acc.append(dk_ext)
            dv_acc.append(dv_ext)
            dq2, row_sums = unpack_heads(f"unpack_dq_{j}", [dq_ext], extract_lane=ROWSUM_LANE)
            df_plus.append(row_sums)
            dhn = matmul(f"attn_q_dx_{j}", dq2, wq[j], "nt", scale=HEAD_DIM ** -0.5)
            g_wqf[j] = matmul(f"attn_q_dw_{j}", sv["hn"], dq2, "tn", scale=HEAD_DIM ** -0.5)
        dh, g_mix[i] = rms_bwd(f"mix_norm_bwd_{i}", sv["h"], row(mix_norm[i]), dhn, add=dh1)
    grad_x = dh[None]

    stack = lambda xs: jnp.stack(xs, axis=1).astype(BF16 if xs[0].ndim == 3 else F32)
    contributions = [
        stack([_shard_cols(g) for g in g_w1f]),
        stack([g.reshape(N_DEV, -1, d_model) for g in g_w2f]),
        stack([_shard_cols(g["w_glu"]) for g in g_ssm]),
        _shard_cols(g_kv[0]).astype(BF16),
        stack([g.reshape(N_DEV, -1, d_model) for g in g_wqf]),
        stack([g.reshape(N_DEV, -1, d_model) for g in g_wof]),
        stack([g["d"].reshape(N_DEV, -1) for g in g_ssm]),
    ]
    from_sibling = pair_swap("pair_swap_grads", contributions)
    core = lax.axis_index("c")
    pair_sums = []
    for k, (mine, theirs) in enumerate(zip(contributions, from_sibling)):
        mine = lax.dynamic_index_in_dim(mine.reshape((N_CHIPS, 2) + mine.shape[1:]), core, axis=1, keepdims=False)
        cols = mine.shape[-1]
        pair_sums.append(rowcall(f"pair_add_{k}", lambda a, b: a.astype(F32) + b.astype(F32),
                                 [mine.reshape(-1, cols), theirs.reshape(-1, cols)], [], [(cols, mine.dtype)])[0].reshape(mine.shape))
    parts = chip_exchange("chip_exchange_grads", pair_sums)
    ssm_g = lambda kk: jnp.stack([g[kk] for g in g_ssm])
    loss_slot = jnp.zeros((LANES,), F32)
    small = {
        "f32": (["mix_norm", "mlp_norm", "ssm_log_dt", "ssm_a_re", "ssm_a_im", "kv_norm", "b_f", "final_norm"],
                [jnp.concatenate(g_mix, axis=0), jnp.concatenate(g_mlpn, axis=0), ssm_g("log_dt"), ssm_g("a_re"), ssm_g("a_im"),
                 g_kv[1], g_kv[2], d_final, loss_part],
                [mix_norm, mlp_norm, ssm_log_dt, ssm_a_re, ssm_a_im, kv_norm, b_f, final_norm, loss_slot],
                [m_mix_norm, m_mlp_norm, m_ssm_log_dt, m_ssm_a_re, m_ssm_a_im, m_kv_norm, m_b_f, m_final_norm, loss_slot],
                [v_mix_norm, v_mlp_norm, v_ssm_log_dt, v_ssm_a_re, v_ssm_a_im, v_kv_norm, v_b_f, v_final_norm, loss_slot]),
        "bf16": (["ssm_b_re", "ssm_b_im", "ssm_c_re", "ssm_c_im"],
                 [ssm_g("b_re"), ssm_g("b_im"), ssm_g("c_re"), ssm_g("c_im")],
                 [ssm_b_re, ssm_b_im, ssm_c_re, ssm_c_im], [m_ssm_b_re, m_ssm_b_im, m_ssm_c_re, m_ssm_c_im],
                 [v_ssm_b_re, v_ssm_b_im, v_ssm_c_re, v_ssm_c_im]),
    }
    packed = {kk: [_pack(arrs) for arrs in grp[1:]] for kk, grp in small.items()}
    small_parts = all_gather("gather_small_grads", [packed["f32"][0][0], packed["bf16"][0][0].astype(BF16)])

    res = {}

    def update(nm, part, w, m, v):
        shp = w.shape
        as2d = lambda a: a.reshape(-1, shp[-1])
        outs = adamw(f"adamw_{nm}", part.reshape(part.shape[:1] + as2d(w).shape), as2d(w), as2d(m), as2d(v))
        res[nm] = [o.reshape(shp) for o in outs]

    update("mlp_w1", parts[0], mlp_w1, m_mlp_w1, v_mlp_w1)
    update("mlp_w2", parts[1], mlp_w2, m_mlp_w2, v_mlp_w2)
    update("ssm_w_glu", parts[2], ssm_w_glu, m_ssm_w_glu, v_ssm_w_glu)
    update("w_kvf", parts[3], w_kvf, m_w_kvf, v_w_kvf)
    update("attn_wq", parts[4], attn_wq, m_attn_wq, v_attn_wq)
    update("attn_wo", parts[5], attn_wo, m_attn_wo, v_attn_wo)
    update("ssm_d", parts[6], ssm_d, m_ssm_d, v_ssm_d)
    loss = None
    for (kk, (names, _, ws, _, _)), part in zip(small.items(), small_parts):
        (_, offs), (pw, _), (pm, _), (pv, _) = packed[kk]
        small_out = adamw(f"adamw_small_{kk}", part, pw, pm, pv)
        unpacked = [_unpack(o, offs, [w.shape for w in ws]) for o in small_out]
        for idx, nm in enumerate(names):
            res[nm] = [u[idx] for u in unpacked]
        if kk == "f32":
            loss = unpacked[0][-1][0]

    order = ["mix_norm", "mlp_norm", "mlp_w1", "mlp_w2", "ssm_log_dt", "ssm_a_re", "ssm_a_im", "ssm_b_re", "ssm_b_im", "ssm_c_re",
             "ssm_c_im", "ssm_d", "ssm_w_glu", "kv_norm", "w_kvf", "b_f", "attn_wq", "attn_wo", "final_norm"]
    out = [loss, grad_x]
    for kind in range(4):
        out += [res[nm][kind] for nm in order]
    return tuple(out)
```

```python
import functools

import jax
import jax.numpy as jnp
from jax import lax
from jax.experimental import pallas as pl
from jax.experimental.pallas import tpu as pltpu

F32 = jnp.float32
BF16 = jnp.bfloat16
HIGHEST = lax.Precision.HIGHEST

V7X_VMEM_BYTES = 64 << 20
VMEM_LIMIT_BYTES = (V7X_VMEM_BYTES * 3) // 4
LANES = 128
SUBLANES = 8

N_DEV = 8
RMS_EPS = 1e-6
SSM_GROUP = 16
SSM_STATE = 64
HEAD_DIM = 64
GROUPS_PER_BLOCK = LANES // SSM_GROUP
BLOCK_STATE = GROUPS_PER_BLOCK * SSM_STATE
BLOCK_COLS = 2 * BLOCK_STATE
NEG_BIG = -1e30
LOG2E = 1.4426950408889634

ADAM_LR = 0.001
ADAM_B1 = 0.9
ADAM_B2 = 0.999
ADAM_EPS = 1e-08
ADAM_WD = 0.01
ADAM_STEP = 10

TILES = {"row": 512, "mm": (1024, 1024, 1024), "blk": 512, "scan": 512, "cum": 256, "attn": 1024, "adam": 256}


def _pick(dim, pref, align=LANES):
    if dim <= pref:
        return dim
    for a in (align, SUBLANES):
        d = (pref // a) * a
        while d >= a:
            if dim % d == 0:
                return d
            d -= a
    return dim


def _params(*sem):
    return pltpu.CompilerParams(dimension_semantics=sem, vmem_limit_bytes=VMEM_LIMIT_BYTES)


def rowcall(name, fn, rows, consts, out_rows, out_accs=(), tm=None):
    n_rows = rows[0].shape[0]
    tm = _pick(n_rows, tm or TILES["row"], SUBLANES)
    nr, nc, no, na = len(rows), len(consts), len(out_rows), len(out_accs)

    def body(*refs):
        ins = [r[...] for r in refs[: nr + nc]]
        outs = fn(*ins)
        if not isinstance(outs, (tuple, list)):
            outs = (outs,)
        for r, o in zip(refs[nr + nc: nr + nc + no], outs[:no]):
            r[...] = o.astype(r.dtype)
        if na:
            i = pl.program_id(0)
            for r, o in zip(refs[nr + nc + no:], outs[no:]):
                @pl.when(i == 0)
                def _(r=r, o=o):
                    r[...] = o

                @pl.when(i > 0)
                def _(r=r, o=o):
                    r[...] += o

    in_specs = [pl.BlockSpec((tm, a.shape[1]), lambda i: (i, 0)) for a in rows]
    in_specs += [pl.BlockSpec(c.shape, lambda i, n=c.ndim: (0,) * n) for c in consts]
    out_shape = [jax.ShapeDtypeStruct((n_rows, c), dt) for c, dt in out_rows]
    out_specs = [pl.BlockSpec((tm, c), lambda i: (i, 0)) for c, _ in out_rows]
    out_shape += [jax.ShapeDtypeStruct(s, F32) for s in out_accs]
    out_specs += [pl.BlockSpec(s, lambda i, n=len(s): (0,) * n) for s in out_accs]
    res = pl.pallas_call(
        body, name=name, grid=(n_rows // tm,), in_specs=in_specs, out_specs=out_specs, out_shape=out_shape,
        compiler_params=_params("arbitrary" if na else "parallel"),
    )(*rows, *consts)
    return res


_DIMS = {"nn": (((1,), (0,)), ((), ())), "nt": (((1,), (1,)), ((), ())), "tn": (((0,), (0,)), ((), ()))}


def matmul(name, a, b, mode="nn", *, a_fn=None, scale=None, resid=None, post=None, post_arg=None, rms_bwd_of=None,
           out_dtype=F32, bm=None, bn=None, bk=None):
    if mode == "nn":
        (m, k), (k2, n) = a.shape, b.shape
    elif mode == "nt":
        (m, k), (n, k2) = a.shape, b.shape
    else:
        (k, m), (k2, n) = a.shape, b.shape
    assert k == k2, (name, a.shape, b.shape, mode)
    if rms_bwd_of is not None:
        bn = n
    bm, bn, bk = _pick(m, bm or TILES["mm"][0]), _pick(n, bn or TILES["mm"][1]), _pick(k, bk or TILES["mm"][2])
    nk = k // bk
    a_spec = pl.BlockSpec((bk, bm), lambda i, j, kk: (kk, i)) if mode == "tn" else pl.BlockSpec((bm, bk), lambda i, j, kk: (i, kk))
    b_spec = pl.BlockSpec((bn, bk), lambda i, j, kk: (j, kk)) if mode == "nt" else pl.BlockSpec((bk, bn), lambda i, j, kk: (kk, j))
    mn_spec = pl.BlockSpec((bm, bn), lambda i, j, kk: (i, j))
    extra = [x for x in (resid, post_arg) if x is not None]
    has_resid, has_post, has_rms = resid is not None, post is not None, rms_bwd_of is not None
    row_spec = pl.BlockSpec((1, bn), lambda i, j, kk: (0, j))
    extra_specs = [mn_spec] * len(extra)
    if has_rms:
        x_in, gain, add = rms_bwd_of
        extra += [x_in, add, gain]
        extra_specs += [mn_spec, mn_spec, row_spec]

    def body(*refs):
        a_ref, b_ref = refs[0], refs[1]
        ex = refs[2: 2 + len(extra)]
        o_ref = refs[2 + len(extra)]
        av = a_ref[...]
        if a_fn is not None:
            av = a_fn(av.astype(F32))
        p = lax.dot_general(av.astype(BF16), b_ref[...].astype(BF16), _DIMS[mode], preferred_element_type=F32)

        def finish(acc):
            if scale is not None:
                acc = acc * scale
            idx = 0
            if has_resid:
                acc = acc + ex[idx][...]
                idx += 1
            if has_post:
                acc = post(acc, ex[idx][...])
                idx += 1
            if has_rms:
                _, vjp = jax.vjp(_rms, ex[idx][...], ex[idx + 2][...])
                dx, d_gain = vjp(acc)
                acc = dx + ex[idx + 1][...]
                dg_ref = refs[3 + len(extra)]
                first = pl.program_id(0) == 0

                @pl.when(first)
                def _():
                    dg_ref[...] = d_gain

                @pl.when(jnp.logical_not(first))
                def _():
                    dg_ref[...] += d_gain
            o_ref[...] = acc.astype(o_ref.dtype)

        if nk == 1:
            finish(p)
        else:
            acc_ref = refs[-1]
            kk = pl.program_id(2)

            @pl.when(kk == 0)
            def _():
                acc_ref[...] = p

            @pl.when(kk > 0)
            def _():
                acc_ref[...] += p

            @pl.when(kk == nk - 1)
            def _():
                finish(acc_ref[...])

    out_shape, out_specs = jax.ShapeDtypeStruct((m, n), out_dtype), mn_spec
    if has_rms:
        out_shape, out_specs = [out_shape, jax.ShapeDtypeStruct((1, n), F32)], [mn_spec, row_spec]
    return pl.pallas_call(
        body, name=name, grid=(m // bm, n // bn, nk),
        in_specs=[a_spec, b_spec] + extra_specs, out_specs=out_specs, out_shape=out_shape,
        scratch_shapes=[pltpu.VMEM((bm, bn), F32)] if nk > 1 else [],
        compiler_params=_params(*(("arbitrary",) * 3 if has_rms else ("parallel", "parallel", "arbitrary"))),
    )(a, b, *extra)


def _cmul(ar, ai, br, bi):
    return ar * br - ai * bi, ar * bi + ai * br


def _scan_tables(lam_ref, reverse):
    shape = (SUBLANES, BLOCK_STATE)
    lr = jnp.broadcast_to(lam_ref[0:1, :], shape)
    li = jnp.broadcast_to(lam_ref[1:2, :], shape)
    if reverse:
        li = -li
    row = lax.broadcasted_iota(jnp.int32, shape, 0)
    tt = (SUBLANES - 1 - row) if reverse else row
    l1 = (lr, li)
    l2 = _cmul(*l1, *l1)
    l4 = _cmul(*l2, *l2)
    pr, pi = l1
    for bit, lp in enumerate((l1, l2, l4)):
        qr, qi = _cmul(pr, pi, *lp)
        on = ((tt >> bit) & 1) == 1
        pr, pi = jnp.where(on, qr, pr), jnp.where(on, qi, pi)
    steps = []
    for d, lp in ((1, l1), (2, l2), (4, l4)):
        ok = tt >= d
        steps.append((d, jnp.where(ok, lp[0], 0.0), jnp.where(ok, lp[1], 0.0)))
    return steps, (pr, pi)


def _rms(x, g):
    return x * lax.rsqrt(jnp.mean(x * x, axis=-1, keepdims=True) + RMS_EPS) * g


def _sigmoid(x):
    return 1.0 / (1.0 + jnp.exp(-x))


def _gelu(x):
    return 0.5 * x * (1.0 + jnp.tanh(0.7978845608028654 * (x + 0.044715 * (x * x * x))))


def _log_sigmoid(x):
    return jnp.minimum(x, 0.0) - jnp.log(1.0 + jnp.exp(-jnp.abs(x)))


def _sqrelu(x):
    r = jnp.maximum(x, 0.0)
    return r * r


def _s5_discretise(ldt, ar, ai, br, bi):
    dt = jnp.exp(ldt)
    er = jnp.exp(ar * dt)
    lr, li = er * jnp.cos(ai * dt), er * jnp.sin(ai * dt)
    nr, ni = lr - 1.0, li
    den = ar * ar + ai * ai
    cr, ci = (nr * ar + ni * ai) / den, (ni * ar - nr * ai) / den
    return lr, li, cr * br - ci * bi, cr * bi + ci * br


def rms_fwd(name, x, g, dtype=F32):
    return rowcall(name, _rms, [x], [g], [(x.shape[1], dtype)])[0]


def rms_bwd(name, x, g, dy, add=None):
    def fn(x, dy, *rest):
        g = rest[-1]
        _, vjp = jax.vjp(_rms, x, g)
        dx, dg = vjp(dy)
        if add is not None:
            dx = dx + rest[0]
        return dx, dg

    rows = [x, dy] + ([add] if add is not None else [])
    return rowcall(name, fn, rows, [g], [(x.shape[1], F32)], [g.shape])


def _split3(x):
    hi = x.astype(BF16).astype(F32)
    r = x - hi
    mid = r.astype(BF16).astype(F32)
    return hi, mid, (r - mid).astype(BF16).astype(F32)


def cum_logf(name, fl, bf, tm=None):
    n_rows, w = fl.shape
    tm = _pick(n_rows, tm or TILES["cum"], SUBLANES)

    def body(fl_ref, bf_ref, hi_ref, mid_ref, lo_ref, carry_ref):
        it = pl.program_id(0)

        @pl.when(it == 0)
        def _():
            carry_ref[...] = jnp.zeros_like(carry_ref)

        ls = _log_sigmoid(fl_ref[...] + bf_ref[...])
        tri = (lax.broadcasted_iota(jnp.int32, (tm, tm), 0) >= lax.broadcasted_iota(jnp.int32, (tm, tm), 1)).astype(F32)
        c = jnp.dot(tri, ls, precision=HIGHEST, preferred_element_type=F32) + carry_ref[0:1, :]
        carry_ref[...] = jnp.broadcast_to(c[tm - 1:tm, :], carry_ref.shape)
        hi_ref[...], mid_ref[...], lo_ref[...] = _split3(c * (-LOG2E))

    spec = pl.BlockSpec((tm, w), lambda i: (i, 0))
    return pl.pallas_call(
        body, name=name, grid=(n_rows // tm,),
        in_specs=[spec, pl.BlockSpec((1, w), lambda i: (0, 0))],
        out_specs=[spec] * 3, out_shape=[jax.ShapeDtypeStruct((n_rows, w), F32)] * 3,
        scratch_shapes=[pltpu.VMEM((SUBLANES, w), F32)],
        compiler_params=_params("arbitrary"),
    )(fl, bf)


def cum_logf_bwd(name, fl, bf, plus, minus, tm=None):
    n_rows, w = fl.shape
    tm = _pick(n_rows, tm or TILES["cum"], SUBLANES)
    nt = n_rows // tm
    n_p, n_m = len(plus), len(minus)

    def body(*refs):
        fl_ref, bf_ref = refs[0], refs[1]
        d_refs = refs[2: 2 + n_p + n_m]
        o_ref, db_ref, carry_ref = refs[2 + n_p + n_m:]
        it = pl.program_id(0)

        @pl.when(it == 0)
        def _():
            carry_ref[...] = jnp.zeros_like(carry_ref)

        d = None
        for r in d_refs[:n_p]:
            d = r[...] if d is None else d + r[...]
        for r in d_refs[n_p:]:
            d = -r[...] if d is None else d - r[...]
        tri = (lax.broadcasted_iota(jnp.int32, (tm, tm), 0) <= lax.broadcasted_iota(jnp.int32, (tm, tm), 1)).astype(F32)
        c = jnp.dot(tri, d, precision=HIGHEST, preferred_element_type=F32) + carry_ref[0:1, :]
        carry_ref[...] = jnp.broadcast_to(c[0:1, :], carry_ref.shape)
        dfl = c * _sigmoid(-(fl_ref[...] + bf_ref[...]))
        o_ref[...] = dfl
        part = jnp.sum(dfl, axis=0, keepdims=True)

        @pl.when(it == 0)
        def _():
            db_ref[...] = part

        @pl.when(it > 0)
        def _():
            db_ref[...] += part

    rev = lambda i: (nt - 1 - i, 0)
    return pl.pallas_call(
        body, name=name, grid=(nt,),
        in_specs=[pl.BlockSpec((tm, w), rev), pl.BlockSpec((1, w), lambda i: (0, 0))] + [pl.BlockSpec((tm, w), rev)] * (n_p + n_m),
        out_specs=[pl.BlockSpec((tm, w), rev), pl.BlockSpec((1, w), lambda i: (0, 0))],
        out_shape=[jax.ShapeDtypeStruct((n_rows, w), F32), jax.ShapeDtypeStruct((1, w), F32)],
        scratch_shapes=[pltpu.VMEM((SUBLANES, w), F32)],
        compiler_params=_params("arbitrary"),
    )(fl, bf, *plus, *minus)


ROWSUM_LANE = HEAD_DIM + 6
F_LANE = HEAD_DIM
LSE_LANE = HEAD_DIM + 3
SUM_LANE = HEAD_DIM
DELTA_LANE = HEAD_DIM + 1


def _lane_consts(pairs):
    lane = lax.broadcasted_iota(jnp.int32, (1, LANES), 1)
    out = jnp.zeros((1, LANES), F32)
    for lo, hi, v in pairs:
        out = jnp.where((lane >= lo) & (lane < hi), v, out)
    return out


def pack_heads(name, x, col_block, n_heads, consts, parts=(), parts_lane=0, tm=None):
    n_rows = x.shape[0]
    d = n_heads * HEAD_DIM
    assert n_heads % 2 == 0
    tm = _pick(n_rows, tm or TILES["row"], 2 * SUBLANES)
    n_parts = len(parts)

    def body(*refs):
        x_ref, c_ref = refs[0], refs[1]
        p_vals = [r[...] for r in refs[2: 2 + n_parts]]
        o_ref = refs[2 + n_parts]
        lane = lax.broadcasted_iota(jnp.int32, (tm, LANES), 1)
        tail0 = jnp.broadcast_to(c_ref[...], (tm, LANES))
        for h in range(n_heads):
            pair = x_ref[:, (h // 2) * LANES: (h // 2 + 1) * LANES].astype(F32)
            base = pair if h % 2 == 0 else pltpu.roll(pair, HEAD_DIM, 1)
            tail = tail0
            for kk, p in enumerate(p_vals):
                col = jnp.sum(jnp.where(lane == h, p, 0.0), axis=1, keepdims=True)
                tail = jnp.where(lane == parts_lane + kk, col, tail)
            o_ref[h] = jnp.where(lane < HEAD_DIM, base, tail).astype(BF16)

    return pl.pallas_call(
        body, name=name, grid=(n_rows // tm,),
        in_specs=[pl.BlockSpec((tm, d), lambda i: (i, col_block)), pl.BlockSpec((1, LANES), lambda i: (0, 0))]
        + [pl.BlockSpec((tm, LANES), lambda i: (i, 0))] * n_parts,
        out_specs=pl.BlockSpec((n_heads, tm, LANES), lambda i: (0, i, 0)),
        out_shape=jax.ShapeDtypeStruct((n_heads, n_rows, LANES), BF16),
        compiler_params=_params("parallel"),
    )(x, consts, *parts)


def unpack_heads(name, xs, extract_lane=None, tm=None):
    n_heads, n_rows, _ = xs[0].shape
    assert n_heads % 2 == 0
    tm = _pick(n_rows, tm or TILES["row"], SUBLANES)
    n = len(xs)

    def body(*refs):
        o_ref = refs[n]
        lane = lax.broadcasted_iota(jnp.int32, (tm, LANES), 1)
        picked = jnp.zeros((tm, LANES), F32)

        def head(h):
            v = refs[0][h]
            for r in refs[1:n]:
                v = v + r[h]
            return v

        for p in range(n_heads // 2):
            a, b = head(2 * p), head(2 * p + 1)
            o_ref[:, p * LANES: (p + 1) * LANES] = jnp.where(lane < HEAD_DIM, a, pltpu.roll(b, HEAD_DIM, 1))
            if extract_lane is not None:
                for hh, v in ((2 * p, a), (2 * p + 1, b)):
                    col = jnp.sum(jnp.where(lane == extract_lane, v, 0.0), axis=1, keepdims=True)
                    picked = jnp.where(lane == hh, col, picked)
        if extract_lane is not None:
            refs[n + 1][...] = picked

    d = n_heads * HEAD_DIM
    out_shape = [jax.ShapeDtypeStruct((n_rows, d), F32)]
    out_specs = [pl.BlockSpec((tm, d), lambda i: (i, 0))]
    if extract_lane is not None:
        out_shape.append(jax.ShapeDtypeStruct((n_rows, LANES), F32))
        out_specs.append(pl.BlockSpec((tm, LANES), lambda i: (i, 0)))
    return pl.pallas_call(
        body, name=name, grid=(n_rows // tm,),
        in_specs=[pl.BlockSpec((n_heads, tm, LANES), lambda i: (0, i, 0))] * n,
        out_specs=out_specs, out_shape=out_shape, compiler_params=_params("parallel"),
    )(*xs)


def fox_fwd(name, q_ext, k_ext, v_ext, t=None):
    nh, n_rows, w = q_ext.shape
    t = _pick(n_rows, t or TILES["attn"])
    nt = n_rows // t
    half = t // 2

    def body(q_ref, k_ref, v_ref, o_ref, qb_ref, m_ref, acc_ref):
        i = pl.program_id(1)
        m_ref[...] = jnp.full_like(m_ref, NEG_BIG)
        acc_ref[...] = jnp.zeros_like(acc_ref)
        q = q_ref[...]

        def piece(rows, kv_off, size, diagonal):
            kv = pl.ds(pl.multiple_of(kv_off, size), size)
            s = lax.dot_general(q_ref[rows, :], k_ref[kv, :], _DIMS["nt"], preferred_element_type=F32)
            if diagonal:
                keep = lax.broadcasted_iota(jnp.int32, s.shape, 0) >= lax.broadcasted_iota(jnp.int32, s.shape, 1)
                s = jnp.where(keep, s, NEG_BIG)
            m_prev = m_ref[rows, :]
            m_new = jnp.maximum(m_prev, jnp.max(s, axis=1, keepdims=True))
            p = jnp.exp2(s - jnp.tile(m_new, (1, size // LANES)))
            acc_ref[rows, :] = jnp.exp2(m_prev - m_new) * acc_ref[rows, :] + jnp.dot(
                p.astype(BF16), v_ref[kv, :], preferred_element_type=F32)
            m_ref[rows, :] = m_new

        def off_diagonal(j, carry):
            piece(slice(0, t), j * t, t, False)
            return carry

        lax.fori_loop(0, i, off_diagonal, 0)
        if half % LANES == 0:
            piece(slice(0, half), i * t, half, True)
            piece(slice(half, t), i * t, half, False)
            piece(slice(half, t), i * t + half, half, True)
        else:
            piece(slice(0, t), i * t, t, True)
        acc = acc_ref[...]
        row_sum = acc[:, HEAD_DIM:HEAD_DIM + 1]
        hi, mid, lo = _split3(m_ref[:, 0:1] + jnp.log2(row_sum))
        lane = lax.broadcasted_iota(jnp.int32, (t, w), 1)
        o_ref[...] = acc / row_sum
        qb = jnp.where(lane == LSE_LANE, hi, jnp.where(lane == LSE_LANE + 1, mid, jnp.where(lane == LSE_LANE + 2, lo, q.astype(F32))))
        qb_ref[...] = qb.astype(BF16)

    whole = pl.BlockSpec((None, n_rows, w), lambda h, i: (h, 0, 0))
    tile = pl.BlockSpec((None, t, w), lambda h, i: (h, i, 0))
    return pl.pallas_call(
        body, name=name, grid=(nh, nt), in_specs=[tile, whole, whole], out_specs=[tile, tile],
        out_shape=[jax.ShapeDtypeStruct((nh, n_rows, w), F32), jax.ShapeDtypeStruct((nh, n_rows, w), BF16)],
        scratch_shapes=[pltpu.VMEM((t, w), F32), pltpu.VMEM((t, w), F32)],
        compiler_params=_params("parallel", "arbitrary"),
    )(q_ext, k_ext, v_ext)


def fox_bwd(name, q_ext, do_ext, k_ext, v_ext, t=None):
    nh, n_rows, w = q_ext.shape
    t = _pick(n_rows, t or TILES["attn"])
    nt = n_rows // t
    half = t // 2

    def body(q_ref, do_ref, k_ref, v_ref, dq_ref, dk_ref, dv_ref):
        j = pl.program_id(1)

        @pl.when(j == 0)
        def _():
            dq_ref[...] = jnp.zeros_like(dq_ref)

        dk_ref[...] = jnp.zeros_like(dk_ref)
        dv_ref[...] = jnp.zeros_like(dv_ref)
        def piece(keys, q_off, size, diagonal):
            qs = pl.ds(pl.multiple_of(q_off, size), size)
            kj, vj, qi, doi = k_ref[keys, :], v_ref[keys, :], q_ref[qs, :], do_ref[qs, :]
            pt = jnp.exp2(lax.dot_general(kj, qi, _DIMS["nt"], preferred_element_type=F32))
            if diagonal:
                keep = lax.broadcasted_iota(jnp.int32, pt.shape, 0) <= lax.broadcasted_iota(jnp.int32, pt.shape, 1)
                pt = jnp.where(keep, pt, 0.0)
            dst = (pt * lax.dot_general(vj, doi, _DIMS["nt"], preferred_element_type=F32)).astype(BF16)
            dv_ref[keys, :] += jnp.dot(pt.astype(BF16), doi, preferred_element_type=F32)
            dk_ref[keys, :] += jnp.dot(dst, qi, preferred_element_type=F32)
            dq_ref[qs, :] += lax.dot_general(dst, kj, _DIMS["tn"], preferred_element_type=F32)

        def off_diagonal(i, carry):
            piece(slice(0, t), i * t, t, False)
            return carry

        if half % LANES == 0:
            piece(slice(0, half), j * t, half, True)
            piece(slice(0, half), j * t + half, half, False)
            piece(slice(half, t), j * t + half, half, True)
        else:
            piece(slice(0, t), j * t, t, True)
        lax.fori_loop(j + 1, nt, off_diagonal, 0)
        lane = lax.broadcasted_iota(jnp.int32, (t, w), 1)
        dk_ref[...] = dk_ref[...] * jnp.where(lane < HEAD_DIM, 1.0 / LOG2E, 1.0)

    whole = pl.BlockSpec((None, n_rows, w), lambda h, j: (h, 0, 0))
    tile = pl.BlockSpec((None, t, w), lambda h, j: (h, j, 0))
    shape = jax.ShapeDtypeStruct((nh, n_rows, w), F32)
    return pl.pallas_call(
        body, name=name, grid=(nh, nt), in_specs=[whole, whole, tile, tile], out_specs=[whole, tile, tile],
        out_shape=[shape, shape, shape],
        compiler_params=_params("parallel", "arbitrary"),
    )(q_ext, do_ext, k_ext, v_ext)


def attn_delta(name, o, do):
    d_model = o.shape[1]
    head_of_col = lax.broadcasted_iota(jnp.int32, (d_model, LANES), 0) // HEAD_DIM
    sel = (head_of_col == lax.broadcasted_iota(jnp.int32, (d_model, LANES), 1)).astype(F32)

    def fn(a, b, s):
        return _split3(jnp.dot(a * b, s, precision=HIGHEST, preferred_element_type=F32))

    return rowcall(name, fn, [o, do], [sel], [(LANES, F32)] * 3)


def all_gather(name, tensors):
    n = len(tensors)

    def body(*refs):
        ins, outs = refs[:n], refs[n: 2 * n]
        send_sems, recv_sems, local_sems = refs[2 * n:]
        x, y, c = lax.axis_index("x"), lax.axis_index("y"), lax.axis_index("c")
        sibling = (x, y, 1 - c)
        chips = [(1 - x, y), (x, 1 - y), (1 - x, 1 - y)]
        slot = lambda px, py, pc: 4 * px + 2 * py + pc

        def copy(t, k, block, to, src=None):
            dst = outs[t].at[slot(*block)]
            return pltpu.make_async_remote_copy(
                src_ref=dst if src is None else src, dst_ref=dst, send_sem=send_sems.at[t, k], recv_sem=recv_sems.at[t, k],
                device_id=to, device_id_type=pl.DeviceIdType.MESH)

        sends, locals_ = [], []
        for t in range(n):
            own = pltpu.make_async_copy(ins[t], outs[t].at[slot(x, y, c)], local_sems.at[t])
            own.start()
            locals_.append(own)
            first = [copy(t, 0, (x, y, c), sibling, ins[t])]
            first += [copy(t, 1 + j, (x, y, c), (*chip, c), ins[t]) for j, chip in enumerate(chips)]
            for cp in first:
                cp.start()
            sends += first
        for t in range(n):
            for j, chip in enumerate(chips):
                copy(t, 1 + j, (*chip, c), (x, y, c)).wait_recv()
                passed = copy(t, 4 + j, (*chip, c), sibling)
                passed.start()
                sends.append(passed)
        for t in range(n):
            copy(t, 0, sibling, (x, y, c)).wait_recv()
            for j, chip in enumerate(chips):
                copy(t, 4 + j, (*chip, 1 - c), (x, y, c)).wait_recv()
        for cp in sends:
            cp.wait_send()
        for cp in locals_:
            cp.wait()

    any_spec = pl.BlockSpec(memory_space=pl.ANY)
    return pl.pallas_call(
        body, name=name, in_specs=[any_spec] * n, out_specs=[any_spec] * n,
        out_shape=[jax.ShapeDtypeStruct((N_DEV,) + t.shape, t.dtype) for t in tensors],
        scratch_shapes=[pltpu.SemaphoreType.DMA((n, 7)), pltpu.SemaphoreType.DMA((n, 7)), pltpu.SemaphoreType.DMA((n,))],
        compiler_params=pltpu.CompilerParams(has_side_effects=True),
    )(*tensors)


def _remote_call(name, body, tensors, out_shapes, n_copies):
    n = len(tensors)
    any_spec = pl.BlockSpec(memory_space=pl.ANY)
    return pl.pallas_call(
        body, name=name, in_specs=[any_spec] * n, out_specs=[any_spec] * n, out_shape=out_shapes,
        scratch_shapes=[pltpu.SemaphoreType.DMA((n, n_copies)), pltpu.SemaphoreType.DMA((n, n_copies)), pltpu.SemaphoreType.DMA((n,))],
        compiler_params=pltpu.CompilerParams(has_side_effects=True),
    )(*tensors)


N_CHIPS = 4
_CHIPS = [(0, 0), (0, 1), (1, 0), (1, 1)]


def pair_swap(name, tensors):
    n = len(tensors)

    def body(*refs):
        ins, outs = refs[:n], refs[n: 2 * n]
        send_sems, recv_sems, _ = refs[2 * n:]
        x, y, c = lax.axis_index("x"), lax.axis_index("y"), lax.axis_index("c")
        copies = []
        for t in range(n):
            for k in range(N_CHIPS):
                cp = pltpu.make_async_remote_copy(
                    src_ref=ins[t].at[2 * k + (1 - c)], dst_ref=outs[t].at[k], send_sem=send_sems.at[t, k], recv_sem=recv_sems.at[t, k],
                    device_id=(x, y, 1 - c), device_id_type=pl.DeviceIdType.MESH)
                cp.start()
                copies.append(cp)
        for cp in copies:
            cp.wait()

    return _remote_call(name, body, tensors, [jax.ShapeDtypeStruct((N_CHIPS,) + t.shape[1:], t.dtype) for t in tensors], N_CHIPS)


def chip_exchange(name, tensors):
    n = len(tensors)
    flips = [(1, 0), (0, 1), (1, 1)]

    def body(*refs):
        ins, outs = refs[:n], refs[n: 2 * n]
        send_sems, recv_sems, local_sems = refs[2 * n:]
        x, y, c = lax.axis_index("x"), lax.axis_index("y"), lax.axis_index("c")
        me = 2 * x + y
        copies = []
        for t in range(n):
            own = pltpu.make_async_copy(ins[t].at[me], outs[t].at[me], local_sems.at[t])
            own.start()
            copies.append(own)
            for j, (bx, by) in enumerate(flips):
                px, py = (1 - x if bx else x), (1 - y if by else y)
                peer = 2 * px + py
                pltpu.make_async_remote_copy(
                    src_ref=ins[t].at[peer], dst_ref=outs[t].at[me], send_sem=send_sems.at[t, j], recv_sem=recv_sems.at[t, j],
                    device_id=(px, py, c), device_id_type=pl.DeviceIdType.MESH).start()
                copies.append(pltpu.make_async_remote_copy(
                    src_ref=ins[t].at[peer], dst_ref=outs[t].at[peer], send_sem=send_sems.at[t, j], recv_sem=recv_sems.at[t, j],
                    device_id=(px, py, c), device_id_type=pl.DeviceIdType.MESH))
        for cp in copies:
            cp.wait()

    return _remote_call(name, body, tensors, [jax.ShapeDtypeStruct(t.shape, t.dtype) for t in tensors], len(flips))


def adamw(name, parts, w, m, v, tr=None):
    n_parts = parts.shape[0]
    n_rows, n_cols = w.shape
    tr = _pick(n_rows, tr or TILES["adam"], SUBLANES)
    c1 = 1.0 - ADAM_B1 ** ADAM_STEP
    c2 = 1.0 - ADAM_B2 ** ADAM_STEP

    def body(p_ref, w_ref, m_ref, v_ref, g_ref, d_ref, mo_ref, vo_ref):
        g = p_ref[0].astype(F32)
        for s in range(1, n_parts):
            g = g + p_ref[s].astype(F32)
        mn = ADAM_B1 * m_ref[...] + (1.0 - ADAM_B1) * g
        vn = ADAM_B2 * v_ref[...] + (1.0 - ADAM_B2) * (g * g)
        g_ref[...] = g
        mo_ref[...] = mn
        vo_ref[...] = vn
        d_ref[...] = -ADAM_LR * ((mn / c1) / (jnp.sqrt(vn / c2) + ADAM_EPS) + ADAM_WD * w_ref[...])

    spec = pl.BlockSpec((tr, n_cols), lambda i: (i, 0))
    return pl.pallas_call(
        body, name=name, grid=(n_rows // tr,),
        in_specs=[pl.BlockSpec((n_parts, tr, n_cols), lambda i: (0, i, 0)), spec, spec, spec],
        out_specs=[spec] * 4, out_shape=[jax.ShapeDtypeStruct(w.shape, F32)] * 4,
        compiler_params=_params("parallel"),
    )(parts, w, m, v)


def _eye_mask():
    return jnp.eye(GROUPS_PER_BLOCK, dtype=F32)


def _b_blocks(bbr, bbi):
    nb = bbr.shape[0] // (GROUPS_PER_BLOCK * SSM_STATE)
    eye = _eye_mask()[None, :, None, :, None]

    def one(z):
        z = z.reshape(nb, GROUPS_PER_BLOCK, SSM_STATE, SSM_GROUP).transpose(0, 1, 3, 2)
        return z[:, :, :, None, :] * eye

    w = jnp.stack([one(bbr), one(bbi)], axis=3)
    return w.reshape(nb, LANES, BLOCK_COLS)


def _b_blocks_t(dw):
    nb = dw.shape[0]
    d6 = dw.reshape(nb, GROUPS_PER_BLOCK, SSM_GROUP, 2, GROUPS_PER_BLOCK, SSM_STATE)
    diag = jnp.sum(d6 * _eye_mask()[None, :, None, None, :, None], axis=4)
    diag = diag.transpose(3, 0, 1, 4, 2).reshape(2, nb * GROUPS_PER_BLOCK * SSM_STATE, SSM_GROUP)
    return diag[0], diag[1]


def _c_blocks(c_re, c_im):
    nb = c_re.shape[0] // GROUPS_PER_BLOCK
    eye = _eye_mask()[None, :, None, :, None]

    def one(z):
        z = z.reshape(nb, GROUPS_PER_BLOCK, SSM_GROUP, SSM_STATE).transpose(0, 1, 3, 2)
        return z[:, :, :, None, :] * eye

    w = jnp.stack([one(c_re), -one(c_im)], axis=1)
    return w.reshape(nb, BLOCK_COLS, LANES)


def _c_blocks_t(dw):
    nb = dw.shape[0]
    d6 = dw.reshape(nb, 2, GROUPS_PER_BLOCK, SSM_STATE, GROUPS_PER_BLOCK, SSM_GROUP)
    diag = jnp.sum(d6 * _eye_mask()[None, None, :, None, :, None], axis=4)
    diag = diag.transpose(1, 0, 2, 4, 3).reshape(2, nb * GROUPS_PER_BLOCK, SSM_GROUP, SSM_STATE)
    return diag[0], -diag[1]


def _unshard_cols(g):
    s, k, n = g.shape
    return g.transpose(1, 0, 2).reshape(k, s * n)


def _shard_cols(w):
    k, n = w.shape
    return w.reshape(k, N_DEV, n // N_DEV).transpose(1, 0, 2)


def _pack(arrays):
    chunks, offs, row = [], [], 0
    for a in arrays:
        flat = a.reshape(-1).astype(F32)
        rows = -(-flat.shape[0] // LANES)
        chunks.append(jnp.pad(flat, (0, rows * LANES - flat.shape[0])))
        offs.append((row, rows))
        row += rows
    pad_rows = (-row) % SUBLANES
    if pad_rows:
        chunks.append(jnp.zeros((pad_rows * LANES,), F32))
    return jnp.concatenate(chunks).reshape(row + pad_rows, LANES), offs


def _unpack(packed, offs, shapes):
    out = []
    for (row, rows), shp in zip(offs, shapes):
        size = 1
        for s in shp:
            size *= s
        out.append(packed[row: row + rows].reshape(-1)[:size].reshape(shp))
    return out


SCAN_SUB_ROWS = 128


def _scan_tables_into(tab_ref, lam_ref, reverse):
    steps, (pr, pi) = _scan_tables(lam_ref, reverse)
    for kk, (_, mr, mi) in enumerate(steps):
        tab_ref[2 * kk] = mr
        tab_ref[2 * kk + 1] = mi
    tab_ref[6] = pr
    tab_ref[7] = pi


def _scan_in_place(buf_ref, tab_ref, carry_ref, ng, reverse, prev_ref=None, acc_ref=None, row0=0, unroll=False):
    shape = (SUBLANES, BLOCK_STATE)
    last = 0 if reverse else SUBLANES - 1
    first_row = lax.broadcasted_iota(jnp.int32, shape, 0) == 0
    re_cols, im_cols = pl.ds(0, BLOCK_STATE), pl.ds(BLOCK_STATE, BLOCK_STATE)

    def group(r, carry):
        cr, ci = carry[0], carry[1]
        rr = (ng - 1 - r) if reverse else r
        off = row0 + rr * SUBLANES
        off = off if unroll else pl.multiple_of(off, SUBLANES)
        xr, xi = buf_ref[pl.ds(off, SUBLANES), re_cols], buf_ref[pl.ds(off, SUBLANES), im_cols]
        for kk, d in enumerate((1, 2, 4)):
            sh = (SUBLANES - d) if reverse else d
            mr, mi = tab_ref[2 * kk], tab_ref[2 * kk + 1]
            yr, yi = pltpu.roll(xr, sh, 0), pltpu.roll(xi, sh, 0)
            xr, xi = xr + mr * yr - mi * yi, xi + mr * yi + mi * yr
        pr, pi = tab_ref[6], tab_ref[7]
        xr, xi = xr + pr * cr - pi * ci, xi + pr * ci + pi * cr
        buf_ref[pl.ds(off, SUBLANES), re_cols] = xr
        buf_ref[pl.ds(off, SUBLANES), im_cols] = xi
        out = (jnp.broadcast_to(xr[last:last + 1, :], shape), jnp.broadcast_to(xi[last:last + 1, :], shape))
        if prev_ref is not None:
            off8 = off + SUBLANES if unroll else pl.multiple_of(off + SUBLANES, SUBLANES)
            before_r, before_i = prev_ref[pl.ds(off, SUBLANES), re_cols], prev_ref[pl.ds(off, SUBLANES), im_cols]
            same_r, same_i = prev_ref[pl.ds(off8, SUBLANES), re_cols], prev_ref[pl.ds(off8, SUBLANES), im_cols]
            sr = jnp.where(first_row, jnp.broadcast_to(before_r[SUBLANES - 1:, :], shape), pltpu.roll(same_r, 1, 0))
            si = jnp.where(first_row, jnp.broadcast_to(before_i[SUBLANES - 1:, :], shape), pltpu.roll(same_i, 1, 0))
            out += (carry[2] + xr * sr + xi * si, carry[3] + xi * sr - xr * si)
        return out

    init = (carry_ref[0], carry_ref[1])
    if prev_ref is not None:
        init += (acc_ref[0], acc_ref[1])
    if unroll:
        res = init
        for r in range(ng):
            res = group(r, res)
    else:
        res = lax.fori_loop(0, ng, group, init)
    carry_ref[0] = res[0]
    carry_ref[1] = res[1]
    if prev_ref is not None:
        acc_ref[0] = res[2]
        acc_ref[1] = res[3]


def s5_fused_fwd(name, u, wb, wc, lam, d_row, tm=None):
    n_rows, d_model = u.shape
    nb = lam.shape[0]
    tm = _pick(n_rows, tm or TILES["scan"], 2 * SUBLANES)
    nt = n_rows // tm
    sub = _pick(tm, SCAN_SUB_ROWS, 2 * SUBLANES)

    def body(u_ref, wb_ref, wc_ref, lam_ref, d_ref, z_ref, st_ref, buf_ref, carry_ref, tab_ref):
        @pl.when(pl.program_id(1) == 0)
        def _():
            carry_ref[...] = jnp.zeros_like(carry_ref)
            _scan_tables_into(tab_ref, lam_ref, False)

        for r0 in range(0, tm, sub):
            rows = slice(r0, r0 + sub)
            uu = u_ref[rows, :]
            buf_ref[rows, :] = jnp.dot(uu.astype(BF16), wb_ref[...], preferred_element_type=F32)
            _scan_in_place(buf_ref, tab_ref, carry_ref, sub // SUBLANES, False, row0=r0, unroll=True)
            st = buf_ref[rows, :].astype(BF16)
            st_ref[rows, :] = st
            z_ref[rows, :] = _gelu(jnp.dot(st, wc_ref[...], preferred_element_type=F32) + d_ref[...] * uu).astype(z_ref.dtype)

    tile = lambda b, it: (it, b)
    blk = lambda b, it: (b, 0, 0)
    return pl.pallas_call(
        body, name=name, grid=(nb, nt),
        in_specs=[pl.BlockSpec((tm, LANES), tile), pl.BlockSpec((None, LANES, BLOCK_COLS), blk),
                  pl.BlockSpec((None, BLOCK_COLS, LANES), blk), pl.BlockSpec((None, 2, BLOCK_STATE), blk),
                  pl.BlockSpec((1, LANES), lambda b, it: (0, b))],
        out_specs=[pl.BlockSpec((tm, LANES), tile), pl.BlockSpec((tm, BLOCK_COLS), tile)],
        out_shape=[jax.ShapeDtypeStruct((n_rows, d_model), BF16), jax.ShapeDtypeStruct((n_rows, nb * BLOCK_COLS), BF16)],
        scratch_shapes=[pltpu.VMEM((tm, BLOCK_COLS), F32), pltpu.VMEM((2, SUBLANES, BLOCK_STATE), F32),
                        pltpu.VMEM((8, SUBLANES, BLOCK_STATE), F32)],
        compiler_params=_params("parallel", "arbitrary"),
    )(u, wb, wc, lam, d_row)


def s5_fused_bwd(name, dz, u, st, wb, wc, lam, d_row, tm=None):
    n_rows, d_model = u.shape
    nb = lam.shape[0]
    tm = _pick(n_rows, tm or TILES["scan"], 2 * SUBLANES)
    nt = n_rows // tm
    tail_rows = 2 * SUBLANES
    sub = _pick(tm, SCAN_SUB_ROWS, 2 * SUBLANES)

    def body(dz_ref, u_ref, st_ref, tail_ref, wb_ref, wc_ref, lam_ref, d_ref,
             du_ref, dwb_ref, dwc_ref, dlam_ref, dd_ref, buf_ref, prev_ref, carry_ref, acc_ref, tab_ref):
        it = pl.program_id(1)

        @pl.when(it == 0)
        def _():
            carry_ref[...] = jnp.zeros_like(carry_ref)
            acc_ref[...] = jnp.zeros_like(acc_ref)
            dwb_ref[...] = jnp.zeros_like(dwb_ref)
            dwc_ref[...] = jnp.zeros_like(dwc_ref)
            dd_ref[...] = jnp.zeros_like(dd_ref)
            _scan_tables_into(tab_ref, lam_ref, True)

        prev_ref[SUBLANES:, :] = st_ref[...].astype(F32)
        before = tail_ref[...].astype(F32)[SUBLANES:, :]
        prev_ref[:SUBLANES, :] = jnp.where(it == nt - 1, 0.0, before)
        for r0 in range(tm - sub, -1, -sub):
            rows = slice(r0, r0 + sub)
            uu, st_b = u_ref[rows, :], st_ref[rows, :]
            y = jnp.dot(st_b, wc_ref[...], preferred_element_type=F32) + d_ref[...] * uu
            _, vjp = jax.vjp(_gelu, y)
            dy = vjp(dz_ref[rows, :])[0]
            dyb = dy.astype(BF16)
            dd_ref[...] += jnp.sum(dy * uu, axis=0, keepdims=True)
            dwc_ref[...] += lax.dot_general(st_b, dyb, _DIMS["tn"], preferred_element_type=F32)
            buf_ref[rows, :] = lax.dot_general(dyb, wc_ref[...], _DIMS["nt"], preferred_element_type=F32)
            _scan_in_place(buf_ref, tab_ref, carry_ref, sub // SUBLANES, True, prev_ref, acc_ref, row0=r0, unroll=True)
            gb = buf_ref[rows, :].astype(BF16)
            du_ref[rows, :] = lax.dot_general(gb, wb_ref[...], _DIMS["nt"], preferred_element_type=F32) + dy * d_ref[...]
            dwb_ref[...] += lax.dot_general(uu.astype(BF16), gb, _DIMS["tn"], preferred_element_type=F32)

        @pl.when(it == nt - 1)
        def _():
            dlam_ref[0:1, :] = jnp.sum(acc_ref[0], axis=0, keepdims=True)
            dlam_ref[1:2, :] = jnp.sum(acc_ref[1], axis=0, keepdims=True)

    tile = lambda b, it: (nt - 1 - it, b)
    blk = lambda b, it: (b, 0, 0)
    per_tile = tm // tail_rows
    tail = lambda b, it: (jnp.maximum((nt - 1 - it) * per_tile - 1, 0), b)
    return pl.pallas_call(
        body, name=name, grid=(nb, nt),
        in_specs=[pl.BlockSpec((tm, LANES), tile), pl.BlockSpec((tm, LANES), tile), pl.BlockSpec((tm, BLOCK_COLS), tile),
                  pl.BlockSpec((tail_rows, BLOCK_COLS), tail), pl.BlockSpec((None, LANES, BLOCK_COLS), blk),
                  pl.BlockSpec((None, BLOCK_COLS, LANES), blk), pl.BlockSpec((None, 2, BLOCK_STATE), blk),
                  pl.BlockSpec((1, LANES), lambda b, it: (0, b))],
        out_specs=[pl.BlockSpec((tm, LANES), tile), pl.BlockSpec((None, LANES, BLOCK_COLS), blk),
                   pl.BlockSpec((None, BLOCK_COLS, LANES), blk), pl.BlockSpec((None, 2, BLOCK_STATE), blk),
                   pl.BlockSpec((1, LANES), lambda b, it: (0, b))],
        out_shape=[jax.ShapeDtypeStruct((n_rows, d_model), F32), jax.ShapeDtypeStruct((nb, LANES, BLOCK_COLS), F32),
                   jax.ShapeDtypeStruct((nb, BLOCK_COLS, LANES), F32), jax.ShapeDtypeStruct((nb, 2, BLOCK_STATE), F32),
                   jax.ShapeDtypeStruct((1, d_model), F32)],
        scratch_shapes=[pltpu.VMEM((tm, BLOCK_COLS), F32), pltpu.VMEM((tm + SUBLANES, BLOCK_COLS), F32),
                        pltpu.VMEM((2, SUBLANES, BLOCK_STATE), F32), pltpu.VMEM((2, SUBLANES, BLOCK_STATE), F32),
                        pltpu.VMEM((8, SUBLANES, BLOCK_STATE), F32)],
        compiler_params=_params("parallel", "arbitrary"),
    )(dz, u, st, st, wb, wc, lam, d_row)


def s5_fwd(tag, u, log_dt, a_re, a_im, b_re, b_im, c_re, c_im, d_row):
    d_model = u.shape[1]
    nb = d_model // LANES
    col = lambda a: a.reshape(-1, 1)
    prm = [col(jnp.repeat(log_dt, SSM_STATE)), col(a_re), col(a_im), b_re.reshape(-1, SSM_GROUP), b_im.reshape(-1, SSM_GROUP)]
    lr, li, bbr, bbi = rowcall(f"s5_prep_{tag}", _s5_discretise, prm, [], [(1, F32), (1, F32), (SSM_GROUP, F32), (SSM_GROUP, F32)])
    lam = jnp.stack([lr.reshape(nb, BLOCK_STATE), li.reshape(nb, BLOCK_STATE)], axis=1)
    wb = _b_blocks(bbr, bbi).astype(BF16)
    wc = _c_blocks(c_re, c_im).astype(BF16)
    z, st = s5_fused_fwd(f"s5_fwd_{tag}", u, wb, wc, lam, d_row)
    return z, dict(prm=prm, lam=lam, wb=wb, wc=wc, st=st)


def s5_bwd(tag, dz, u, d_row, sv):
    d_model = u.shape[1]
    nb = d_model // LANES
    n_groups = d_model // SSM_GROUP
    col = lambda a: a.reshape(-1, 1)

    dhn, d_wb, d_wc, d_lam, d_dskip = s5_fused_bwd(f"s5_bwd_{tag}", dz, u, sv["st"], sv["wb"], sv["wc"], sv["lam"], d_row)
    d_bbr, d_bbi = _b_blocks_t(d_wb)
    d_cre, d_cim = _c_blocks_t(d_wc)

    def prep_bwd(ldt, ar, ai, br, bi, dlr, dli, dbr, dbi):
        _, vjp = jax.vjp(_s5_discretise, ldt, ar, ai, br, bi)
        return vjp((dlr, dli, dbr, dbi))

    d_ldt, d_are, d_aim, d_bre, d_bim = rowcall(
        f"s5_prep_bwd_{tag}", prep_bwd, sv["prm"] + [col(d_lam[:, 0]), col(d_lam[:, 1]), d_bbr, d_bbi], [],
        [(1, F32), (1, F32), (1, F32), (SSM_GROUP, F32), (SSM_GROUP, F32)])
    d_logdt = rowcall(f"s5_dlogdt_{tag}", lambda a: jnp.sum(a, axis=1, keepdims=True), [d_ldt.reshape(n_groups, SSM_STATE)], [], [(1, F32)])[0]
    grads = dict(log_dt=d_logdt.reshape(n_groups), a_re=d_are.reshape(n_groups, SSM_STATE), a_im=d_aim.reshape(n_groups, SSM_STATE),
                 b_re=d_bre.reshape(n_groups, SSM_STATE, SSM_GROUP), b_im=d_bim.reshape(n_groups, SSM_STATE, SSM_GROUP),
                 c_re=d_cre, c_im=d_cim, d=d_dskip)
    return dhn, grads


def kernel(x, mix_norm, mlp_norm, mlp_w1, mlp_w2, ssm_log_dt, ssm_a_re, ssm_a_im, ssm_b_re, ssm_b_im, ssm_c_re, ssm_c_im, ssm_d, ssm_w_glu, kv_norm, w_kvf, b_f, attn_wq, attn_wo, final_norm, loss_target, m_mix_norm, m_mlp_norm, m_mlp_w1, m_mlp_w2, m_ssm_log_dt, m_ssm_a_re, m_ssm_a_im, m_ssm_b_re, m_ssm_b_im, m_ssm_c_re, m_ssm_c_im, m_ssm_d, m_ssm_w_glu, m_kv_norm, m_w_kvf, m_b_f, m_attn_wq, m_attn_wo, m_final_norm, v_mix_norm, v_mlp_norm, v_mlp_w1, v_mlp_w2, v_ssm_log_dt, v_ssm_a_re, v_ssm_a_im, v_ssm_b_re, v_ssm_b_im, v_ssm_c_re, v_ssm_c_im, v_ssm_d, v_ssm_w_glu, v_kv_norm, v_w_kvf, v_b_f, v_attn_wq, v_attn_wo, v_final_norm):
    n_rows, d_model = x.shape[1], x.shape[2]
    depth = mix_norm.shape[0]
    n_a = ssm_log_dt.shape[0]
    n_b = depth - n_a
    n_heads = d_model // HEAD_DIM
    n_groups = d_model // SSM_GROUP
    nb = n_groups // GROUPS_PER_BLOCK
    kvf_cols = 2 * d_model + n_heads
    kvf_pad = 2 * d_model + LANES

    g_w1, g_w2, g_glu, g_kvf, g_wq, g_wo, g_d = all_gather(
        "gather_weights",
        [mlp_w1.astype(BF16), mlp_w2.astype(BF16), ssm_w_glu.astype(BF16), w_kvf.astype(BF16),
         attn_wq.astype(BF16), attn_wo.astype(BF16), ssm_d])
    w1 = [_unshard_cols(g_w1[:, i]) for i in range(depth)]
    w2 = [g_w2[:, i].reshape(-1, d_model) for i in range(depth)]
    wglu = [_unshard_cols(g_glu[:, i]) for i in range(n_a)]
    wkvf = _unshard_cols(g_kvf)
    wkvf = jnp.pad(wkvf, ((0, 0), (0, kvf_pad - kvf_cols)))
    wq = [g_wq[:, j].reshape(-1, d_model) for j in range(n_b)]
    wo = [g_wo[:, j].reshape(-1, d_model) for j in range(n_b)]
    d_skip = [g_d[:, i].reshape(1, d_model) for i in range(n_a)]

    row = lambda a: a.reshape(1, -1)
    col = lambda a: a.reshape(-1, 1)

    h = x[0]
    saved = []
    k_ext = v_ext = None
    q_consts = _lane_consts([(F_LANE, F_LANE + 3, 1.0)])
    k_consts = _lane_consts([(LSE_LANE, LSE_LANE + 3, -1.0), (ROWSUM_LANE, ROWSUM_LANE + 1, 1.0)])
    v_consts = _lane_consts([(SUM_LANE, SUM_LANE + 1, 1.0), (DELTA_LANE, DELTA_LANE + 3, -1.0)])
    for i in range(depth):
        sv = {"h": h}
        hn = rms_fwd(f"mix_norm_{i}", h, row(mix_norm[i]), F32 if i < n_a else BF16)
        sv["hn"] = hn
        if i < n_a:
            z, s5_saved = s5_fwd(str(i), hn, ssm_log_dt[i], ssm_a_re[i], ssm_a_im[i], ssm_b_re[i], ssm_b_im[i],
                                 ssm_c_re[i], ssm_c_im[i], d_skip[i])
            zw = matmul(f"s5_glu_{i}", z, wglu[i])
            def gate(hh, zz, gg):
                out = hh + zz[:, :d_model] * _sigmoid(zz[:, d_model:])
                return out, _rms(out, gg)

            h1, h2n = rowcall(f"s5_gate_{i}", gate, [h, zw], [row(mlp_norm[i])], [(d_model, F32), (d_model, BF16)])
            sv.update(s5=s5_saved, z=z, zw=zw)
        else:
            j = i - n_a
            q = matmul(f"attn_q_{j}", hn, wq[j], scale=LOG2E * HEAD_DIM ** -0.5, out_dtype=BF16)
            o_ext, q_ext_b = fox_fwd(f"attn_fwd_{j}", pack_heads(f"pack_q_{j}", q, 0, n_heads, q_consts), k_ext, v_ext)
            o2 = unpack_heads(f"unpack_o_{j}", [o_ext])[0]
            h1 = matmul(f"attn_o_{j}", o2, wo[j], resid=h)
            sv.update(q_ext_b=q_ext_b, o2=o2)
            h2n = rms_fwd(f"mlp_norm_{i}", h1, row(mlp_norm[i]), BF16)
        ap = matmul(f"mlp_up_{i}", h2n, w1[i], out_dtype=BF16)
        h = matmul(f"mlp_down_{i}", ap, w2[i], a_fn=_sqrelu, resid=h1, bk=2048)
        sv.update(h1=h1, h2n=h2n, ap=ap)
        saved.append(sv)
        if i == n_a - 1:
            h_mid = h
            hk = rms_fwd("kv_norm", h, row(kv_norm), BF16)
            kvf = matmul("kvf_proj", hk, wkvf, bn=kvf_pad)
            fl = kvf[:, 2 * d_model:]
            bfp = jnp.pad(row(b_f), ((0, 0), (0, LANES - n_heads)))
            k_ext = pack_heads("pack_k", kvf, 0, n_heads, k_consts, cum_logf("cum_logf", fl, bfp), F_LANE)
            v_ext = pack_heads("pack_v", kvf, 1, n_heads, v_consts)

    def loss_fn(hh, tgt, g):
        y, vjp = jax.vjp(_rms, hh, g)
        err = y - tgt
        part = 0.5 * jnp.sum(jnp.mean(err * err, axis=-1, keepdims=True), axis=0, keepdims=True)
        dh, dg = vjp(err * (1.0 / d_model))
        return dh, jnp.broadcast_to(part, (1, LANES)), dg

    dh, loss_part, d_final = rowcall("loss_head", loss_fn, [h, loss_target[0]], [row(final_norm)],
                                     [(d_model, F32)], [(1, LANES), (1, d_model)])

    g_mix, g_mlpn = [None] * depth, [None] * depth
    g_w1f, g_w2f = [None] * depth, [None] * depth
    g_ssm = [None] * n_a
    g_wqf, g_wof = [None] * n_b, [None] * n_b
    dk_acc, dv_acc, df_plus = [], [], []
    g_kv = None
    for i in reversed(range(depth)):
        sv = saved[i]
        if i == n_a - 1:
            dk, col_sums = unpack_heads("unpack_dk", dk_acc, extract_lane=HEAD_DIM)
            dv = unpack_heads("unpack_dv", dv_acc)[0]
            dfl, db_f = cum_logf_bwd("cum_logf_bwd", fl, bfp, df_plus, [col_sums])
            dkvf = jnp.concatenate([dk, dv, dfl], axis=1).astype(BF16)
            d_wkvf = matmul("kvf_dw", hk, dkvf, "tn", bm=512, bn=kvf_pad)
            dh, d_kvn = matmul("kvf_dx", dkvf, wkvf, "nt", bm=512, bk=kvf_pad, rms_bwd_of=(h_mid, row(kv_norm), dh))
            g_kv = (d_wkvf[:, :kvf_cols], d_kvn, db_f[:, :n_heads])
        dap = matmul(f"mlp_down_dx_{i}", dh, w2[i], "nt", post=lambda acc, apt: acc * (2.0 * jnp.maximum(apt.astype(F32), 0.0)),
                     post_arg=sv["ap"], out_dtype=BF16)
        g_w2f[i] = matmul(f"mlp_down_dw_{i}", sv["ap"], dh, "tn", a_fn=_sqrelu)
        g_w1f[i] = matmul(f"mlp_up_dw_{i}", sv["h2n"], dap, "tn")
        dh1, g_mlpn[i] = matmul(f"mlp_up_dx_{i}", dap, w1[i], "nt", bm=512, bk=2048, rms_bwd_of=(sv["h1"], row(mlp_norm[i]), dh))
        if i < n_a:
            def glu_bwd(zz, dd):
                val, gate = zz[:, :d_model], zz[:, d_model:]
                sg = _sigmoid(gate)
                return jnp.concatenate([dd * sg, dd * val * sg * (1.0 - sg)], axis=1)

            dzw = rowcall(f"s5_gate_bwd_{i}", glu_bwd, [sv["zw"], dh1], [], [(2 * d_model, BF16)])[0]
            dz = matmul(f"s5_glu_dx_{i}", dzw, wglu[i], "nt")
            d_wglu = matmul(f"s5_glu_dw_{i}", sv["z"], dzw, "tn")

            dhn, g_ssm[i] = s5_bwd(str(i), dz, sv["hn"], d_skip[i], sv["s5"])
            g_ssm[i]["w_glu"] = d_wglu
        else:
            j = i - n_a
            do2 = matmul(f"attn_o_dx_{j}", dh1, wo[j], "nt")
            g_wof[j] = matmul(f"attn_o_dw_{j}", sv["o2"], dh1, "tn")
            do_ext = pack_heads(f"pack_do_{j}", do2, 0, n_heads, jnp.zeros((1, LANES), F32),
                                attn_delta(f"attn_delta_{j}", sv["o2"], do2), DELTA_LANE)
            dq_ext, dk_ext, dv_ext = fox_bwd(f"attn_bwd_{j}", sv["q_ext_b"], do_ext, k_ext, v_ext)
            dk_acc.append(dk_ext)
            dv_acc.append(dv_ext)
            dq2, row_sums = unpack_heads(f"unpack_dq_{j}", [dq_ext], extract_lane=ROWSUM_LANE)
            df_plus.append(row_sums)
            g_wqf[j] = matmul(f"attn_q_dw_{j}", sv["hn"], dq2, "tn", scale=HEAD_DIM ** -0.5)
            dh, g_mix[i] = matmul(f"attn_q_dx_{j}", dq2, wq[j], "nt", scale=HEAD_DIM ** -0.5, bm=512,
                                  rms_bwd_of=(sv["h"], row(mix_norm[i]), dh1))
        if i < n_a:
            dh, g_mix[i] = rms_bwd(f"mix_norm_bwd_{i}", sv["h"], row(mix_norm[i]), dhn, add=dh1)
    grad_x = dh[None]

    stack = lambda xs: jnp.stack(xs, axis=1).astype(BF16 if xs[0].ndim == 3 else F32)
    contributions = [
        stack([_shard_cols(g) for g in g_w1f]),
        stack([g.reshape(N_DEV, -1, d_model) for g in g_w2f]),
        stack([_shard_cols(g["w_glu"]) for g in g_ssm]),
        _shard_cols(g_kv[0]).astype(BF16),
        stack([g.reshape(N_DEV, -1, d_model) for g in g_wqf]),
        stack([g.reshape(N_DEV, -1, d_model) for g in g_wof]),
        stack([g["d"].reshape(N_DEV, -1) for g in g_ssm]),
    ]
    from_sibling = pair_swap("pair_swap_grads", contributions)
    core = lax.axis_index("c")
    pair_sums = []
    for k, (mine, theirs) in enumerate(zip(contributions, from_sibling)):
        mine = lax.dynamic_index_in_dim(mine.reshape((N_CHIPS, 2) + mine.shape[1:]), core, axis=1, keepdims=False)
        cols = mine.shape[-1]
        pair_sums.append(rowcall(f"pair_add_{k}", lambda a, b: a.astype(F32) + b.astype(F32),
                                 [mine.reshape(-1, cols), theirs.reshape(-1, cols)], [], [(cols, mine.dtype)])[0].reshape(mine.shape))
    parts = chip_exchange("chip_exchange_grads", pair_sums)
    ssm_g = lambda kk: jnp.stack([g[kk] for g in g_ssm])
    loss_slot = jnp.zeros((LANES,), F32)
    small = {
        "f32": (["mix_norm", "mlp_norm", "ssm_log_dt", "ssm_a_re", "ssm_a_im", "kv_norm", "b_f", "final_norm"],
                [jnp.concatenate(g_mix, axis=0), jnp.concatenate(g_mlpn, axis=0), ssm_g("log_dt"), ssm_g("a_re"), ssm_g("a_im"),
                 g_kv[1], g_kv[2], d_final, loss_part],
                [mix_norm, mlp_norm, ssm_log_dt, ssm_a_re, ssm_a_im, kv_norm, b_f, final_norm, loss_slot],
                [m_mix_norm, m_mlp_norm, m_ssm_log_dt, m_ssm_a_re, m_ssm_a_im, m_kv_norm, m_b_f, m_final_norm, loss_slot],
                [v_mix_norm, v_mlp_norm, v_ssm_log_dt, v_ssm_a_re, v_ssm_a_im, v_kv_norm, v_b_f, v_final_norm, loss_slot]),
        "bf16": (["ssm_b_re", "ssm_b_im", "ssm_c_re", "ssm_c_im"],
                 [ssm_g("b_re"), ssm_g("b_im"), ssm_g("c_re"), ssm_g("c_im")],
                 [ssm_b_re, ssm_b_im, ssm_c_re, ssm_c_im], [m_ssm_b_re, m_ssm_b_im, m_ssm_c_re, m_ssm_c_im],
                 [v_ssm_b_re, v_ssm_b_im, v_ssm_c_re, v_ssm_c_im]),
    }
    packed = {kk: [_pack(arrs) for arrs in grp[1:]] for kk, grp in small.items()}
    small_parts = all_gather("gather_small_grads", [packed["f32"][0][0], packed["bf16"][0][0].astype(BF16)])

    res = {}

    def update(nm, part, w, m, v):
        shp = w.shape
        as2d = lambda a: a.reshape(-1, shp[-1])
        outs = adamw(f"adamw_{nm}", part.reshape(part.shape[:1] + as2d(w).shape), as2d(w), as2d(m), as2d(v))
        res[nm] = [o.reshape(shp) for o in outs]

    update("mlp_w1", parts[0], mlp_w1, m_mlp_w1, v_mlp_w1)
    update("mlp_w2", parts[1], mlp_w2, m_mlp_w2, v_mlp_w2)
    update("ssm_w_glu", parts[2], ssm_w_glu, m_ssm_w_glu, v_ssm_w_glu)
    update("w_kvf", parts[3], w_kvf, m_w_kvf, v_w_kvf)
    update("attn_wq", parts[4], attn_wq, m_attn_wq, v_attn_wq)
    update("attn_wo", parts[5], attn_wo, m_attn_wo, v_attn_wo)
    update("ssm_d", parts[6], ssm_d, m_ssm_d, v_ssm_d)
    loss = None
    for (kk, (names, _, ws, _, _)), part in zip(small.items(), small_parts):
        (_, offs), (pw, _), (pm, _), (pv, _) = packed[kk]
        small_out = adamw(f"adamw_small_{kk}", part, pw, pm, pv)
        unpacked = [_unpack(o, offs, [w.shape for w in ws]) for o in small_out]
        for idx, nm in enumerate(names):
            res[nm] = [u[idx] for u in unpacked]
        if kk == "f32":
            loss = unpacked[0][-1][0]

    order = ["mix_norm", "mlp_norm", "mlp_w1", "mlp_w2", "ssm_log_dt", "ssm_a_re", "ssm_a_im", "ssm_b_re", "ssm_b_im", "ssm_c_re",
             "ssm_c_im", "ssm_d", "ssm_w_glu", "kv_norm", "w_kvf", "b_f", "attn_wq", "attn_wo", "final_norm"]
    out = [loss, grad_x]
    for kind in range(4):
        out += [res[nm][kind] for nm in order]
    return tuple(out)
```

```python
import functools

import jax
import jax.numpy as jnp
from jax import lax
from jax.experimental import pallas as pl
from jax.experimental.pallas import tpu as pltpu

F32 = jnp.float32
BF16 = jnp.bfloat16
HIGHEST = lax.Precision.HIGHEST

V7X_VMEM_BYTES = 64 << 20
VMEM_LIMIT_BYTES = (V7X_VMEM_BYTES * 3) // 4
LANES = 128
SUBLANES = 8

N_DEV = 8
RMS_EPS = 1e-6
SSM_GROUP = 16
SSM_STATE = 64
HEAD_DIM = 64
GROUPS_PER_BLOCK = LANES // SSM_GROUP
BLOCK_STATE = GROUPS_PER_BLOCK * SSM_STATE
BLOCK_COLS = 2 * BLOCK_STATE
NEG_BIG = -1e30
LOG2E = 1.4426950408889634

ADAM_LR = 0.001
ADAM_B1 = 0.9
ADAM_B2 = 0.999
ADAM_EPS = 1e-08
ADAM_WD = 0.01
ADAM_STEP = 10

TILES = {"row": 512, "mm": (1024, 1024, 1024), "blk": 512, "scan": 512, "cum": 256, "attn": 1024, "adam": 256}


def _pick(dim, pref, align=LANES):
    if dim <= pref:
        return dim
    for a in (align, SUBLANES):
        d = (pref // a) * a
        while d >= a:
            if dim % d == 0:
                return d
            d -= a
    return dim


def _params(*sem):
    return pltpu.CompilerParams(dimension_semantics=sem, vmem_limit_bytes=VMEM_LIMIT_BYTES)


def rowcall(name, fn, rows, consts, out_rows, out_accs=(), tm=None):
    n_rows = rows[0].shape[0]
    tm = _pick(n_rows, tm or TILES["row"], SUBLANES)
    nr, nc, no, na = len(rows), len(consts), len(out_rows), len(out_accs)

    def body(*refs):
        ins = [r[...] for r in refs[: nr + nc]]
        outs = fn(*ins)
        if not isinstance(outs, (tuple, list)):
            outs = (outs,)
        for r, o in zip(refs[nr + nc: nr + nc + no], outs[:no]):
            r[...] = o.astype(r.dtype)
        if na:
            i = pl.program_id(0)
            for r, o in zip(refs[nr + nc + no:], outs[no:]):
                @pl.when(i == 0)
                def _(r=r, o=o):
                    r[...] = o

                @pl.when(i > 0)
                def _(r=r, o=o):
                    r[...] += o

    in_specs = [pl.BlockSpec((tm, a.shape[1]), lambda i: (i, 0)) for a in rows]
    in_specs += [pl.BlockSpec(c.shape, lambda i, n=c.ndim: (0,) * n) for c in consts]
    out_shape = [jax.ShapeDtypeStruct((n_rows, c), dt) for c, dt in out_rows]
    out_specs = [pl.BlockSpec((tm, c), lambda i: (i, 0)) for c, _ in out_rows]
    out_shape += [jax.ShapeDtypeStruct(s, F32) for s in out_accs]
    out_specs += [pl.BlockSpec(s, lambda i, n=len(s): (0,) * n) for s in out_accs]
    res = pl.pallas_call(
        body, name=name, grid=(n_rows // tm,), in_specs=in_specs, out_specs=out_specs, out_shape=out_shape,
        compiler_params=_params("arbitrary" if na else "parallel"),
    )(*rows, *consts)
    return res


_DIMS = {"nn": (((1,), (0,)), ((), ())), "nt": (((1,), (1,)), ((), ())), "tn": (((0,), (0,)), ((), ()))}


def matmul(name, a, b, mode="nn", *, a_fn=None, scale=None, resid=None, post=None, post_arg=None, rms_bwd_of=None,
           out_dtype=F32, bm=None, bn=None, bk=None):
    if mode == "nn":
        (m, k), (k2, n) = a.shape, b.shape
    elif mode == "nt":
        (m, k), (n, k2) = a.shape, b.shape
    else:
        (k, m), (k2, n) = a.shape, b.shape
    assert k == k2, (name, a.shape, b.shape, mode)
    if rms_bwd_of is not None:
        bn = n
    bm, bn, bk = _pick(m, bm or TILES["mm"][0]), _pick(n, bn or TILES["mm"][1]), _pick(k, bk or TILES["mm"][2])
    nk = k // bk
    a_spec = pl.BlockSpec((bk, bm), lambda i, j, kk: (kk, i)) if mode == "tn" else pl.BlockSpec((bm, bk), lambda i, j, kk: (i, kk))
    b_spec = pl.BlockSpec((bn, bk), lambda i, j, kk: (j, kk)) if mode == "nt" else pl.BlockSpec((bk, bn), lambda i, j, kk: (kk, j))
    mn_spec = pl.BlockSpec((bm, bn), lambda i, j, kk: (i, j))
    extra = [x for x in (resid, post_arg) if x is not None]
    has_resid, has_post, has_rms = resid is not None, post is not None, rms_bwd_of is not None
    row_spec = pl.BlockSpec((1, bn), lambda i, j, kk: (0, j))
    extra_specs = [mn_spec] * len(extra)
    if has_rms:
        x_in, gain, add = rms_bwd_of
        extra += [x_in, add, gain]
        extra_specs += [mn_spec, mn_spec, row_spec]

    def body(*refs):
        a_ref, b_ref = refs[0], refs[1]
        ex = refs[2: 2 + len(extra)]
        o_ref = refs[2 + len(extra)]
        av = a_ref[...]
        if a_fn is not None:
            av = a_fn(av.astype(F32))
        p = lax.dot_general(av.astype(BF16), b_ref[...].astype(BF16), _DIMS[mode], preferred_element_type=F32)

        def finish(acc):
            if scale is not None:
                acc = acc * scale
            idx = 0
            if has_resid:
                acc = acc + ex[idx][...]
                idx += 1
            if has_post:
                acc = post(acc, ex[idx][...])
                idx += 1
            if has_rms:
                _, vjp = jax.vjp(_rms, ex[idx][...], ex[idx + 2][...])
                dx, d_gain = vjp(acc)
                acc = dx + ex[idx + 1][...]
                dg_ref = refs[3 + len(extra)]
                first = pl.program_id(0) == 0

                @pl.when(first)
                def _():
                    dg_ref[...] = d_gain

                @pl.when(jnp.logical_not(first))
                def _():
                    dg_ref[...] += d_gain
            o_ref[...] = acc.astype(o_ref.dtype)

        if nk == 1:
            finish(p)
        else:
            acc_ref = refs[-1]
            kk = pl.program_id(2)

            @pl.when(kk == 0)
            def _():
                acc_ref[...] = p

            @pl.when(kk > 0)
            def _():
                acc_ref[...] += p

            @pl.when(kk == nk - 1)
            def _():
                finish(acc_ref[...])

    out_shape, out_specs = jax.ShapeDtypeStruct((m, n), out_dtype), mn_spec
    if has_rms:
        out_shape, out_specs = [out_shape, jax.ShapeDtypeStruct((1, n), F32)], [mn_spec, row_spec]
    return pl.pallas_call(
        body, name=name, grid=(m // bm, n // bn, nk),
        in_specs=[a_spec, b_spec] + extra_specs, out_specs=out_specs, out_shape=out_shape,
        scratch_shapes=[pltpu.VMEM((bm, bn), F32)] if nk > 1 else [],
        compiler_params=_params(*(("arbitrary",) * 3 if has_rms else ("parallel", "parallel", "arbitrary"))),
    )(a, b, *extra)


def _cmul(ar, ai, br, bi):
    return ar * br - ai * bi, ar * bi + ai * br


def _scan_tables(lam_ref, reverse):
    shape = (SUBLANES, BLOCK_STATE)
    lr = jnp.broadcast_to(lam_ref[0:1, :], shape)
    li = jnp.broadcast_to(lam_ref[1:2, :], shape)
    if reverse:
        li = -li
    row = lax.broadcasted_iota(jnp.int32, shape, 0)
    tt = (SUBLANES - 1 - row) if reverse else row
    l1 = (lr, li)
    l2 = _cmul(*l1, *l1)
    l4 = _cmul(*l2, *l2)
    pr, pi = l1
    for bit, lp in enumerate((l1, l2, l4)):
        qr, qi = _cmul(pr, pi, *lp)
        on = ((tt >> bit) & 1) == 1
        pr, pi = jnp.where(on, qr, pr), jnp.where(on, qi, pi)
    steps = []
    for d, lp in ((1, l1), (2, l2), (4, l4)):
        ok = tt >= d
        steps.append((d, jnp.where(ok, lp[0], 0.0), jnp.where(ok, lp[1], 0.0)))
    return steps, (pr, pi)


def _rms(x, g):
    return x * lax.rsqrt(jnp.mean(x * x, axis=-1, keepdims=True) + RMS_EPS) * g


def _sigmoid(x):
    return 1.0 / (1.0 + jnp.exp(-x))


def _gelu(x):
    return 0.5 * x * (1.0 + jnp.tanh(0.7978845608028654 * (x + 0.044715 * (x * x * x))))


def _log_sigmoid(x):
    return jnp.minimum(x, 0.0) - jnp.log(1.0 + jnp.exp(-jnp.abs(x)))


def _sqrelu(x):
    r = jnp.maximum(x, 0.0)
    return r * r


def _s5_discretise(ldt, ar, ai, br, bi):
    dt = jnp.exp(ldt)
    er = jnp.exp(ar * dt)
    lr, li = er * jnp.cos(ai * dt), er * jnp.sin(ai * dt)
    nr, ni = lr - 1.0, li
    den = ar * ar + ai * ai
    cr, ci = (nr * ar + ni * ai) / den, (ni * ar - nr * ai) / den
    return lr, li, cr * br - ci * bi, cr * bi + ci * br


def rms_fwd(name, x, g, dtype=F32):
    return rowcall(name, _rms, [x], [g], [(x.shape[1], dtype)])[0]


def rms_bwd(name, x, g, dy, add=None):
    def fn(x, dy, *rest):
        g = rest[-1]
        _, vjp = jax.vjp(_rms, x, g)
        dx, dg = vjp(dy)
        if add is not None:
            dx = dx + rest[0]
        return dx, dg

    rows = [x, dy] + ([add] if add is not None else [])
    return rowcall(name, fn, rows, [g], [(x.shape[1], F32)], [g.shape])


def _split3(x):
    hi = x.astype(BF16).astype(F32)
    r = x - hi
    mid = r.astype(BF16).astype(F32)
    return hi, mid, (r - mid).astype(BF16).astype(F32)


def cum_logf(name, fl, bf, tm=None):
    n_rows, w = fl.shape
    tm = _pick(n_rows, tm or TILES["cum"], SUBLANES)

    def body(fl_ref, bf_ref, hi_ref, mid_ref, lo_ref, carry_ref):
        it = pl.program_id(0)

        @pl.when(it == 0)
        def _():
            carry_ref[...] = jnp.zeros_like(carry_ref)

        ls = _log_sigmoid(fl_ref[...] + bf_ref[...])
        tri = (lax.broadcasted_iota(jnp.int32, (tm, tm), 0) >= lax.broadcasted_iota(jnp.int32, (tm, tm), 1)).astype(F32)
        c = jnp.dot(tri, ls, precision=HIGHEST, preferred_element_type=F32) + carry_ref[0:1, :]
        carry_ref[...] = jnp.broadcast_to(c[tm - 1:tm, :], carry_ref.shape)
        hi_ref[...], mid_ref[...], lo_ref[...] = _split3(c * (-LOG2E))

    spec = pl.BlockSpec((tm, w), lambda i: (i, 0))
    return pl.pallas_call(
        body, name=name, grid=(n_rows // tm,),
        in_specs=[spec, pl.BlockSpec((1, w), lambda i: (0, 0))],
        out_specs=[spec] * 3, out_shape=[jax.ShapeDtypeStruct((n_rows, w), F32)] * 3,
        scratch_shapes=[pltpu.VMEM((SUBLANES, w), F32)],
        compiler_params=_params("arbitrary"),
    )(fl, bf)


def cum_logf_bwd(name, fl, bf, plus, minus, tm=None):
    n_rows, w = fl.shape
    tm = _pick(n_rows, tm or TILES["cum"], SUBLANES)
    nt = n_rows // tm
    n_p, n_m = len(plus), len(minus)

    def body(*refs):
        fl_ref, bf_ref = refs[0], refs[1]
        d_refs = refs[2: 2 + n_p + n_m]
        o_ref, db_ref, carry_ref = refs[2 + n_p + n_m:]
        it = pl.program_id(0)

        @pl.when(it == 0)
        def _():
            carry_ref[...] = jnp.zeros_like(carry_ref)

        d = None
        for r in d_refs[:n_p]:
            d = r[...] if d is None else d + r[...]
        for r in d_refs[n_p:]:
            d = -r[...] if d is None else d - r[...]
        tri = (lax.broadcasted_iota(jnp.int32, (tm, tm), 0) <= lax.broadcasted_iota(jnp.int32, (tm, tm), 1)).astype(F32)
        c = jnp.dot(tri, d, precision=HIGHEST, preferred_element_type=F32) + carry_ref[0:1, :]
        carry_ref[...] = jnp.broadcast_to(c[0:1, :], carry_ref.shape)
        dfl = c * _sigmoid(-(fl_ref[...] + bf_ref[...]))
        o_ref[...] = dfl
        part = jnp.sum(dfl, axis=0, keepdims=True)

        @pl.when(it == 0)
        def _():
            db_ref[...] = part

        @pl.when(it > 0)
        def _():
            db_ref[...] += part

    rev = lambda i: (nt - 1 - i, 0)
    return pl.pallas_call(
        body, name=name, grid=(nt,),
        in_specs=[pl.BlockSpec((tm, w), rev), pl.BlockSpec((1, w), lambda i: (0, 0))] + [pl.BlockSpec((tm, w), rev)] * (n_p + n_m),
        out_specs=[pl.BlockSpec((tm, w), rev), pl.BlockSpec((1, w), lambda i: (0, 0))],
        out_shape=[jax.ShapeDtypeStruct((n_rows, w), F32), jax.ShapeDtypeStruct((1, w), F32)],
        scratch_shapes=[pltpu.VMEM((SUBLANES, w), F32)],
        compiler_params=_params("arbitrary"),
    )(fl, bf, *plus, *minus)


ROWSUM_LANE = HEAD_DIM + 6
F_LANE = HEAD_DIM
LSE_LANE = HEAD_DIM + 3
SUM_LANE = HEAD_DIM
DELTA_LANE = HEAD_DIM + 1


def _lane_consts(pairs):
    lane = lax.broadcasted_iota(jnp.int32, (1, LANES), 1)
    out = jnp.zeros((1, LANES), F32)
    for lo, hi, v in pairs:
        out = jnp.where((lane >= lo) & (lane < hi), v, out)
    return out


def pack_heads(name, x, col_block, n_heads, consts, parts=(), parts_lane=0, tm=None):
    n_rows = x.shape[0]
    d = n_heads * HEAD_DIM
    assert n_heads % 2 == 0
    tm = _pick(n_rows, tm or TILES["row"], 2 * SUBLANES)
    n_parts = len(parts)

    def body(*refs):
        x_ref, c_ref = refs[0], refs[1]
        p_vals = [r[...] for r in refs[2: 2 + n_parts]]
        o_ref = refs[2 + n_parts]
        lane = lax.broadcasted_iota(jnp.int32, (tm, LANES), 1)
        tail0 = jnp.broadcast_to(c_ref[...], (tm, LANES))
        for h in range(n_heads):
            pair = x_ref[:, (h // 2) * LANES: (h // 2 + 1) * LANES].astype(F32)
            base = pair if h % 2 == 0 else pltpu.roll(pair, HEAD_DIM, 1)
            tail = tail0
            for kk, p in enumerate(p_vals):
                col = jnp.sum(jnp.where(lane == h, p, 0.0), axis=1, keepdims=True)
                tail = jnp.where(lane == parts_lane + kk, col, tail)
            o_ref[h] = jnp.where(lane < HEAD_DIM, base, tail).astype(BF16)

    return pl.pallas_call(
        body, name=name, grid=(n_rows // tm,),
        in_specs=[pl.BlockSpec((tm, d), lambda i: (i, col_block)), pl.BlockSpec((1, LANES), lambda i: (0, 0))]
        + [pl.BlockSpec((tm, LANES), lambda i: (i, 0))] * n_parts,
        out_specs=pl.BlockSpec((n_heads, tm, LANES), lambda i: (0, i, 0)),
        out_shape=jax.ShapeDtypeStruct((n_heads, n_rows, LANES), BF16),
        compiler_params=_params("parallel"),
    )(x, consts, *parts)


def unpack_heads(name, xs, extract_lane=None, tm=None):
    n_heads, n_rows, _ = xs[0].shape
    assert n_heads % 2 == 0
    tm = _pick(n_rows, tm or TILES["row"], SUBLANES)
    n = len(xs)

    def body(*refs):
        o_ref = refs[n]
        lane = lax.broadcasted_iota(jnp.int32, (tm, LANES), 1)
        picked = jnp.zeros((tm, LANES), F32)

        def head(h):
            v = refs[0][h]
            for r in refs[1:n]:
                v = v + r[h]
            return v

        for p in range(n_heads // 2):
            a, b = head(2 * p), head(2 * p + 1)
            o_ref[:, p * LANES: (p + 1) * LANES] = jnp.where(lane < HEAD_DIM, a, pltpu.roll(b, HEAD_DIM, 1))
            if extract_lane is not None:
                for hh, v in ((2 * p, a), (2 * p + 1, b)):
                    col = jnp.sum(jnp.where(lane == extract_lane, v, 0.0), axis=1, keepdims=True)
                    picked = jnp.where(lane == hh, col, picked)
        if extract_lane is not None:
            refs[n + 1][...] = picked

    d = n_heads * HEAD_DIM
    out_shape = [jax.ShapeDtypeStruct((n_rows, d), F32)]
    out_specs = [pl.BlockSpec((tm, d), lambda i: (i, 0))]
    if extract_lane is not None:
        out_shape.append(jax.ShapeDtypeStruct((n_rows, LANES), F32))
        out_specs.append(pl.BlockSpec((tm, LANES), lambda i: (i, 0)))
    return pl.pallas_call(
        body, name=name, grid=(n_rows // tm,),
        in_specs=[pl.BlockSpec((n_heads, tm, LANES), lambda i: (0, i, 0))] * n,
        out_specs=out_specs, out_shape=out_shape, compiler_params=_params("parallel"),
    )(*xs)


def fox_fwd(name, q_ext, k_ext, v_ext, t=None):
    nh, n_rows, w = q_ext.shape
    t = _pick(n_rows, t or TILES["attn"])
    nt = n_rows // t
    half = t // 2

    def body(q_ref, k_ref, v_ref, o_ref, qb_ref, m_ref, acc_ref):
        i = pl.program_id(1)
        m_ref[...] = jnp.full_like(m_ref, NEG_BIG)
        acc_ref[...] = jnp.zeros_like(acc_ref)
        q = q_ref[...]

        def piece(rows, kv_off, size, diagonal):
            kv = pl.ds(pl.multiple_of(kv_off, size), size)
            s = lax.dot_general(q_ref[rows, :], k_ref[kv, :], _DIMS["nt"], preferred_element_type=F32)
            if diagonal:
                keep = lax.broadcasted_iota(jnp.int32, s.shape, 0) >= lax.broadcasted_iota(jnp.int32, s.shape, 1)
                s = jnp.where(keep, s, NEG_BIG)
            m_prev = m_ref[rows, :]
            m_new = jnp.maximum(m_prev, jnp.max(s, axis=1, keepdims=True))
            p = jnp.exp2(s - jnp.tile(m_new, (1, size // LANES)))
            acc_ref[rows, :] = jnp.exp2(m_prev - m_new) * acc_ref[rows, :] + jnp.dot(
                p.astype(BF16), v_ref[kv, :], preferred_element_type=F32)
            m_ref[rows, :] = m_new

        def off_diagonal(j, carry):
            piece(slice(0, t), j * t, t, False)
            return carry

        lax.fori_loop(0, i, off_diagonal, 0)
        if half % LANES == 0:
            piece(slice(0, half), i * t, half, True)
            piece(slice(half, t), i * t, half, False)
            piece(slice(half, t), i * t + half, half, True)
        else:
            piece(slice(0, t), i * t, t, True)
        acc = acc_ref[...]
        row_sum = acc[:, HEAD_DIM:HEAD_DIM + 1]
        hi, mid, lo = _split3(m_ref[:, 0:1] + jnp.log2(row_sum))
        lane = lax.broadcasted_iota(jnp.int32, (t, w), 1)
        o_ref[...] = acc / row_sum
        qb = jnp.where(lane == LSE_LANE, hi, jnp.where(lane == LSE_LANE + 1, mid, jnp.where(lane == LSE_LANE + 2, lo, q.astype(F32))))
        qb_ref[...] = qb.astype(BF16)

    whole = pl.BlockSpec((None, n_rows, w), lambda h, i: (h, 0, 0))
    tile = pl.BlockSpec((None, t, w), lambda h, i: (h, i, 0))
    return pl.pallas_call(
        body, name=name, grid=(nh, nt), in_specs=[tile, whole, whole], out_specs=[tile, tile],
        out_shape=[jax.ShapeDtypeStruct((nh, n_rows, w), F32), jax.ShapeDtypeStruct((nh, n_rows, w), BF16)],
        scratch_shapes=[pltpu.VMEM((t, w), F32), pltpu.VMEM((t, w), F32)],
        compiler_params=_params("parallel", "arbitrary"),
    )(q_ext, k_ext, v_ext)


def fox_bwd(name, q_ext, do_ext, k_ext, v_ext, t=None):
    nh, n_rows, w = q_ext.shape
    t = _pick(n_rows, t or TILES["attn"])
    nt = n_rows // t
    half = t // 2

    def body(q_ref, do_ref, k_ref, v_ref, dq_ref, dk_ref, dv_ref):
        j = pl.program_id(1)

        @pl.when(j == 0)
        def _():
            dq_ref[...] = jnp.zeros_like(dq_ref)

        dk_ref[...] = jnp.zeros_like(dk_ref)
        dv_ref[...] = jnp.zeros_like(dv_ref)
        def piece(keys, q_off, size, diagonal):
            qs = pl.ds(pl.multiple_of(q_off, size), size)
            kj, vj, qi, doi = k_ref[keys, :], v_ref[keys, :], q_ref[qs, :], do_ref[qs, :]
            pt = jnp.exp2(lax.dot_general(kj, qi, _DIMS["nt"], preferred_element_type=F32))
            if diagonal:
                keep = lax.broadcasted_iota(jnp.int32, pt.shape, 0) <= lax.broadcasted_iota(jnp.int32, pt.shape, 1)
                pt = jnp.where(keep, pt, 0.0)
            dst = (pt * lax.dot_general(vj, doi, _DIMS["nt"], preferred_element_type=F32)).astype(BF16)
            dv_ref[keys, :] += jnp.dot(pt.astype(BF16), doi, preferred_element_type=F32)
            dk_ref[keys, :] += jnp.dot(dst, qi, preferred_element_type=F32)
            dq_ref[qs, :] += lax.dot_general(dst, kj, _DIMS["tn"], preferred_element_type=F32)

        def off_diagonal(i, carry):
            piece(slice(0, t), i * t, t, False)
            return carry

        if half % LANES == 0:
            piece(slice(0, half), j * t, half, True)
            piece(slice(0, half), j * t + half, half, False)
            piece(slice(half, t), j * t + half, half, True)
        else:
            piece(slice(0, t), j * t, t, True)
        lax.fori_loop(j + 1, nt, off_diagonal, 0)
        lane = lax.broadcasted_iota(jnp.int32, (t, w), 1)
        dk_ref[...] = dk_ref[...] * jnp.where(lane < HEAD_DIM, 1.0 / LOG2E, 1.0)

    whole = pl.BlockSpec((None, n_rows, w), lambda h, j: (h, 0, 0))
    tile = pl.BlockSpec((None, t, w), lambda h, j: (h, j, 0))
    shape = jax.ShapeDtypeStruct((nh, n_rows, w), F32)
    return pl.pallas_call(
        body, name=name, grid=(nh, nt), in_specs=[whole, whole, tile, tile], out_specs=[whole, tile, tile],
        out_shape=[shape, shape, shape],
        compiler_params=_params("parallel", "arbitrary"),
    )(q_ext, do_ext, k_ext, v_ext)


def attn_delta(name, o, do):
    d_model = o.shape[1]
    head_of_col = lax.broadcasted_iota(jnp.int32, (d_model, LANES), 0) // HEAD_DIM
    sel = (head_of_col == lax.broadcasted_iota(jnp.int32, (d_model, LANES), 1)).astype(F32)

    def fn(a, b, s):
        return _split3(jnp.dot(a * b, s, precision=HIGHEST, preferred_element_type=F32))

    return rowcall(name, fn, [o, do], [sel], [(LANES, F32)] * 3)


def all_gather(name, tensors):
    n = len(tensors)

    def body(*refs):
        start, finish = _gather_phases(refs[:n], refs[n: 2 * n], *refs[2 * n:])
        start()
        finish()

    any_spec = pl.BlockSpec(memory_space=pl.ANY)
    return pl.pallas_call(
        body, name=name, in_specs=[any_spec] * n, out_specs=[any_spec] * n,
        out_shape=_gather_shapes(tensors), scratch_shapes=_gather_semaphores(n),
        compiler_params=pltpu.CompilerParams(has_side_effects=True),
    )(*tensors)


def _gather_shapes(tensors):
    return [jax.ShapeDtypeStruct((N_DEV,) + t.shape, t.dtype) for t in tensors]


def _gather_semaphores(n):
    return [pltpu.SemaphoreType.DMA((n, 7)), pltpu.SemaphoreType.DMA((n, 7)), pltpu.SemaphoreType.DMA((n,))]


def _gather_phases(ins, outs, send_sems, recv_sems, local_sems):
    n = len(ins)
    x, y, c = lax.axis_index("x"), lax.axis_index("y"), lax.axis_index("c")
    sibling = (x, y, 1 - c)
    chips = [(1 - x, y), (x, 1 - y), (1 - x, 1 - y)]
    slot = lambda px, py, pc: 4 * px + 2 * py + pc

    def copy(t, k, block, to, src=None):
        dst = outs[t].at[slot(*block)]
        return pltpu.make_async_remote_copy(
            src_ref=dst if src is None else src, dst_ref=dst, send_sem=send_sems.at[t, k], recv_sem=recv_sems.at[t, k],
            device_id=to, device_id_type=pl.DeviceIdType.MESH)

    own = lambda t: pltpu.make_async_copy(ins[t], outs[t].at[slot(x, y, c)], local_sems.at[t])
    first = lambda t: [copy(t, 0, (x, y, c), sibling, ins[t])] + [copy(t, 1 + j, (x, y, c), (*chip, c), ins[t]) for j, chip in enumerate(chips)]

    def start():
        for t in range(n):
            own(t).start()
            for cp in first(t):
                cp.start()

    def finish():
        sends = []
        for t in range(n):
            sends += first(t)
            for j, chip in enumerate(chips):
                copy(t, 1 + j, (*chip, c), (x, y, c)).wait_recv()
                passed = copy(t, 4 + j, (*chip, c), sibling)
                passed.start()
                sends.append(passed)
        for t in range(n):
            copy(t, 0, sibling, (x, y, c)).wait_recv()
            for j, chip in enumerate(chips):
                copy(t, 4 + j, (*chip, 1 - c), (x, y, c)).wait_recv()
        for cp in sends:
            cp.wait_send()
        for t in range(n):
            own(t).wait()

    return start, finish


def _remote_call(name, body, tensors, out_shapes, n_copies):
    n = len(tensors)
    any_spec = pl.BlockSpec(memory_space=pl.ANY)
    return pl.pallas_call(
        body, name=name, in_specs=[any_spec] * n, out_specs=[any_spec] * n, out_shape=out_shapes,
        scratch_shapes=[pltpu.SemaphoreType.DMA((n, n_copies)), pltpu.SemaphoreType.DMA((n, n_copies)), pltpu.SemaphoreType.DMA((n,))],
        compiler_params=pltpu.CompilerParams(has_side_effects=True),
    )(*tensors)


N_CHIPS = 4
_CHIPS = [(0, 0), (0, 1), (1, 0), (1, 1)]


def pair_swap(name, tensors):
    n = len(tensors)

    def body(*refs):
        ins, outs = refs[:n], refs[n: 2 * n]
        send_sems, recv_sems, _ = refs[2 * n:]
        x, y, c = lax.axis_index("x"), lax.axis_index("y"), lax.axis_index("c")
        copies = []
        for t in range(n):
            for k in range(N_CHIPS):
                cp = pltpu.make_async_remote_copy(
                    src_ref=ins[t].at[2 * k + (1 - c)], dst_ref=outs[t].at[k], send_sem=send_sems.at[t, k], recv_sem=recv_sems.at[t, k],
                    device_id=(x, y, 1 - c), device_id_type=pl.DeviceIdType.MESH)
                cp.start()
                copies.append(cp)
        for cp in copies:
            cp.wait()

    return _remote_call(name, body, tensors, [jax.ShapeDtypeStruct((N_CHIPS,) + t.shape[1:], t.dtype) for t in tensors], N_CHIPS)


def chip_exchange(name, tensors):
    n = len(tensors)
    flips = [(1, 0), (0, 1), (1, 1)]

    def body(*refs):
        ins, outs = refs[:n], refs[n: 2 * n]
        send_sems, recv_sems, local_sems = refs[2 * n:]
        x, y, c = lax.axis_index("x"), lax.axis_index("y"), lax.axis_index("c")
        me = 2 * x + y
        copies = []
        for t in range(n):
            own = pltpu.make_async_copy(ins[t].at[me], outs[t].at[me], local_sems.at[t])
            own.start()
            copies.append(own)
            for j, (bx, by) in enumerate(flips):
                px, py = (1 - x if bx else x), (1 - y if by else y)
                peer = 2 * px + py
                pltpu.make_async_remote_copy(
                    src_ref=ins[t].at[peer], dst_ref=outs[t].at[me], send_sem=send_sems.at[t, j], recv_sem=recv_sems.at[t, j],
                    device_id=(px, py, c), device_id_type=pl.DeviceIdType.MESH).start()
                copies.append(pltpu.make_async_remote_copy(
                    src_ref=ins[t].at[peer], dst_ref=outs[t].at[peer], send_sem=send_sems.at[t, j], recv_sem=recv_sems.at[t, j],
                    device_id=(px, py, c), device_id_type=pl.DeviceIdType.MESH))
        for cp in copies:
            cp.wait()

    return _remote_call(name, body, tensors, [jax.ShapeDtypeStruct(t.shape, t.dtype) for t in tensors], len(flips))


def adamw(name, parts, w, m, v, tr=None):
    n_parts = parts.shape[0]
    n_rows, n_cols = w.shape
    tr = _pick(n_rows, tr or TILES["adam"], SUBLANES)
    c1 = 1.0 - ADAM_B1 ** ADAM_STEP
    c2 = 1.0 - ADAM_B2 ** ADAM_STEP

    def body(p_ref, w_ref, m_ref, v_ref, g_ref, d_ref, mo_ref, vo_ref):
        g = p_ref[0].astype(F32)
        for s in range(1, n_parts):
            g = g + p_ref[s].astype(F32)
        mn = ADAM_B1 * m_ref[...] + (1.0 - ADAM_B1) * g
        vn = ADAM_B2 * v_ref[...] + (1.0 - ADAM_B2) * (g * g)
        g_ref[...] = g
        mo_ref[...] = mn
        vo_ref[...] = vn
        d_ref[...] = -ADAM_LR * ((mn / c1) / (jnp.sqrt(vn / c2) + ADAM_EPS) + ADAM_WD * w_ref[...])

    spec = pl.BlockSpec((tr, n_cols), lambda i: (i, 0))
    return pl.pallas_call(
        body, name=name, grid=(n_rows // tr,),
        in_specs=[pl.BlockSpec((n_parts, tr, n_cols), lambda i: (0, i, 0)), spec, spec, spec],
        out_specs=[spec] * 4, out_shape=[jax.ShapeDtypeStruct(w.shape, F32)] * 4,
        compiler_params=_params("parallel"),
    )(parts, w, m, v)


def _eye_mask():
    return jnp.eye(GROUPS_PER_BLOCK, dtype=F32)


def _b_blocks(bbr, bbi):
    nb = bbr.shape[0] // (GROUPS_PER_BLOCK * SSM_STATE)
    eye = _eye_mask()[None, :, None, :, None]

    def one(z):
        z = z.reshape(nb, GROUPS_PER_BLOCK, SSM_STATE, SSM_GROUP).transpose(0, 1, 3, 2)
        return z[:, :, :, None, :] * eye

    w = jnp.stack([one(bbr), one(bbi)], axis=3)
    return w.reshape(nb, LANES, BLOCK_COLS)


def _b_blocks_t(dw):
    nb = dw.shape[0]
    d6 = dw.reshape(nb, GROUPS_PER_BLOCK, SSM_GROUP, 2, GROUPS_PER_BLOCK, SSM_STATE)
    diag = jnp.sum(d6 * _eye_mask()[None, :, None, None, :, None], axis=4)
    diag = diag.transpose(3, 0, 1, 4, 2).reshape(2, nb * GROUPS_PER_BLOCK * SSM_STATE, SSM_GROUP)
    return diag[0], diag[1]


def _c_blocks(c_re, c_im):
    nb = c_re.shape[0] // GROUPS_PER_BLOCK
    eye = _eye_mask()[None, :, None, :, None]

    def one(z):
        z = z.reshape(nb, GROUPS_PER_BLOCK, SSM_GROUP, SSM_STATE).transpose(0, 1, 3, 2)
        return z[:, :, :, None, :] * eye

    w = jnp.stack([one(c_re), -one(c_im)], axis=1)
    return w.reshape(nb, BLOCK_COLS, LANES)


def _c_blocks_t(dw):
    nb = dw.shape[0]
    d6 = dw.reshape(nb, 2, GROUPS_PER_BLOCK, SSM_STATE, GROUPS_PER_BLOCK, SSM_GROUP)
    diag = jnp.sum(d6 * _eye_mask()[None, None, :, None, :, None], axis=4)
    diag = diag.transpose(1, 0, 2, 4, 3).reshape(2, nb * GROUPS_PER_BLOCK, SSM_GROUP, SSM_STATE)
    return diag[0], -diag[1]


def _unshard_cols(g):
    s, k, n = g.shape
    return g.transpose(1, 0, 2).reshape(k, s * n)


def _shard_cols(w):
    k, n = w.shape
    return w.reshape(k, N_DEV, n // N_DEV).transpose(1, 0, 2)


def _pack(arrays):
    chunks, offs, row = [], [], 0
    for a in arrays:
        flat = a.reshape(-1).astype(F32)
        rows = -(-flat.shape[0] // LANES)
        chunks.append(jnp.pad(flat, (0, rows * LANES - flat.shape[0])))
        offs.append((row, rows))
        row += rows
    pad_rows = (-row) % SUBLANES
    if pad_rows:
        chunks.append(jnp.zeros((pad_rows * LANES,), F32))
    return jnp.concatenate(chunks).reshape(row + pad_rows, LANES), offs


def _unpack(packed, offs, shapes):
    out = []
    for (row, rows), shp in zip(offs, shapes):
        size = 1
        for s in shp:
            size *= s
        out.append(packed[row: row + rows].reshape(-1)[:size].reshape(shp))
    return out


SCAN_SUB_ROWS = 128


def _scan_tables_into(tab_ref, lam_ref, reverse):
    steps, (pr, pi) = _scan_tables(lam_ref, reverse)
    for kk, (_, mr, mi) in enumerate(steps):
        tab_ref[2 * kk] = mr
        tab_ref[2 * kk + 1] = mi
    tab_ref[6] = pr
    tab_ref[7] = pi


def _scan_in_place(buf_ref, tab_ref, carry_ref, ng, reverse, prev_ref=None, acc_ref=None, row0=0, unroll=False):
    shape = (SUBLANES, BLOCK_STATE)
    last = 0 if reverse else SUBLANES - 1
    first_row = lax.broadcasted_iota(jnp.int32, shape, 0) == 0
    re_cols, im_cols = pl.ds(0, BLOCK_STATE), pl.ds(BLOCK_STATE, BLOCK_STATE)

    def group(r, carry):
        cr, ci = carry[0], carry[1]
        rr = (ng - 1 - r) if reverse else r
        off = row0 + rr * SUBLANES
        off = off if unroll else pl.multiple_of(off, SUBLANES)
        xr, xi = buf_ref[pl.ds(off, SUBLANES), re_cols], buf_ref[pl.ds(off, SUBLANES), im_cols]
        for kk, d in enumerate((1, 2, 4)):
            sh = (SUBLANES - d) if reverse else d
            mr, mi = tab_ref[2 * kk], tab_ref[2 * kk + 1]
            yr, yi = pltpu.roll(xr, sh, 0), pltpu.roll(xi, sh, 0)
            xr, xi = xr + mr * yr - mi * yi, xi + mr * yi + mi * yr
        pr, pi = tab_ref[6], tab_ref[7]
        xr, xi = xr + pr * cr - pi * ci, xi + pr * ci + pi * cr
        buf_ref[pl.ds(off, SUBLANES), re_cols] = xr
        buf_ref[pl.ds(off, SUBLANES), im_cols] = xi
        out = (jnp.broadcast_to(xr[last:last + 1, :], shape), jnp.broadcast_to(xi[last:last + 1, :], shape))
        if prev_ref is not None:
            off8 = off + SUBLANES if unroll else pl.multiple_of(off + SUBLANES, SUBLANES)
            before_r, before_i = prev_ref[pl.ds(off, SUBLANES), re_cols], prev_ref[pl.ds(off, SUBLANES), im_cols]
            same_r, same_i = prev_ref[pl.ds(off8, SUBLANES), re_cols], prev_ref[pl.ds(off8, SUBLANES), im_cols]
            sr = jnp.where(first_row, jnp.broadcast_to(before_r[SUBLANES - 1:, :], shape), pltpu.roll(same_r, 1, 0))
            si = jnp.where(first_row, jnp.broadcast_to(before_i[SUBLANES - 1:, :], shape), pltpu.roll(same_i, 1, 0))
            out += (carry[2] + xr * sr + xi * si, carry[3] + xi * sr - xr * si)
        return out

    init = (carry_ref[0], carry_ref[1])
    if prev_ref is not None:
        init += (acc_ref[0], acc_ref[1])
    if unroll:
        res = init
        for r in range(ng):
            res = group(r, res)
    else:
        res = lax.fori_loop(0, ng, group, init)
    carry_ref[0] = res[0]
    carry_ref[1] = res[1]
    if prev_ref is not None:
        acc_ref[0] = res[2]
        acc_ref[1] = res[3]


def s5_fused_fwd(name, u, wb, wc, lam, d_row, tm=None, gather=()):
    n_rows, d_model = u.shape
    nb = lam.shape[0]
    tm = _pick(n_rows, tm or TILES["scan"], 2 * SUBLANES)
    nt = n_rows // tm
    sub = _pick(tm, SCAN_SUB_ROWS, 2 * SUBLANES)
    ng = len(gather)

    def body(u_ref, wb_ref, wc_ref, lam_ref, d_ref, *refs):
        g_in, (z_ref, st_ref), g_out = refs[:ng], refs[ng: ng + 2], refs[ng + 2: 2 * ng + 2]
        buf_ref, carry_ref, tab_ref = refs[2 * ng + 2: 2 * ng + 5]
        b, it = pl.program_id(0), pl.program_id(1)
        if ng:
            start, finish = _gather_phases(g_in, g_out, *refs[2 * ng + 5:])
            pl.when((b == 0) & (it == 0))(start)

        @pl.when(it == 0)
        def _():
            carry_ref[...] = jnp.zeros_like(carry_ref)
            _scan_tables_into(tab_ref, lam_ref, False)

        for r0 in range(0, tm, sub):
            rows = slice(r0, r0 + sub)
            uu = u_ref[rows, :]
            buf_ref[rows, :] = jnp.dot(uu.astype(BF16), wb_ref[...], preferred_element_type=F32)
            _scan_in_place(buf_ref, tab_ref, carry_ref, sub // SUBLANES, False, row0=r0, unroll=True)
            st = buf_ref[rows, :].astype(BF16)
            st_ref[rows, :] = st
            z_ref[rows, :] = _gelu(jnp.dot(st, wc_ref[...], preferred_element_type=F32) + d_ref[...] * uu).astype(z_ref.dtype)
        if ng:
            pl.when((b == nb - 1) & (it == nt - 1))(finish)

    tile = lambda b, it: (it, b)
    blk = lambda b, it: (b, 0, 0)
    any_spec = pl.BlockSpec(memory_space=pl.ANY)
    params = pltpu.CompilerParams(dimension_semantics=("arbitrary", "arbitrary"), vmem_limit_bytes=VMEM_LIMIT_BYTES,
                                  has_side_effects=True) if ng else _params("parallel", "arbitrary")
    return pl.pallas_call(
        body, name=name, grid=(nb, nt),
        in_specs=[pl.BlockSpec((tm, LANES), tile), pl.BlockSpec((None, LANES, BLOCK_COLS), blk),
                  pl.BlockSpec((None, BLOCK_COLS, LANES), blk), pl.BlockSpec((None, 2, BLOCK_STATE), blk),
                  pl.BlockSpec((1, LANES), lambda b, it: (0, b))] + [any_spec] * ng,
        out_specs=[pl.BlockSpec((tm, LANES), tile), pl.BlockSpec((tm, BLOCK_COLS), tile)] + [any_spec] * ng,
        out_shape=[jax.ShapeDtypeStruct((n_rows, d_model), BF16), jax.ShapeDtypeStruct((n_rows, nb * BLOCK_COLS), BF16)]
        + _gather_shapes(gather),
        scratch_shapes=[pltpu.VMEM((tm, BLOCK_COLS), F32), pltpu.VMEM((2, SUBLANES, BLOCK_STATE), F32),
                        pltpu.VMEM((8, SUBLANES, BLOCK_STATE), F32)] + (_gather_semaphores(ng) if ng else []),
        compiler_params=params,
    )(u, wb, wc, lam, d_row, *gather)


def s5_fused_bwd(name, dz, u, st, wb, wc, lam, d_row, tm=None):
    n_rows, d_model = u.shape
    nb = lam.shape[0]
    tm = _pick(n_rows, tm or TILES["scan"], 2 * SUBLANES)
    nt = n_rows // tm
    tail_rows = 2 * SUBLANES
    sub = _pick(tm, SCAN_SUB_ROWS, 2 * SUBLANES)

    def body(dz_ref, u_ref, st_ref, tail_ref, wb_ref, wc_ref, lam_ref, d_ref,
             du_ref, dwb_ref, dwc_ref, dlam_ref, dd_ref, buf_ref, prev_ref, carry_ref, acc_ref, tab_ref):
        it = pl.program_id(1)

        @pl.when(it == 0)
        def _():
            carry_ref[...] = jnp.zeros_like(carry_ref)
            acc_ref[...] = jnp.zeros_like(acc_ref)
            dwb_ref[...] = jnp.zeros_like(dwb_ref)
            dwc_ref[...] = jnp.zeros_like(dwc_ref)
            dd_ref[...] = jnp.zeros_like(dd_ref)
            _scan_tables_into(tab_ref, lam_ref, True)

        prev_ref[SUBLANES:, :] = st_ref[...].astype(F32)
        before = tail_ref[...].astype(F32)[SUBLANES:, :]
        prev_ref[:SUBLANES, :] = jnp.where(it == nt - 1, 0.0, before)
        for r0 in range(tm - sub, -1, -sub):
            rows = slice(r0, r0 + sub)
            uu, st_b = u_ref[rows, :], st_ref[rows, :]
            y = jnp.dot(st_b, wc_ref[...], preferred_element_type=F32) + d_ref[...] * uu
            _, vjp = jax.vjp(_gelu, y)
            dy = vjp(dz_ref[rows, :])[0]
            dyb = dy.astype(BF16)
            dd_ref[...] += jnp.sum(dy * uu, axis=0, keepdims=True)
            dwc_ref[...] += lax.dot_general(st_b, dyb, _DIMS["tn"], preferred_element_type=F32)
            buf_ref[rows, :] = lax.dot_general(dyb, wc_ref[...], _DIMS["nt"], preferred_element_type=F32)
            _scan_in_place(buf_ref, tab_ref, carry_ref, sub // SUBLANES, True, prev_ref, acc_ref, row0=r0, unroll=True)
            gb = buf_ref[rows, :].astype(BF16)
            du_ref[rows, :] = lax.dot_general(gb, wb_ref[...], _DIMS["nt"], preferred_element_type=F32) + dy * d_ref[...]
            dwb_ref[...] += lax.dot_general(uu.astype(BF16), gb, _DIMS["tn"], preferred_element_type=F32)

        @pl.when(it == nt - 1)
        def _():
            dlam_ref[0:1, :] = jnp.sum(acc_ref[0], axis=0, keepdims=True)
            dlam_ref[1:2, :] = jnp.sum(acc_ref[1], axis=0, keepdims=True)

    tile = lambda b, it: (nt - 1 - it, b)
    blk = lambda b, it: (b, 0, 0)
    per_tile = tm // tail_rows
    tail = lambda b, it: (jnp.maximum((nt - 1 - it) * per_tile - 1, 0), b)
    return pl.pallas_call(
        body, name=name, grid=(nb, nt),
        in_specs=[pl.BlockSpec((tm, LANES), tile), pl.BlockSpec((tm, LANES), tile), pl.BlockSpec((tm, BLOCK_COLS), tile),
                  pl.BlockSpec((tail_rows, BLOCK_COLS), tail), pl.BlockSpec((None, LANES, BLOCK_COLS), blk),
                  pl.BlockSpec((None, BLOCK_COLS, LANES), blk), pl.BlockSpec((None, 2, BLOCK_STATE), blk),
                  pl.BlockSpec((1, LANES), lambda b, it: (0, b))],
        out_specs=[pl.BlockSpec((tm, LANES), tile), pl.BlockSpec((None, LANES, BLOCK_COLS), blk),
                   pl.BlockSpec((None, BLOCK_COLS, LANES), blk), pl.BlockSpec((None, 2, BLOCK_STATE), blk),
                   pl.BlockSpec((1, LANES), lambda b, it: (0, b))],
        out_shape=[jax.ShapeDtypeStruct((n_rows, d_model), F32), jax.ShapeDtypeStruct((nb, LANES, BLOCK_COLS), F32),
                   jax.ShapeDtypeStruct((nb, BLOCK_COLS, LANES), F32), jax.ShapeDtypeStruct((nb, 2, BLOCK_STATE), F32),
                   jax.ShapeDtypeStruct((1, d_model), F32)],
        scratch_shapes=[pltpu.VMEM((tm, BLOCK_COLS), F32), pltpu.VMEM((tm + SUBLANES, BLOCK_COLS), F32),
                        pltpu.VMEM((2, SUBLANES, BLOCK_STATE), F32), pltpu.VMEM((2, SUBLANES, BLOCK_STATE), F32),
                        pltpu.VMEM((8, SUBLANES, BLOCK_STATE), F32)],
        compiler_params=_params("parallel", "arbitrary"),
    )(dz, u, st, st, wb, wc, lam, d_row)


def s5_fwd(tag, u, log_dt, a_re, a_im, b_re, b_im, c_re, c_im, d_row, gather=()):
    d_model = u.shape[1]
    nb = d_model // LANES
    col = lambda a: a.reshape(-1, 1)
    prm = [col(jnp.repeat(log_dt, SSM_STATE)), col(a_re), col(a_im), b_re.reshape(-1, SSM_GROUP), b_im.reshape(-1, SSM_GROUP)]
    lr, li, bbr, bbi = rowcall(f"s5_prep_{tag}", _s5_discretise, prm, [], [(1, F32), (1, F32), (SSM_GROUP, F32), (SSM_GROUP, F32)])
    lam = jnp.stack([lr.reshape(nb, BLOCK_STATE), li.reshape(nb, BLOCK_STATE)], axis=1)
    wb = _b_blocks(bbr, bbi).astype(BF16)
    wc = _c_blocks(c_re, c_im).astype(BF16)
    z, st, *gathered = s5_fused_fwd(f"s5_fwd_{tag}", u, wb, wc, lam, d_row, gather=gather)
    return z, dict(prm=prm, lam=lam, wb=wb, wc=wc, st=st), gathered


def s5_bwd(tag, dz, u, d_row, sv):
    d_model = u.shape[1]
    nb = d_model // LANES
    n_groups = d_model // SSM_GROUP
    col = lambda a: a.reshape(-1, 1)

    dhn, d_wb, d_wc, d_lam, d_dskip = s5_fused_bwd(f"s5_bwd_{tag}", dz, u, sv["st"], sv["wb"], sv["wc"], sv["lam"], d_row)
    d_bbr, d_bbi = _b_blocks_t(d_wb)
    d_cre, d_cim = _c_blocks_t(d_wc)

    def prep_bwd(ldt, ar, ai, br, bi, dlr, dli, dbr, dbi):
        _, vjp = jax.vjp(_s5_discretise, ldt, ar, ai, br, bi)
        return vjp((dlr, dli, dbr, dbi))

    d_ldt, d_are, d_aim, d_bre, d_bim = rowcall(
        f"s5_prep_bwd_{tag}", prep_bwd, sv["prm"] + [col(d_lam[:, 0]), col(d_lam[:, 1]), d_bbr, d_bbi], [],
        [(1, F32), (1, F32), (1, F32), (SSM_GROUP, F32), (SSM_GROUP, F32)])
    d_logdt = rowcall(f"s5_dlogdt_{tag}", lambda a: jnp.sum(a, axis=1, keepdims=True), [d_ldt.reshape(n_groups, SSM_STATE)], [], [(1, F32)])[0]
    grads = dict(log_dt=d_logdt.reshape(n_groups), a_re=d_are.reshape(n_groups, SSM_STATE), a_im=d_aim.reshape(n_groups, SSM_STATE),
                 b_re=d_bre.reshape(n_groups, SSM_STATE, SSM_GROUP), b_im=d_bim.reshape(n_groups, SSM_STATE, SSM_GROUP),
                 c_re=d_cre, c_im=d_cim, d=d_dskip)
    return dhn, grads


def kernel(x, mix_norm, mlp_norm, mlp_w1, mlp_w2, ssm_log_dt, ssm_a_re, ssm_a_im, ssm_b_re, ssm_b_im, ssm_c_re, ssm_c_im, ssm_d, ssm_w_glu, kv_norm, w_kvf, b_f, attn_wq, attn_wo, final_norm, loss_target, m_mix_norm, m_mlp_norm, m_mlp_w1, m_mlp_w2, m_ssm_log_dt, m_ssm_a_re, m_ssm_a_im, m_ssm_b_re, m_ssm_b_im, m_ssm_c_re, m_ssm_c_im, m_ssm_d, m_ssm_w_glu, m_kv_norm, m_w_kvf, m_b_f, m_attn_wq, m_attn_wo, m_final_norm, v_mix_norm, v_mlp_norm, v_mlp_w1, v_mlp_w2, v_ssm_log_dt, v_ssm_a_re, v_ssm_a_im, v_ssm_b_re, v_ssm_b_im, v_ssm_c_re, v_ssm_c_im, v_ssm_d, v_ssm_w_glu, v_kv_norm, v_w_kvf, v_b_f, v_attn_wq, v_attn_wo, v_final_norm):
    n_rows, d_model = x.shape[1], x.shape[2]
    depth = mix_norm.shape[0]
    n_a = ssm_log_dt.shape[0]
    n_b = depth - n_a
    n_heads = d_model // HEAD_DIM
    n_groups = d_model // SSM_GROUP
    nb = n_groups // GROUPS_PER_BLOCK
    kvf_cols = 2 * d_model + n_heads
    kvf_pad = 2 * d_model + LANES

    (g_d,) = all_gather("gather_skip_gain", [ssm_d])
    d_skip = [g_d[:, i].reshape(1, d_model) for i in range(n_a)]
    sharded = [mlp_w1.astype(BF16), mlp_w2.astype(BF16), ssm_w_glu.astype(BF16), w_kvf.astype(BF16),
               attn_wq.astype(BF16), attn_wo.astype(BF16)]
    w1 = w2 = wglu = wkvf = wq = wo = None

    row = lambda a: a.reshape(1, -1)
    col = lambda a: a.reshape(-1, 1)

    h = x[0]
    saved = []
    k_ext = v_ext = None
    q_consts = _lane_consts([(F_LANE, F_LANE + 3, 1.0)])
    k_consts = _lane_consts([(LSE_LANE, LSE_LANE + 3, -1.0), (ROWSUM_LANE, ROWSUM_LANE + 1, 1.0)])
    v_consts = _lane_consts([(SUM_LANE, SUM_LANE + 1, 1.0), (DELTA_LANE, DELTA_LANE + 3, -1.0)])
    for i in range(depth):
        sv = {"h": h}
        hn = rms_fwd(f"mix_norm_{i}", h, row(mix_norm[i]), F32 if i < n_a else BF16)
        sv["hn"] = hn
        if i < n_a:
            z, s5_saved, gathered = s5_fwd(str(i), hn, ssm_log_dt[i], ssm_a_re[i], ssm_a_im[i], ssm_b_re[i], ssm_b_im[i],
                                           ssm_c_re[i], ssm_c_im[i], d_skip[i], gather=sharded if i == 0 else ())
            if i == 0:
                g_w1, g_w2, g_glu, g_kvf, g_wq, g_wo = gathered
                w1 = [_unshard_cols(g_w1[:, li]) for li in range(depth)]
                w2 = [g_w2[:, li].reshape(-1, d_model) for li in range(depth)]
                wglu = [_unshard_cols(g_glu[:, li]) for li in range(n_a)]
                wkvf = jnp.pad(_unshard_cols(g_kvf), ((0, 0), (0, kvf_pad - kvf_cols)))
                wq = [g_wq[:, li].reshape(-1, d_model) for li in range(n_b)]
                wo = [g_wo[:, li].reshape(-1, d_model) for li in range(n_b)]
            zw = matmul(f"s5_glu_{i}", z, wglu[i])
            def gate(hh, zz, gg):
                out = hh + zz[:, :d_model] * _sigmoid(zz[:, d_model:])
                return out, _rms(out, gg)

            h1, h2n = rowcall(f"s5_gate_{i}", gate, [h, zw], [row(mlp_norm[i])], [(d_model, F32), (d_model, BF16)])
            sv.update(s5=s5_saved, z=z, zw=zw)
        else:
            j = i - n_a
            q = matmul(f"attn_q_{j}", hn, wq[j], scale=LOG2E * HEAD_DIM ** -0.5, out_dtype=BF16)
            o_ext, q_ext_b = fox_fwd(f"attn_fwd_{j}", pack_heads(f"pack_q_{j}", q, 0, n_heads, q_consts), k_ext, v_ext)
            o2 = unpack_heads(f"unpack_o_{j}", [o_ext])[0]
            h1 = matmul(f"attn_o_{j}", o2, wo[j], resid=h)
            sv.update(q_ext_b=q_ext_b, o2=o2)
            h2n = rms_fwd(f"mlp_norm_{i}", h1, row(mlp_norm[i]), BF16)
        ap = matmul(f"mlp_up_{i}", h2n, w1[i], out_dtype=BF16)
        h = matmul(f"mlp_down_{i}", ap, w2[i], a_fn=_sqrelu, resid=h1, bk=2048)
        sv.update(h1=h1, h2n=h2n, ap=ap)
        saved.append(sv)
        if i == n_a - 1:
            h_mid = h
            hk = rms_fwd("kv_norm", h, row(kv_norm), BF16)
            kvf = matmul("kvf_proj", hk, wkvf, bn=kvf_pad)
            fl = kvf[:, 2 * d_model:]
            bfp = jnp.pad(row(b_f), ((0, 0), (0, LANES - n_heads)))
            k_ext = pack_heads("pack_k", kvf, 0, n_heads, k_consts, cum_logf("cum_logf", fl, bfp), F_LANE)
            v_ext = pack_heads("pack_v", kvf, 1, n_heads, v_consts)

    def loss_fn(hh, tgt, g):
        y, vjp = jax.vjp(_rms, hh, g)
        err = y - tgt
        part = 0.5 * jnp.sum(jnp.mean(err * err, axis=-1, keepdims=True), axis=0, keepdims=True)
        dh, dg = vjp(err * (1.0 / d_model))
        return dh, jnp.broadcast_to(part, (1, LANES)), dg

    dh, loss_part, d_final = rowcall("loss_head", loss_fn, [h, loss_target[0]], [row(final_norm)],
                                     [(d_model, F32)], [(1, LANES), (1, d_model)])

    g_mix, g_mlpn = [None] * depth, [None] * depth
    g_w1f, g_w2f = [None] * depth, [None] * depth
    g_ssm = [None] * n_a
    g_wqf, g_wof = [None] * n_b, [None] * n_b
    dk_acc, dv_acc, df_plus = [], [], []
    g_kv = None
    for i in reversed(range(depth)):
        sv = saved[i]
        if i == n_a - 1:
            dk, col_sums = unpack_heads("unpack_dk", dk_acc, extract_lane=HEAD_DIM)
            dv = unpack_heads("unpack_dv", dv_acc)[0]
            dfl, db_f = cum_logf_bwd("cum_logf_bwd", fl, bfp, df_plus, [col_sums])
            dkvf = jnp.concatenate([dk, dv, dfl], axis=1).astype(BF16)
            d_wkvf = matmul("kvf_dw", hk, dkvf, "tn", bm=512, bn=kvf_pad)
            dh, d_kvn = matmul("kvf_dx", dkvf, wkvf, "nt", bm=512, bk=kvf_pad, rms_bwd_of=(h_mid, row(kv_norm), dh))
            g_kv = (d_wkvf[:, :kvf_cols], d_kvn, db_f[:, :n_heads])
        dap = matmul(f"mlp_down_dx_{i}", dh, w2[i], "nt", post=lambda acc, apt: acc * (2.0 * jnp.maximum(apt.astype(F32), 0.0)),
                     post_arg=sv["ap"], out_dtype=BF16)
        g_w2f[i] = matmul(f"mlp_down_dw_{i}", sv["ap"], dh, "tn", a_fn=_sqrelu)
        g_w1f[i] = matmul(f"mlp_up_dw_{i}", sv["h2n"], dap, "tn")
        dh1, g_mlpn[i] = matmul(f"mlp_up_dx_{i}", dap, w1[i], "nt", bm=512, bk=2048, rms_bwd_of=(sv["h1"], row(mlp_norm[i]), dh))
        if i < n_a:
            def glu_bwd(zz, dd):
                val, gate = zz[:, :d_model], zz[:, d_model:]
                sg = _sigmoid(gate)
                return jnp.concatenate([dd * sg, dd * val * sg * (1.0 - sg)], axis=1)

            dzw = rowcall(f"s5_gate_bwd_{i}", glu_bwd, [sv["zw"], dh1], [], [(2 * d_model, BF16)])[0]
            dz = matmul(f"s5_glu_dx_{i}", dzw, wglu[i], "nt")
            d_wglu = matmul(f"s5_glu_dw_{i}", sv["z"], dzw, "tn")

            dhn, g_ssm[i] = s5_bwd(str(i), dz, sv["hn"], d_skip[i], sv["s5"])
            g_ssm[i]["w_glu"] = d_wglu
        else:
            j = i - n_a
            do2 = matmul(f"attn_o_dx_{j}", dh1, wo[j], "nt")
            g_wof[j] = matmul(f"attn_o_dw_{j}", sv["o2"], dh1, "tn")
            do_ext = pack_heads(f"pack_do_{j}", do2, 0, n_heads, jnp.zeros((1, LANES), F32),
                                attn_delta(f"attn_delta_{j}", sv["o2"], do2), DELTA_LANE)
            dq_ext, dk_ext, dv_ext = fox_bwd(f"attn_bwd_{j}", sv["q_ext_b"], do_ext, k_ext, v_ext)
            dk_acc.append(dk_ext)
            dv_acc.append(dv_ext)
            dq2, row_sums = unpack_heads(f"unpack_dq_{j}", [dq_ext], extract_lane=ROWSUM_LANE)
            df_plus.append(row_sums)
            g_wqf[j] = matmul(f"attn_q_dw_{j}", sv["hn"], dq2, "tn", scale=HEAD_DIM ** -0.5)
            dh, g_mix[i] = matmul(f"attn_q_dx_{j}", dq2, wq[j], "nt", scale=HEAD_DIM ** -0.5, bm=512,
                                  rms_bwd_of=(sv["h"], row(mix_norm[i]), dh1))
        if i < n_a:
            dh, g_mix[i] = rms_bwd(f"mix_norm_bwd_{i}", sv["h"], row(mix_norm[i]), dhn, add=dh1)
    grad_x = dh[None]

    stack = lambda xs: jnp.stack(xs, axis=1).astype(BF16 if xs[0].ndim == 3 else F32)
    contributions = [
        stack([_shard_cols(g) for g in g_w1f]),
        stack([g.reshape(N_DEV, -1, d_model) for g in g_w2f]),
        stack([_shard_cols(g["w_glu"]) for g in g_ssm]),
        _shard_cols(g_kv[0]).astype(BF16),
        stack([g.reshape(N_DEV, -1, d_model) for g in g_wqf]),
        stack([g.reshape(N_DEV, -1, d_model) for g in g_wof]),
        stack([g["d"].reshape(N_DEV, -1) for g in g_ssm]),
    ]
    from_sibling = pair_swap("pair_swap_grads", contributions)
    core = lax.axis_index("c")
    pair_sums = []
    for k, (mine, theirs) in enumerate(zip(contributions, from_sibling)):
        mine = lax.dynamic_index_in_dim(mine.reshape((N_CHIPS, 2) + mine.shape[1:]), core, axis=1, keepdims=False)
        cols = mine.shape[-1]
        pair_sums.append(rowcall(f"pair_add_{k}", lambda a, b: a.astype(F32) + b.astype(F32),
                                 [mine.reshape(-1, cols), theirs.reshape(-1, cols)], [], [(cols, mine.dtype)])[0].reshape(mine.shape))
    parts = chip_exchange("chip_exchange_grads", pair_sums)
    ssm_g = lambda kk: jnp.stack([g[kk] for g in g_ssm])
    loss_slot = jnp.zeros((LANES,), F32)
    small = {
        "f32": (["mix_norm", "mlp_norm", "ssm_log_dt", "ssm_a_re", "ssm_a_im", "kv_norm", "b_f", "final_norm"],
                [jnp.concatenate(g_mix, axis=0), jnp.concatenate(g_mlpn, axis=0), ssm_g("log_dt"), ssm_g("a_re"), ssm_g("a_im"),
                 g_kv[1], g_kv[2], d_final, loss_part],
                [mix_norm, mlp_norm, ssm_log_dt, ssm_a_re, ssm_a_im, kv_norm, b_f, final_norm, loss_slot],
                [m_mix_norm, m_mlp_norm, m_ssm_log_dt, m_ssm_a_re, m_ssm_a_im, m_kv_norm, m_b_f, m_final_norm, loss_slot],
                [v_mix_norm, v_mlp_norm, v_ssm_log_dt, v_ssm_a_re, v_ssm_a_im, v_kv_norm, v_b_f, v_final_norm, loss_slot]),
        "bf16": (["ssm_b_re", "ssm_b_im", "ssm_c_re", "ssm_c_im"],
                 [ssm_g("b_re"), ssm_g("b_im"), ssm_g("c_re"), ssm_g("c_im")],
                 [ssm_b_re, ssm_b_im, ssm_c_re, ssm_c_im], [m_ssm_b_re, m_ssm_b_im, m_ssm_c_re, m_ssm_c_im],
                 [v_ssm_b_re, v_ssm_b_im, v_ssm_c_re, v_ssm_c_im]),
    }
    packed = {kk: [_pack(arrs) for arrs in grp[1:]] for kk, grp in small.items()}
    small_parts = all_gather("gather_small_grads", [packed["f32"][0][0], packed["bf16"][0][0].astype(BF16)])

    res = {}

    def update(nm, part, w, m, v):
        shp = w.shape
        as2d = lambda a: a.reshape(-1, shp[-1])
        outs = adamw(f"adamw_{nm}", part.reshape(part.shape[:1] + as2d(w).shape), as2d(w), as2d(m), as2d(v))
        res[nm] = [o.reshape(shp) for o in outs]

    update("mlp_w1", parts[0], mlp_w1, m_mlp_w1, v_mlp_w1)
    update("mlp_w2", parts[1], mlp_w2, m_mlp_w2, v_mlp_w2)
    update("ssm_w_glu", parts[2], ssm_w_glu, m_ssm_w_glu, v_ssm_w_glu)
    update("w_kvf", parts[3], w_kvf, m_w_kvf, v_w_kvf)
    update("attn_wq", parts[4], attn_wq, m_attn_wq, v_attn_wq)
    update("attn_wo", parts[5], attn_wo, m_attn_wo, v_attn_wo)
    update("ssm_d", parts[6], ssm_d, m_ssm_d, v_ssm_d)
    loss = None
    for (kk, (names, _, ws, _, _)), part in zip(small.items(), small_parts):
        (_, offs), (pw, _), (pm, _), (pv, _) = packed[kk]
        small_out = adamw(f"adamw_small_{kk}", part, pw, pm, pv)
        unpacked = [_unpack(o, offs, [w.shape for w in ws]) for o in small_out]
        for idx, nm in enumerate(names):
            res[nm] = [u[idx] for u in unpacked]
        if kk == "f32":
            loss = unpacked[0][-1][0]

    order = ["mix_norm", "mlp_norm", "mlp_w1", "mlp_w2", "ssm_log_dt", "ssm_a_re", "ssm_a_im", "ssm_b_re", "ssm_b_im", "ssm_c_re",
             "ssm_c_im", "ssm_d", "ssm_w_glu", "kv_norm", "w_kvf", "b_f", "attn_wq", "attn_wo", "final_norm"]
    out = [loss, grad_x]
    for kind in range(4):
        out += [res[nm][kind] for nm in order]
    return tuple(out)
```

```python
import functools

import jax
import jax.numpy as jnp
from jax import lax
from jax.experimental import pallas as pl
from jax.experimental.pallas import tpu as pltpu

F32 = jnp.float32
BF16 = jnp.bfloat16
HIGHEST = lax.Precision.HIGHEST

V7X_VMEM_BYTES = 64 << 20
VMEM_LIMIT_BYTES = (V7X_VMEM_BYTES * 3) // 4
LANES = 128
SUBLANES = 8

N_DEV = 8
RMS_EPS = 1e-6
SSM_GROUP = 16
SSM_STATE = 64
HEAD_DIM = 64
GROUPS_PER_BLOCK = LANES // SSM_GROUP
BLOCK_STATE = GROUPS_PER_BLOCK * SSM_STATE
BLOCK_COLS = 2 * BLOCK_STATE
NEG_BIG = -1e30
LOG2E = 1.4426950408889634

ADAM_LR = 0.001
ADAM_B1 = 0.9
ADAM_B2 = 0.999
ADAM_EPS = 1e-08
ADAM_WD = 0.01
ADAM_STEP = 10

TILES = {"row": 512, "mm": (1024, 1024, 1024), "blk": 512, "scan": 512, "cum": 256, "attn": 1024, "adam": 256}


def _pick(dim, pref, align=LANES):
    if dim <= pref:
        return dim
    for a in (align, SUBLANES):
        d = (pref // a) * a
        while d >= a:
            if dim % d == 0:
                return d
            d -= a
    return dim


def _params(*sem):
    return pltpu.CompilerParams(dimension_semantics=sem, vmem_limit_bytes=VMEM_LIMIT_BYTES)


def rowcall(name, fn, rows, consts, out_rows, out_accs=(), tm=None):
    n_rows = rows[0].shape[0]
    tm = _pick(n_rows, tm or TILES["row"], SUBLANES)
    nr, nc, no, na = len(rows), len(consts), len(out_rows), len(out_accs)

    def body(*refs):
        ins = [r[...] for r in refs[: nr + nc]]
        outs = fn(*ins)
        if not isinstance(outs, (tuple, list)):
            outs = (outs,)
        for r, o in zip(refs[nr + nc: nr + nc + no], outs[:no]):
            r[...] = o.astype(r.dtype)
        if na:
            i = pl.program_id(0)
            for r, o in zip(refs[nr + nc + no:], outs[no:]):
                @pl.when(i == 0)
                def _(r=r, o=o):
                    r[...] = o

                @pl.when(i > 0)
                def _(r=r, o=o):
                    r[...] += o

    in_specs = [pl.BlockSpec((tm, a.shape[1]), lambda i: (i, 0)) for a in rows]
    in_specs += [pl.BlockSpec(c.shape, lambda i, n=c.ndim: (0,) * n) for c in consts]
    out_shape = [jax.ShapeDtypeStruct((n_rows, c), dt) for c, dt in out_rows]
    out_specs = [pl.BlockSpec((tm, c), lambda i: (i, 0)) for c, _ in out_rows]
    out_shape += [jax.ShapeDtypeStruct(s, F32) for s in out_accs]
    out_specs += [pl.BlockSpec(s, lambda i, n=len(s): (0,) * n) for s in out_accs]
    res = pl.pallas_call(
        body, name=name, grid=(n_rows // tm,), in_specs=in_specs, out_specs=out_specs, out_shape=out_shape,
        compiler_params=_params("arbitrary" if na else "parallel"),
    )(*rows, *consts)
    return res


_DIMS = {"nn": (((1,), (0,)), ((), ())), "nt": (((1,), (1,)), ((), ())), "tn": (((0,), (0,)), ((), ()))}


def matmul(name, a, b, mode="nn", *, a_fn=None, scale=None, resid=None, post=None, post_arg=None, rms_bwd_of=None,
           out_dtype=F32, bm=None, bn=None, bk=None):
    if mode == "nn":
        (m, k), (k2, n) = a.shape, b.shape
    elif mode == "nt":
        (m, k), (n, k2) = a.shape, b.shape
    else:
        (k, m), (k2, n) = a.shape, b.shape
    assert k == k2, (name, a.shape, b.shape, mode)
    if rms_bwd_of is not None:
        bn = n
    bm, bn, bk = _pick(m, bm or TILES["mm"][0]), _pick(n, bn or TILES["mm"][1]), _pick(k, bk or TILES["mm"][2])
    nk = k // bk
    a_spec = pl.BlockSpec((bk, bm), lambda i, j, kk: (kk, i)) if mode == "tn" else pl.BlockSpec((bm, bk), lambda i, j, kk: (i, kk))
    b_spec = pl.BlockSpec((bn, bk), lambda i, j, kk: (j, kk)) if mode == "nt" else pl.BlockSpec((bk, bn), lambda i, j, kk: (kk, j))
    mn_spec = pl.BlockSpec((bm, bn), lambda i, j, kk: (i, j))
    extra = [x for x in (resid, post_arg) if x is not None]
    has_resid, has_post, has_rms = resid is not None, post is not None, rms_bwd_of is not None
    row_spec = pl.BlockSpec((1, bn), lambda i, j, kk: (0, j))
    extra_specs = [mn_spec] * len(extra)
    if has_rms:
        x_in, gain, add = rms_bwd_of
        extra += [x_in, add, gain]
        extra_specs += [mn_spec, mn_spec, row_spec]

    def body(*refs):
        a_ref, b_ref = refs[0], refs[1]
        ex = refs[2: 2 + len(extra)]
        o_ref = refs[2 + len(extra)]
        av = a_ref[...]
        if a_fn is not None:
            av = a_fn(av.astype(F32))
        p = lax.dot_general(av.astype(BF16), b_ref[...].astype(BF16), _DIMS[mode], preferred_element_type=F32)

        def finish(acc):
            if scale is not None:
                acc = acc * scale
            idx = 0
            if has_resid:
                acc = acc + ex[idx][...]
                idx += 1
            if has_post:
                acc = post(acc, ex[idx][...])
                idx += 1
            if has_rms:
                _, vjp = jax.vjp(_rms, ex[idx][...], ex[idx + 2][...])
                dx, d_gain = vjp(acc)
                acc = dx + ex[idx + 1][...]
                dg_ref = refs[3 + len(extra)]
                first = pl.program_id(0) == 0

                @pl.when(first)
                def _():
                    dg_ref[...] = d_gain

                @pl.when(jnp.logical_not(first))
                def _():
                    dg_ref[...] += d_gain
            o_ref[...] = acc.astype(o_ref.dtype)

        if nk == 1:
            finish(p)
        else:
            acc_ref = refs[-1]
            kk = pl.program_id(2)

            @pl.when(kk == 0)
            def _():
                acc_ref[...] = p

            @pl.when(kk > 0)
            def _():
                acc_ref[...] += p

            @pl.when(kk == nk - 1)
            def _():
                finish(acc_ref[...])

    out_shape, out_specs = jax.ShapeDtypeStruct((m, n), out_dtype), mn_spec
    if has_rms:
        out_shape, out_specs = [out_shape, jax.ShapeDtypeStruct((1, n), F32)], [mn_spec, row_spec]
    return pl.pallas_call(
        body, name=name, grid=(m // bm, n // bn, nk),
        in_specs=[a_spec, b_spec] + extra_specs, out_specs=out_specs, out_shape=out_shape,
        scratch_shapes=[pltpu.VMEM((bm, bn), F32)] if nk > 1 else [],
        compiler_params=_params(*(("arbitrary",) * 3 if has_rms else ("parallel", "parallel", "arbitrary"))),
    )(a, b, *extra)


def _cmul(ar, ai, br, bi):
    return ar * br - ai * bi, ar * bi + ai * br


def _scan_tables(lam_ref, reverse):
    shape = (SUBLANES, BLOCK_STATE)
    lr = jnp.broadcast_to(lam_ref[0:1, :], shape)
    li = jnp.broadcast_to(lam_ref[1:2, :], shape)
    if reverse:
        li = -li
    row = lax.broadcasted_iota(jnp.int32, shape, 0)
    tt = (SUBLANES - 1 - row) if reverse else row
    l1 = (lr, li)
    l2 = _cmul(*l1, *l1)
    l4 = _cmul(*l2, *l2)
    pr, pi = l1
    for bit, lp in enumerate((l1, l2, l4)):
        qr, qi = _cmul(pr, pi, *lp)
        on = ((tt >> bit) & 1) == 1
        pr, pi = jnp.where(on, qr, pr), jnp.where(on, qi, pi)
    steps = []
    for d, lp in ((1, l1), (2, l2), (4, l4)):
        ok = tt >= d
        steps.append((d, jnp.where(ok, lp[0], 0.0), jnp.where(ok, lp[1], 0.0)))
    return steps, (pr, pi)


def _rms(x, g):
    return x * lax.rsqrt(jnp.mean(x * x, axis=-1, keepdims=True) + RMS_EPS) * g


def _sigmoid(x):
    return 1.0 / (1.0 + jnp.exp(-x))


def _gelu(x):
    return 0.5 * x * (1.0 + jnp.tanh(0.7978845608028654 * (x + 0.044715 * (x * x * x))))


def _log_sigmoid(x):
    return jnp.minimum(x, 0.0) - jnp.log(1.0 + jnp.exp(-jnp.abs(x)))


def _sqrelu(x):
    r = jnp.maximum(x, 0.0)
    return r * r


def _s5_discretise(ldt, ar, ai, br, bi):
    dt = jnp.exp(ldt)
    er = jnp.exp(ar * dt)
    lr, li = er * jnp.cos(ai * dt), er * jnp.sin(ai * dt)
    nr, ni = lr - 1.0, li
    den = ar * ar + ai * ai
    cr, ci = (nr * ar + ni * ai) / den, (ni * ar - nr * ai) / den
    return lr, li, cr * br - ci * bi, cr * bi + ci * br


def rms_fwd(name, x, g, dtype=F32):
    return rowcall(name, _rms, [x], [g], [(x.shape[1], dtype)])[0]


def rms_bwd(name, x, g, dy, add=None):
    def fn(x, dy, *rest):
        g = rest[-1]
        _, vjp = jax.vjp(_rms, x, g)
        dx, dg = vjp(dy)
        if add is not None:
            dx = dx + rest[0]
        return dx, dg

    rows = [x, dy] + ([add] if add is not None else [])
    return rowcall(name, fn, rows, [g], [(x.shape[1], F32)], [g.shape])


def _split3(x):
    hi = x.astype(BF16).astype(F32)
    r = x - hi
    mid = r.astype(BF16).astype(F32)
    return hi, mid, (r - mid).astype(BF16).astype(F32)


def cum_logf(name, fl, bf, tm=None):
    n_rows, w = fl.shape
    tm = _pick(n_rows, tm or TILES["cum"], SUBLANES)

    def body(fl_ref, bf_ref, hi_ref, mid_ref, lo_ref, carry_ref):
        it = pl.program_id(0)

        @pl.when(it == 0)
        def _():
            carry_ref[...] = jnp.zeros_like(carry_ref)

        ls = _log_sigmoid(fl_ref[...] + bf_ref[...])
        tri = (lax.broadcasted_iota(jnp.int32, (tm, tm), 0) >= lax.broadcasted_iota(jnp.int32, (tm, tm), 1)).astype(F32)
        c = jnp.dot(tri, ls, precision=HIGHEST, preferred_element_type=F32) + carry_ref[0:1, :]
        carry_ref[...] = jnp.broadcast_to(c[tm - 1:tm, :], carry_ref.shape)
        hi_ref[...], mid_ref[...], lo_ref[...] = _split3(c * (-LOG2E))

    spec = pl.BlockSpec((tm, w), lambda i: (i, 0))
    return pl.pallas_call(
        body, name=name, grid=(n_rows // tm,),
        in_specs=[spec, pl.BlockSpec((1, w), lambda i: (0, 0))],
        out_specs=[spec] * 3, out_shape=[jax.ShapeDtypeStruct((n_rows, w), F32)] * 3,
        scratch_shapes=[pltpu.VMEM((SUBLANES, w), F32)],
        compiler_params=_params("arbitrary"),
    )(fl, bf)


def cum_logf_bwd(name, fl, bf, plus, minus, tm=None):
    n_rows, w = fl.shape
    tm = _pick(n_rows, tm or TILES["cum"], SUBLANES)
    nt = n_rows // tm
    n_p, n_m = len(plus), len(minus)

    def body(*refs):
        fl_ref, bf_ref = refs[0], refs[1]
        d_refs = refs[2: 2 + n_p + n_m]
        o_ref, db_ref, carry_ref = refs[2 + n_p + n_m:]
        it = pl.program_id(0)

        @pl.when(it == 0)
        def _():
            carry_ref[...] = jnp.zeros_like(carry_ref)

        d = None
        for r in d_refs[:n_p]:
            d = r[...] if d is None else d + r[...]
        for r in d_refs[n_p:]:
            d = -r[...] if d is None else d - r[...]
        tri = (lax.broadcasted_iota(jnp.int32, (tm, tm), 0) <= lax.broadcasted_iota(jnp.int32, (tm, tm), 1)).astype(F32)
        c = jnp.dot(tri, d, precision=HIGHEST, preferred_element_type=F32) + carry_ref[0:1, :]
        carry_ref[...] = jnp.broadcast_to(c[0:1, :], carry_ref.shape)
        dfl = c * _sigmoid(-(fl_ref[...] + bf_ref[...]))
        o_ref[...] = dfl
        part = jnp.sum(dfl, axis=0, keepdims=True)

        @pl.when(it == 0)
        def _():
            db_ref[...] = part

        @pl.when(it > 0)
        def _():
            db_ref[...] += part

    rev = lambda i: (nt - 1 - i, 0)
    return pl.pallas_call(
        body, name=name, grid=(nt,),
        in_specs=[pl.BlockSpec((tm, w), rev), pl.BlockSpec((1, w), lambda i: (0, 0))] + [pl.BlockSpec((tm, w), rev)] * (n_p + n_m),
        out_specs=[pl.BlockSpec((tm, w), rev), pl.BlockSpec((1, w), lambda i: (0, 0))],
        out_shape=[jax.ShapeDtypeStruct((n_rows, w), F32), jax.ShapeDtypeStruct((1, w), F32)],
        scratch_shapes=[pltpu.VMEM((SUBLANES, w), F32)],
        compiler_params=_params("arbitrary"),
    )(fl, bf, *plus, *minus)


ROWSUM_LANE = HEAD_DIM + 6
F_LANE = HEAD_DIM
LSE_LANE = HEAD_DIM + 3
SUM_LANE = HEAD_DIM
DELTA_LANE = HEAD_DIM + 1


def _lane_consts(pairs):
    lane = lax.broadcasted_iota(jnp.int32, (1, LANES), 1)
    out = jnp.zeros((1, LANES), F32)
    for lo, hi, v in pairs:
        out = jnp.where((lane >= lo) & (lane < hi), v, out)
    return out


def pack_heads(name, x, col_block, n_heads, consts, parts=(), parts_lane=0, tm=None):
    n_rows = x.shape[0]
    d = n_heads * HEAD_DIM
    assert n_heads % 2 == 0
    tm = _pick(n_rows, tm or TILES["row"], 2 * SUBLANES)
    n_parts = len(parts)

    def body(*refs):
        x_ref, c_ref = refs[0], refs[1]
        p_vals = [r[...] for r in refs[2: 2 + n_parts]]
        o_ref = refs[2 + n_parts]
        lane = lax.broadcasted_iota(jnp.int32, (tm, LANES), 1)
        tail0 = jnp.broadcast_to(c_ref[...], (tm, LANES))
        for h in range(n_heads):
            pair = x_ref[:, (h // 2) * LANES: (h // 2 + 1) * LANES].astype(F32)
            base = pair if h % 2 == 0 else pltpu.roll(pair, HEAD_DIM, 1)
            tail = tail0
            for kk, p in enumerate(p_vals):
                col = jnp.sum(jnp.where(lane == h, p, 0.0), axis=1, keepdims=True)
                tail = jnp.where(lane == parts_lane + kk, col, tail)
            o_ref[h] = jnp.where(lane < HEAD_DIM, base, tail).astype(BF16)

    return pl.pallas_call(
        body, name=name, grid=(n_rows // tm,),
        in_specs=[pl.BlockSpec((tm, d), lambda i: (i, col_block)), pl.BlockSpec((1, LANES), lambda i: (0, 0))]
        + [pl.BlockSpec((tm, LANES), lambda i: (i, 0))] * n_parts,
        out_specs=pl.BlockSpec((n_heads, tm, LANES), lambda i: (0, i, 0)),
        out_shape=jax.ShapeDtypeStruct((n_heads, n_rows, LANES), BF16),
        compiler_params=_params("parallel"),
    )(x, consts, *parts)


def unpack_heads(name, xs, extract_lane=None, tm=None):
    n_heads, n_rows, _ = xs[0].shape
    assert n_heads % 2 == 0
    tm = _pick(n_rows, tm or TILES["row"], SUBLANES)
    n = len(xs)

    def body(*refs):
        o_ref = refs[n]
        lane = lax.broadcasted_iota(jnp.int32, (tm, LANES), 1)
        picked = jnp.zeros((tm, LANES), F32)

        def head(h):
            v = refs[0][h]
            for r in refs[1:n]:
                v = v + r[h]
            return v

        for p in range(n_heads // 2):
            a, b = head(2 * p), head(2 * p + 1)
            o_ref[:, p * LANES: (p + 1) * LANES] = jnp.where(lane < HEAD_DIM, a, pltpu.roll(b, HEAD_DIM, 1))
            if extract_lane is not None:
                for hh, v in ((2 * p, a), (2 * p + 1, b)):
                    col = jnp.sum(jnp.where(lane == extract_lane, v, 0.0), axis=1, keepdims=True)
                    picked = jnp.where(lane == hh, col, picked)
        if extract_lane is not None:
            refs[n + 1][...] = picked

    d = n_heads * HEAD_DIM
    out_shape = [jax.ShapeDtypeStruct((n_rows, d), F32)]
    out_specs = [pl.BlockSpec((tm, d), lambda i: (i, 0))]
    if extract_lane is not None:
        out_shape.append(jax.ShapeDtypeStruct((n_rows, LANES), F32))
        out_specs.append(pl.BlockSpec((tm, LANES), lambda i: (i, 0)))
    return pl.pallas_call(
        body, name=name, grid=(n_rows // tm,),
        in_specs=[pl.BlockSpec((n_heads, tm, LANES), lambda i: (0, i, 0))] * n,
        out_specs=out_specs, out_shape=out_shape, compiler_params=_params("parallel"),
    )(*xs)


def fox_fwd(name, q_ext, k_ext, v_ext, t=None):
    nh, n_rows, w = q_ext.shape
    t = _pick(n_rows, t or TILES["attn"])
    nt = n_rows // t
    half = t // 2

    def body(q_ref, k_ref, v_ref, o_ref, qb_ref, m_ref, acc_ref):
        i = pl.program_id(1)
        m_ref[...] = jnp.full_like(m_ref, NEG_BIG)
        acc_ref[...] = jnp.zeros_like(acc_ref)
        q = q_ref[...]

        def piece(rows, kv_off, size, diagonal):
            kv = pl.ds(pl.multiple_of(kv_off, size), size)
            s = lax.dot_general(q_ref[rows, :], k_ref[kv, :], _DIMS["nt"], preferred_element_type=F32)
            if diagonal:
                keep = lax.broadcasted_iota(jnp.int32, s.shape, 0) >= lax.broadcasted_iota(jnp.int32, s.shape, 1)
                s = jnp.where(keep, s, NEG_BIG)
            m_prev = m_ref[rows, :]
            m_new = jnp.maximum(m_prev, jnp.max(s, axis=1, keepdims=True))
            p = jnp.exp2(s - jnp.tile(m_new, (1, size // LANES)))
            acc_ref[rows, :] = jnp.exp2(m_prev - m_new) * acc_ref[rows, :] + jnp.dot(
                p.astype(BF16), v_ref[kv, :], preferred_element_type=F32)
            m_ref[rows, :] = m_new

        def off_diagonal(j, carry):
            piece(slice(0, t), j * t, t, False)
            return carry

        lax.fori_loop(0, i, off_diagonal, 0)
        if half % LANES == 0:
            piece(slice(0, half), i * t, half, True)
            piece(slice(half, t), i * t, half, False)
            piece(slice(half, t), i * t + half, half, True)
        else:
            piece(slice(0, t), i * t, t, True)
        acc = acc_ref[...]
        row_sum = acc[:, HEAD_DIM:HEAD_DIM + 1]
        hi, mid, lo = _split3(m_ref[:, 0:1] + jnp.log2(row_sum))
        lane = lax.broadcasted_iota(jnp.int32, (t, w), 1)
        o_ref[...] = acc / row_sum
        qb = jnp.where(lane == LSE_LANE, hi, jnp.where(lane == LSE_LANE + 1, mid, jnp.where(lane == LSE_LANE + 2, lo, q.astype(F32))))
        qb_ref[...] = qb.astype(BF16)

    whole = pl.BlockSpec((None, n_rows, w), lambda h, i: (h, 0, 0))
    tile = pl.BlockSpec((None, t, w), lambda h, i: (h, i, 0))
    return pl.pallas_call(
        body, name=name, grid=(nh, nt), in_specs=[tile, whole, whole], out_specs=[tile, tile],
        out_shape=[jax.ShapeDtypeStruct((nh, n_rows, w), F32), jax.ShapeDtypeStruct((nh, n_rows, w), BF16)],
        scratch_shapes=[pltpu.VMEM((t, w), F32), pltpu.VMEM((t, w), F32)],
        compiler_params=_params("parallel", "arbitrary"),
    )(q_ext, k_ext, v_ext)


def fox_bwd(name, q_ext, do_ext, k_ext, v_ext, t=None):
    nh, n_rows, w = q_ext.shape
    t = _pick(n_rows, t or TILES["attn"])
    nt = n_rows // t
    half = t // 2

    def body(q_ref, do_ref, k_ref, v_ref, dq_ref, dk_ref, dv_ref):
        j = pl.program_id(1)

        @pl.when(j == 0)
        def _():
            dq_ref[...] = jnp.zeros_like(dq_ref)

        dk_ref[...] = jnp.zeros_like(dk_ref)
        dv_ref[...] = jnp.zeros_like(dv_ref)
        def piece(keys, q_off, size, diagonal):
            qs = pl.ds(pl.multiple_of(q_off, size), size)
            kj, vj, qi, doi = k_ref[keys, :], v_ref[keys, :], q_ref[qs, :], do_ref[qs, :]
            pt = jnp.exp2(lax.dot_general(kj, qi, _DIMS["nt"], preferred_element_type=F32))
            if diagonal:
                keep = lax.broadcasted_iota(jnp.int32, pt.shape, 0) <= lax.broadcasted_iota(jnp.int32, pt.shape, 1)
                pt = jnp.where(keep, pt, 0.0)
            dst = (pt * lax.dot_general(vj, doi, _DIMS["nt"], preferred_element_type=F32)).astype(BF16)
            dv_ref[keys, :] += jnp.dot(pt.astype(BF16), doi, preferred_element_type=F32)
            dk_ref[keys, :] += jnp.dot(dst, qi, preferred_element_type=F32)
            dq_ref[qs, :] += lax.dot_general(dst, kj, _DIMS["tn"], preferred_element_type=F32)

        def off_diagonal(i, carry):
            piece(slice(0, t), i * t, t, False)
            return carry

        if half % LANES == 0:
            piece(slice(0, half), j * t, half, True)
            piece(slice(0, half), j * t + half, half, False)
            piece(slice(half, t), j * t + half, half, True)
        else:
            piece(slice(0, t), j * t, t, True)
        lax.fori_loop(j + 1, nt, off_diagonal, 0)
        lane = lax.broadcasted_iota(jnp.int32, (t, w), 1)
        dk_ref[...] = dk_ref[...] * jnp.where(lane < HEAD_DIM, 1.0 / LOG2E, 1.0)

    whole = pl.BlockSpec((None, n_rows, w), lambda h, j: (h, 0, 0))
    tile = pl.BlockSpec((None, t, w), lambda h, j: (h, j, 0))
    shape = jax.ShapeDtypeStruct((nh, n_rows, w), F32)
    return pl.pallas_call(
        body, name=name, grid=(nh, nt), in_specs=[whole, whole, tile, tile], out_specs=[whole, tile, tile],
        out_shape=[shape, shape, shape],
        compiler_params=_params("parallel", "arbitrary"),
    )(q_ext, do_ext, k_ext, v_ext)


def attn_delta(name, o, do):
    d_model = o.shape[1]
    head_of_col = lax.broadcasted_iota(jnp.int32, (d_model, LANES), 0) // HEAD_DIM
    sel = (head_of_col == lax.broadcasted_iota(jnp.int32, (d_model, LANES), 1)).astype(F32)

    def fn(a, b, s):
        return _split3(jnp.dot(a * b, s, precision=HIGHEST, preferred_element_type=F32))

    return rowcall(name, fn, [o, do], [sel], [(LANES, F32)] * 3)


def all_gather(name, tensors):
    n = len(tensors)

    def body(*refs):
        start, finish = _gather_phases(refs[:n], refs[n: 2 * n], *refs[2 * n:])
        start()
        finish()

    any_spec = pl.BlockSpec(memory_space=pl.ANY)
    return pl.pallas_call(
        body, name=name, in_specs=[any_spec] * n, out_specs=[any_spec] * n,
        out_shape=_gather_shapes(tensors), scratch_shapes=_gather_semaphores(n),
        compiler_params=pltpu.CompilerParams(has_side_effects=True),
    )(*tensors)


def _gather_shapes(tensors):
    return [jax.ShapeDtypeStruct((N_DEV,) + t.shape, t.dtype) for t in tensors]


def _gather_semaphores(n):
    return [pltpu.SemaphoreType.DMA((n, 7)), pltpu.SemaphoreType.DMA((n, 7)), pltpu.SemaphoreType.DMA((n,))]


def _gather_phases(ins, outs, send_sems, recv_sems, local_sems):
    n = len(ins)
    x, y, c = lax.axis_index("x"), lax.axis_index("y"), lax.axis_index("c")
    sibling = (x, y, 1 - c)
    chips = [(1 - x, y), (x, 1 - y), (1 - x, 1 - y)]
    slot = lambda px, py, pc: 4 * px + 2 * py + pc

    def copy(t, k, block, to, src=None):
        dst = outs[t].at[slot(*block)]
        return pltpu.make_async_remote_copy(
            src_ref=dst if src is None else src, dst_ref=dst, send_sem=send_sems.at[t, k], recv_sem=recv_sems.at[t, k],
            device_id=to, device_id_type=pl.DeviceIdType.MESH)

    own = lambda t: pltpu.make_async_copy(ins[t], outs[t].at[slot(x, y, c)], local_sems.at[t])
    first = lambda t: [copy(t, 0, (x, y, c), sibling, ins[t])] + [copy(t, 1 + j, (x, y, c), (*chip, c), ins[t]) for j, chip in enumerate(chips)]

    def start():
        for t in range(n):
            own(t).start()
            for cp in first(t):
                cp.start()

    def finish():
        sends = []
        for t in range(n):
            sends += first(t)
            for j, chip in enumerate(chips):
                copy(t, 1 + j, (*chip, c), (x, y, c)).wait_recv()
                passed = copy(t, 4 + j, (*chip, c), sibling)
                passed.start()
                sends.append(passed)
        for t in range(n):
            copy(t, 0, sibling, (x, y, c)).wait_recv()
            for j, chip in enumerate(chips):
                copy(t, 4 + j, (*chip, 1 - c), (x, y, c)).wait_recv()
        for cp in sends:
            cp.wait_send()
        for t in range(n):
            own(t).wait()

    return start, finish


def _remote_call(name, body, tensors, out_shapes, n_copies):
    n = len(tensors)
    any_spec = pl.BlockSpec(memory_space=pl.ANY)
    return pl.pallas_call(
        body, name=name, in_specs=[any_spec] * n, out_specs=[any_spec] * n, out_shape=out_shapes,
        scratch_shapes=[pltpu.SemaphoreType.DMA((n, n_copies)), pltpu.SemaphoreType.DMA((n, n_copies)), pltpu.SemaphoreType.DMA((n,))],
        compiler_params=pltpu.CompilerParams(has_side_effects=True),
    )(*tensors)


N_CHIPS = 4
_CHIPS = [(0, 0), (0, 1), (1, 0), (1, 1)]


def pair_swap(name, tensors):
    n = len(tensors)

    def body(*refs):
        ins, outs = refs[:n], refs[n: 2 * n]
        send_sems, recv_sems, _ = refs[2 * n:]
        x, y, c = lax.axis_index("x"), lax.axis_index("y"), lax.axis_index("c")
        copies = []
        for t in range(n):
            for k in range(N_CHIPS):
                cp = pltpu.make_async_remote_copy(
                    src_ref=ins[t].at[2 * k + (1 - c)], dst_ref=outs[t].at[k], send_sem=send_sems.at[t, k], recv_sem=recv_sems.at[t, k],
                    device_id=(x, y, 1 - c), device_id_type=pl.DeviceIdType.MESH)
                cp.start()
                copies.append(cp)
        for cp in copies:
            cp.wait()

    return _remote_call(name, body, tensors, [jax.ShapeDtypeStruct((N_CHIPS,) + t.shape[1:], t.dtype) for t in tensors], N_CHIPS)


def chip_exchange(name, tensors):
    n = len(tensors)

    def body(*refs):
        start, finish = _chip_exchange_phases(refs[:n], refs[n: 2 * n], *refs[2 * n:])
        start()
        finish()

    return _remote_call(name, body, tensors, [jax.ShapeDtypeStruct(t.shape, t.dtype) for t in tensors], len(_CHIP_FLIPS))


_CHIP_FLIPS = [(1, 0), (0, 1), (1, 1)]


def _chip_exchange_semaphores(n):
    return [pltpu.SemaphoreType.DMA((n, len(_CHIP_FLIPS))), pltpu.SemaphoreType.DMA((n, len(_CHIP_FLIPS))), pltpu.SemaphoreType.DMA((n,))]


def _chip_exchange_phases(ins, outs, send_sems, recv_sems, local_sems):
    n = len(ins)
    x, y, c = lax.axis_index("x"), lax.axis_index("y"), lax.axis_index("c")
    me = 2 * x + y

    def copies(landing):
        out = []
        for t in range(n):
            for j, (bx, by) in enumerate(_CHIP_FLIPS):
                px, py = (1 - x if bx else x), (1 - y if by else y)
                peer = 2 * px + py
                out.append(pltpu.make_async_remote_copy(
                    src_ref=ins[t].at[peer], dst_ref=outs[t].at[peer if landing else me], send_sem=send_sems.at[t, j],
                    recv_sem=recv_sems.at[t, j], device_id=(px, py, c), device_id_type=pl.DeviceIdType.MESH))
        return out

    own = lambda t: pltpu.make_async_copy(ins[t].at[me], outs[t].at[me], local_sems.at[t])

    def start():
        for t in range(n):
            own(t).start()
        for cp in copies(False):
            cp.start()

    def finish():
        for cp in copies(True):
            cp.wait()
        for t in range(n):
            own(t).wait()

    return start, finish


def adamw(name, parts, w, m, v, tr=None):
    n_parts = parts.shape[0]
    n_rows, n_cols = w.shape
    tr = _pick(n_rows, tr or TILES["adam"], SUBLANES)
    c1 = 1.0 - ADAM_B1 ** ADAM_STEP
    c2 = 1.0 - ADAM_B2 ** ADAM_STEP

    def body(p_ref, w_ref, m_ref, v_ref, g_ref, d_ref, mo_ref, vo_ref):
        g = p_ref[0].astype(F32)
        for s in range(1, n_parts):
            g = g + p_ref[s].astype(F32)
        mn = ADAM_B1 * m_ref[...] + (1.0 - ADAM_B1) * g
        vn = ADAM_B2 * v_ref[...] + (1.0 - ADAM_B2) * (g * g)
        g_ref[...] = g
        mo_ref[...] = mn
        vo_ref[...] = vn
        d_ref[...] = -ADAM_LR * ((mn / c1) / (jnp.sqrt(vn / c2) + ADAM_EPS) + ADAM_WD * w_ref[...])

    spec = pl.BlockSpec((tr, n_cols), lambda i: (i, 0))
    return pl.pallas_call(
        body, name=name, grid=(n_rows // tr,),
        in_specs=[pl.BlockSpec((n_parts, tr, n_cols), lambda i: (0, i, 0)), spec, spec, spec],
        out_specs=[spec] * 4, out_shape=[jax.ShapeDtypeStruct(w.shape, F32)] * 4,
        compiler_params=_params("parallel"),
    )(parts, w, m, v)


def _eye_mask():
    return jnp.eye(GROUPS_PER_BLOCK, dtype=F32)


def _b_blocks(bbr, bbi):
    nb = bbr.shape[0] // (GROUPS_PER_BLOCK * SSM_STATE)
    eye = _eye_mask()[None, :, None, :, None]

    def one(z):
        z = z.reshape(nb, GROUPS_PER_BLOCK, SSM_STATE, SSM_GROUP).transpose(0, 1, 3, 2)
        return z[:, :, :, None, :] * eye

    w = jnp.stack([one(bbr), one(bbi)], axis=3)
    return w.reshape(nb, LANES, BLOCK_COLS)


def _b_blocks_t(dw):
    nb = dw.shape[0]
    d6 = dw.reshape(nb, GROUPS_PER_BLOCK, SSM_GROUP, 2, GROUPS_PER_BLOCK, SSM_STATE)
    diag = jnp.sum(d6 * _eye_mask()[None, :, None, None, :, None], axis=4)
    diag = diag.transpose(3, 0, 1, 4, 2).reshape(2, nb * GROUPS_PER_BLOCK * SSM_STATE, SSM_GROUP)
    return diag[0], diag[1]


def _c_blocks(c_re, c_im):
    nb = c_re.shape[0] // GROUPS_PER_BLOCK
    eye = _eye_mask()[None, :, None, :, None]

    def one(z):
        z = z.reshape(nb, GROUPS_PER_BLOCK, SSM_GROUP, SSM_STATE).transpose(0, 1, 3, 2)
        return z[:, :, :, None, :] * eye

    w = jnp.stack([one(c_re), -one(c_im)], axis=1)
    return w.reshape(nb, BLOCK_COLS, LANES)


def _c_blocks_t(dw):
    nb = dw.shape[0]
    d6 = dw.reshape(nb, 2, GROUPS_PER_BLOCK, SSM_STATE, GROUPS_PER_BLOCK, SSM_GROUP)
    diag = jnp.sum(d6 * _eye_mask()[None, None, :, None, :, None], axis=4)
    diag = diag.transpose(1, 0, 2, 4, 3).reshape(2, nb * GROUPS_PER_BLOCK, SSM_GROUP, SSM_STATE)
    return diag[0], -diag[1]


def _unshard_cols(g):
    s, k, n = g.shape
    return g.transpose(1, 0, 2).reshape(k, s * n)


def _shard_cols(w):
    k, n = w.shape
    return w.reshape(k, N_DEV, n // N_DEV).transpose(1, 0, 2)


def _pack(arrays):
    chunks, offs, row = [], [], 0
    for a in arrays:
        flat = a.reshape(-1).astype(F32)
        rows = -(-flat.shape[0] // LANES)
        chunks.append(jnp.pad(flat, (0, rows * LANES - flat.shape[0])))
        offs.append((row, rows))
        row += rows
    pad_rows = (-row) % SUBLANES
    if pad_rows:
        chunks.append(jnp.zeros((pad_rows * LANES,), F32))
    return jnp.concatenate(chunks).reshape(row + pad_rows, LANES), offs


def _unpack(packed, offs, shapes):
    out = []
    for (row, rows), shp in zip(offs, shapes):
        size = 1
        for s in shp:
            size *= s
        out.append(packed[row: row + rows].reshape(-1)[:size].reshape(shp))
    return out


SCAN_SUB_ROWS = 128


def _scan_tables_into(tab_ref, lam_ref, reverse):
    steps, (pr, pi) = _scan_tables(lam_ref, reverse)
    for kk, (_, mr, mi) in enumerate(steps):
        tab_ref[2 * kk] = mr
        tab_ref[2 * kk + 1] = mi
    tab_ref[6] = pr
    tab_ref[7] = pi


def _scan_in_place(buf_ref, tab_ref, carry_ref, ng, reverse, prev_ref=None, acc_ref=None, row0=0, unroll=False):
    shape = (SUBLANES, BLOCK_STATE)
    last = 0 if reverse else SUBLANES - 1
    first_row = lax.broadcasted_iota(jnp.int32, shape, 0) == 0
    re_cols, im_cols = pl.ds(0, BLOCK_STATE), pl.ds(BLOCK_STATE, BLOCK_STATE)

    def group(r, carry):
        cr, ci = carry[0], carry[1]
        rr = (ng - 1 - r) if reverse else r
        off = row0 + rr * SUBLANES
        off = off if unroll else pl.multiple_of(off, SUBLANES)
        xr, xi = buf_ref[pl.ds(off, SUBLANES), re_cols], buf_ref[pl.ds(off, SUBLANES), im_cols]
        for kk, d in enumerate((1, 2, 4)):
            sh = (SUBLANES - d) if reverse else d
            mr, mi = tab_ref[2 * kk], tab_ref[2 * kk + 1]
            yr, yi = pltpu.roll(xr, sh, 0), pltpu.roll(xi, sh, 0)
            xr, xi = xr + mr * yr - mi * yi, xi + mr * yi + mi * yr
        pr, pi = tab_ref[6], tab_ref[7]
        xr, xi = xr + pr * cr - pi * ci, xi + pr * ci + pi * cr
        buf_ref[pl.ds(off, SUBLANES), re_cols] = xr
        buf_ref[pl.ds(off, SUBLANES), im_cols] = xi
        out = (jnp.broadcast_to(xr[last:last + 1, :], shape), jnp.broadcast_to(xi[last:last + 1, :], shape))
        if prev_ref is not None:
            off8 = off + SUBLANES if unroll else pl.multiple_of(off + SUBLANES, SUBLANES)
            before_r, before_i = prev_ref[pl.ds(off, SUBLANES), re_cols], prev_ref[pl.ds(off, SUBLANES), im_cols]
            same_r, same_i = prev_ref[pl.ds(off8, SUBLANES), re_cols], prev_ref[pl.ds(off8, SUBLANES), im_cols]
            sr = jnp.where(first_row, jnp.broadcast_to(before_r[SUBLANES - 1:, :], shape), pltpu.roll(same_r, 1, 0))
            si = jnp.where(first_row, jnp.broadcast_to(before_i[SUBLANES - 1:, :], shape), pltpu.roll(same_i, 1, 0))
            out += (carry[2] + xr * sr + xi * si, carry[3] + xi * sr - xr * si)
        return out

    init = (carry_ref[0], carry_ref[1])
    if prev_ref is not None:
        init += (acc_ref[0], acc_ref[1])
    if unroll:
        res = init
        for r in range(ng):
            res = group(r, res)
    else:
        res = lax.fori_loop(0, ng, group, init)
    carry_ref[0] = res[0]
    carry_ref[1] = res[1]
    if prev_ref is not None:
        acc_ref[0] = res[2]
        acc_ref[1] = res[3]


def s5_fused_fwd(name, u, wb, wc, lam, d_row, tm=None, gather=()):
    n_rows, d_model = u.shape
    nb = lam.shape[0]
    tm = _pick(n_rows, tm or TILES["scan"], 2 * SUBLANES)
    nt = n_rows // tm
    sub = _pick(tm, SCAN_SUB_ROWS, 2 * SUBLANES)
    ng = len(gather)

    def body(u_ref, wb_ref, wc_ref, lam_ref, d_ref, *refs):
        g_in, (z_ref, st_ref), g_out = refs[:ng], refs[ng: ng + 2], refs[ng + 2: 2 * ng + 2]
        buf_ref, carry_ref, tab_ref = refs[2 * ng + 2: 2 * ng + 5]
        b, it = pl.program_id(0), pl.program_id(1)
        if ng:
            start, finish = _gather_phases(g_in, g_out, *refs[2 * ng + 5:])
            pl.when((b == 0) & (it == 0))(start)

        @pl.when(it == 0)
        def _():
            carry_ref[...] = jnp.zeros_like(carry_ref)
            _scan_tables_into(tab_ref, lam_ref, False)

        for r0 in range(0, tm, sub):
            rows = slice(r0, r0 + sub)
            uu = u_ref[rows, :]
            buf_ref[rows, :] = jnp.dot(uu.astype(BF16), wb_ref[...], preferred_element_type=F32)
            _scan_in_place(buf_ref, tab_ref, carry_ref, sub // SUBLANES, False, row0=r0, unroll=True)
            st = buf_ref[rows, :].astype(BF16)
            st_ref[rows, :] = st
            z_ref[rows, :] = _gelu(jnp.dot(st, wc_ref[...], preferred_element_type=F32) + d_ref[...] * uu).astype(z_ref.dtype)
        if ng:
            pl.when((b == nb - 1) & (it == nt - 1))(finish)

    tile = lambda b, it: (it, b)
    blk = lambda b, it: (b, 0, 0)
    any_spec = pl.BlockSpec(memory_space=pl.ANY)
    params = pltpu.CompilerParams(dimension_semantics=("arbitrary", "arbitrary"), vmem_limit_bytes=VMEM_LIMIT_BYTES,
                                  has_side_effects=True) if ng else _params("parallel", "arbitrary")
    return pl.pallas_call(
        body, name=name, grid=(nb, nt),
        in_specs=[pl.BlockSpec((tm, LANES), tile), pl.BlockSpec((None, LANES, BLOCK_COLS), blk),
                  pl.BlockSpec((None, BLOCK_COLS, LANES), blk), pl.BlockSpec((None, 2, BLOCK_STATE), blk),
                  pl.BlockSpec((1, LANES), lambda b, it: (0, b))] + [any_spec] * ng,
        out_specs=[pl.BlockSpec((tm, LANES), tile), pl.BlockSpec((tm, BLOCK_COLS), tile)] + [any_spec] * ng,
        out_shape=[jax.ShapeDtypeStruct((n_rows, d_model), BF16), jax.ShapeDtypeStruct((n_rows, nb * BLOCK_COLS), BF16)]
        + _gather_shapes(gather),
        scratch_shapes=[pltpu.VMEM((tm, BLOCK_COLS), F32), pltpu.VMEM((2, SUBLANES, BLOCK_STATE), F32),
                        pltpu.VMEM((8, SUBLANES, BLOCK_STATE), F32)] + (_gather_semaphores(ng) if ng else []),
        compiler_params=params,
    )(u, wb, wc, lam, d_row, *gather)


def s5_fused_bwd(name, dz, u, st, wb, wc, lam, d_row, tm=None, exchange=()):
    ne = len(exchange)
    n_rows, d_model = u.shape
    nb = lam.shape[0]
    tm = _pick(n_rows, tm or TILES["scan"], 2 * SUBLANES)
    nt = n_rows // tm
    tail_rows = 2 * SUBLANES
    sub = _pick(tm, SCAN_SUB_ROWS, 2 * SUBLANES)

    def body(dz_ref, u_ref, st_ref, tail_ref, wb_ref, wc_ref, lam_ref, d_ref, *refs):
        e_in, (du_ref, dwb_ref, dwc_ref, dlam_ref, dd_ref), e_out = refs[:ne], refs[ne: ne + 5], refs[ne + 5: 2 * ne + 5]
        buf_ref, prev_ref, carry_ref, acc_ref, tab_ref = refs[2 * ne + 5: 2 * ne + 10]
        b, it = pl.program_id(0), pl.program_id(1)
        if ne:
            start, finish = _chip_exchange_phases(e_in, e_out, *refs[2 * ne + 10:])
            pl.when((b == 0) & (it == 0))(start)

        @pl.when(it == 0)
        def _():
            carry_ref[...] = jnp.zeros_like(carry_ref)
            acc_ref[...] = jnp.zeros_like(acc_ref)
            dwb_ref[...] = jnp.zeros_like(dwb_ref)
            dwc_ref[...] = jnp.zeros_like(dwc_ref)
            dd_ref[...] = jnp.zeros_like(dd_ref)
            _scan_tables_into(tab_ref, lam_ref, True)

        prev_ref[SUBLANES:, :] = st_ref[...].astype(F32)
        before = tail_ref[...].astype(F32)[SUBLANES:, :]
        prev_ref[:SUBLANES, :] = jnp.where(it == nt - 1, 0.0, before)
        for r0 in range(tm - sub, -1, -sub):
            rows = slice(r0, r0 + sub)
            uu, st_b = u_ref[rows, :], st_ref[rows, :]
            y = jnp.dot(st_b, wc_ref[...], preferred_element_type=F32) + d_ref[...] * uu
            _, vjp = jax.vjp(_gelu, y)
            dy = vjp(dz_ref[rows, :])[0]
            dyb = dy.astype(BF16)
            dd_ref[...] += jnp.sum(dy * uu, axis=0, keepdims=True)
            dwc_ref[...] += lax.dot_general(st_b, dyb, _DIMS["tn"], preferred_element_type=F32)
            buf_ref[rows, :] = lax.dot_general(dyb, wc_ref[...], _DIMS["nt"], preferred_element_type=F32)
            _scan_in_place(buf_ref, tab_ref, carry_ref, sub // SUBLANES, True, prev_ref, acc_ref, row0=r0, unroll=True)
            gb = buf_ref[rows, :].astype(BF16)
            du_ref[rows, :] = lax.dot_general(gb, wb_ref[...], _DIMS["nt"], preferred_element_type=F32) + dy * d_ref[...]
            dwb_ref[...] += lax.dot_general(uu.astype(BF16), gb, _DIMS["tn"], preferred_element_type=F32)

        @pl.when(it == nt - 1)
        def _():
            dlam_ref[0:1, :] = jnp.sum(acc_ref[0], axis=0, keepdims=True)
            dlam_ref[1:2, :] = jnp.sum(acc_ref[1], axis=0, keepdims=True)

        if ne:
            pl.when((b == nb - 1) & (it == nt - 1))(finish)

    any_spec = pl.BlockSpec(memory_space=pl.ANY)
    params = pltpu.CompilerParams(dimension_semantics=("arbitrary", "arbitrary"), vmem_limit_bytes=VMEM_LIMIT_BYTES,
                                  has_side_effects=True) if ne else _params("parallel", "arbitrary")
    tile = lambda b, it: (nt - 1 - it, b)
    blk = lambda b, it: (b, 0, 0)
    per_tile = tm // tail_rows
    tail = lambda b, it: (jnp.maximum((nt - 1 - it) * per_tile - 1, 0), b)
    return pl.pallas_call(
        body, name=name, grid=(nb, nt),
        in_specs=[pl.BlockSpec((tm, LANES), tile), pl.BlockSpec((tm, LANES), tile), pl.BlockSpec((tm, BLOCK_COLS), tile),
                  pl.BlockSpec((tail_rows, BLOCK_COLS), tail), pl.BlockSpec((None, LANES, BLOCK_COLS), blk),
                  pl.BlockSpec((None, BLOCK_COLS, LANES), blk), pl.BlockSpec((None, 2, BLOCK_STATE), blk),
                  pl.BlockSpec((1, LANES), lambda b, it: (0, b))] + [any_spec] * ne,
        out_specs=[pl.BlockSpec((tm, LANES), tile), pl.BlockSpec((None, LANES, BLOCK_COLS), blk),
                   pl.BlockSpec((None, BLOCK_COLS, LANES), blk), pl.BlockSpec((None, 2, BLOCK_STATE), blk),
                   pl.BlockSpec((1, LANES), lambda b, it: (0, b))] + [any_spec] * ne,
        out_shape=[jax.ShapeDtypeStruct((n_rows, d_model), F32), jax.ShapeDtypeStruct((nb, LANES, BLOCK_COLS), F32),
                   jax.ShapeDtypeStruct((nb, BLOCK_COLS, LANES), F32), jax.ShapeDtypeStruct((nb, 2, BLOCK_STATE), F32),
                   jax.ShapeDtypeStruct((1, d_model), F32)] + [jax.ShapeDtypeStruct(t.shape, t.dtype) for t in exchange],
        scratch_shapes=[pltpu.VMEM((tm, BLOCK_COLS), F32), pltpu.VMEM((tm + SUBLANES, BLOCK_COLS), F32),
                        pltpu.VMEM((2, SUBLANES, BLOCK_STATE), F32), pltpu.VMEM((2, SUBLANES, BLOCK_STATE), F32),
                        pltpu.VMEM((8, SUBLANES, BLOCK_STATE), F32)] + (_chip_exchange_semaphores(ne) if ne else []),
        compiler_params=params,
    )(dz, u, st, st, wb, wc, lam, d_row, *exchange)


def s5_fwd(tag, u, log_dt, a_re, a_im, b_re, b_im, c_re, c_im, d_row, gather=()):
    d_model = u.shape[1]
    nb = d_model // LANES
    col = lambda a: a.reshape(-1, 1)
    prm = [col(jnp.repeat(log_dt, SSM_STATE)), col(a_re), col(a_im), b_re.reshape(-1, SSM_GROUP), b_im.reshape(-1, SSM_GROUP)]
    lr, li, bbr, bbi = rowcall(f"s5_prep_{tag}", _s5_discretise, prm, [], [(1, F32), (1, F32), (SSM_GROUP, F32), (SSM_GROUP, F32)])
    lam = jnp.stack([lr.reshape(nb, BLOCK_STATE), li.reshape(nb, BLOCK_STATE)], axis=1)
    wb = _b_blocks(bbr, bbi).astype(BF16)
    wc = _c_blocks(c_re, c_im).astype(BF16)
    z, st, *gathered = s5_fused_fwd(f"s5_fwd_{tag}", u, wb, wc, lam, d_row, gather=gather)
    return z, dict(prm=prm, lam=lam, wb=wb, wc=wc, st=st), gathered


def s5_bwd(tag, dz, u, d_row, sv, exchange=()):
    d_model = u.shape[1]
    nb = d_model // LANES
    n_groups = d_model // SSM_GROUP
    col = lambda a: a.reshape(-1, 1)

    dhn, d_wb, d_wc, d_lam, d_dskip, *exchanged = s5_fused_bwd(
        f"s5_bwd_{tag}", dz, u, sv["st"], sv["wb"], sv["wc"], sv["lam"], d_row, exchange=exchange)
    d_bbr, d_bbi = _b_blocks_t(d_wb)
    d_cre, d_cim = _c_blocks_t(d_wc)

    def prep_bwd(ldt, ar, ai, br, bi, dlr, dli, dbr, dbi):
        _, vjp = jax.vjp(_s5_discretise, ldt, ar, ai, br, bi)
        return vjp((dlr, dli, dbr, dbi))

    d_ldt, d_are, d_aim, d_bre, d_bim = rowcall(
        f"s5_prep_bwd_{tag}", prep_bwd, sv["prm"] + [col(d_lam[:, 0]), col(d_lam[:, 1]), d_bbr, d_bbi], [],
        [(1, F32), (1, F32), (1, F32), (SSM_GROUP, F32), (SSM_GROUP, F32)])
    d_logdt = rowcall(f"s5_dlogdt_{tag}", lambda a: jnp.sum(a, axis=1, keepdims=True), [d_ldt.reshape(n_groups, SSM_STATE)], [], [(1, F32)])[0]
    grads = dict(log_dt=d_logdt.reshape(n_groups), a_re=d_are.reshape(n_groups, SSM_STATE), a_im=d_aim.reshape(n_groups, SSM_STATE),
                 b_re=d_bre.reshape(n_groups, SSM_STATE, SSM_GROUP), b_im=d_bim.reshape(n_groups, SSM_STATE, SSM_GROUP),
                 c_re=d_cre, c_im=d_cim, d=d_dskip)
    return dhn, grads, exchanged


def kernel(x, mix_norm, mlp_norm, mlp_w1, mlp_w2, ssm_log_dt, ssm_a_re, ssm_a_im, ssm_b_re, ssm_b_im, ssm_c_re, ssm_c_im, ssm_d, ssm_w_glu, kv_norm, w_kvf, b_f, attn_wq, attn_wo, final_norm, loss_target, m_mix_norm, m_mlp_norm, m_mlp_w1, m_mlp_w2, m_ssm_log_dt, m_ssm_a_re, m_ssm_a_im, m_ssm_b_re, m_ssm_b_im, m_ssm_c_re, m_ssm_c_im, m_ssm_d, m_ssm_w_glu, m_kv_norm, m_w_kvf, m_b_f, m_attn_wq, m_attn_wo, m_final_norm, v_mix_norm, v_mlp_norm, v_mlp_w1, v_mlp_w2, v_ssm_log_dt, v_ssm_a_re, v_ssm_a_im, v_ssm_b_re, v_ssm_b_im, v_ssm_c_re, v_ssm_c_im, v_ssm_d, v_ssm_w_glu, v_kv_norm, v_w_kvf, v_b_f, v_attn_wq, v_attn_wo, v_final_norm):
    n_rows, d_model = x.shape[1], x.shape[2]
    depth = mix_norm.shape[0]
    n_a = ssm_log_dt.shape[0]
    n_b = depth - n_a
    n_heads = d_model // HEAD_DIM
    n_groups = d_model // SSM_GROUP
    nb = n_groups // GROUPS_PER_BLOCK
    kvf_cols = 2 * d_model + n_heads
    kvf_pad = 2 * d_model + LANES

    (g_d,) = all_gather("gather_skip_gain", [ssm_d])
    d_skip = [g_d[:, i].reshape(1, d_model) for i in range(n_a)]
    sharded = [mlp_w1.astype(BF16), mlp_w2.astype(BF16), ssm_w_glu.astype(BF16), w_kvf.astype(BF16),
               attn_wq.astype(BF16), attn_wo.astype(BF16)]
    w1 = w2 = wglu = wkvf = wq = wo = None

    row = lambda a: a.reshape(1, -1)
    col = lambda a: a.reshape(-1, 1)

    h = x[0]
    saved = []
    k_ext = v_ext = None
    q_consts = _lane_consts([(F_LANE, F_LANE + 3, 1.0)])
    k_consts = _lane_consts([(LSE_LANE, LSE_LANE + 3, -1.0), (ROWSUM_LANE, ROWSUM_LANE + 1, 1.0)])
    v_consts = _lane_consts([(SUM_LANE, SUM_LANE + 1, 1.0), (DELTA_LANE, DELTA_LANE + 3, -1.0)])
    for i in range(depth):
        sv = {"h": h}
        hn = rms_fwd(f"mix_norm_{i}", h, row(mix_norm[i]), F32 if i < n_a else BF16)
        sv["hn"] = hn
        if i < n_a:
            z, s5_saved, gathered = s5_fwd(str(i), hn, ssm_log_dt[i], ssm_a_re[i], ssm_a_im[i], ssm_b_re[i], ssm_b_im[i],
                                           ssm_c_re[i], ssm_c_im[i], d_skip[i], gather=sharded if i == 0 else ())
            if i == 0:
                g_w1, g_w2, g_glu, g_kvf, g_wq, g_wo = gathered
                w1 = [_unshard_cols(g_w1[:, li]) for li in range(depth)]
                w2 = [g_w2[:, li].reshape(-1, d_model) for li in range(depth)]
                wglu = [_unshard_cols(g_glu[:, li]) for li in range(n_a)]
                wkvf = jnp.pad(_unshard_cols(g_kvf), ((0, 0), (0, kvf_pad - kvf_cols)))
                wq = [g_wq[:, li].reshape(-1, d_model) for li in range(n_b)]
                wo = [g_wo[:, li].reshape(-1, d_model) for li in range(n_b)]
            zw = matmul(f"s5_glu_{i}", z, wglu[i])
            def gate(hh, zz, gg):
                out = hh + zz[:, :d_model] * _sigmoid(zz[:, d_model:])
                return out, _rms(out, gg)

            h1, h2n = rowcall(f"s5_gate_{i}", gate, [h, zw], [row(mlp_norm[i])], [(d_model, F32), (d_model, BF16)])
            sv.update(s5=s5_saved, z=z, zw=zw)
        else:
            j = i - n_a
            q = matmul(f"attn_q_{j}", hn, wq[j], scale=LOG2E * HEAD_DIM ** -0.5, out_dtype=BF16)
            o_ext, q_ext_b = fox_fwd(f"attn_fwd_{j}", pack_heads(f"pack_q_{j}", q, 0, n_heads, q_consts), k_ext, v_ext)
            o2 = unpack_heads(f"unpack_o_{j}", [o_ext])[0]
            h1 = matmul(f"attn_o_{j}", o2, wo[j], resid=h)
            sv.update(q_ext_b=q_ext_b, o2=o2)
            h2n = rms_fwd(f"mlp_norm_{i}", h1, row(mlp_norm[i]), BF16)
        ap = matmul(f"mlp_up_{i}", h2n, w1[i], out_dtype=BF16)
        h = matmul(f"mlp_down_{i}", ap, w2[i], a_fn=_sqrelu, resid=h1, bk=2048)
        sv.update(h1=h1, h2n=h2n, ap=ap)
        saved.append(sv)
        if i == n_a - 1:
            h_mid = h
            hk = rms_fwd("kv_norm", h, row(kv_norm), BF16)
            kvf = matmul("kvf_proj", hk, wkvf, bn=kvf_pad)
            fl = kvf[:, 2 * d_model:]
            bfp = jnp.pad(row(b_f), ((0, 0), (0, LANES - n_heads)))
            k_ext = pack_heads("pack_k", kvf, 0, n_heads, k_consts, cum_logf("cum_logf", fl, bfp), F_LANE)
            v_ext = pack_heads("pack_v", kvf, 1, n_heads, v_consts)

    def loss_fn(hh, tgt, g):
        y, vjp = jax.vjp(_rms, hh, g)
        err = y - tgt
        part = 0.5 * jnp.sum(jnp.mean(err * err, axis=-1, keepdims=True), axis=0, keepdims=True)
        dh, dg = vjp(err * (1.0 / d_model))
        return dh, jnp.broadcast_to(part, (1, LANES)), dg

    dh, loss_part, d_final = rowcall("loss_head", loss_fn, [h, loss_target[0]], [row(final_norm)],
                                     [(d_model, F32)], [(1, LANES), (1, d_model)])

    g_mix, g_mlpn = [None] * depth, [None] * depth
    g_w1f, g_w2f = [None] * depth, [None] * depth
    g_ssm = [None] * n_a
    g_wqf, g_wof = [None] * n_b, [None] * n_b
    dk_acc, dv_acc, df_plus = [], [], []
    parts = None
    stack = lambda xs: jnp.stack(xs, axis=1).astype(BF16 if xs[0].ndim == 3 else F32)

    def reduce_pairs(tag, contributions):
        from_sibling = pair_swap(f"pair_swap_{tag}", contributions)
        core = lax.axis_index("c")
        sums = []
        for k, (mine, theirs) in enumerate(zip(contributions, from_sibling)):
            mine = lax.dynamic_index_in_dim(mine.reshape((N_CHIPS, 2) + mine.shape[1:]), core, axis=1, keepdims=False)
            cols = mine.shape[-1]
            sums.append(rowcall(f"pair_add_{tag}_{k}", lambda a, b: a.astype(F32) + b.astype(F32),
                                [mine.reshape(-1, cols), theirs.reshape(-1, cols)], [], [(cols, mine.dtype)])[0].reshape(mine.shape))
        return sums
    g_kv = None
    for i in reversed(range(depth)):
        sv = saved[i]
        if i == n_a - 1:
            dk, col_sums = unpack_heads("unpack_dk", dk_acc, extract_lane=HEAD_DIM)
            dv = unpack_heads("unpack_dv", dv_acc)[0]
            dfl, db_f = cum_logf_bwd("cum_logf_bwd", fl, bfp, df_plus, [col_sums])
            dkvf = jnp.concatenate([dk, dv, dfl], axis=1).astype(BF16)
            d_wkvf = matmul("kvf_dw", hk, dkvf, "tn", bm=512, bn=kvf_pad)
            dh, d_kvn = matmul("kvf_dx", dkvf, wkvf, "nt", bm=512, bk=kvf_pad, rms_bwd_of=(h_mid, row(kv_norm), dh))
            g_kv = (d_wkvf[:, :kvf_cols], d_kvn, db_f[:, :n_heads])
        dap = matmul(f"mlp_down_dx_{i}", dh, w2[i], "nt", post=lambda acc, apt: acc * (2.0 * jnp.maximum(apt.astype(F32), 0.0)),
                     post_arg=sv["ap"], out_dtype=BF16)
        g_w2f[i] = matmul(f"mlp_down_dw_{i}", sv["ap"], dh, "tn", a_fn=_sqrelu)
        g_w1f[i] = matmul(f"mlp_up_dw_{i}", sv["h2n"], dap, "tn")
        dh1, g_mlpn[i] = matmul(f"mlp_up_dx_{i}", dap, w1[i], "nt", bm=512, bk=2048, rms_bwd_of=(sv["h1"], row(mlp_norm[i]), dh))
        if i < n_a:
            def glu_bwd(zz, dd):
                val, gate = zz[:, :d_model], zz[:, d_model:]
                sg = _sigmoid(gate)
                return jnp.concatenate([dd * sg, dd * val * sg * (1.0 - sg)], axis=1)

            dzw = rowcall(f"s5_gate_bwd_{i}", glu_bwd, [sv["zw"], dh1], [], [(2 * d_model, BF16)])[0]
            dz = matmul(f"s5_glu_dx_{i}", dzw, wglu[i], "nt")
            d_wglu = matmul(f"s5_glu_dw_{i}", sv["z"], dzw, "tn")

            pair_sums = ()
            if i == 0:
                glu_grads = [d_wglu] + [g_ssm[li]["w_glu"] for li in range(1, n_a)]
                pair_sums = reduce_pairs("grads", [
                    stack([_shard_cols(g) for g in g_w1f]),
                    stack([g.reshape(N_DEV, -1, d_model) for g in g_w2f]),
                    stack([_shard_cols(g) for g in glu_grads]),
                    _shard_cols(g_kv[0]).astype(BF16),
                    stack([g.reshape(N_DEV, -1, d_model) for g in g_wqf]),
                    stack([g.reshape(N_DEV, -1, d_model) for g in g_wof]),
                ])
            dhn, g_ssm[i], exchanged = s5_bwd(str(i), dz, sv["hn"], d_skip[i], sv["s5"], exchange=pair_sums)
            g_ssm[i]["w_glu"] = d_wglu
            if i == 0:
                parts = exchanged
        else:
            j = i - n_a
            do2 = matmul(f"attn_o_dx_{j}", dh1, wo[j], "nt")
            g_wof[j] = matmul(f"attn_o_dw_{j}", sv["o2"], dh1, "tn")
            do_ext = pack_heads(f"pack_do_{j}", do2, 0, n_heads, jnp.zeros((1, LANES), F32),
                                attn_delta(f"attn_delta_{j}", sv["o2"], do2), DELTA_LANE)
            dq_ext, dk_ext, dv_ext = fox_bwd(f"attn_bwd_{j}", sv["q_ext_b"], do_ext, k_ext, v_ext)
            dk_acc.append(dk_ext)
            dv_acc.append(dv_ext)
            dq2, row_sums = unpack_heads(f"unpack_dq_{j}", [dq_ext], extract_lane=ROWSUM_LANE)
            df_plus.append(row_sums)
            g_wqf[j] = matmul(f"attn_q_dw_{j}", sv["hn"], dq2, "tn", scale=HEAD_DIM ** -0.5)
            dh, g_mix[i] = matmul(f"attn_q_dx_{j}", dq2, wq[j], "nt", scale=HEAD_DIM ** -0.5, bm=512,
                                  rms_bwd_of=(sv["h"], row(mix_norm[i]), dh1))
        if i < n_a:
            dh, g_mix[i] = rms_bwd(f"mix_norm_bwd_{i}", sv["h"], row(mix_norm[i]), dhn, add=dh1)
    grad_x = dh[None]

    parts = list(parts) + list(chip_exchange("chip_exchange_skip_gain", reduce_pairs("skip_gain", [
        stack([g["d"].reshape(N_DEV, -1) for g in g_ssm])])))
    ssm_g = lambda kk: jnp.stack([g[kk] for g in g_ssm])
    loss_slot = jnp.zeros((LANES,), F32)
    small = {
        "f32": (["mix_norm", "mlp_norm", "ssm_log_dt", "ssm_a_re", "ssm_a_im", "kv_norm", "b_f", "final_norm"],
                [jnp.concatenate(g_mix, axis=0), jnp.concatenate(g_mlpn, axis=0), ssm_g("log_dt"), ssm_g("a_re"), ssm_g("a_im"),
                 g_kv[1], g_kv[2], d_final, loss_part],
                [mix_norm, mlp_norm, ssm_log_dt, ssm_a_re, ssm_a_im, kv_norm, b_f, final_norm, loss_slot],
                [m_mix_norm, m_mlp_norm, m_ssm_log_dt, m_ssm_a_re, m_ssm_a_im, m_kv_norm, m_b_f, m_final_norm, loss_slot],
                [v_mix_norm, v_mlp_norm, v_ssm_log_dt, v_ssm_a_re, v_ssm_a_im, v_kv_norm, v_b_f, v_final_norm, loss_slot]),
        "bf16": (["ssm_b_re", "ssm_b_im", "ssm_c_re", "ssm_c_im"],
                 [ssm_g("b_re"), ssm_g("b_im"), ssm_g("c_re"), ssm_g("c_im")],
                 [ssm_b_re, ssm_b_im, ssm_c_re, ssm_c_im], [m_ssm_b_re, m_ssm_b_im, m_ssm_c_re, m_ssm_c_im],
                 [v_ssm_b_re, v_ssm_b_im, v_ssm_c_re, v_ssm_c_im]),
    }
    packed = {kk: [_pack(arrs) for arrs in grp[1:]] for kk, grp in small.items()}
    small_parts = all_gather("gather_small_grads", [packed["f32"][0][0], packed["bf16"][0][0].astype(BF16)])

    res = {}

    def update(nm, part, w, m, v):
        shp = w.shape
        as2d = lambda a: a.reshape(-1, shp[-1])
        outs = adamw(f"adamw_{nm}", part.reshape(part.shape[:1] + as2d(w).shape), as2d(w), as2d(m), as2d(v))
        res[nm] = [o.reshape(shp) for o in outs]

    update("mlp_w1", parts[0], mlp_w1, m_mlp_w1, v_mlp_w1)
    update("mlp_w2", parts[1], mlp_w2, m_mlp_w2, v_mlp_w2)
    update("ssm_w_glu", parts[2], ssm_w_glu, m_ssm_w_glu, v_ssm_w_glu)
    update("w_kvf", parts[3], w_kvf, m_w_kvf, v_w_kvf)
    update("attn_wq", parts[4], attn_wq, m_attn_wq, v_attn_wq)
    update("attn_wo", parts[5], attn_wo, m_attn_wo, v_attn_wo)
    update("ssm_d", parts[6], ssm_d, m_ssm_d, v_ssm_d)
    loss = None
    for (kk, (names, _, ws, _, _)), part in zip(small.items(), small_parts):
        (_, offs), (pw, _), (pm, _), (pv, _) = packed[kk]
        small_out = adamw(f"adamw_small_{kk}", part, pw, pm, pv)
        unpacked = [_unpack(o, offs, [w.shape for w in ws]) for o in small_out]
        for idx, nm in enumerate(names):
            res[nm] = [u[idx] for u in unpacked]
        if kk == "f32":
            loss = unpacked[0][-1][0]

    order = ["mix_norm", "mlp_norm", "mlp_w1", "mlp_w2", "ssm_log_dt", "ssm_a_re", "ssm_a_im", "ssm_b_re", "ssm_b_im", "ssm_c_re",
             "ssm_c_im", "ssm_d", "ssm_w_glu", "kv_norm", "w_kvf", "b_f", "attn_wq", "attn_wo", "final_norm"]
    out = [loss, grad_x]
    for kind in range(4):
        out += [res[nm][kind] for nm in order]
    return tuple(out)
```

```python
import functools

import jax
import jax.numpy as jnp
from jax import lax
from jax.experimental import pallas as pl
from jax.experimental.pallas import tpu as pltpu

F32 = jnp.float32
BF16 = jnp.bfloat16
HIGHEST = lax.Precision.HIGHEST

V7X_VMEM_BYTES = 64 << 20
VMEM_LIMIT_BYTES = (V7X_VMEM_BYTES * 3) // 4
LANES = 128
SUBLANES = 8

N_DEV = 8
RMS_EPS = 1e-6
SSM_GROUP = 16
SSM_STATE = 64
HEAD_DIM = 64
GROUPS_PER_BLOCK = LANES // SSM_GROUP
BLOCK_STATE = GROUPS_PER_BLOCK * SSM_STATE
BLOCK_COLS = 2 * BLOCK_STATE
NEG_BIG = -1e30
LOG2E = 1.4426950408889634

ADAM_LR = 0.001
ADAM_B1 = 0.9
ADAM_B2 = 0.999
ADAM_EPS = 1e-08
ADAM_WD = 0.01
ADAM_STEP = 10

TILES = {"row": 512, "mm": (1024, 1024, 1024), "blk": 512, "scan": 512, "cum": 256, "attn": 1024, "adam": 256}


def _pick(dim, pref, align=LANES):
    if dim <= pref:
        return dim
    for a in (align, SUBLANES):
        d = (pref // a) * a
        while d >= a:
            if dim % d == 0:
                return d
            d -= a
    return dim


def _params(*sem):
    return pltpu.CompilerParams(dimension_semantics=sem, vmem_limit_bytes=VMEM_LIMIT_BYTES)


def rowcall(name, fn, rows, consts, out_rows, out_accs=(), tm=None):
    n_rows = rows[0].shape[0]
    tm = _pick(n_rows, tm or TILES["row"], SUBLANES)
    nr, nc, no, na = len(rows), len(consts), len(out_rows), len(out_accs)

    def body(*refs):
        ins = [r[...] for r in refs[: nr + nc]]
        outs = fn(*ins)
        if not isinstance(outs, (tuple, list)):
            outs = (outs,)
        for r, o in zip(refs[nr + nc: nr + nc + no], outs[:no]):
            r[...] = o.astype(r.dtype)
        if na:
            i = pl.program_id(0)
            for r, o in zip(refs[nr + nc + no:], outs[no:]):
                @pl.when(i == 0)
                def _(r=r, o=o):
                    r[...] = o

                @pl.when(i > 0)
                def _(r=r, o=o):
                    r[...] += o

    in_specs = [pl.BlockSpec((tm, a.shape[1]), lambda i: (i, 0)) for a in rows]
    in_specs += [pl.BlockSpec(c.shape, lambda i, n=c.ndim: (0,) * n) for c in consts]
    out_shape = [jax.ShapeDtypeStruct((n_rows, c), dt) for c, dt in out_rows]
    out_specs = [pl.BlockSpec((tm, c), lambda i: (i, 0)) for c, _ in out_rows]
    out_shape += [jax.ShapeDtypeStruct(s, F32) for s in out_accs]
    out_specs += [pl.BlockSpec(s, lambda i, n=len(s): (0,) * n) for s in out_accs]
    res = pl.pallas_call(
        body, name=name, grid=(n_rows // tm,), in_specs=in_specs, out_specs=out_specs, out_shape=out_shape,
        compiler_params=_params("arbitrary" if na else "parallel"),
    )(*rows, *consts)
    return res


_DIMS = {"nn": (((1,), (0,)), ((), ())), "nt": (((1,), (1,)), ((), ())), "tn": (((0,), (0,)), ((), ()))}


def matmul(name, a, b, mode="nn", *, a_fn=None, scale=None, resid=None, post=None, post_arg=None, rms_bwd_of=None,
           out_dtype=F32, bm=None, bn=None, bk=None):
    if mode == "nn":
        (m, k), (k2, n) = a.shape, b.shape
    elif mode == "nt":
        (m, k), (n, k2) = a.shape, b.shape
    else:
        (k, m), (k2, n) = a.shape, b.shape
    assert k == k2, (name, a.shape, b.shape, mode)
    if rms_bwd_of is not None:
        bn = n
    bm, bn, bk = _pick(m, bm or TILES["mm"][0]), _pick(n, bn or TILES["mm"][1]), _pick(k, bk or TILES["mm"][2])
    nk = k // bk
    a_spec = pl.BlockSpec((bk, bm), lambda i, j, kk: (kk, i)) if mode == "tn" else pl.BlockSpec((bm, bk), lambda i, j, kk: (i, kk))
    b_spec = pl.BlockSpec((bn, bk), lambda i, j, kk: (j, kk)) if mode == "nt" else pl.BlockSpec((bk, bn), lambda i, j, kk: (kk, j))
    mn_spec = pl.BlockSpec((bm, bn), lambda i, j, kk: (i, j))
    extra = [x for x in (resid, post_arg) if x is not None]
    has_resid, has_post, has_rms = resid is not None, post is not None, rms_bwd_of is not None
    row_spec = pl.BlockSpec((1, bn), lambda i, j, kk: (0, j))
    extra_specs = [mn_spec] * len(extra)
    if has_rms:
        x_in, gain, add = rms_bwd_of
        extra += [x_in, add, gain]
        extra_specs += [mn_spec, mn_spec, row_spec]

    def body(*refs):
        a_ref, b_ref = refs[0], refs[1]
        ex = refs[2: 2 + len(extra)]
        o_ref = refs[2 + len(extra)]
        av = a_ref[...]
        if a_fn is not None:
            av = a_fn(av.astype(F32))
        p = lax.dot_general(av.astype(BF16), b_ref[...].astype(BF16), _DIMS[mode], preferred_element_type=F32)

        def finish(acc):
            if scale is not None:
                acc = acc * scale
            idx = 0
            if has_resid:
                acc = acc + ex[idx][...]
                idx += 1
            if has_post:
                acc = post(acc, ex[idx][...])
                idx += 1
            if has_rms:
                _, vjp = jax.vjp(_rms, ex[idx][...], ex[idx + 2][...])
                dx, d_gain = vjp(acc)
                acc = dx + ex[idx + 1][...]
                dg_ref = refs[3 + len(extra)]
                first = pl.program_id(0) == 0

                @pl.when(first)
                def _():
                    dg_ref[...] = d_gain

                @pl.when(jnp.logical_not(first))
                def _():
                    dg_ref[...] += d_gain
            o_ref[...] = acc.astype(o_ref.dtype)

        if nk == 1:
            finish(p)
        else:
            acc_ref = refs[-1]
            kk = pl.program_id(2)

            @pl.when(kk == 0)
            def _():
                acc_ref[...] = p

            @pl.when(kk > 0)
            def _():
                acc_ref[...] += p

            @pl.when(kk == nk - 1)
            def _():
                finish(acc_ref[...])

    out_shape, out_specs = jax.ShapeDtypeStruct((m, n), out_dtype), mn_spec
    if has_rms:
        out_shape, out_specs = [out_shape, jax.ShapeDtypeStruct((1, n), F32)], [mn_spec, row_spec]
    return pl.pallas_call(
        body, name=name, grid=(m // bm, n // bn, nk),
        in_specs=[a_spec, b_spec] + extra_specs, out_specs=out_specs, out_shape=out_shape,
        scratch_shapes=[pltpu.VMEM((bm, bn), F32)] if nk > 1 else [],
        compiler_params=_params(*(("arbitrary",) * 3 if has_rms else ("parallel", "parallel", "arbitrary"))),
    )(a, b, *extra)


def _cmul(ar, ai, br, bi):
    return ar * br - ai * bi, ar * bi + ai * br


def _scan_tables(lam_ref, reverse):
    shape = (SUBLANES, BLOCK_STATE)
    lr = jnp.broadcast_to(lam_ref[0:1, :], shape)
    li = jnp.broadcast_to(lam_ref[1:2, :], shape)
    if reverse:
        li = -li
    row = lax.broadcasted_iota(jnp.int32, shape, 0)
    tt = (SUBLANES - 1 - row) if reverse else row
    l1 = (lr, li)
    l2 = _cmul(*l1, *l1)
    l4 = _cmul(*l2, *l2)
    pr, pi = l1
    for bit, lp in enumerate((l1, l2, l4)):
        qr, qi = _cmul(pr, pi, *lp)
        on = ((tt >> bit) & 1) == 1
        pr, pi = jnp.where(on, qr, pr), jnp.where(on, qi, pi)
    steps = []
    for d, lp in ((1, l1), (2, l2), (4, l4)):
        ok = tt >= d
        steps.append((d, jnp.where(ok, lp[0], 0.0), jnp.where(ok, lp[1], 0.0)))
    return steps, (pr, pi)


def _rms(x, g):
    return x * lax.rsqrt(jnp.mean(x * x, axis=-1, keepdims=True) + RMS_EPS) * g


def _sigmoid(x):
    return 1.0 / (1.0 + jnp.exp(-x))


def _gelu(x):
    return 0.5 * x * (1.0 + jnp.tanh(0.7978845608028654 * (x + 0.044715 * (x * x * x))))


def _log_sigmoid(x):
    return jnp.minimum(x, 0.0) - jnp.log(1.0 + jnp.exp(-jnp.abs(x)))


def _sqrelu(x):
    r = jnp.maximum(x, 0.0)
    return r * r


def _s5_discretise(ldt, ar, ai, br, bi):
    dt = jnp.exp(ldt)
    er = jnp.exp(ar * dt)
    lr, li = er * jnp.cos(ai * dt), er * jnp.sin(ai * dt)
    nr, ni = lr - 1.0, li
    den = ar * ar + ai * ai
    cr, ci = (nr * ar + ni * ai) / den, (ni * ar - nr * ai) / den
    return lr, li, cr * br - ci * bi, cr * bi + ci * br


def rms_fwd(name, x, g, dtype=F32):
    return rowcall(name, _rms, [x], [g], [(x.shape[1], dtype)])[0]


def rms_bwd(name, x, g, dy, add=None):
    def fn(x, dy, *rest):
        g = rest[-1]
        _, vjp = jax.vjp(_rms, x, g)
        dx, dg = vjp(dy)
        if add is not None:
            dx = dx + rest[0]
        return dx, dg

    rows = [x, dy] + ([add] if add is not None else [])
    return rowcall(name, fn, rows, [g], [(x.shape[1], F32)], [g.shape])


def _split3(x):
    hi = x.astype(BF16).astype(F32)
    r = x - hi
    mid = r.astype(BF16).astype(F32)
    return hi, mid, (r - mid).astype(BF16).astype(F32)


def cum_logf(name, fl, bf, tm=None):
    n_rows, w = fl.shape
    tm = _pick(n_rows, tm or TILES["cum"], SUBLANES)

    def body(fl_ref, bf_ref, hi_ref, mid_ref, lo_ref, carry_ref):
        it = pl.program_id(0)

        @pl.when(it == 0)
        def _():
            carry_ref[...] = jnp.zeros_like(carry_ref)

        ls = _log_sigmoid(fl_ref[...] + bf_ref[...])
        tri = (lax.broadcasted_iota(jnp.int32, (tm, tm), 0) >= lax.broadcasted_iota(jnp.int32, (tm, tm), 1)).astype(F32)
        c = jnp.dot(tri, ls, precision=HIGHEST, preferred_element_type=F32) + carry_ref[0:1, :]
        carry_ref[...] = jnp.broadcast_to(c[tm - 1:tm, :], carry_ref.shape)
        hi_ref[...], mid_ref[...], lo_ref[...] = _split3(c * (-LOG2E))

    spec = pl.BlockSpec((tm, w), lambda i: (i, 0))
    return pl.pallas_call(
        body, name=name, grid=(n_rows // tm,),
        in_specs=[spec, pl.BlockSpec((1, w), lambda i: (0, 0))],
        out_specs=[spec] * 3, out_shape=[jax.ShapeDtypeStruct((n_rows, w), F32)] * 3,
        scratch_shapes=[pltpu.VMEM((SUBLANES, w), F32)],
        compiler_params=_params("arbitrary"),
    )(fl, bf)


def cum_logf_bwd(name, fl, bf, plus, minus, tm=None):
    n_rows, w = fl.shape
    tm = _pick(n_rows, tm or TILES["cum"], SUBLANES)
    nt = n_rows // tm
    n_p, n_m = len(plus), len(minus)

    def body(*refs):
        fl_ref, bf_ref = refs[0], refs[1]
        d_refs = refs[2: 2 + n_p + n_m]
        o_ref, db_ref, carry_ref = refs[2 + n_p + n_m:]
        it = pl.program_id(0)

        @pl.when(it == 0)
        def _():
            carry_ref[...] = jnp.zeros_like(carry_ref)

        d = None
        for r in d_refs[:n_p]:
            d = r[...] if d is None else d + r[...]
        for r in d_refs[n_p:]:
            d = -r[...] if d is None else d - r[...]
        tri = (lax.broadcasted_iota(jnp.int32, (tm, tm), 0) <= lax.broadcasted_iota(jnp.int32, (tm, tm), 1)).astype(F32)
        c = jnp.dot(tri, d, precision=HIGHEST, preferred_element_type=F32) + carry_ref[0:1, :]
        carry_ref[...] = jnp.broadcast_to(c[0:1, :], carry_ref.shape)
        dfl = c * _sigmoid(-(fl_ref[...] + bf_ref[...]))
        o_ref[...] = dfl
        part = jnp.sum(dfl, axis=0, keepdims=True)

        @pl.when(it == 0)
        def _():
            db_ref[...] = part

        @pl.when(it > 0)
        def _():
            db_ref[...] += part

    rev = lambda i: (nt - 1 - i, 0)
    return pl.pallas_call(
        body, name=name, grid=(nt,),
        in_specs=[pl.BlockSpec((tm, w), rev), pl.BlockSpec((1, w), lambda i: (0, 0))] + [pl.BlockSpec((tm, w), rev)] * (n_p + n_m),
        out_specs=[pl.BlockSpec((tm, w), rev), pl.BlockSpec((1, w), lambda i: (0, 0))],
        out_shape=[jax.ShapeDtypeStruct((n_rows, w), F32), jax.ShapeDtypeStruct((1, w), F32)],
        scratch_shapes=[pltpu.VMEM((SUBLANES, w), F32)],
        compiler_params=_params("arbitrary"),
    )(fl, bf, *plus, *minus)


ROWSUM_LANE = HEAD_DIM + 6
F_LANE = HEAD_DIM
LSE_LANE = HEAD_DIM + 3
SUM_LANE = HEAD_DIM
DELTA_LANE = HEAD_DIM + 1


def _lane_consts(pairs):
    lane = lax.broadcasted_iota(jnp.int32, (1, LANES), 1)
    out = jnp.zeros((1, LANES), F32)
    for lo, hi, v in pairs:
        out = jnp.where((lane >= lo) & (lane < hi), v, out)
    return out


def pack_heads(name, x, col_block, n_heads, consts, parts=(), parts_lane=0, tm=None):
    n_rows = x.shape[0]
    d = n_heads * HEAD_DIM
    assert n_heads % 2 == 0
    tm = _pick(n_rows, tm or TILES["row"], 2 * SUBLANES)
    n_parts = len(parts)

    def body(*refs):
        x_ref, c_ref = refs[0], refs[1]
        p_vals = [r[...] for r in refs[2: 2 + n_parts]]
        o_ref = refs[2 + n_parts]
        lane = lax.broadcasted_iota(jnp.int32, (tm, LANES), 1)
        tail0 = jnp.broadcast_to(c_ref[...], (tm, LANES))
        for h in range(n_heads):
            pair = x_ref[:, (h // 2) * LANES: (h // 2 + 1) * LANES].astype(F32)
            base = pair if h % 2 == 0 else pltpu.roll(pair, HEAD_DIM, 1)
            tail = tail0
            for kk, p in enumerate(p_vals):
                col = jnp.sum(jnp.where(lane == h, p, 0.0), axis=1, keepdims=True)
                tail = jnp.where(lane == parts_lane + kk, col, tail)
            o_ref[h] = jnp.where(lane < HEAD_DIM, base, tail).astype(BF16)

    return pl.pallas_call(
        body, name=name, grid=(n_rows // tm,),
        in_specs=[pl.BlockSpec((tm, d), lambda i: (i, col_block)), pl.BlockSpec((1, LANES), lambda i: (0, 0))]
        + [pl.BlockSpec((tm, LANES), lambda i: (i, 0))] * n_parts,
        out_specs=pl.BlockSpec((n_heads, tm, LANES), lambda i: (0, i, 0)),
        out_shape=jax.ShapeDtypeStruct((n_heads, n_rows, LANES), BF16),
        compiler_params=_params("parallel"),
    )(x, consts, *parts)


def unpack_heads(name, xs, extract_lane=None, tm=None):
    n_heads, n_rows, _ = xs[0].shape
    assert n_heads % 2 == 0
    tm = _pick(n_rows, tm or TILES["row"], SUBLANES)
    n = len(xs)

    def body(*refs):
        o_ref = refs[n]
        lane = lax.broadcasted_iota(jnp.int32, (tm, LANES), 1)
        picked = jnp.zeros((tm, LANES), F32)

        def head(h):
            v = refs[0][h]
            for r in refs[1:n]:
                v = v + r[h]
            return v

        for p in range(n_heads // 2):
            a, b = head(2 * p), head(2 * p + 1)
            o_ref[:, p * LANES: (p + 1) * LANES] = jnp.where(lane < HEAD_DIM, a, pltpu.roll(b, HEAD_DIM, 1))
            if extract_lane is not None:
                for hh, v in ((2 * p, a), (2 * p + 1, b)):
                    col = jnp.sum(jnp.where(lane == extract_lane, v, 0.0), axis=1, keepdims=True)
                    picked = jnp.where(lane == hh, col, picked)
        if extract_lane is not None:
            refs[n + 1][...] = picked

    d = n_heads * HEAD_DIM
    out_shape = [jax.ShapeDtypeStruct((n_rows, d), F32)]
    out_specs = [pl.BlockSpec((tm, d), lambda i: (i, 0))]
    if extract_lane is not None:
        out_shape.append(jax.ShapeDtypeStruct((n_rows, LANES), F32))
        out_specs.append(pl.BlockSpec((tm, LANES), lambda i: (i, 0)))
    return pl.pallas_call(
        body, name=name, grid=(n_rows // tm,),
        in_specs=[pl.BlockSpec((n_heads, tm, LANES), lambda i: (0, i, 0))] * n,
        out_specs=out_specs, out_shape=out_shape, compiler_params=_params("parallel"),
    )(*xs)


def fox_fwd(name, q_ext, k_ext, v_ext, t=None):
    nh, n_rows, w = q_ext.shape
    t = _pick(n_rows, t or TILES["attn"])
    nt = n_rows // t
    half = t // 2

    def body(q_ref, k_ref, v_ref, o_ref, qb_ref, m_ref, acc_ref):
        i = pl.program_id(1)
        m_ref[...] = jnp.full_like(m_ref, NEG_BIG)
        acc_ref[...] = jnp.zeros_like(acc_ref)
        q = q_ref[...]

        def piece(rows, kv_off, size, diagonal):
            kv = pl.ds(pl.multiple_of(kv_off, size), size)
            s = lax.dot_general(q_ref[rows, :], k_ref[kv, :], _DIMS["nt"], preferred_element_type=F32)
            if diagonal:
                keep = lax.broadcasted_iota(jnp.int32, s.shape, 0) >= lax.broadcasted_iota(jnp.int32, s.shape, 1)
                s = jnp.where(keep, s, NEG_BIG)
            m_prev = m_ref[rows, :]
            m_new = jnp.maximum(m_prev, jnp.max(s, axis=1, keepdims=True))
            p = jnp.exp2(s - jnp.tile(m_new, (1, size // LANES)))
            acc_ref[rows, :] = jnp.exp2(m_prev - m_new) * acc_ref[rows, :] + jnp.dot(
                p.astype(BF16), v_ref[kv, :], preferred_element_type=F32)
            m_ref[rows, :] = m_new

        def off_diagonal(j, carry):
            piece(slice(0, t), j * t, t, False)
            return carry

        lax.fori_loop(0, i, off_diagonal, 0)
        if half % LANES == 0:
            piece(slice(0, half), i * t, half, True)
            piece(slice(half, t), i * t, half, False)
            piece(slice(half, t), i * t + half, half, True)
        else:
            piece(slice(0, t), i * t, t, True)
        acc = acc_ref[...]
        row_sum = acc[:, HEAD_DIM:HEAD_DIM + 1]
        hi, mid, lo = _split3(m_ref[:, 0:1] + jnp.log2(row_sum))
        lane = lax.broadcasted_iota(jnp.int32, (t, w), 1)
        o_ref[...] = acc / row_sum
        qb = jnp.where(lane == LSE_LANE, hi, jnp.where(lane == LSE_LANE + 1, mid, jnp.where(lane == LSE_LANE + 2, lo, q.astype(F32))))
        qb_ref[...] = qb.astype(BF16)

    whole = pl.BlockSpec((None, n_rows, w), lambda h, i: (h, 0, 0))
    tile = pl.BlockSpec((None, t, w), lambda h, i: (h, i, 0))
    return pl.pallas_call(
        body, name=name, grid=(nh, nt), in_specs=[tile, whole, whole], out_specs=[tile, tile],
        out_shape=[jax.ShapeDtypeStruct((nh, n_rows, w), F32), jax.ShapeDtypeStruct((nh, n_rows, w), BF16)],
        scratch_shapes=[pltpu.VMEM((t, w), F32), pltpu.VMEM((t, w), F32)],
        compiler_params=_params("parallel", "arbitrary"),
    )(q_ext, k_ext, v_ext)


def fox_bwd(name, q_ext, do_ext, k_ext, v_ext, t=None):
    nh, n_rows, w = q_ext.shape
    t = _pick(n_rows, t or TILES["attn"])
    nt = n_rows // t
    half = t // 2

    def body(q_ref, do_ref, k_ref, v_ref, dq_ref, dk_ref, dv_ref):
        j = pl.program_id(1)

        @pl.when(j == 0)
        def _():
            dq_ref[...] = jnp.zeros_like(dq_ref)

        dk_ref[...] = jnp.zeros_like(dk_ref)
        dv_ref[...] = jnp.zeros_like(dv_ref)
        def piece(keys, q_off, size, diagonal):
            qs = pl.ds(pl.multiple_of(q_off, size), size)
            kj, vj, qi, doi = k_ref[keys, :], v_ref[keys, :], q_ref[qs, :], do_ref[qs, :]
            pt = jnp.exp2(lax.dot_general(kj, qi, _DIMS["nt"], preferred_element_type=F32))
            if diagonal:
                keep = lax.broadcasted_iota(jnp.int32, pt.shape, 0) <= lax.broadcasted_iota(jnp.int32, pt.shape, 1)
                pt = jnp.where(keep, pt, 0.0)
            dst = (pt * lax.dot_general(vj, doi, _DIMS["nt"], preferred_element_type=F32)).astype(BF16)
            dv_ref[keys, :] += jnp.dot(pt.astype(BF16), doi, preferred_element_type=F32)
            dk_ref[keys, :] += jnp.dot(dst, qi, preferred_element_type=F32)
            dq_ref[qs, :] += lax.dot_general(dst, kj, _DIMS["tn"], preferred_element_type=F32)

        def off_diagonal(i, carry):
            piece(slice(0, t), i * t, t, False)
            return carry

        if half % LANES == 0:
            piece(slice(0, half), j * t, half, True)
            piece(slice(0, half), j * t + half, half, False)
            piece(slice(half, t), j * t + half, half, True)
        else:
            piece(slice(0, t), j * t, t, True)
        lax.fori_loop(j + 1, nt, off_diagonal, 0)
        lane = lax.broadcasted_iota(jnp.int32, (t, w), 1)
        dk_ref[...] = dk_ref[...] * jnp.where(lane < HEAD_DIM, 1.0 / LOG2E, 1.0)

    whole = pl.BlockSpec((None, n_rows, w), lambda h, j: (h, 0, 0))
    tile = pl.BlockSpec((None, t, w), lambda h, j: (h, j, 0))
    shape = jax.ShapeDtypeStruct((nh, n_rows, w), F32)
    return pl.pallas_call(
        body, name=name, grid=(nh, nt), in_specs=[whole, whole, tile, tile], out_specs=[whole, tile, tile],
        out_shape=[shape, shape, shape],
        compiler_params=_params("parallel", "arbitrary"),
    )(q_ext, do_ext, k_ext, v_ext)


def attn_delta(name, o, do):
    d_model = o.shape[1]
    head_of_col = lax.broadcasted_iota(jnp.int32, (d_model, LANES), 0) // HEAD_DIM
    sel = (head_of_col == lax.broadcasted_iota(jnp.int32, (d_model, LANES), 1)).astype(F32)

    def fn(a, b, s):
        return _split3(jnp.dot(a * b, s, precision=HIGHEST, preferred_element_type=F32))

    return rowcall(name, fn, [o, do], [sel], [(LANES, F32)] * 3)


def all_gather(name, tensors):
    n = len(tensors)

    def body(*refs):
        start, finish = _gather_phases(refs[:n], refs[n: 2 * n], *refs[2 * n:])
        start()
        finish()

    any_spec = pl.BlockSpec(memory_space=pl.ANY)
    return pl.pallas_call(
        body, name=name, in_specs=[any_spec] * n, out_specs=[any_spec] * n,
        out_shape=_gather_shapes(tensors), scratch_shapes=_gather_semaphores(n),
        compiler_params=pltpu.CompilerParams(has_side_effects=True),
    )(*tensors)


def _gather_shapes(tensors):
    return [jax.ShapeDtypeStruct((N_DEV,) + t.shape, t.dtype) for t in tensors]


def _gather_semaphores(n):
    return [pltpu.SemaphoreType.DMA((n, 7)), pltpu.SemaphoreType.DMA((n, 7)), pltpu.SemaphoreType.DMA((n,))]


def _gather_phases(ins, outs, send_sems, recv_sems, local_sems):
    n = len(ins)
    x, y, c = lax.axis_index("x"), lax.axis_index("y"), lax.axis_index("c")
    sibling = (x, y, 1 - c)
    chips = [(1 - x, y), (x, 1 - y), (1 - x, 1 - y)]
    slot = lambda px, py, pc: 4 * px + 2 * py + pc

    def copy(t, k, block, to, src=None):
        dst = outs[t].at[slot(*block)]
        return pltpu.make_async_remote_copy(
            src_ref=dst if src is None else src, dst_ref=dst, send_sem=send_sems.at[t, k], recv_sem=recv_sems.at[t, k],
            device_id=to, device_id_type=pl.DeviceIdType.MESH)

    own = lambda t: pltpu.make_async_copy(ins[t], outs[t].at[slot(x, y, c)], local_sems.at[t])
    first = lambda t: [copy(t, 0, (x, y, c), sibling, ins[t])] + [copy(t, 1 + j, (x, y, c), (*chip, c), ins[t]) for j, chip in enumerate(chips)]

    def start():
        for t in range(n):
            own(t).start()
            for cp in first(t):
                cp.start()

    def finish():
        sends = []
        for t in range(n):
            sends += first(t)
            for j, chip in enumerate(chips):
                copy(t, 1 + j, (*chip, c), (x, y, c)).wait_recv()
                passed = copy(t, 4 + j, (*chip, c), sibling)
                passed.start()
                sends.append(passed)
        for t in range(n):
            copy(t, 0, sibling, (x, y, c)).wait_recv()
            for j, chip in enumerate(chips):
                copy(t, 4 + j, (*chip, 1 - c), (x, y, c)).wait_recv()
        for cp in sends:
            cp.wait_send()
        for t in range(n):
            own(t).wait()

    return start, finish


def _remote_call(name, body, tensors, out_shapes, n_copies):
    n = len(tensors)
    any_spec = pl.BlockSpec(memory_space=pl.ANY)
    return pl.pallas_call(
        body, name=name, in_specs=[any_spec] * n, out_specs=[any_spec] * n, out_shape=out_shapes,
        scratch_shapes=[pltpu.SemaphoreType.DMA((n, n_copies)), pltpu.SemaphoreType.DMA((n, n_copies)), pltpu.SemaphoreType.DMA((n,))],
        compiler_params=pltpu.CompilerParams(has_side_effects=True),
    )(*tensors)


N_CHIPS = 4
_CHIPS = [(0, 0), (0, 1), (1, 0), (1, 1)]


def pair_swap(name, tensors):
    n = len(tensors)

    def body(*refs):
        ins, outs = refs[:n], refs[n: 2 * n]
        send_sems, recv_sems, _ = refs[2 * n:]
        x, y, c = lax.axis_index("x"), lax.axis_index("y"), lax.axis_index("c")
        copies = []
        for t in range(n):
            for k in range(N_CHIPS):
                cp = pltpu.make_async_remote_copy(
                    src_ref=ins[t].at[2 * k + (1 - c)], dst_ref=outs[t].at[k], send_sem=send_sems.at[t, k], recv_sem=recv_sems.at[t, k],
                    device_id=(x, y, 1 - c), device_id_type=pl.DeviceIdType.MESH)
                cp.start()
                copies.append(cp)
        for cp in copies:
            cp.wait()

    return _remote_call(name, body, tensors, [jax.ShapeDtypeStruct((N_CHIPS,) + t.shape[1:], t.dtype) for t in tensors], N_CHIPS)


def chip_exchange(name, tensors):
    n = len(tensors)

    def body(*refs):
        start, finish = _chip_exchange_phases(refs[:n], refs[n: 2 * n], *refs[2 * n:])
        start()
        finish()

    return _remote_call(name, body, tensors, [jax.ShapeDtypeStruct(t.shape, t.dtype) for t in tensors], len(_CHIP_FLIPS))


_CHIP_FLIPS = [(1, 0), (0, 1), (1, 1)]


def _chip_exchange_semaphores(n):
    return [pltpu.SemaphoreType.DMA((n, len(_CHIP_FLIPS))), pltpu.SemaphoreType.DMA((n, len(_CHIP_FLIPS))), pltpu.SemaphoreType.DMA((n,))]


def _chip_exchange_phases(ins, outs, send_sems, recv_sems, local_sems):
    n = len(ins)
    x, y, c = lax.axis_index("x"), lax.axis_index("y"), lax.axis_index("c")
    me = 2 * x + y

    def copies(landing):
        out = []
        for t in range(n):
            for j, (bx, by) in enumerate(_CHIP_FLIPS):
                px, py = (1 - x if bx else x), (1 - y if by else y)
                peer = 2 * px + py
                out.append(pltpu.make_async_remote_copy(
                    src_ref=ins[t].at[peer], dst_ref=outs[t].at[peer if landing else me], send_sem=send_sems.at[t, j],
                    recv_sem=recv_sems.at[t, j], device_id=(px, py, c), device_id_type=pl.DeviceIdType.MESH))
        return out

    own = lambda t: pltpu.make_async_copy(ins[t].at[me], outs[t].at[me], local_sems.at[t])

    def start():
        for t in range(n):
            own(t).start()
        for cp in copies(False):
            cp.start()

    def finish():
        for cp in copies(True):
            cp.wait()
        for t in range(n):
            own(t).wait()

    return start, finish


def adamw(name, parts, w, m, v, tr=None):
    n_parts = parts.shape[0]
    n_rows, n_cols = w.shape
    tr = _pick(n_rows, tr or TILES["adam"], SUBLANES)
    c1 = 1.0 - ADAM_B1 ** ADAM_STEP
    c2 = 1.0 - ADAM_B2 ** ADAM_STEP

    def body(p_ref, w_ref, m_ref, v_ref, g_ref, d_ref, mo_ref, vo_ref):
        g = p_ref[0].astype(F32)
        for s in range(1, n_parts):
            g = g + p_ref[s].astype(F32)
        mn = ADAM_B1 * m_ref[...] + (1.0 - ADAM_B1) * g
        vn = ADAM_B2 * v_ref[...] + (1.0 - ADAM_B2) * (g * g)
        g_ref[...] = g
        mo_ref[...] = mn
        vo_ref[...] = vn
        d_ref[...] = -ADAM_LR * ((mn / c1) / (jnp.sqrt(vn / c2) + ADAM_EPS) + ADAM_WD * w_ref[...])

    spec = pl.BlockSpec((tr, n_cols), lambda i: (i, 0))
    return pl.pallas_call(
        body, name=name, grid=(n_rows // tr,),
        in_specs=[pl.BlockSpec((n_parts, tr, n_cols), lambda i: (0, i, 0)), spec, spec, spec],
        out_specs=[spec] * 4, out_shape=[jax.ShapeDtypeStruct(w.shape, F32)] * 4,
        compiler_params=_params("parallel"),
    )(parts, w, m, v)


def _eye_mask():
    return jnp.eye(GROUPS_PER_BLOCK, dtype=F32)


def _b_blocks(bbr, bbi):
    nb = bbr.shape[0] // (GROUPS_PER_BLOCK * SSM_STATE)
    eye = _eye_mask()[None, :, None, :, None]

    def one(z):
        z = z.reshape(nb, GROUPS_PER_BLOCK, SSM_STATE, SSM_GROUP).transpose(0, 1, 3, 2)
        return z[:, :, :, None, :] * eye

    w = jnp.stack([one(bbr), one(bbi)], axis=3)
    return w.reshape(nb, LANES, BLOCK_COLS)


def _b_blocks_t(dw):
    nb = dw.shape[0]
    d6 = dw.reshape(nb, GROUPS_PER_BLOCK, SSM_GROUP, 2, GROUPS_PER_BLOCK, SSM_STATE)
    diag = jnp.sum(d6 * _eye_mask()[None, :, None, None, :, None], axis=4)
    diag = diag.transpose(3, 0, 1, 4, 2).reshape(2, nb * GROUPS_PER_BLOCK * SSM_STATE, SSM_GROUP)
    return diag[0], diag[1]


def _c_blocks(c_re, c_im):
    nb = c_re.shape[0] // GROUPS_PER_BLOCK
    eye = _eye_mask()[None, :, None, :, None]

    def one(z):
        z = z.reshape(nb, GROUPS_PER_BLOCK, SSM_GROUP, SSM_STATE).transpose(0, 1, 3, 2)
        return z[:, :, :, None, :] * eye

    w = jnp.stack([one(c_re), -one(c_im)], axis=1)
    return w.reshape(nb, BLOCK_COLS, LANES)


def _c_blocks_t(dw):
    nb = dw.shape[0]
    d6 = dw.reshape(nb, 2, GROUPS_PER_BLOCK, SSM_STATE, GROUPS_PER_BLOCK, SSM_GROUP)
    diag = jnp.sum(d6 * _eye_mask()[None, None, :, None, :, None], axis=4)
    diag = diag.transpose(1, 0, 2, 4, 3).reshape(2, nb * GROUPS_PER_BLOCK, SSM_GROUP, SSM_STATE)
    return diag[0], -diag[1]


def _unshard_cols(g):
    s, k, n = g.shape
    return g.transpose(1, 0, 2).reshape(k, s * n)


def _shard_cols(w):
    k, n = w.shape
    return w.reshape(k, N_DEV, n // N_DEV).transpose(1, 0, 2)


def _pack(arrays):
    chunks, offs, row = [], [], 0
    for a in arrays:
        flat = a.reshape(-1).astype(F32)
        rows = -(-flat.shape[0] // LANES)
        chunks.append(jnp.pad(flat, (0, rows * LANES - flat.shape[0])))
        offs.append((row, rows))
        row += rows
    pad_rows = (-row) % SUBLANES
    if pad_rows:
        chunks.append(jnp.zeros((pad_rows * LANES,), F32))
    return jnp.concatenate(chunks).reshape(row + pad_rows, LANES), offs


def _unpack(packed, offs, shapes):
    out = []
    for (row, rows), shp in zip(offs, shapes):
        size = 1
        for s in shp:
            size *= s
        out.append(packed[row: row + rows].reshape(-1)[:size].reshape(shp))
    return out


SCAN_SUB_ROWS = 128


def _scan_tables_into(tab_ref, lam_ref, reverse):
    steps, (pr, pi) = _scan_tables(lam_ref, reverse)
    for kk, (_, mr, mi) in enumerate(steps):
        tab_ref[2 * kk] = mr
        tab_ref[2 * kk + 1] = mi
    tab_ref[6] = pr
    tab_ref[7] = pi


def _scan_in_place(buf_ref, tab_ref, carry_ref, ng, reverse, prev_ref=None, acc_ref=None, row0=0, unroll=False):
    shape = (SUBLANES, BLOCK_STATE)
    last = 0 if reverse else SUBLANES - 1
    first_row = lax.broadcasted_iota(jnp.int32, shape, 0) == 0
    re_cols, im_cols = pl.ds(0, BLOCK_STATE), pl.ds(BLOCK_STATE, BLOCK_STATE)

    def group(r, carry):
        cr, ci = carry[0], carry[1]
        rr = (ng - 1 - r) if reverse else r
        off = row0 + rr * SUBLANES
        off = off if unroll else pl.multiple_of(off, SUBLANES)
        xr, xi = buf_ref[pl.ds(off, SUBLANES), re_cols], buf_ref[pl.ds(off, SUBLANES), im_cols]
        for kk, d in enumerate((1, 2, 4)):
            sh = (SUBLANES - d) if reverse else d
            mr, mi = tab_ref[2 * kk], tab_ref[2 * kk + 1]
            yr, yi = pltpu.roll(xr, sh, 0), pltpu.roll(xi, sh, 0)
            xr, xi = xr + mr * yr - mi * yi, xi + mr * yi + mi * yr
        pr, pi = tab_ref[6], tab_ref[7]
        xr, xi = xr + pr * cr - pi * ci, xi + pr * ci + pi * cr
        buf_ref[pl.ds(off, SUBLANES), re_cols] = xr
        buf_ref[pl.ds(off, SUBLANES), im_cols] = xi
        out = (jnp.broadcast_to(xr[last:last + 1, :], shape), jnp.broadcast_to(xi[last:last + 1, :], shape))
        if prev_ref is not None:
            off8 = off + SUBLANES if unroll else pl.multiple_of(off + SUBLANES, SUBLANES)
            before_r, before_i = prev_ref[pl.ds(off, SUBLANES), re_cols], prev_ref[pl.ds(off, SUBLANES), im_cols]
            same_r, same_i = prev_ref[pl.ds(off8, SUBLANES), re_cols], prev_ref[pl.ds(off8, SUBLANES), im_cols]
            sr = jnp.where(first_row, jnp.broadcast_to(before_r[SUBLANES - 1:, :], shape), pltpu.roll(same_r, 1, 0))
            si = jnp.where(first_row, jnp.broadcast_to(before_i[SUBLANES - 1:, :], shape), pltpu.roll(same_i, 1, 0))
            out += (carry[2] + xr * sr + xi * si, carry[3] + xi * sr - xr * si)
        return out

    init = (carry_ref[0], carry_ref[1])
    if prev_ref is not None:
        init += (acc_ref[0], acc_ref[1])
    if unroll:
        res = init
        for r in range(ng):
            res = group(r, res)
    else:
        res = lax.fori_loop(0, ng, group, init)
    carry_ref[0] = res[0]
    carry_ref[1] = res[1]
    if prev_ref is not None:
        acc_ref[0] = res[2]
        acc_ref[1] = res[3]


def s5_fused_fwd(name, u, wb, wc, lam, d_row, tm=None, gather=()):
    n_rows, d_model = u.shape
    nb = lam.shape[0]
    tm = _pick(n_rows, tm or TILES["scan"], 2 * SUBLANES)
    nt = n_rows // tm
    sub = _pick(tm, SCAN_SUB_ROWS, 2 * SUBLANES)
    ng = len(gather)

    def body(u_ref, wb_ref, wc_ref, lam_ref, d_ref, *refs):
        g_in, (z_ref, st_ref), g_out = refs[:ng], refs[ng: ng + 2], refs[ng + 2: 2 * ng + 2]
        buf_ref, carry_ref, tab_ref = refs[2 * ng + 2: 2 * ng + 5]
        b, it = pl.program_id(0), pl.program_id(1)
        if ng:
            start, finish = _gather_phases(g_in, g_out, *refs[2 * ng + 5:])
            pl.when((b == 0) & (it == 0))(start)

        @pl.when(it == 0)
        def _():
            carry_ref[...] = jnp.zeros_like(carry_ref)
            _scan_tables_into(tab_ref, lam_ref, False)

        for r0 in range(0, tm, sub):
            rows = slice(r0, r0 + sub)
            uu = u_ref[rows, :]
            buf_ref[rows, :] = jnp.dot(uu.astype(BF16), wb_ref[...], preferred_element_type=F32)
            _scan_in_place(buf_ref, tab_ref, carry_ref, sub // SUBLANES, False, row0=r0, unroll=True)
            st = buf_ref[rows, :].astype(BF16)
            st_ref[rows, :] = st
            z_ref[rows, :] = _gelu(jnp.dot(st, wc_ref[...], preferred_element_type=F32) + d_ref[...] * uu).astype(z_ref.dtype)
        if ng:
            pl.when((b == nb - 1) & (it == nt - 1))(finish)

    tile = lambda b, it: (it, b)
    blk = lambda b, it: (b, 0, 0)
    any_spec = pl.BlockSpec(memory_space=pl.ANY)
    params = pltpu.CompilerParams(dimension_semantics=("arbitrary", "arbitrary"), vmem_limit_bytes=VMEM_LIMIT_BYTES,
                                  has_side_effects=True) if ng else _params("parallel", "arbitrary")
    return pl.pallas_call(
        body, name=name, grid=(nb, nt),
        in_specs=[pl.BlockSpec((tm, LANES), tile), pl.BlockSpec((None, LANES, BLOCK_COLS), blk),
                  pl.BlockSpec((None, BLOCK_COLS, LANES), blk), pl.BlockSpec((None, 2, BLOCK_STATE), blk),
                  pl.BlockSpec((1, LANES), lambda b, it: (0, b))] + [any_spec] * ng,
        out_specs=[pl.BlockSpec((tm, LANES), tile), pl.BlockSpec((tm, BLOCK_COLS), tile)] + [any_spec] * ng,
        out_shape=[jax.ShapeDtypeStruct((n_rows, d_model), BF16), jax.ShapeDtypeStruct((n_rows, nb * BLOCK_COLS), BF16)]
        + _gather_shapes(gather),
        scratch_shapes=[pltpu.VMEM((tm, BLOCK_COLS), F32), pltpu.VMEM((2, SUBLANES, BLOCK_STATE), F32),
                        pltpu.VMEM((8, SUBLANES, BLOCK_STATE), F32)] + (_gather_semaphores(ng) if ng else []),
        compiler_params=params,
    )(u, wb, wc, lam, d_row, *gather)


def s5_fused_bwd(name, dz, u, st, wb, wc, lam, d_row, tm=None, exchange=()):
    ne = len(exchange)
    n_rows, d_model = u.shape
    nb = lam.shape[0]
    tm = _pick(n_rows, tm or TILES["scan"], 2 * SUBLANES)
    nt = n_rows // tm
    tail_rows = 2 * SUBLANES
    sub = _pick(tm, SCAN_SUB_ROWS, 2 * SUBLANES)

    def body(dz_ref, u_ref, st_ref, tail_ref, wb_ref, wc_ref, lam_ref, d_ref, *refs):
        e_in, (du_ref, dwb_ref, dwc_ref, dlam_ref, dd_ref), e_out = refs[:ne], refs[ne: ne + 5], refs[ne + 5: 2 * ne + 5]
        buf_ref, prev_ref, carry_ref, acc_ref, tab_ref = refs[2 * ne + 5: 2 * ne + 10]
        b, it = pl.program_id(0), pl.program_id(1)
        if ne:
            start, finish = _chip_exchange_phases(e_in, e_out, *refs[2 * ne + 10:])
            pl.when((b == 0) & (it == 0))(start)

        @pl.when(it == 0)
        def _():
            carry_ref[...] = jnp.zeros_like(carry_ref)
            acc_ref[...] = jnp.zeros_like(acc_ref)
            dwb_ref[...] = jnp.zeros_like(dwb_ref)
            dwc_ref[...] = jnp.zeros_like(dwc_ref)
            dd_ref[...] = jnp.zeros_like(dd_ref)
            _scan_tables_into(tab_ref, lam_ref, True)

        prev_ref[SUBLANES:, :] = st_ref[...].astype(F32)
        before = tail_ref[...].astype(F32)[SUBLANES:, :]
        prev_ref[:SUBLANES, :] = jnp.where(it == nt - 1, 0.0, before)
        for r0 in range(tm - sub, -1, -sub):
            rows = slice(r0, r0 + sub)
            uu, st_b = u_ref[rows, :], st_ref[rows, :]
            y = jnp.dot(st_b, wc_ref[...], preferred_element_type=F32) + d_ref[...] * uu
            _, vjp = jax.vjp(_gelu, y)
            dy = vjp(dz_ref[rows, :])[0]
            dyb = dy.astype(BF16)
            dd_ref[...] += jnp.sum(dy * uu, axis=0, keepdims=True)
            dwc_ref[...] += lax.dot_general(st_b, dyb, _DIMS["tn"], preferred_element_type=F32)
            buf_ref[rows, :] = lax.dot_general(dyb, wc_ref[...], _DIMS["nt"], preferred_element_type=F32)
            _scan_in_place(buf_ref, tab_ref, carry_ref, sub // SUBLANES, True, prev_ref, acc_ref, row0=r0, unroll=True)
            gb = buf_ref[rows, :].astype(BF16)
            du_ref[rows, :] = lax.dot_general(gb, wb_ref[...], _DIMS["nt"], preferred_element_type=F32) + dy * d_ref[...]
            dwb_ref[...] += lax.dot_general(uu.astype(BF16), gb, _DIMS["tn"], preferred_element_type=F32)

        @pl.when(it == nt - 1)
        def _():
            dlam_ref[0:1, :] = jnp.sum(acc_ref[0], axis=0, keepdims=True)
            dlam_ref[1:2, :] = jnp.sum(acc_ref[1], axis=0, keepdims=True)

        if ne:
            pl.when((b == nb - 1) & (it == nt - 1))(finish)

    any_spec = pl.BlockSpec(memory_space=pl.ANY)
    params = pltpu.CompilerParams(dimension_semantics=("arbitrary", "arbitrary"), vmem_limit_bytes=VMEM_LIMIT_BYTES,
                                  has_side_effects=True) if ne else _params("parallel", "arbitrary")
    tile = lambda b, it: (nt - 1 - it, b)
    blk = lambda b, it: (b, 0, 0)
    per_tile = tm // tail_rows
    tail = lambda b, it: (jnp.maximum((nt - 1 - it) * per_tile - 1, 0), b)
    return pl.pallas_call(
        body, name=name, grid=(nb, nt),
        in_specs=[pl.BlockSpec((tm, LANES), tile), pl.BlockSpec((tm, LANES), tile), pl.BlockSpec((tm, BLOCK_COLS), tile),
                  pl.BlockSpec((tail_rows, BLOCK_COLS), tail), pl.BlockSpec((None, LANES, BLOCK_COLS), blk),
                  pl.BlockSpec((None, BLOCK_COLS, LANES), blk), pl.BlockSpec((None, 2, BLOCK_STATE), blk),
                  pl.BlockSpec((1, LANES), lambda b, it: (0, b))] + [any_spec] * ne,
        out_specs=[pl.BlockSpec((tm, LANES), tile), pl.BlockSpec((None, LANES, BLOCK_COLS), blk),
                   pl.BlockSpec((None, BLOCK_COLS, LANES), blk), pl.BlockSpec((None, 2, BLOCK_STATE), blk),
                   pl.BlockSpec((1, LANES), lambda b, it: (0, b))] + [any_spec] * ne,
        out_shape=[jax.ShapeDtypeStruct((n_rows, d_model), F32), jax.ShapeDtypeStruct((nb, LANES, BLOCK_COLS), F32),
                   jax.ShapeDtypeStruct((nb, BLOCK_COLS, LANES), F32), jax.ShapeDtypeStruct((nb, 2, BLOCK_STATE), F32),
                   jax.ShapeDtypeStruct((1, d_model), F32)] + [jax.ShapeDtypeStruct(t.shape, t.dtype) for t in exchange],
        scratch_shapes=[pltpu.VMEM((tm, BLOCK_COLS), F32), pltpu.VMEM((tm + SUBLANES, BLOCK_COLS), F32),
                        pltpu.VMEM((2, SUBLANES, BLOCK_STATE), F32), pltpu.VMEM((2, SUBLANES, BLOCK_STATE), F32),
                        pltpu.VMEM((8, SUBLANES, BLOCK_STATE), F32)] + (_chip_exchange_semaphores(ne) if ne else []),
        compiler_params=params,
    )(dz, u, st, st, wb, wc, lam, d_row, *exchange)


def s5_fwd(tag, u, log_dt, a_re, a_im, b_re, b_im, c_re, c_im, d_row, gather=()):
    d_model = u.shape[1]
    nb = d_model // LANES
    col = lambda a: a.reshape(-1, 1)
    prm = [col(jnp.repeat(log_dt, SSM_STATE)), col(a_re), col(a_im), b_re.reshape(-1, SSM_GROUP), b_im.reshape(-1, SSM_GROUP)]
    lr, li, bbr, bbi = rowcall(f"s5_prep_{tag}", _s5_discretise, prm, [], [(1, F32), (1, F32), (SSM_GROUP, F32), (SSM_GROUP, F32)])
    lam = jnp.stack([lr.reshape(nb, BLOCK_STATE), li.reshape(nb, BLOCK_STATE)], axis=1)
    wb = _b_blocks(bbr, bbi).astype(BF16)
    wc = _c_blocks(c_re, c_im).astype(BF16)
    z, st, *gathered = s5_fused_fwd(f"s5_fwd_{tag}", u, wb, wc, lam, d_row, gather=gather)
    return z, dict(prm=prm, lam=lam, wb=wb, wc=wc, st=st), gathered


def s5_bwd(tag, dz, u, d_row, sv, exchange=()):
    d_model = u.shape[1]
    nb = d_model // LANES
    n_groups = d_model // SSM_GROUP
    col = lambda a: a.reshape(-1, 1)

    dhn, d_wb, d_wc, d_lam, d_dskip, *exchanged = s5_fused_bwd(
        f"s5_bwd_{tag}", dz, u, sv["st"], sv["wb"], sv["wc"], sv["lam"], d_row, exchange=exchange)
    d_bbr, d_bbi = _b_blocks_t(d_wb)
    d_cre, d_cim = _c_blocks_t(d_wc)

    def prep_bwd(ldt, ar, ai, br, bi, dlr, dli, dbr, dbi):
        _, vjp = jax.vjp(_s5_discretise, ldt, ar, ai, br, bi)
        return vjp((dlr, dli, dbr, dbi))

    d_ldt, d_are, d_aim, d_bre, d_bim = rowcall(
        f"s5_prep_bwd_{tag}", prep_bwd, sv["prm"] + [col(d_lam[:, 0]), col(d_lam[:, 1]), d_bbr, d_bbi], [],
        [(1, F32), (1, F32), (1, F32), (SSM_GROUP, F32), (SSM_GROUP, F32)])
    d_logdt = rowcall(f"s5_dlogdt_{tag}", lambda a: jnp.sum(a, axis=1, keepdims=True), [d_ldt.reshape(n_groups, SSM_STATE)], [], [(1, F32)])[0]
    grads = dict(log_dt=d_logdt.reshape(n_groups), a_re=d_are.reshape(n_groups, SSM_STATE), a_im=d_aim.reshape(n_groups, SSM_STATE),
                 b_re=d_bre.reshape(n_groups, SSM_STATE, SSM_GROUP), b_im=d_bim.reshape(n_groups, SSM_STATE, SSM_GROUP),
                 c_re=d_cre, c_im=d_cim, d=d_dskip)
    return dhn, grads, exchanged


def kernel(x, mix_norm, mlp_norm, mlp_w1, mlp_w2, ssm_log_dt, ssm_a_re, ssm_a_im, ssm_b_re, ssm_b_im, ssm_c_re, ssm_c_im, ssm_d, ssm_w_glu, kv_norm, w_kvf, b_f, attn_wq, attn_wo, final_norm, loss_target, m_mix_norm, m_mlp_norm, m_mlp_w1, m_mlp_w2, m_ssm_log_dt, m_ssm_a_re, m_ssm_a_im, m_ssm_b_re, m_ssm_b_im, m_ssm_c_re, m_ssm_c_im, m_ssm_d, m_ssm_w_glu, m_kv_norm, m_w_kvf, m_b_f, m_attn_wq, m_attn_wo, m_final_norm, v_mix_norm, v_mlp_norm, v_mlp_w1, v_mlp_w2, v_ssm_log_dt, v_ssm_a_re, v_ssm_a_im, v_ssm_b_re, v_ssm_b_im, v_ssm_c_re, v_ssm_c_im, v_ssm_d, v_ssm_w_glu, v_kv_norm, v_w_kvf, v_b_f, v_attn_wq, v_attn_wo, v_final_norm):
    n_rows, d_model = x.shape[1], x.shape[2]
    depth = mix_norm.shape[0]
    n_a = ssm_log_dt.shape[0]
    n_b = depth - n_a
    n_heads = d_model // HEAD_DIM
    n_groups = d_model // SSM_GROUP
    nb = n_groups // GROUPS_PER_BLOCK
    kvf_cols = 2 * d_model + n_heads
    kvf_pad = 2 * d_model + LANES

    (g_d,) = all_gather("gather_skip_gain", [ssm_d])
    d_skip = [g_d[:, i].reshape(1, d_model) for i in range(n_a)]
    bf = lambda a: a.astype(BF16)
    early = n_a if n_a >= 2 else depth
    gather_sets = [[bf(mlp_w1[:early]), bf(mlp_w2[:early]), bf(ssm_w_glu)], [bf(w_kvf), bf(attn_wq), bf(attn_wo)]]
    if early < depth:
        gather_sets[1] += [bf(mlp_w1[early:]), bf(mlp_w2[early:])]
    else:
        gather_sets = [gather_sets[0] + gather_sets[1]]
    w1, w2 = [None] * depth, [None] * depth
    wglu = wkvf = wq = wo = None

    row = lambda a: a.reshape(1, -1)
    col = lambda a: a.reshape(-1, 1)

    h = x[0]
    saved = []
    k_ext = v_ext = None
    q_consts = _lane_consts([(F_LANE, F_LANE + 3, 1.0)])
    k_consts = _lane_consts([(LSE_LANE, LSE_LANE + 3, -1.0), (ROWSUM_LANE, ROWSUM_LANE + 1, 1.0)])
    v_consts = _lane_consts([(SUM_LANE, SUM_LANE + 1, 1.0), (DELTA_LANE, DELTA_LANE + 3, -1.0)])
    for i in range(depth):
        sv = {"h": h}
        hn = rms_fwd(f"mix_norm_{i}", h, row(mix_norm[i]), F32 if i < n_a else BF16)
        sv["hn"] = hn
        if i < n_a:
            z, s5_saved, gathered = s5_fwd(str(i), hn, ssm_log_dt[i], ssm_a_re[i], ssm_a_im[i], ssm_b_re[i], ssm_b_im[i],
                                           ssm_c_re[i], ssm_c_im[i], d_skip[i],
                                           gather=gather_sets[i] if i < len(gather_sets) else ())
            if i == 0:
                g_w1, g_w2, g_glu = gathered[:3]
                gathered = gathered[3:]
                for li in range(early):
                    w1[li] = _unshard_cols(g_w1[:, li])
                    w2[li] = g_w2[:, li].reshape(-1, d_model)
                wglu = [_unshard_cols(g_glu[:, li]) for li in range(n_a)]
            if i == len(gather_sets) - 1:
                g_kvf, g_wq, g_wo = gathered[:3]
                wkvf = jnp.pad(_unshard_cols(g_kvf), ((0, 0), (0, kvf_pad - kvf_cols)))
                wq = [g_wq[:, li].reshape(-1, d_model) for li in range(n_b)]
                wo = [g_wo[:, li].reshape(-1, d_model) for li in range(n_b)]
                for li in range(early, depth):
                    w1[li] = _unshard_cols(gathered[3][:, li - early])
                    w2[li] = gathered[4][:, li - early].reshape(-1, d_model)
            zw = matmul(f"s5_glu_{i}", z, wglu[i])
            def gate(hh, zz, gg):
                out = hh + zz[:, :d_model] * _sigmoid(zz[:, d_model:])
                return out, _rms(out, gg)

            h1, h2n = rowcall(f"s5_gate_{i}", gate, [h, zw], [row(mlp_norm[i])], [(d_model, F32), (d_model, BF16)])
            sv.update(s5=s5_saved, z=z, zw=zw)
        else:
            j = i - n_a
            q = matmul(f"attn_q_{j}", hn, wq[j], scale=LOG2E * HEAD_DIM ** -0.5, out_dtype=BF16)
            o_ext, q_ext_b = fox_fwd(f"attn_fwd_{j}", pack_heads(f"pack_q_{j}", q, 0, n_heads, q_consts), k_ext, v_ext)
            o2 = unpack_heads(f"unpack_o_{j}", [o_ext])[0]
            h1 = matmul(f"attn_o_{j}", o2, wo[j], resid=h)
            sv.update(q_ext_b=q_ext_b, o2=o2)
            h2n = rms_fwd(f"mlp_norm_{i}", h1, row(mlp_norm[i]), BF16)
        ap = matmul(f"mlp_up_{i}", h2n, w1[i], out_dtype=BF16)
        h = matmul(f"mlp_down_{i}", ap, w2[i], a_fn=_sqrelu, resid=h1, bk=2048)
        sv.update(h1=h1, h2n=h2n, ap=ap)
        saved.append(sv)
        if i == n_a - 1:
            h_mid = h
            hk = rms_fwd("kv_norm", h, row(kv_norm), BF16)
            kvf = matmul("kvf_proj", hk, wkvf, bn=kvf_pad)
            fl = kvf[:, 2 * d_model:]
            bfp = jnp.pad(row(b_f), ((0, 0), (0, LANES - n_heads)))
            k_ext = pack_heads("pack_k", kvf, 0, n_heads, k_consts, cum_logf("cum_logf", fl, bfp), F_LANE)
            v_ext = pack_heads("pack_v", kvf, 1, n_heads, v_consts)

    def loss_fn(hh, tgt, g):
        y, vjp = jax.vjp(_rms, hh, g)
        err = y - tgt
        part = 0.5 * jnp.sum(jnp.mean(err * err, axis=-1, keepdims=True), axis=0, keepdims=True)
        dh, dg = vjp(err * (1.0 / d_model))
        return dh, jnp.broadcast_to(part, (1, LANES)), dg

    dh, loss_part, d_final = rowcall("loss_head", loss_fn, [h, loss_target[0]], [row(final_norm)],
                                     [(d_model, F32)], [(1, LANES), (1, d_model)])

    g_mix, g_mlpn = [None] * depth, [None] * depth
    g_w1f, g_w2f = [None] * depth, [None] * depth
    g_ssm = [None] * n_a
    g_wqf, g_wof = [None] * n_b, [None] * n_b
    dk_acc, dv_acc, df_plus = [], [], []
    parts = None
    stack = lambda xs: jnp.stack(xs, axis=1).astype(BF16 if xs[0].ndim == 3 else F32)

    def reduce_pairs(tag, contributions):
        from_sibling = pair_swap(f"pair_swap_{tag}", contributions)
        core = lax.axis_index("c")
        sums = []
        for k, (mine, theirs) in enumerate(zip(contributions, from_sibling)):
            mine = lax.dynamic_index_in_dim(mine.reshape((N_CHIPS, 2) + mine.shape[1:]), core, axis=1, keepdims=False)
            cols = mine.shape[-1]
            sums.append(rowcall(f"pair_add_{tag}_{k}", lambda a, b: a.astype(F32) + b.astype(F32),
                                [mine.reshape(-1, cols), theirs.reshape(-1, cols)], [], [(cols, mine.dtype)])[0].reshape(mine.shape))
        return sums
    g_kv = None
    for i in reversed(range(depth)):
        sv = saved[i]
        if i == n_a - 1:
            dk, col_sums = unpack_heads("unpack_dk", dk_acc, extract_lane=HEAD_DIM)
            dv = unpack_heads("unpack_dv", dv_acc)[0]
            dfl, db_f = cum_logf_bwd("cum_logf_bwd", fl, bfp, df_plus, [col_sums])
            dkvf = jnp.concatenate([dk, dv, dfl], axis=1).astype(BF16)
            d_wkvf = matmul("kvf_dw", hk, dkvf, "tn", bm=512, bn=kvf_pad)
            dh, d_kvn = matmul("kvf_dx", dkvf, wkvf, "nt", bm=512, bk=kvf_pad, rms_bwd_of=(h_mid, row(kv_norm), dh))
            g_kv = (d_wkvf[:, :kvf_cols], d_kvn, db_f[:, :n_heads])
        dap = matmul(f"mlp_down_dx_{i}", dh, w2[i], "nt", post=lambda acc, apt: acc * (2.0 * jnp.maximum(apt.astype(F32), 0.0)),
                     post_arg=sv["ap"], out_dtype=BF16)
        g_w2f[i] = matmul(f"mlp_down_dw_{i}", sv["ap"], dh, "tn", a_fn=_sqrelu)
        g_w1f[i] = matmul(f"mlp_up_dw_{i}", sv["h2n"], dap, "tn")
        dh1, g_mlpn[i] = matmul(f"mlp_up_dx_{i}", dap, w1[i], "nt", bm=512, bk=2048, rms_bwd_of=(sv["h1"], row(mlp_norm[i]), dh))
        if i < n_a:
            def glu_bwd(zz, dd):
                val, gate = zz[:, :d_model], zz[:, d_model:]
                sg = _sigmoid(gate)
                return jnp.concatenate([dd * sg, dd * val * sg * (1.0 - sg)], axis=1)

            dzw = rowcall(f"s5_gate_bwd_{i}", glu_bwd, [sv["zw"], dh1], [], [(2 * d_model, BF16)])[0]
            dz = matmul(f"s5_glu_dx_{i}", dzw, wglu[i], "nt")
            d_wglu = matmul(f"s5_glu_dw_{i}", sv["z"], dzw, "tn")

            pair_sums = ()
            if i == 0:
                glu_grads = [d_wglu] + [g_ssm[li]["w_glu"] for li in range(1, n_a)]
                pair_sums = reduce_pairs("grads", [
                    stack([_shard_cols(g) for g in g_w1f]),
                    stack([g.reshape(N_DEV, -1, d_model) for g in g_w2f]),
                    stack([_shard_cols(g) for g in glu_grads]),
                    _shard_cols(g_kv[0]).astype(BF16),
                    stack([g.reshape(N_DEV, -1, d_model) for g in g_wqf]),
                    stack([g.reshape(N_DEV, -1, d_model) for g in g_wof]),
                ])
            dhn, g_ssm[i], exchanged = s5_bwd(str(i), dz, sv["hn"], d_skip[i], sv["s5"], exchange=pair_sums)
            g_ssm[i]["w_glu"] = d_wglu
            if i == 0:
                parts = exchanged
        else:
            j = i - n_a
            do2 = matmul(f"attn_o_dx_{j}", dh1, wo[j], "nt")
            g_wof[j] = matmul(f"attn_o_dw_{j}", sv["o2"], dh1, "tn")
            do_ext = pack_heads(f"pack_do_{j}", do2, 0, n_heads, jnp.zeros((1, LANES), F32),
                                attn_delta(f"attn_delta_{j}", sv["o2"], do2), DELTA_LANE)
            dq_ext, dk_ext, dv_ext = fox_bwd(f"attn_bwd_{j}", sv["q_ext_b"], do_ext, k_ext, v_ext)
            dk_acc.append(dk_ext)
            dv_acc.append(dv_ext)
            dq2, row_sums = unpack_heads(f"unpack_dq_{j}", [dq_ext], extract_lane=ROWSUM_LANE)
            df_plus.append(row_sums)
            g_wqf[j] = matmul(f"attn_q_dw_{j}", sv["hn"], dq2, "tn", scale=HEAD_DIM ** -0.5)
            dh, g_mix[i] = matmul(f"attn_q_dx_{j}", dq2, wq[j], "nt", scale=HEAD_DIM ** -0.5, bm=512,
                                  rms_bwd_of=(sv["h"], row(mix_norm[i]), dh1))
        if i < n_a:
            dh, g_mix[i] = rms_bwd(f"mix_norm_bwd_{i}", sv["h"], row(mix_norm[i]), dhn, add=dh1)
    grad_x = dh[None]

    parts = list(parts) + list(chip_exchange("chip_exchange_skip_gain", reduce_pairs("skip_gain", [
        stack([g["d"].reshape(N_DEV, -1) for g in g_ssm])])))
    ssm_g = lambda kk: jnp.stack([g[kk] for g in g_ssm])
    loss_slot = jnp.zeros((LANES,), F32)
    small = {
        "f32": (["mix_norm", "mlp_norm", "ssm_log_dt", "ssm_a_re", "ssm_a_im", "kv_norm", "b_f", "final_norm"],
                [jnp.concatenate(g_mix, axis=0), jnp.concatenate(g_mlpn, axis=0), ssm_g("log_dt"), ssm_g("a_re"), ssm_g("a_im"),
                 g_kv[1], g_kv[2], d_final, loss_part],
                [mix_norm, mlp_norm, ssm_log_dt, ssm_a_re, ssm_a_im, kv_norm, b_f, final_norm, loss_slot],
                [m_mix_norm, m_mlp_norm, m_ssm_log_dt, m_ssm_a_re, m_ssm_a_im, m_kv_norm, m_b_f, m_final_norm, loss_slot],
                [v_mix_norm, v_mlp_norm, v_ssm_log_dt, v_ssm_a_re, v_ssm_a_im, v_kv_norm, v_b_f, v_final_norm, loss_slot]),
        "bf16": (["ssm_b_re", "ssm_b_im", "ssm_c_re", "ssm_c_im"],
                 [ssm_g("b_re"), ssm_g("b_im"), ssm_g("c_re"), ssm_g("c_im")],
                 [ssm_b_re, ssm_b_im, ssm_c_re, ssm_c_im], [m_ssm_b_re, m_ssm_b_im, m_ssm_c_re, m_ssm_c_im],
                 [v_ssm_b_re, v_ssm_b_im, v_ssm_c_re, v_ssm_c_im]),
    }
    packed = {kk: [_pack(arrs) for arrs in grp[1:]] for kk, grp in small.items()}
    small_parts = all_gather("gather_small_grads", [packed["f32"][0][0], packed["bf16"][0][0].astype(BF16)])

    res = {}

    def update(nm, part, w, m, v):
        shp = w.shape
        as2d = lambda a: a.reshape(-1, shp[-1])
        outs = adamw(f"adamw_{nm}", part.reshape(part.shape[:1] + as2d(w).shape), as2d(w), as2d(m), as2d(v))
        res[nm] = [o.reshape(shp) for o in outs]

    update("mlp_w1", parts[0], mlp_w1, m_mlp_w1, v_mlp_w1)
    update("mlp_w2", parts[1], mlp_w2, m_mlp_w2, v_mlp_w2)
    update("ssm_w_glu", parts[2], ssm_w_glu, m_ssm_w_glu, v_ssm_w_glu)
    update("w_kvf", parts[3], w_kvf, m_w_kvf, v_w_kvf)
    update("attn_wq", parts[4], attn_wq, m_attn_wq, v_attn_wq)
    update("attn_wo", parts[5], attn_wo, m_attn_wo, v_attn_wo)
    update("ssm_d", parts[6], ssm_d, m_ssm_d, v_ssm_d)
    loss = None
    for (kk, (names, _, ws, _, _)), part in zip(small.items(), small_parts):
        (_, offs), (pw, _), (pm, _), (pv, _) = packed[kk]
        small_out = adamw(f"adamw_small_{kk}", part, pw, pm, pv)
        unpacked = [_unpack(o, offs, [w.shape for w in ws]) for o in small_out]
        for idx, nm in enumerate(names):
            res[nm] = [u[idx] for u in unpacked]
        if kk == "f32":
            loss = unpacked[0][-1][0]

    order = ["mix_norm", "mlp_norm", "mlp_w1", "mlp_w2", "ssm_log_dt", "ssm_a_re", "ssm_a_im", "ssm_b_re", "ssm_b_im", "ssm_c_re",
             "ssm_c_im", "ssm_d", "ssm_w_glu", "kv_norm", "w_kvf", "b_f", "attn_wq", "attn_wo", "final_norm"]
    out = [loss, grad_x]
    for kind in range(4):
        out += [res[nm][kind] for nm in order]
    return tuple(out)
```

```python
import functools

import jax
import jax.numpy as jnp
from jax import lax
from jax.experimental import pallas as pl
from jax.experimental.pallas import tpu as pltpu

F32 = jnp.float32
BF16 = jnp.bfloat16
HIGHEST = lax.Precision.HIGHEST

V7X_VMEM_BYTES = 64 << 20
VMEM_LIMIT_BYTES = (V7X_VMEM_BYTES * 3) // 4
LANES = 128
SUBLANES = 8

N_DEV = 8
RMS_EPS = 1e-6
SSM_GROUP = 16
SSM_STATE = 64
HEAD_DIM = 64
GROUPS_PER_BLOCK = LANES // SSM_GROUP
BLOCK_STATE = GROUPS_PER_BLOCK * SSM_STATE
BLOCK_COLS = 2 * BLOCK_STATE
NEG_BIG = -1e30
LOG2E = 1.4426950408889634

ADAM_LR = 0.001
ADAM_B1 = 0.9
ADAM_B2 = 0.999
ADAM_EPS = 1e-08
ADAM_WD = 0.01
ADAM_STEP = 10

TILES = {"row": 512, "mm": (1024, 1024, 1024), "blk": 512, "scan": 512, "cum": 256, "attn": 1024, "adam": 256}


def _pick(dim, pref, align=LANES):
    if dim <= pref:
        return dim
    for a in (align, SUBLANES):
        d = (pref // a) * a
        while d >= a:
            if dim % d == 0:
                return d
            d -= a
    return dim


def _params(*sem):
    return pltpu.CompilerParams(dimension_semantics=sem, vmem_limit_bytes=VMEM_LIMIT_BYTES)


def rowcall(name, fn, rows, consts, out_rows, out_accs=(), tm=None):
    n_rows = rows[0].shape[0]
    tm = _pick(n_rows, tm or TILES["row"], SUBLANES)
    nr, nc, no, na = len(rows), len(consts), len(out_rows), len(out_accs)

    def body(*refs):
        ins = [r[...] for r in refs[: nr + nc]]
        outs = fn(*ins)
        if not isinstance(outs, (tuple, list)):
            outs = (outs,)
        for r, o in zip(refs[nr + nc: nr + nc + no], outs[:no]):
            r[...] = o.astype(r.dtype)
        if na:
            i = pl.program_id(0)
            for r, o in zip(refs[nr + nc + no:], outs[no:]):
                @pl.when(i == 0)
                def _(r=r, o=o):
                    r[...] = o

                @pl.when(i > 0)
                def _(r=r, o=o):
                    r[...] += o

    in_specs = [pl.BlockSpec((tm, a.shape[1]), lambda i: (i, 0)) for a in rows]
    in_specs += [pl.BlockSpec(c.shape, lambda i, n=c.ndim: (0,) * n) for c in consts]
    out_shape = [jax.ShapeDtypeStruct((n_rows, c), dt) for c, dt in out_rows]
    out_specs = [pl.BlockSpec((tm, c), lambda i: (i, 0)) for c, _ in out_rows]
    out_shape += [jax.ShapeDtypeStruct(s, F32) for s in out_accs]
    out_specs += [pl.BlockSpec(s, lambda i, n=len(s): (0,) * n) for s in out_accs]
    res = pl.pallas_call(
        body, name=name, grid=(n_rows // tm,), in_specs=in_specs, out_specs=out_specs, out_shape=out_shape,
        compiler_params=_params("arbitrary" if na else "parallel"),
    )(*rows, *consts)
    return res


_DIMS = {"nn": (((1,), (0,)), ((), ())), "nt": (((1,), (1,)), ((), ())), "tn": (((0,), (0,)), ((), ()))}


def matmul(name, a, b, mode="nn", *, a_fn=None, scale=None, resid=None, post=None, post_arg=None, rms_bwd_of=None,
           out_dtype=F32, bm=None, bn=None, bk=None):
    if mode == "nn":
        (m, k), (k2, n) = a.shape, b.shape
    elif mode == "nt":
        (m, k), (n, k2) = a.shape, b.shape
    else:
        (k, m), (k2, n) = a.shape, b.shape
    assert k == k2, (name, a.shape, b.shape, mode)
    if rms_bwd_of is not None:
        bn = n
    bm, bn, bk = _pick(m, bm or TILES["mm"][0]), _pick(n, bn or TILES["mm"][1]), _pick(k, bk or TILES["mm"][2])
    nk = k // bk
    a_spec = pl.BlockSpec((bk, bm), lambda i, j, kk: (kk, i)) if mode == "tn" else pl.BlockSpec((bm, bk), lambda i, j, kk: (i, kk))
    b_spec = pl.BlockSpec((bn, bk), lambda i, j, kk: (j, kk)) if mode == "nt" else pl.BlockSpec((bk, bn), lambda i, j, kk: (kk, j))
    mn_spec = pl.BlockSpec((bm, bn), lambda i, j, kk: (i, j))
    extra = [x for x in (resid, post_arg) if x is not None]
    has_resid, has_post, has_rms = resid is not None, post is not None, rms_bwd_of is not None
    row_spec = pl.BlockSpec((1, bn), lambda i, j, kk: (0, j))
    extra_specs = [mn_spec] * len(extra)
    if has_rms:
        x_in, gain, add = rms_bwd_of
        extra += [x_in, add, gain]
        extra_specs += [mn_spec, mn_spec, row_spec]

    def body(*refs):
        a_ref, b_ref = refs[0], refs[1]
        ex = refs[2: 2 + len(extra)]
        o_ref = refs[2 + len(extra)]
        av = a_ref[...]
        if a_fn is not None:
            av = a_fn(av.astype(F32))
        p = lax.dot_general(av.astype(BF16), b_ref[...].astype(BF16), _DIMS[mode], preferred_element_type=F32)

        def finish(acc):
            if scale is not None:
                acc = acc * scale
            idx = 0
            if has_resid:
                acc = acc + ex[idx][...]
                idx += 1
            if has_post:
                acc = post(acc, ex[idx][...])
                idx += 1
            if has_rms:
                _, vjp = jax.vjp(_rms, ex[idx][...], ex[idx + 2][...])
                dx, d_gain = vjp(acc)
                acc = dx + ex[idx + 1][...]
                dg_ref = refs[3 + len(extra)]
                first = pl.program_id(0) == 0

                @pl.when(first)
                def _():
                    dg_ref[...] = d_gain

                @pl.when(jnp.logical_not(first))
                def _():
                    dg_ref[...] += d_gain
            o_ref[...] = acc.astype(o_ref.dtype)

        if nk == 1:
            finish(p)
        else:
            acc_ref = refs[-1]
            kk = pl.program_id(2)

            @pl.when(kk == 0)
            def _():
                acc_ref[...] = p

            @pl.when(kk > 0)
            def _():
                acc_ref[...] += p

            @pl.when(kk == nk - 1)
            def _():
                finish(acc_ref[...])

    out_shape, out_specs = jax.ShapeDtypeStruct((m, n), out_dtype), mn_spec
    if has_rms:
        out_shape, out_specs = [out_shape, jax.ShapeDtypeStruct((1, n), F32)], [mn_spec, row_spec]
    return pl.pallas_call(
        body, name=name, grid=(m // bm, n // bn, nk),
        in_specs=[a_spec, b_spec] + extra_specs, out_specs=out_specs, out_shape=out_shape,
        scratch_shapes=[pltpu.VMEM((bm, bn), F32)] if nk > 1 else [],
        compiler_params=_params(*(("arbitrary",) * 3 if has_rms else ("parallel", "parallel", "arbitrary"))),
    )(a, b, *extra)


def _cmul(ar, ai, br, bi):
    return ar * br - ai * bi, ar * bi + ai * br


def _scan_tables(lam_ref, reverse):
    shape = (SUBLANES, BLOCK_STATE)
    lr = jnp.broadcast_to(lam_ref[0:1, :], shape)
    li = jnp.broadcast_to(lam_ref[1:2, :], shape)
    if reverse:
        li = -li
    row = lax.broadcasted_iota(jnp.int32, shape, 0)
    tt = (SUBLANES - 1 - row) if reverse else row
    l1 = (lr, li)
    l2 = _cmul(*l1, *l1)
    l4 = _cmul(*l2, *l2)
    pr, pi = l1
    for bit, lp in enumerate((l1, l2, l4)):
        qr, qi = _cmul(pr, pi, *lp)
        on = ((tt >> bit) & 1) == 1
        pr, pi = jnp.where(on, qr, pr), jnp.where(on, qi, pi)
    steps = []
    for d, lp in ((1, l1), (2, l2), (4, l4)):
        ok = tt >= d
        steps.append((d, jnp.where(ok, lp[0], 0.0), jnp.where(ok, lp[1], 0.0)))
    return steps, (pr, pi)


def _rms(x, g):
    return x * lax.rsqrt(jnp.mean(x * x, axis=-1, keepdims=True) + RMS_EPS) * g


def _sigmoid(x):
    return 1.0 / (1.0 + jnp.exp(-x))


def _gelu(x):
    return 0.5 * x * (1.0 + jnp.tanh(0.7978845608028654 * (x + 0.044715 * (x * x * x))))


def _log_sigmoid(x):
    return jnp.minimum(x, 0.0) - jnp.log(1.0 + jnp.exp(-jnp.abs(x)))


def _sqrelu(x):
    r = jnp.maximum(x, 0.0)
    return r * r


def _s5_discretise(ldt, ar, ai, br, bi):
    dt = jnp.exp(ldt)
    er = jnp.exp(ar * dt)
    lr, li = er * jnp.cos(ai * dt), er * jnp.sin(ai * dt)
    nr, ni = lr - 1.0, li
    den = ar * ar + ai * ai
    cr, ci = (nr * ar + ni * ai) / den, (ni * ar - nr * ai) / den
    return lr, li, cr * br - ci * bi, cr * bi + ci * br


def rms_fwd(name, x, g, dtype=F32):
    return rowcall(name, _rms, [x], [g], [(x.shape[1], dtype)])[0]


def rms_bwd(name, x, g, dy, add=None):
    def fn(x, dy, *rest):
        g = rest[-1]
        _, vjp = jax.vjp(_rms, x, g)
        dx, dg = vjp(dy)
        if add is not None:
            dx = dx + rest[0]
        return dx, dg

    rows = [x, dy] + ([add] if add is not None else [])
    return rowcall(name, fn, rows, [g], [(x.shape[1], F32)], [g.shape])


def _split3(x):
    hi = x.astype(BF16).astype(F32)
    r = x - hi
    mid = r.astype(BF16).astype(F32)
    return hi, mid, (r - mid).astype(BF16).astype(F32)


def cum_logf(name, fl, bf, tm=None):
    n_rows, w = fl.shape
    tm = _pick(n_rows, tm or TILES["cum"], SUBLANES)

    def body(fl_ref, bf_ref, hi_ref, mid_ref, lo_ref, carry_ref):
        it = pl.program_id(0)

        @pl.when(it == 0)
        def _():
            carry_ref[...] = jnp.zeros_like(carry_ref)

        ls = _log_sigmoid(fl_ref[...] + bf_ref[...])
        tri = (lax.broadcasted_iota(jnp.int32, (tm, tm), 0) >= lax.broadcasted_iota(jnp.int32, (tm, tm), 1)).astype(F32)
        c = jnp.dot(tri, ls, precision=HIGHEST, preferred_element_type=F32) + carry_ref[0:1, :]
        carry_ref[...] = jnp.broadcast_to(c[tm - 1:tm, :], carry_ref.shape)
        hi_ref[...], mid_ref[...], lo_ref[...] = _split3(c * (-LOG2E))

    spec = pl.BlockSpec((tm, w), lambda i: (i, 0))
    return pl.pallas_call(
        body, name=name, grid=(n_rows // tm,),
        in_specs=[spec, pl.BlockSpec((1, w), lambda i: (0, 0))],
        out_specs=[spec] * 3, out_shape=[jax.ShapeDtypeStruct((n_rows, w), F32)] * 3,
        scratch_shapes=[pltpu.VMEM((SUBLANES, w), F32)],
        compiler_params=_params("arbitrary"),
    )(fl, bf)


def cum_logf_bwd(name, fl, bf, plus, minus, tm=None):
    n_rows, w = fl.shape
    tm = _pick(n_rows, tm or TILES["cum"], SUBLANES)
    nt = n_rows // tm
    n_p, n_m = len(plus), len(minus)

    def body(*refs):
        fl_ref, bf_ref = refs[0], refs[1]
        d_refs = refs[2: 2 + n_p + n_m]
        o_ref, db_ref, carry_ref = refs[2 + n_p + n_m:]
        it = pl.program_id(0)

        @pl.when(it == 0)
        def _():
            carry_ref[...] = jnp.zeros_like(carry_ref)

        d = None
        for r in d_refs[:n_p]:
            d = r[...] if d is None else d + r[...]
        for r in d_refs[n_p:]:
            d = -r[...] if d is None else d - r[...]
        tri = (lax.broadcasted_iota(jnp.int32, (tm, tm), 0) <= lax.broadcasted_iota(jnp.int32, (tm, tm), 1)).astype(F32)
        c = jnp.dot(tri, d, precision=HIGHEST, preferred_element_type=F32) + carry_ref[0:1, :]
        carry_ref[...] = jnp.broadcast_to(c[0:1, :], carry_ref.shape)
        dfl = c * _sigmoid(-(fl_ref[...] + bf_ref[...]))
        o_ref[...] = dfl
        part = jnp.sum(dfl, axis=0, keepdims=True)

        @pl.when(it == 0)
        def _():
            db_ref[...] = part

        @pl.when(it > 0)
        def _():
            db_ref[...] += part

    rev = lambda i: (nt - 1 - i, 0)
    return pl.pallas_call(
        body, name=name, grid=(nt,),
        in_specs=[pl.BlockSpec((tm, w), rev), pl.BlockSpec((1, w), lambda i: (0, 0))] + [pl.BlockSpec((tm, w), rev)] * (n_p + n_m),
        out_specs=[pl.BlockSpec((tm, w), rev), pl.BlockSpec((1, w), lambda i: (0, 0))],
        out_shape=[jax.ShapeDtypeStruct((n_rows, w), F32), jax.ShapeDtypeStruct((1, w), F32)],
        scratch_shapes=[pltpu.VMEM((SUBLANES, w), F32)],
        compiler_params=_params("arbitrary"),
    )(fl, bf, *plus, *minus)


ROWSUM_LANE = HEAD_DIM + 6
F_LANE = HEAD_DIM
LSE_LANE = HEAD_DIM + 3
SUM_LANE = HEAD_DIM
DELTA_LANE = HEAD_DIM + 1


def _lane_consts(pairs):
    lane = lax.broadcasted_iota(jnp.int32, (1, LANES), 1)
    out = jnp.zeros((1, LANES), F32)
    for lo, hi, v in pairs:
        out = jnp.where((lane >= lo) & (lane < hi), v, out)
    return out


def pack_heads(name, x, col_block, n_heads, consts, parts=(), parts_lane=0, tm=None):
    n_rows = x.shape[0]
    d = n_heads * HEAD_DIM
    assert n_heads % 2 == 0
    tm = _pick(n_rows, tm or TILES["row"], 2 * SUBLANES)
    n_parts = len(parts)

    def body(*refs):
        x_ref, c_ref = refs[0], refs[1]
        p_vals = [r[...] for r in refs[2: 2 + n_parts]]
        o_ref = refs[2 + n_parts]
        lane = lax.broadcasted_iota(jnp.int32, (tm, LANES), 1)
        tail0 = jnp.broadcast_to(c_ref[...], (tm, LANES))
        for h in range(n_heads):
            pair = x_ref[:, (h // 2) * LANES: (h // 2 + 1) * LANES].astype(F32)
            base = pair if h % 2 == 0 else pltpu.roll(pair, HEAD_DIM, 1)
            tail = tail0
            for kk, p in enumerate(p_vals):
                col = jnp.sum(jnp.where(lane == h, p, 0.0), axis=1, keepdims=True)
                tail = jnp.where(lane == parts_lane + kk, col, tail)
            o_ref[h] = jnp.where(lane < HEAD_DIM, base, tail).astype(BF16)

    return pl.pallas_call(
        body, name=name, grid=(n_rows // tm,),
        in_specs=[pl.BlockSpec((tm, d), lambda i: (i, col_block)), pl.BlockSpec((1, LANES), lambda i: (0, 0))]
        + [pl.BlockSpec((tm, LANES), lambda i: (i, 0))] * n_parts,
        out_specs=pl.BlockSpec((n_heads, tm, LANES), lambda i: (0, i, 0)),
        out_shape=jax.ShapeDtypeStruct((n_heads, n_rows, LANES), BF16),
        compiler_params=_params("parallel"),
    )(x, consts, *parts)


def unpack_heads(name, xs, extract_lane=None, tm=None):
    n_heads, n_rows, _ = xs[0].shape
    assert n_heads % 2 == 0
    tm = _pick(n_rows, tm or TILES["row"], SUBLANES)
    n = len(xs)

    def body(*refs):
        o_ref = refs[n]
        lane = lax.broadcasted_iota(jnp.int32, (tm, LANES), 1)
        picked = jnp.zeros((tm, LANES), F32)

        def head(h):
            v = refs[0][h]
            for r in refs[1:n]:
                v = v + r[h]
            return v

        for p in range(n_heads // 2):
            a, b = head(2 * p), head(2 * p + 1)
            o_ref[:, p * LANES: (p + 1) * LANES] = jnp.where(lane < HEAD_DIM, a, pltpu.roll(b, HEAD_DIM, 1))
            if extract_lane is not None:
                for hh, v in ((2 * p, a), (2 * p + 1, b)):
                    col = jnp.sum(jnp.where(lane == extract_lane, v, 0.0), axis=1, keepdims=True)
                    picked = jnp.where(lane == hh, col, picked)
        if extract_lane is not None:
            refs[n + 1][...] = picked

    d = n_heads * HEAD_DIM
    out_shape = [jax.ShapeDtypeStruct((n_rows, d), F32)]
    out_specs = [pl.BlockSpec((tm, d), lambda i: (i, 0))]
    if extract_lane is not None:
        out_shape.append(jax.ShapeDtypeStruct((n_rows, LANES), F32))
        out_specs.append(pl.BlockSpec((tm, LANES), lambda i: (i, 0)))
    return pl.pallas_call(
        body, name=name, grid=(n_rows // tm,),
        in_specs=[pl.BlockSpec((n_heads, tm, LANES), lambda i: (0, i, 0))] * n,
        out_specs=out_specs, out_shape=out_shape, compiler_params=_params("parallel"),
    )(*xs)


def fox_fwd(name, q_ext, k_ext, v_ext, t=None):
    nh, n_rows, w = q_ext.shape
    t = _pick(n_rows, t or TILES["attn"])
    nt = n_rows // t
    half = t // 2

    def body(q_ref, k_ref, v_ref, o_ref, qb_ref, m_ref, acc_ref):
        i = pl.program_id(1)
        m_ref[...] = jnp.full_like(m_ref, NEG_BIG)
        acc_ref[...] = jnp.zeros_like(acc_ref)
        q = q_ref[...]

        def piece(rows, kv_off, size, diagonal):
            kv = pl.ds(pl.multiple_of(kv_off, size), size)
            s = lax.dot_general(q_ref[rows, :], k_ref[kv, :], _DIMS["nt"], preferred_element_type=F32)
            if diagonal:
                keep = lax.broadcasted_iota(jnp.int32, s.shape, 0) >= lax.broadcasted_iota(jnp.int32, s.shape, 1)
                s = jnp.where(keep, s, NEG_BIG)
            m_prev = m_ref[rows, :]
            m_new = jnp.maximum(m_prev, jnp.max(s, axis=1, keepdims=True))
            p = jnp.exp2(s - jnp.tile(m_new, (1, size // LANES)))
            acc_ref[rows, :] = jnp.exp2(m_prev - m_new) * acc_ref[rows, :] + jnp.dot(
                p.astype(BF16), v_ref[kv, :], preferred_element_type=F32)
            m_ref[rows, :] = m_new

        def off_diagonal(j, carry):
            piece(slice(0, t), j * t, t, False)
            return carry

        lax.fori_loop(0, i, off_diagonal, 0)
        if half % LANES == 0:
            piece(slice(0, half), i * t, half, True)
            piece(slice(half, t), i * t, half, False)
            piece(slice(half, t), i * t + half, half, True)
        else:
            piece(slice(0, t), i * t, t, True)
        acc = acc_ref[...]
        row_sum = acc[:, HEAD_DIM:HEAD_DIM + 1]
        hi, mid, lo = _split3(m_ref[:, 0:1] + jnp.log2(row_sum))
        lane = lax.broadcasted_iota(jnp.int32, (t, w), 1)
        o_ref[...] = acc / row_sum
        qb = jnp.where(lane == LSE_LANE, hi, jnp.where(lane == LSE_LANE + 1, mid, jnp.where(lane == LSE_LANE + 2, lo, q.astype(F32))))
        qb_ref[...] = qb.astype(BF16)

    whole = pl.BlockSpec((None, n_rows, w), lambda h, i: (h, 0, 0))
    tile = pl.BlockSpec((None, t, w), lambda h, i: (h, i, 0))
    return pl.pallas_call(
        body, name=name, grid=(nh, nt), in_specs=[tile, whole, whole], out_specs=[tile, tile],
        out_shape=[jax.ShapeDtypeStruct((nh, n_rows, w), F32), jax.ShapeDtypeStruct((nh, n_rows, w), BF16)],
        scratch_shapes=[pltpu.VMEM((t, w), F32), pltpu.VMEM((t, w), F32)],
        compiler_params=_params("parallel", "arbitrary"),
    )(q_ext, k_ext, v_ext)


def fox_bwd(name, q_ext, do_ext, k_ext, v_ext, t=None):
    nh, n_rows, w = q_ext.shape
    t = _pick(n_rows, t or TILES["attn"])
    nt = n_rows // t
    half = t // 2

    def body(q_ref, do_ref, k_ref, v_ref, dq_ref, dk_ref, dv_ref):
        j = pl.program_id(1)

        @pl.when(j == 0)
        def _():
            dq_ref[...] = jnp.zeros_like(dq_ref)

        dk_ref[...] = jnp.zeros_like(dk_ref)
        dv_ref[...] = jnp.zeros_like(dv_ref)
        def piece(keys, q_off, size, diagonal):
            qs = pl.ds(pl.multiple_of(q_off, size), size)
            kj, vj, qi, doi = k_ref[keys, :], v_ref[keys, :], q_ref[qs, :], do_ref[qs, :]
            pt = jnp.exp2(lax.dot_general(kj, qi, _DIMS["nt"], preferred_element_type=F32))
            if diagonal:
                keep = lax.broadcasted_iota(jnp.int32, pt.shape, 0) <= lax.broadcasted_iota(jnp.int32, pt.shape, 1)
                pt = jnp.where(keep, pt, 0.0)
            dst = (pt * lax.dot_general(vj, doi, _DIMS["nt"], preferred_element_type=F32)).astype(BF16)
            dv_ref[keys, :] += jnp.dot(pt.astype(BF16), doi, preferred_element_type=F32)
            dk_ref[keys, :] += jnp.dot(dst, qi, preferred_element_type=F32)
            dq_ref[qs, :] += lax.dot_general(dst, kj, _DIMS["tn"], preferred_element_type=F32)

        def off_diagonal(i, carry):
            piece(slice(0, t), i * t, t, False)
            return carry

        if half % LANES == 0:
            piece(slice(0, half), j * t, half, True)
            piece(slice(0, half), j * t + half, half, False)
            piece(slice(half, t), j * t + half, half, True)
        else:
            piece(slice(0, t), j * t, t, True)
        lax.fori_loop(j + 1, nt, off_diagonal, 0)
        lane = lax.broadcasted_iota(jnp.int32, (t, w), 1)
        dk_ref[...] = dk_ref[...] * jnp.where(lane < HEAD_DIM, 1.0 / LOG2E, 1.0)

    whole = pl.BlockSpec((None, n_rows, w), lambda h, j: (h, 0, 0))
    tile = pl.BlockSpec((None, t, w), lambda h, j: (h, j, 0))
    shape = jax.ShapeDtypeStruct((nh, n_rows, w), F32)
    return pl.pallas_call(
        body, name=name, grid=(nh, nt), in_specs=[whole, whole, tile, tile], out_specs=[whole, tile, tile],
        out_shape=[shape, shape, shape],
        compiler_params=_params("parallel", "arbitrary"),
    )(q_ext, do_ext, k_ext, v_ext)


def attn_delta(name, o, do):
    d_model = o.shape[1]
    head_of_col = lax.broadcasted_iota(jnp.int32, (d_model, LANES), 0) // HEAD_DIM
    sel = (head_of_col == lax.broadcasted_iota(jnp.int32, (d_model, LANES), 1)).astype(F32)

    def fn(a, b, s):
        return _split3(jnp.dot(a * b, s, precision=HIGHEST, preferred_element_type=F32))

    return rowcall(name, fn, [o, do], [sel], [(LANES, F32)] * 3)


def all_gather(name, tensors):
    n = len(tensors)

    def body(*refs):
        start, finish = _gather_phases(refs[:n], refs[n: 2 * n], *refs[2 * n:])
        start()
        finish()

    any_spec = pl.BlockSpec(memory_space=pl.ANY)
    return pl.pallas_call(
        body, name=name, in_specs=[any_spec] * n, out_specs=[any_spec] * n,
        out_shape=_gather_shapes(tensors), scratch_shapes=_gather_semaphores(n),
        compiler_params=pltpu.CompilerParams(has_side_effects=True),
    )(*tensors)


def _gather_shapes(tensors):
    return [jax.ShapeDtypeStruct((N_DEV,) + t.shape, t.dtype) for t in tensors]


def _gather_semaphores(n):
    return [pltpu.SemaphoreType.DMA((n, 7)), pltpu.SemaphoreType.DMA((n, 7)), pltpu.SemaphoreType.DMA((n,))]


def _gather_phases(ins, outs, send_sems, recv_sems, local_sems):
    n = len(ins)
    x, y, c = lax.axis_index("x"), lax.axis_index("y"), lax.axis_index("c")
    sibling = (x, y, 1 - c)
    chips = [(1 - x, y), (x, 1 - y), (1 - x, 1 - y)]
    slot = lambda px, py, pc: 4 * px + 2 * py + pc

    def copy(t, k, block, to, src=None):
        dst = outs[t].at[slot(*block)]
        return pltpu.make_async_remote_copy(
            src_ref=dst if src is None else src, dst_ref=dst, send_sem=send_sems.at[t, k], recv_sem=recv_sems.at[t, k],
            device_id=to, device_id_type=pl.DeviceIdType.MESH)

    own = lambda t: pltpu.make_async_copy(ins[t], outs[t].at[slot(x, y, c)], local_sems.at[t])
    first = lambda t: [copy(t, 0, (x, y, c), sibling, ins[t])] + [copy(t, 1 + j, (x, y, c), (*chip, c), ins[t]) for j, chip in enumerate(chips)]

    def start():
        for t in range(n):
            own(t).start()
            for cp in first(t):
                cp.start()

    def finish():
        sends = []
        for t in range(n):
            sends += first(t)
            for j, chip in enumerate(chips):
                copy(t, 1 + j, (*chip, c), (x, y, c)).wait_recv()
                passed = copy(t, 4 + j, (*chip, c), sibling)
                passed.start()
                sends.append(passed)
        for t in range(n):
            copy(t, 0, sibling, (x, y, c)).wait_recv()
            for j, chip in enumerate(chips):
                copy(t, 4 + j, (*chip, 1 - c), (x, y, c)).wait_recv()
        for cp in sends:
            cp.wait_send()
        for t in range(n):
            own(t).wait()

    return start, finish


def _remote_call(name, body, tensors, out_shapes, n_copies):
    n = len(tensors)
    any_spec = pl.BlockSpec(memory_space=pl.ANY)
    return pl.pallas_call(
        body, name=name, in_specs=[any_spec] * n, out_specs=[any_spec] * n, out_shape=out_shapes,
        scratch_shapes=[pltpu.SemaphoreType.DMA((n, n_copies)), pltpu.SemaphoreType.DMA((n, n_copies)), pltpu.SemaphoreType.DMA((n,))],
        compiler_params=pltpu.CompilerParams(has_side_effects=True),
    )(*tensors)


N_CHIPS = 4
_CHIPS = [(0, 0), (0, 1), (1, 0), (1, 1)]


def pair_swap(name, tensors):
    n = len(tensors)

    def body(*refs):
        ins, outs = refs[:n], refs[n: 2 * n]
        send_sems, recv_sems, _ = refs[2 * n:]
        x, y, c = lax.axis_index("x"), lax.axis_index("y"), lax.axis_index("c")
        copies = []
        for t in range(n):
            for k in range(N_CHIPS):
                cp = pltpu.make_async_remote_copy(
                    src_ref=ins[t].at[2 * k + (1 - c)], dst_ref=outs[t].at[k], send_sem=send_sems.at[t, k], recv_sem=recv_sems.at[t, k],
                    device_id=(x, y, 1 - c), device_id_type=pl.DeviceIdType.MESH)
                cp.start()
                copies.append(cp)
        for cp in copies:
            cp.wait()

    return _remote_call(name, body, tensors, [jax.ShapeDtypeStruct((N_CHIPS,) + t.shape[1:], t.dtype) for t in tensors], N_CHIPS)


def chip_exchange(name, tensors):
    n = len(tensors)

    def body(*refs):
        start, finish = _chip_exchange_phases(refs[:n], refs[n: 2 * n], *refs[2 * n:])
        start()
        finish()

    return _remote_call(name, body, tensors, [jax.ShapeDtypeStruct(t.shape, t.dtype) for t in tensors], len(_CHIP_FLIPS))


_CHIP_FLIPS = [(1, 0), (0, 1), (1, 1)]


def _chip_exchange_semaphores(n):
    return [pltpu.SemaphoreType.DMA((n, len(_CHIP_FLIPS))), pltpu.SemaphoreType.DMA((n, len(_CHIP_FLIPS))), pltpu.SemaphoreType.DMA((n,))]


def _chip_exchange_phases(ins, outs, send_sems, recv_sems, local_sems):
    n = len(ins)
    x, y, c = lax.axis_index("x"), lax.axis_index("y"), lax.axis_index("c")
    me = 2 * x + y

    def copies(landing):
        out = []
        for t in range(n):
            for j, (bx, by) in enumerate(_CHIP_FLIPS):
                px, py = (1 - x if bx else x), (1 - y if by else y)
                peer = 2 * px + py
                out.append(pltpu.make_async_remote_copy(
                    src_ref=ins[t].at[peer], dst_ref=outs[t].at[peer if landing else me], send_sem=send_sems.at[t, j],
                    recv_sem=recv_sems.at[t, j], device_id=(px, py, c), device_id_type=pl.DeviceIdType.MESH))
        return out

    own = lambda t: pltpu.make_async_copy(ins[t].at[me], outs[t].at[me], local_sems.at[t])

    def start():
        for t in range(n):
            own(t).start()
        for cp in copies(False):
            cp.start()

    def finish():
        for cp in copies(True):
            cp.wait()
        for t in range(n):
            own(t).wait()

    return start, finish


def adamw(name, parts, w, m, v, tr=None):
    n_parts = parts.shape[0]
    n_rows, n_cols = w.shape
    tr = _pick(n_rows, tr or TILES["adam"], SUBLANES)
    c1 = 1.0 - ADAM_B1 ** ADAM_STEP
    c2 = 1.0 - ADAM_B2 ** ADAM_STEP

    def body(p_ref, w_ref, m_ref, v_ref, g_ref, d_ref, mo_ref, vo_ref):
        g = p_ref[0].astype(F32)
        for s in range(1, n_parts):
            g = g + p_ref[s].astype(F32)
        mn = ADAM_B1 * m_ref[...] + (1.0 - ADAM_B1) * g
        vn = ADAM_B2 * v_ref[...] + (1.0 - ADAM_B2) * (g * g)
        g_ref[...] = g
        mo_ref[...] = mn
        vo_ref[...] = vn
        d_ref[...] = -ADAM_LR * ((mn / c1) / (jnp.sqrt(vn / c2) + ADAM_EPS) + ADAM_WD * w_ref[...])

    spec = pl.BlockSpec((tr, n_cols), lambda i: (i, 0))
    return pl.pallas_call(
        body, name=name, grid=(n_rows // tr,),
        in_specs=[pl.BlockSpec((n_parts, tr, n_cols), lambda i: (0, i, 0)), spec, spec, spec],
        out_specs=[spec] * 4, out_shape=[jax.ShapeDtypeStruct(w.shape, F32)] * 4,
        compiler_params=_params("parallel"),
    )(parts, w, m, v)


def _eye_mask():
    return jnp.eye(GROUPS_PER_BLOCK, dtype=F32)


def _b_blocks(bbr, bbi):
    nb = bbr.shape[0] // (GROUPS_PER_BLOCK * SSM_STATE)
    eye = _eye_mask()[None, :, None, :, None]

    def one(z):
        z = z.reshape(nb, GROUPS_PER_BLOCK, SSM_STATE, SSM_GROUP).transpose(0, 1, 3, 2)
        return z[:, :, :, None, :] * eye

    w = jnp.stack([one(bbr), one(bbi)], axis=3)
    return w.reshape(nb, LANES, BLOCK_COLS)


def _b_blocks_t(dw):
    nb = dw.shape[0]
    d6 = dw.reshape(nb, GROUPS_PER_BLOCK, SSM_GROUP, 2, GROUPS_PER_BLOCK, SSM_STATE)
    diag = jnp.sum(d6 * _eye_mask()[None, :, None, None, :, None], axis=4)
    diag = diag.transpose(3, 0, 1, 4, 2).reshape(2, nb * GROUPS_PER_BLOCK * SSM_STATE, SSM_GROUP)
    return diag[0], diag[1]


def _c_blocks(c_re, c_im):
    nb = c_re.shape[0] // GROUPS_PER_BLOCK
    eye = _eye_mask()[None, :, None, :, None]

    def one(z):
        z = z.reshape(nb, GROUPS_PER_BLOCK, SSM_GROUP, SSM_STATE).transpose(0, 1, 3, 2)
        return z[:, :, :, None, :] * eye

    w = jnp.stack([one(c_re), -one(c_im)], axis=1)
    return w.reshape(nb, BLOCK_COLS, LANES)


def _c_blocks_t(dw):
    nb = dw.shape[0]
    d6 = dw.reshape(nb, 2, GROUPS_PER_BLOCK, SSM_STATE, GROUPS_PER_BLOCK, SSM_GROUP)
    diag = jnp.sum(d6 * _eye_mask()[None, None, :, None, :, None], axis=4)
    diag = diag.transpose(1, 0, 2, 4, 3).reshape(2, nb * GROUPS_PER_BLOCK, SSM_GROUP, SSM_STATE)
    return diag[0], -diag[1]


def _unshard_cols(g):
    s, k, n = g.shape
    return g.transpose(1, 0, 2).reshape(k, s * n)


def _shard_cols(w):
    k, n = w.shape
    return w.reshape(k, N_DEV, n // N_DEV).transpose(1, 0, 2)


def _pack(arrays):
    chunks, offs, row = [], [], 0
    for a in arrays:
        flat = a.reshape(-1).astype(F32)
        rows = -(-flat.shape[0] // LANES)
        chunks.append(jnp.pad(flat, (0, rows * LANES - flat.shape[0])))
        offs.append((row, rows))
        row += rows
    pad_rows = (-row) % SUBLANES
    if pad_rows:
        chunks.append(jnp.zeros((pad_rows * LANES,), F32))
    return jnp.concatenate(chunks).reshape(row + pad_rows, LANES), offs


def _unpack(packed, offs, shapes):
    out = []
    for (row, rows), shp in zip(offs, shapes):
        size = 1
        for s in shp:
            size *= s
        out.append(packed[row: row + rows].reshape(-1)[:size].reshape(shp))
    return out


SCAN_SUB_ROWS = 128


def _scan_tables_into(tab_ref, lam_ref, reverse):
    steps, (pr, pi) = _scan_tables(lam_ref, reverse)
    for kk, (_, mr, mi) in enumerate(steps):
        tab_ref[2 * kk] = mr
        tab_ref[2 * kk + 1] = mi
    tab_ref[6] = pr
    tab_ref[7] = pi


def _scan_in_place(buf_ref, tab_ref, carry_ref, ng, reverse, prev_ref=None, acc_ref=None, row0=0, unroll=False):
    shape = (SUBLANES, BLOCK_STATE)
    last = 0 if reverse else SUBLANES - 1
    first_row = lax.broadcasted_iota(jnp.int32, shape, 0) == 0
    re_cols, im_cols = pl.ds(0, BLOCK_STATE), pl.ds(BLOCK_STATE, BLOCK_STATE)

    def group(r, carry):
        cr, ci = carry[0], carry[1]
        rr = (ng - 1 - r) if reverse else r
        off = row0 + rr * SUBLANES
        off = off if unroll else pl.multiple_of(off, SUBLANES)
        xr, xi = buf_ref[pl.ds(off, SUBLANES), re_cols], buf_ref[pl.ds(off, SUBLANES), im_cols]
        for kk, d in enumerate((1, 2, 4)):
            sh = (SUBLANES - d) if reverse else d
            mr, mi = tab_ref[2 * kk], tab_ref[2 * kk + 1]
            yr, yi = pltpu.roll(xr, sh, 0), pltpu.roll(xi, sh, 0)
            xr, xi = xr + mr * yr - mi * yi, xi + mr * yi + mi * yr
        pr, pi = tab_ref[6], tab_ref[7]
        xr, xi = xr + pr * cr - pi * ci, xi + pr * ci + pi * cr
        buf_ref[pl.ds(off, SUBLANES), re_cols] = xr
        buf_ref[pl.ds(off, SUBLANES), im_cols] = xi
        out = (jnp.broadcast_to(xr[last:last + 1, :], shape), jnp.broadcast_to(xi[last:last + 1, :], shape))
        if prev_ref is not None:
            off8 = off + SUBLANES if unroll else pl.multiple_of(off + SUBLANES, SUBLANES)
            before_r, before_i = prev_ref[pl.ds(off, SUBLANES), re_cols], prev_ref[pl.ds(off, SUBLANES), im_cols]
            same_r, same_i = prev_ref[pl.ds(off8, SUBLANES), re_cols], prev_ref[pl.ds(off8, SUBLANES), im_cols]
            sr = jnp.where(first_row, jnp.broadcast_to(before_r[SUBLANES - 1:, :], shape), pltpu.roll(same_r, 1, 0))
            si = jnp.where(first_row, jnp.broadcast_to(before_i[SUBLANES - 1:, :], shape), pltpu.roll(same_i, 1, 0))
            out += (carry[2] + xr * sr + xi * si, carry[3] + xi * sr - xr * si)
        return out

    init = (carry_ref[0], carry_ref[1])
    if prev_ref is not None:
        init += (acc_ref[0], acc_ref[1])
    if unroll:
        res = init
        for r in range(ng):
            res = group(r, res)
    else:
        res = lax.fori_loop(0, ng, group, init)
    carry_ref[0] = res[0]
    carry_ref[1] = res[1]
    if prev_ref is not None:
        acc_ref[0] = res[2]
        acc_ref[1] = res[3]


def s5_fused_fwd(name, u, wb, wc, lam, d_row, tm=None, gather=()):
    n_rows, d_model = u.shape
    nb = lam.shape[0]
    tm = _pick(n_rows, tm or TILES["scan"], 2 * SUBLANES)
    nt = n_rows // tm
    sub = _pick(tm, SCAN_SUB_ROWS, 2 * SUBLANES)
    ng = len(gather)

    def body(u_ref, wb_ref, wc_ref, lam_ref, d_ref, *refs):
        g_in, (z_ref, st_ref), g_out = refs[:ng], refs[ng: ng + 2], refs[ng + 2: 2 * ng + 2]
        buf_ref, carry_ref, tab_ref = refs[2 * ng + 2: 2 * ng + 5]
        b, it = pl.program_id(0), pl.program_id(1)
        if ng:
            start, finish = _gather_phases(g_in, g_out, *refs[2 * ng + 5:])
            pl.when((b == 0) & (it == 0))(start)

        @pl.when(it == 0)
        def _():
            carry_ref[...] = jnp.zeros_like(carry_ref)
            _scan_tables_into(tab_ref, lam_ref, False)

        for r0 in range(0, tm, sub):
            rows = slice(r0, r0 + sub)
            uu = u_ref[rows, :]
            buf_ref[rows, :] = jnp.dot(uu.astype(BF16), wb_ref[...], preferred_element_type=F32)
            _scan_in_place(buf_ref, tab_ref, carry_ref, sub // SUBLANES, False, row0=r0, unroll=True)
            st = buf_ref[rows, :].astype(BF16)
            st_ref[rows, :] = st
            z_ref[rows, :] = _gelu(jnp.dot(st, wc_ref[...], preferred_element_type=F32) + d_ref[...] * uu).astype(z_ref.dtype)
        if ng:
            pl.when((b == nb - 1) & (it == nt - 1))(finish)

    tile = lambda b, it: (it, b)
    blk = lambda b, it: (b, 0, 0)
    any_spec = pl.BlockSpec(memory_space=pl.ANY)
    params = pltpu.CompilerParams(dimension_semantics=("arbitrary", "arbitrary"), vmem_limit_bytes=VMEM_LIMIT_BYTES,
                                  has_side_effects=True) if ng else _params("parallel", "arbitrary")
    return pl.pallas_call(
        body, name=name, grid=(nb, nt),
        in_specs=[pl.BlockSpec((tm, LANES), tile), pl.BlockSpec((None, LANES, BLOCK_COLS), blk),
                  pl.BlockSpec((None, BLOCK_COLS, LANES), blk), pl.BlockSpec((None, 2, BLOCK_STATE), blk),
                  pl.BlockSpec((1, LANES), lambda b, it: (0, b))] + [any_spec] * ng,
        out_specs=[pl.BlockSpec((tm, LANES), tile), pl.BlockSpec((tm, BLOCK_COLS), tile)] + [any_spec] * ng,
        out_shape=[jax.ShapeDtypeStruct((n_rows, d_model), BF16), jax.ShapeDtypeStruct((n_rows, nb * BLOCK_COLS), BF16)]
        + _gather_shapes(gather),
        scratch_shapes=[pltpu.VMEM((tm, BLOCK_COLS), F32), pltpu.VMEM((2, SUBLANES, BLOCK_STATE), F32),
                        pltpu.VMEM((8, SUBLANES, BLOCK_STATE), F32)] + (_gather_semaphores(ng) if ng else []),
        compiler_params=params,
    )(u, wb, wc, lam, d_row, *gather)


def s5_fused_bwd(name, dz, u, st, wb, wc, lam, d_row, tm=None, exchange=()):
    ne = len(exchange)
    n_rows, d_model = u.shape
    nb = lam.shape[0]
    tm = _pick(n_rows, tm or TILES["scan"], 2 * SUBLANES)
    nt = n_rows // tm
    tail_rows = 2 * SUBLANES
    sub = _pick(tm, SCAN_SUB_ROWS, 2 * SUBLANES)

    def body(dz_ref, u_ref, st_ref, tail_ref, wb_ref, wc_ref, lam_ref, d_ref, *refs):
        e_in, (du_ref, dwb_ref, dwc_ref, dlam_ref, dd_ref), e_out = refs[:ne], refs[ne: ne + 5], refs[ne + 5: 2 * ne + 5]
        buf_ref, prev_ref, carry_ref, acc_ref, tab_ref = refs[2 * ne + 5: 2 * ne + 10]
        b, it = pl.program_id(0), pl.program_id(1)
        if ne:
            start, finish = _chip_exchange_phases(e_in, e_out, *refs[2 * ne + 10:])
            pl.when((b == 0) & (it == 0))(start)

        @pl.when(it == 0)
        def _():
            carry_ref[...] = jnp.zeros_like(carry_ref)
            acc_ref[...] = jnp.zeros_like(acc_ref)
            dwb_ref[...] = jnp.zeros_like(dwb_ref)
            dwc_ref[...] = jnp.zeros_like(dwc_ref)
            dd_ref[...] = jnp.zeros_like(dd_ref)
            _scan_tables_into(tab_ref, lam_ref, True)

        prev_ref[SUBLANES:, :] = st_ref[...].astype(F32)
        before = tail_ref[...].astype(F32)[SUBLANES:, :]
        prev_ref[:SUBLANES, :] = jnp.where(it == nt - 1, 0.0, before)
        for r0 in range(tm - sub, -1, -sub):
            rows = slice(r0, r0 + sub)
            uu, st_b = u_ref[rows, :], st_ref[rows, :]
            y = jnp.dot(st_b, wc_ref[...], preferred_element_type=F32) + d_ref[...] * uu
            _, vjp = jax.vjp(_gelu, y)
            dy = vjp(dz_ref[rows, :])[0]
            dyb = dy.astype(BF16)
            dd_ref[...] += jnp.sum(dy * uu, axis=0, keepdims=True)
            dwc_ref[...] += lax.dot_general(st_b, dyb, _DIMS["tn"], preferred_element_type=F32)
            buf_ref[rows, :] = lax.dot_general(dyb, wc_ref[...], _DIMS["nt"], preferred_element_type=F32)
            _scan_in_place(buf_ref, tab_ref, carry_ref, sub // SUBLANES, True, prev_ref, acc_ref, row0=r0, unroll=True)
            gb = buf_ref[rows, :].astype(BF16)
            du_ref[rows, :] = lax.dot_general(gb, wb_ref[...], _DIMS["nt"], preferred_element_type=F32) + dy * d_ref[...]
            dwb_ref[...] += lax.dot_general(uu.astype(BF16), gb, _DIMS["tn"], preferred_element_type=F32)

        @pl.when(it == nt - 1)
        def _():
            dlam_ref[0:1, :] = jnp.sum(acc_ref[0], axis=0, keepdims=True)
            dlam_ref[1:2, :] = jnp.sum(acc_ref[1], axis=0, keepdims=True)

        if ne:
            pl.when((b == nb - 1) & (it == nt - 1))(finish)

    any_spec = pl.BlockSpec(memory_space=pl.ANY)
    params = pltpu.CompilerParams(dimension_semantics=("arbitrary", "arbitrary"), vmem_limit_bytes=VMEM_LIMIT_BYTES,
                                  has_side_effects=True) if ne else _params("parallel", "arbitrary")
    tile = lambda b, it: (nt - 1 - it, b)
    blk = lambda b, it: (b, 0, 0)
    per_tile = tm // tail_rows
    tail = lambda b, it: (jnp.maximum((nt - 1 - it) * per_tile - 1, 0), b)
    return pl.pallas_call(
        body, name=name, grid=(nb, nt),
        in_specs=[pl.BlockSpec((tm, LANES), tile), pl.BlockSpec((tm, LANES), tile), pl.BlockSpec((tm, BLOCK_COLS), tile),
                  pl.BlockSpec((tail_rows, BLOCK_COLS), tail), pl.BlockSpec((None, LANES, BLOCK_COLS), blk),
                  pl.BlockSpec((None, BLOCK_COLS, LANES), blk), pl.BlockSpec((None, 2, BLOCK_STATE), blk),
                  pl.BlockSpec((1, LANES), lambda b, it: (0, b))] + [any_spec] * ne,
        out_specs=[pl.BlockSpec((tm, LANES), tile), pl.BlockSpec((None, LANES, BLOCK_COLS), blk),
                   pl.BlockSpec((None, BLOCK_COLS, LANES), blk), pl.BlockSpec((None, 2, BLOCK_STATE), blk),
                   pl.BlockSpec((1, LANES), lambda b, it: (0, b))] + [any_spec] * ne,
        out_shape=[jax.ShapeDtypeStruct((n_rows, d_model), F32), jax.ShapeDtypeStruct((nb, LANES, BLOCK_COLS), F32),
                   jax.ShapeDtypeStruct((nb, BLOCK_COLS, LANES), F32), jax.ShapeDtypeStruct((nb, 2, BLOCK_STATE), F32),
                   jax.ShapeDtypeStruct((1, d_model), F32)] + [jax.ShapeDtypeStruct(t.shape, t.dtype) for t in exchange],
        scratch_shapes=[pltpu.VMEM((tm, BLOCK_COLS), F32), pltpu.VMEM((tm + SUBLANES, BLOCK_COLS), F32),
                        pltpu.VMEM((2, SUBLANES, BLOCK_STATE), F32), pltpu.VMEM((2, SUBLANES, BLOCK_STATE), F32),
                        pltpu.VMEM((8, SUBLANES, BLOCK_STATE), F32)] + (_chip_exchange_semaphores(ne) if ne else []),
        compiler_params=params,
    )(dz, u, st, st, wb, wc, lam, d_row, *exchange)


def s5_fwd(tag, u, log_dt, a_re, a_im, b_re, b_im, c_re, c_im, d_row, gather=()):
    d_model = u.shape[1]
    nb = d_model // LANES
    col = lambda a: a.reshape(-1, 1)
    prm = [col(jnp.repeat(log_dt, SSM_STATE)), col(a_re), col(a_im), b_re.reshape(-1, SSM_GROUP), b_im.reshape(-1, SSM_GROUP)]
    lr, li, bbr, bbi = rowcall(f"s5_prep_{tag}", _s5_discretise, prm, [], [(1, F32), (1, F32), (SSM_GROUP, F32), (SSM_GROUP, F32)])
    lam = jnp.stack([lr.reshape(nb, BLOCK_STATE), li.reshape(nb, BLOCK_STATE)], axis=1)
    wb = _b_blocks(bbr, bbi).astype(BF16)
    wc = _c_blocks(c_re, c_im).astype(BF16)
    z, st, *gathered = s5_fused_fwd(f"s5_fwd_{tag}", u, wb, wc, lam, d_row, gather=gather)
    return z, dict(prm=prm, lam=lam, wb=wb, wc=wc, st=st), gathered


def s5_bwd(tag, dz, u, d_row, sv, exchange=()):
    d_model = u.shape[1]
    nb = d_model // LANES
    n_groups = d_model // SSM_GROUP
    col = lambda a: a.reshape(-1, 1)

    dhn, d_wb, d_wc, d_lam, d_dskip, *exchanged = s5_fused_bwd(
        f"s5_bwd_{tag}", dz, u, sv["st"], sv["wb"], sv["wc"], sv["lam"], d_row, exchange=exchange)
    d_bbr, d_bbi = _b_blocks_t(d_wb)
    d_cre, d_cim = _c_blocks_t(d_wc)

    def prep_bwd(ldt, ar, ai, br, bi, dlr, dli, dbr, dbi):
        _, vjp = jax.vjp(_s5_discretise, ldt, ar, ai, br, bi)
        return vjp((dlr, dli, dbr, dbi))

    d_ldt, d_are, d_aim, d_bre, d_bim = rowcall(
        f"s5_prep_bwd_{tag}", prep_bwd, sv["prm"] + [col(d_lam[:, 0]), col(d_lam[:, 1]), d_bbr, d_bbi], [],
        [(1, F32), (1, F32), (1, F32), (SSM_GROUP, F32), (SSM_GROUP, F32)])
    d_logdt = rowcall(f"s5_dlogdt_{tag}", lambda a: jnp.sum(a, axis=1, keepdims=True), [d_ldt.reshape(n_groups, SSM_STATE)], [], [(1, F32)])[0]
    grads = dict(log_dt=d_logdt.reshape(n_groups), a_re=d_are.reshape(n_groups, SSM_STATE), a_im=d_aim.reshape(n_groups, SSM_STATE),
                 b_re=d_bre.reshape(n_groups, SSM_STATE, SSM_GROUP), b_im=d_bim.reshape(n_groups, SSM_STATE, SSM_GROUP),
                 c_re=d_cre, c_im=d_cim, d=d_dskip)
    return dhn, grads, exchanged


def kernel(x, mix_norm, mlp_norm, mlp_w1, mlp_w2, ssm_log_dt, ssm_a_re, ssm_a_im, ssm_b_re, ssm_b_im, ssm_c_re, ssm_c_im, ssm_d, ssm_w_glu, kv_norm, w_kvf, b_f, attn_wq, attn_wo, final_norm, loss_target, m_mix_norm, m_mlp_norm, m_mlp_w1, m_mlp_w2, m_ssm_log_dt, m_ssm_a_re, m_ssm_a_im, m_ssm_b_re, m_ssm_b_im, m_ssm_c_re, m_ssm_c_im, m_ssm_d, m_ssm_w_glu, m_kv_norm, m_w_kvf, m_b_f, m_attn_wq, m_attn_wo, m_final_norm, v_mix_norm, v_mlp_norm, v_mlp_w1, v_mlp_w2, v_ssm_log_dt, v_ssm_a_re, v_ssm_a_im, v_ssm_b_re, v_ssm_b_im, v_ssm_c_re, v_ssm_c_im, v_ssm_d, v_ssm_w_glu, v_kv_norm, v_w_kvf, v_b_f, v_attn_wq, v_attn_wo, v_final_norm):
    n_rows, d_model = x.shape[1], x.shape[2]
    depth = mix_norm.shape[0]
    n_a = ssm_log_dt.shape[0]
    n_b = depth - n_a
    n_heads = d_model // HEAD_DIM
    n_groups = d_model // SSM_GROUP
    nb = n_groups // GROUPS_PER_BLOCK
    kvf_cols = 2 * d_model + n_heads
    kvf_pad = 2 * d_model + LANES

    (g_d,) = all_gather("gather_skip_gain", [ssm_d])
    d_skip = [g_d[:, i].reshape(1, d_model) for i in range(n_a)]
    bf = lambda a: a.astype(BF16)
    early = n_a if n_a >= 2 else depth
    gather_sets = [[bf(mlp_w1[:early]), bf(mlp_w2[:early]), bf(ssm_w_glu)], [bf(w_kvf), bf(attn_wq), bf(attn_wo)]]
    if early < depth:
        gather_sets[1] += [bf(mlp_w1[early:]), bf(mlp_w2[early:])]
    else:
        gather_sets = [gather_sets[0] + gather_sets[1]]
    w1, w2 = [None] * depth, [None] * depth
    wglu = wkvf = wq = wo = None

    row = lambda a: a.reshape(1, -1)
    col = lambda a: a.reshape(-1, 1)

    h = x[0]
    saved = []
    k_ext = v_ext = None
    q_consts = _lane_consts([(F_LANE, F_LANE + 3, 1.0)])
    k_consts = _lane_consts([(LSE_LANE, LSE_LANE + 3, -1.0), (ROWSUM_LANE, ROWSUM_LANE + 1, 1.0)])
    v_consts = _lane_consts([(SUM_LANE, SUM_LANE + 1, 1.0), (DELTA_LANE, DELTA_LANE + 3, -1.0)])
    for i in range(depth):
        sv = {"h": h}
        hn = rms_fwd(f"mix_norm_{i}", h, row(mix_norm[i]), F32 if i < n_a else BF16)
        sv["hn"] = hn
        if i < n_a:
            z, s5_saved, gathered = s5_fwd(str(i), hn, ssm_log_dt[i], ssm_a_re[i], ssm_a_im[i], ssm_b_re[i], ssm_b_im[i],
                                           ssm_c_re[i], ssm_c_im[i], d_skip[i],
                                           gather=gather_sets[i] if i < len(gather_sets) else ())
            if i == 0:
                g_w1, g_w2, g_glu = gathered[:3]
                gathered = gathered[3:]
                for li in range(early):
                    w1[li] = _unshard_cols(g_w1[:, li])
                    w2[li] = g_w2[:, li].reshape(-1, d_model)
                wglu = [_unshard_cols(g_glu[:, li]) for li in range(n_a)]
            if i == len(gather_sets) - 1:
                g_kvf, g_wq, g_wo = gathered[:3]
                wkvf = jnp.pad(_unshard_cols(g_kvf), ((0, 0), (0, kvf_pad - kvf_cols)))
                wq = [g_wq[:, li].reshape(-1, d_model) for li in range(n_b)]
                wo = [g_wo[:, li].reshape(-1, d_model) for li in range(n_b)]
                for li in range(early, depth):
                    w1[li] = _unshard_cols(gathered[3][:, li - early])
                    w2[li] = gathered[4][:, li - early].reshape(-1, d_model)
            zw = matmul(f"s5_glu_{i}", z, wglu[i])
            def gate(hh, zz, gg):
                out = hh + zz[:, :d_model] * _sigmoid(zz[:, d_model:])
                return out, _rms(out, gg)

            h1, h2n = rowcall(f"s5_gate_{i}", gate, [h, zw], [row(mlp_norm[i])], [(d_model, F32), (d_model, BF16)])
            sv.update(s5=s5_saved, z=z, zw=zw)
        else:
            j = i - n_a
            q = matmul(f"attn_q_{j}", hn, wq[j], scale=LOG2E * HEAD_DIM ** -0.5, out_dtype=BF16)
            o_ext, q_ext_b = fox_fwd(f"attn_fwd_{j}", pack_heads(f"pack_q_{j}", q, 0, n_heads, q_consts), k_ext, v_ext)
            o2 = unpack_heads(f"unpack_o_{j}", [o_ext])[0]
            h1 = matmul(f"attn_o_{j}", o2, wo[j], resid=h)
            sv.update(q_ext_b=q_ext_b, o2=o2)
            h2n = rms_fwd(f"mlp_norm_{i}", h1, row(mlp_norm[i]), BF16)
        ap = matmul(f"mlp_up_{i}", h2n, w1[i], out_dtype=BF16)
        h = matmul(f"mlp_down_{i}", ap, w2[i], a_fn=_sqrelu, resid=h1, bk=2048)
        sv.update(h1=h1, h2n=h2n, ap=ap)
        saved.append(sv)
        if i == n_a - 1:
            h_mid = h
            hk = rms_fwd("kv_norm", h, row(kv_norm), BF16)
            kvf = matmul("kvf_proj", hk, wkvf, bn=kvf_pad)
            fl = kvf[:, 2 * d_model:]
            bfp = jnp.pad(row(b_f), ((0, 0), (0, LANES - n_heads)))
            k_ext = pack_heads("pack_k", kvf, 0, n_heads, k_consts, cum_logf("cum_logf", fl, bfp), F_LANE)
            v_ext = pack_heads("pack_v", kvf, 1, n_heads, v_consts)

    def loss_fn(hh, tgt, g):
        y, vjp = jax.vjp(_rms, hh, g)
        err = y - tgt
        part = 0.5 * jnp.sum(jnp.mean(err * err, axis=-1, keepdims=True), axis=0, keepdims=True)
        dh, dg = vjp(err * (1.0 / d_model))
        return dh, jnp.broadcast_to(part, (1, LANES)), dg

    dh, loss_part, d_final = rowcall("loss_head", loss_fn, [h, loss_target[0]], [row(final_norm)],
                                     [(d_model, F32)], [(1, LANES), (1, d_model)])

    g_mix, g_mlpn = [None] * depth, [None] * depth
    g_w1f, g_w2f = [None] * depth, [None] * depth
    g_ssm = [None] * n_a
    g_wqf, g_wof = [None] * n_b, [None] * n_b
    dk_acc, dv_acc, df_plus = [], [], []
    parts = None
    stack = lambda xs: jnp.stack(xs, axis=1).astype(BF16 if xs[0].ndim == 3 else F32)

    def reduce_pairs(tag, contributions):
        from_sibling = pair_swap(f"pair_swap_{tag}", contributions)
        core = lax.axis_index("c")
        sums = []
        for k, (mine, theirs) in enumerate(zip(contributions, from_sibling)):
            mine = lax.dynamic_index_in_dim(mine.reshape((N_CHIPS, 2) + mine.shape[1:]), core, axis=1, keepdims=False)
            cols = mine.shape[-1]
            sums.append(rowcall(f"pair_add_{tag}_{k}", lambda a, b: a.astype(F32) + b.astype(F32),
                                [mine.reshape(-1, cols), theirs.reshape(-1, cols)], [], [(cols, mine.dtype)])[0].reshape(mine.shape))
        return sums
    g_kv = None
    for i in reversed(range(depth)):
        sv = saved[i]
        if i == n_a - 1:
            dk, col_sums = unpack_heads("unpack_dk", dk_acc, extract_lane=HEAD_DIM)
            dv = unpack_heads("unpack_dv", dv_acc)[0]
            dfl, db_f = cum_logf_bwd("cum_logf_bwd", fl, bfp, df_plus, [col_sums])
            dkvf = jnp.concatenate([dk, dv, dfl], axis=1).astype(BF16)
            d_wkvf = matmul("kvf_dw", hk, dkvf, "tn", bm=512, bn=kvf_pad)
            dh, d_kvn = matmul("kvf_dx", dkvf, wkvf, "nt", bm=512, bk=kvf_pad, rms_bwd_of=(h_mid, row(kv_norm), dh))
            g_kv = (d_wkvf[:, :kvf_cols], d_kvn, db_f[:, :n_heads])
        dap = matmul(f"mlp_down_dx_{i}", dh, w2[i], "nt", post=lambda acc, apt: acc * (2.0 * jnp.maximum(apt.astype(F32), 0.0)),
                     post_arg=sv["ap"], out_dtype=BF16)
        g_w2f[i] = matmul(f"mlp_down_dw_{i}", sv["ap"], dh, "tn", a_fn=_sqrelu)
        g_w1f[i] = matmul(f"mlp_up_dw_{i}", sv["h2n"], dap, "tn")
        dh1, g_mlpn[i] = matmul(f"mlp_up_dx_{i}", dap, w1[i], "nt", bk=1024, rms_bwd_of=(sv["h1"], row(mlp_norm[i]), dh))
        if i < n_a:
            def glu_bwd(zz, dd):
                val, gate = zz[:, :d_model], zz[:, d_model:]
                sg = _sigmoid(gate)
                return jnp.concatenate([dd * sg, dd * val * sg * (1.0 - sg)], axis=1)

            dzw = rowcall(f"s5_gate_bwd_{i}", glu_bwd, [sv["zw"], dh1], [], [(2 * d_model, BF16)])[0]
            dz = matmul(f"s5_glu_dx_{i}", dzw, wglu[i], "nt")
            d_wglu = matmul(f"s5_glu_dw_{i}", sv["z"], dzw, "tn")

            pair_sums = ()
            if i == 0:
                glu_grads = [d_wglu] + [g_ssm[li]["w_glu"] for li in range(1, n_a)]
                pair_sums = reduce_pairs("grads", [
                    stack([_shard_cols(g) for g in g_w1f]),
                    stack([g.reshape(N_DEV, -1, d_model) for g in g_w2f]),
                    stack([_shard_cols(g) for g in glu_grads]),
                    _shard_cols(g_kv[0]).astype(BF16),
                    stack([g.reshape(N_DEV, -1, d_model) for g in g_wqf]),
                    stack([g.reshape(N_DEV, -1, d_model) for g in g_wof]),
                ])
            dhn, g_ssm[i], exchanged = s5_bwd(str(i), dz, sv["hn"], d_skip[i], sv["s5"], exchange=pair_sums)
            g_ssm[i]["w_glu"] = d_wglu
            if i == 0:
                parts = exchanged
        else:
            j = i - n_a
            do2 = matmul(f"attn_o_dx_{j}", dh1, wo[j], "nt")
            g_wof[j] = matmul(f"attn_o_dw_{j}", sv["o2"], dh1, "tn")
            do_ext = pack_heads(f"pack_do_{j}", do2, 0, n_heads, jnp.zeros((1, LANES), F32),
                                attn_delta(f"attn_delta_{j}", sv["o2"], do2), DELTA_LANE)
            dq_ext, dk_ext, dv_ext = fox_bwd(f"attn_bwd_{j}", sv["q_ext_b"], do_ext, k_ext, v_ext)
            dk_acc.append(dk_ext)
            dv_acc.append(dv_ext)
            dq2, row_sums = unpack_heads(f"unpack_dq_{j}", [dq_ext], extract_lane=ROWSUM_LANE)
            df_plus.append(row_sums)
            g_wqf[j] = matmul(f"attn_q_dw_{j}", sv["hn"], dq2, "tn", scale=HEAD_DIM ** -0.5)
            dh, g_mix[i] = matmul(f"attn_q_dx_{j}", dq2, wq[j], "nt", scale=HEAD_DIM ** -0.5, bm=512,
                                  rms_bwd_of=(sv["h"], row(mix_norm[i]), dh1))
        if i < n_a:
            dh, g_mix[i] = rms_bwd(f"mix_norm_bwd_{i}", sv["h"], row(mix_norm[i]), dhn, add=dh1)
    grad_x = dh[None]

    parts = list(parts) + list(chip_exchange("chip_exchange_skip_gain", reduce_pairs("skip_gain", [
        stack([g["d"].reshape(N_DEV, -1) for g in g_ssm])])))
    ssm_g = lambda kk: jnp.stack([g[kk] for g in g_ssm])
    loss_slot = jnp.zeros((LANES,), F32)
    small = {
        "f32": (["mix_norm", "mlp_norm", "ssm_log_dt", "ssm_a_re", "ssm_a_im", "kv_norm", "b_f", "final_norm"],
                [jnp.concatenate(g_mix, axis=0), jnp.concatenate(g_mlpn, axis=0), ssm_g("log_dt"), ssm_g("a_re"), ssm_g("a_im"),
                 g_kv[1], g_kv[2], d_final, loss_part],
                [mix_norm, mlp_norm, ssm_log_dt, ssm_a_re, ssm_a_im, kv_norm, b_f, final_norm, loss_slot],
                [m_mix_norm, m_mlp_norm, m_ssm_log_dt, m_ssm_a_re, m_ssm_a_im, m_kv_norm, m_b_f, m_final_norm, loss_slot],
                [v_mix_norm, v_mlp_norm, v_ssm_log_dt, v_ssm_a_re, v_ssm_a_im, v_kv_norm, v_b_f, v_final_norm, loss_slot]),
        "bf16": (["ssm_b_re", "ssm_b_im", "ssm_c_re", "ssm_c_im"],
                 [ssm_g("b_re"), ssm_g("b_im"), ssm_g("c_re"), ssm_g("c_im")],
                 [ssm_b_re, ssm_b_im, ssm_c_re, ssm_c_im], [m_ssm_b_re, m_ssm_b_im, m_ssm_c_re, m_ssm_c_im],
                 [v_ssm_b_re, v_ssm_b_im, v_ssm_c_re, v_ssm_c_im]),
    }
    packed = {kk: [_pack(arrs) for arrs in grp[1:]] for kk, grp in small.items()}
    small_parts = all_gather("gather_small_grads", [packed["f32"][0][0], packed["bf16"][0][0].astype(BF16)])

    res = {}

    def update(nm, part, w, m, v):
        shp = w.shape
        as2d = lambda a: a.reshape(-1, shp[-1])
        outs = adamw(f"adamw_{nm}", part.reshape(part.shape[:1] + as2d(w).shape), as2d(w), as2d(m), as2d(v))
        res[nm] = [o.reshape(shp) for o in outs]

    update("mlp_w1", parts[0], mlp_w1, m_mlp_w1, v_mlp_w1)
    update("mlp_w2", parts[1], mlp_w2, m_mlp_w2, v_mlp_w2)
    update("ssm_w_glu", parts[2], ssm_w_glu, m_ssm_w_glu, v_ssm_w_glu)
    update("w_kvf", parts[3], w_kvf, m_w_kvf, v_w_kvf)
    update("attn_wq", parts[4], attn_wq, m_attn_wq, v_attn_wq)
    update("attn_wo", parts[5], attn_wo, m_attn_wo, v_attn_wo)
    update("ssm_d", parts[6], ssm_d, m_ssm_d, v_ssm_d)
    loss = None
    for (kk, (names, _, ws, _, _)), part in zip(small.items(), small_parts):
        (_, offs), (pw, _), (pm, _), (pv, _) = packed[kk]
        small_out = adamw(f"adamw_small_{kk}", part, pw, pm, pv)
        unpacked = [_unpack(o, offs, [w.shape for w in ws]) for o in small_out]
        for idx, nm in enumerate(names):
            res[nm] = [u[idx] for u in unpacked]
        if kk == "f32":
            loss = unpacked[0][-1][0]

    order = ["mix_norm", "mlp_norm", "mlp_w1", "mlp_w2", "ssm_log_dt", "ssm_a_re", "ssm_a_im", "ssm_b_re", "ssm_b_im", "ssm_c_re",
             "ssm_c_im", "ssm_d", "ssm_w_glu", "kv_norm", "w_kvf", "b_f", "attn_wq", "attn_wo", "final_norm"]
    out = [loss, grad_x]
    for kind in range(4):
        out += [res[nm][kind] for nm in order]
    return tuple(out)
```
